```python
import jax
import jax.numpy as jnp
from jax import lax
import numpy as np

D_MODEL = 1024
BATCH = 8
SEQ = 4096
DEPTH = 1

HEAD_DIM = 64
RWKV_WIDTH = D_MODEL // 2
RWKV_HEADS = RWKV_WIDTH // HEAD_DIM
D_DECAY_LORA = 64
D_AAA_LORA = 64
D_GATE_LORA = 128
RWKV_COLS = 3 * RWKV_WIDTH + D_DECAY_LORA + D_AAA_LORA + D_GATE_LORA
NSA_WIDTH = D_MODEL - RWKV_WIDTH
NSA_Q_HEADS = NSA_WIDTH // HEAD_DIM
NSA_KV_HEADS = 2
NSA_HPG = NSA_Q_HEADS // NSA_KV_HEADS
NSA_KV_WIDTH = NSA_KV_HEADS * HEAD_DIM
N_BRANCH = 3
NSA_COLS = NSA_WIDTH + 6 * NSA_KV_WIDTH + N_BRANCH * NSA_Q_HEADS
IN_COLS = RWKV_COLS + NSA_COLS
CMP_BLOCK = 32
CMP_STRIDE = 16
CMP_HIDDEN = 256
SEL_BLOCK = 64
SEL_TOPK = 16
WINDOW = 512
Q_BLOCK = 64
ROPE_THETA = 10000.0
D_FF = -(-8 * D_MODEL // (3 * 256)) * 256
RMS_EPS = 1e-6
LNX_EPS = 64e-5
FORCE_BONUS = 1e4
NEG_INF = -1e30

kernel_name = 'hybrid_rwkv7_nsa_block'


def rmsnorm(x, w):
    xf = x.astype(jnp.float32)
    y = xf * lax.rsqrt(jnp.mean(xf * xf, axis=-1, keepdims=True) + RMS_EPS)
    return (y * w.astype(jnp.float32)).astype(x.dtype)


def rope(x, positions):
    half = x.shape[-1] // 2
    inv_freq = ROPE_THETA ** (-jnp.arange(half, dtype=jnp.float32) / half)
    ang = positions.astype(jnp.float32)[..., None] * inv_freq
    cos, sin = jnp.cos(ang)[:, :, None, :], jnp.sin(ang)[:, :, None, :]
    x1, x2 = x[..., :half], x[..., half:]
    return jnp.concatenate([x1 * cos - x2 * sin, x2 * cos + x1 * sin], axis=-1)


def token_shift(y):
    return jnp.pad(y, ((0, 0), (1, 0), (0, 0)))[:, :-1]


def rwkv7_group(y, mu, w0, w_lora_up, a0, a_lora_up, g_lora_up, k_k, k_a, r_k, lnx_w, lnx_b):
    B, T, _ = y.shape
    H, N = RWKV_HEADS, HEAD_DIM
    y = y + (token_shift(y) - y) * mu
    splits = [RWKV_WIDTH, 2 * RWKV_WIDTH, 3 * RWKV_WIDTH,
              3 * RWKV_WIDTH + D_DECAY_LORA, 3 * RWKV_WIDTH + D_DECAY_LORA + D_AAA_LORA]
    r, k, v, wd, ad, gd = jnp.split(y, splits, axis=-1)
    w_raw = w0 + jnp.tanh(wd) @ w_lora_up
    decay = jnp.exp(-jnp.exp(-jax.nn.softplus(-w_raw) - 0.5))
    a = jax.nn.sigmoid(a0 + ad @ a_lora_up)
    g = jax.nn.sigmoid(gd) @ g_lora_up
    kk = (k * k_k).reshape(B, T, H, N)
    kk = kk * lax.rsqrt(jnp.maximum(jnp.sum(kk * kk, axis=-1, keepdims=True), 1e-12))
    k = k * (1.0 + (a - 1.0) * k_a)
    r, decay, k, v, a = (z.reshape(B, T, H, N) for z in (r, decay, k, v, a))
    xs = tuple(jnp.moveaxis(z, 1, 0) for z in (r, decay, k, v, kk, a))

    def step(S, inp):
        r_t, w_t, k_t, v_t, kk_t, a_t = inp
        sa = jnp.einsum('bhij,bhj->bhi', S, -kk_t)
        S = (S * w_t[:, :, None, :] + sa[..., None] * (kk_t * a_t)[:, :, None, :]
             + v_t[..., None] * k_t[:, :, None, :])
        return S, jnp.einsum('bhij,bhj->bhi', S, r_t)

    S0 = jnp.zeros((B, H, N, N), jnp.float32)
    _, o = lax.scan(step, S0, xs)
    o = jnp.moveaxis(o, 0, 1)
    mean = jnp.mean(o, axis=-1, keepdims=True)
    var = jnp.mean(jnp.square(o - mean), axis=-1, keepdims=True)
    o = ((o - mean) * lax.rsqrt(var + LNX_EPS)).reshape(B, T, RWKV_WIDTH) * lnx_w + lnx_b
    bonus = jnp.sum(r * k * r_k, axis=-1, keepdims=True) * v
    o = o + bonus.reshape(B, T, RWKV_WIDTH)
    return o * g


def nsa_group(y, positions, cmp_pos_k, cmp_pos_v, cmp_k_w1, cmp_k_w2, cmp_v_w1, cmp_v_w2):
    B, T, _ = y.shape
    G, HPG, D = NSA_KV_HEADS, NSA_HPG, HEAD_DIM
    splits = [NSA_WIDTH + i * NSA_KV_WIDTH for i in range(7)]
    q, kc, vc, ks, vs, kw, vw, gl = jnp.split(y, splits, axis=-1)
    q = rope(q.reshape(B, T, NSA_Q_HEADS, D), positions)
    q = q.reshape(B, T, G, HPG, D).transpose(0, 2, 3, 1, 4)
    gates = jax.nn.sigmoid(gl).reshape(B, T, G, HPG, N_BRANCH).transpose(0, 2, 3, 1, 4)
    kc, ks, kw = (rope(z.reshape(B, T, G, D), positions) for z in (kc, ks, kw))
    vc, vs, vw = (z.reshape(B, T, G, D) for z in (vc, vs, vw))

    n_cmp = (T - CMP_BLOCK) // CMP_STRIDE + 1
    cmp_start = jnp.arange(n_cmp) * CMP_STRIDE
    cmp_idx = cmp_start[:, None] + jnp.arange(CMP_BLOCK)[None, :]

    def compress(z, pos_emb, w1, w2):
        blk = z[:, cmp_idx] + pos_emb[None, None, :, None, :]
        blk = jnp.moveaxis(blk, 3, 1).reshape(B, G, n_cmp, CMP_BLOCK * D)
        return jax.nn.gelu(blk @ w1) @ w2

    k_cmp = compress(kc, cmp_pos_k, cmp_k_w1, cmp_k_w2)
    v_cmp = compress(vc, cmp_pos_v, cmp_v_w1, cmp_v_w2)
    cmp_end = cmp_start + CMP_BLOCK - 1

    n_sb = T // SEL_BLOCK
    n_sel = min(SEL_TOPK, n_sb)
    sb = jnp.arange(n_sb)
    overlap = ((cmp_start[:, None] < (sb[None, :] + 1) * SEL_BLOCK)
               & (cmp_start[:, None] + CMP_BLOCK > sb[None, :] * SEL_BLOCK)).astype(jnp.float32)
    ks_blk = ks.transpose(0, 2, 1, 3).reshape(B, G, n_sb, SEL_BLOCK, D)
    vs_blk = vs.transpose(0, 2, 1, 3).reshape(B, G, n_sb, SEL_BLOCK, D)
    gather = jax.vmap(jax.vmap(lambda blocks, ix: blocks[ix]))

    pad = ((0, 0), (0, 0), (WINDOW, 0), (0, 0))
    kw_pad = jnp.pad(kw.transpose(0, 2, 1, 3), pad)
    vw_pad = jnp.pad(vw.transpose(0, 2, 1, 3), pad)
    scale = HEAD_DIM ** -0.5

    def block(s):
        t = s + jnp.arange(Q_BLOCK)
        qb = lax.dynamic_slice_in_dim(q, s, Q_BLOCK, axis=3)
        gb = lax.dynamic_slice_in_dim(gates, s, Q_BLOCK, axis=3)
        cmask = cmp_end[None, :] <= t[:, None]
        sc = jnp.einsum('bghqd,bgnd->bghqn', qb, k_cmp) * scale
        p_c = jnp.where(cmask, jax.nn.softmax(jnp.where(cmask, sc, NEG_INF), axis=-1), 0.0)
        o_c = jnp.einsum('bghqn,bgnd->bghqd', p_c, v_cmp)
        imp = jnp.einsum('bghqn,nj->bgqj', p_c, overlap)
        tb = t // SEL_BLOCK
        forced = (sb[None, :] == 0) | (sb[None, :] == tb[:, None]) | (sb[None, :] == tb[:, None] - 1)
        imp = jnp.where(sb[None, :] * SEL_BLOCK <= t[:, None], imp + FORCE_BONUS * forced, NEG_INF)
        _, sel = lax.top_k(imp, n_sel)
        k_sel = gather(ks_blk, sel)
        v_sel = gather(vs_blk, sel)
        kpos = sel[..., None] * SEL_BLOCK + jnp.arange(SEL_BLOCK)
        smask = (kpos <= t[None, None, :, None, None])[:, :, None]
        ss = jnp.einsum('bghqd,bgqnld->bghqnl', qb, k_sel) * scale
        ss = jnp.where(smask, ss, NEG_INF).reshape(B, G, HPG, Q_BLOCK, n_sel * SEL_BLOCK)
        p_s = jax.nn.softmax(ss, axis=-1).reshape(B, G, HPG, Q_BLOCK, n_sel, SEL_BLOCK)
        o_s = jnp.einsum('bghqnl,bgqnld->bghqd', p_s, v_sel)
        kwb = lax.dynamic_slice_in_dim(kw_pad, s, WINDOW + Q_BLOCK, axis=2)
        vwb = lax.dynamic_slice_in_dim(vw_pad, s, WINDOW + Q_BLOCK, axis=2)
        kp = s - WINDOW + jnp.arange(WINDOW + Q_BLOCK)
        wmask = (kp[None, :] <= t[:, None]) & (kp[None, :] > t[:, None] - WINDOW) & (kp[None, :] >= 0)
        sw = jnp.einsum('bghqd,bgkd->bghqk', qb, kwb) * scale
        p_w = jax.nn.softmax(jnp.where(wmask, sw, NEG_INF), axis=-1)
        o_w = jnp.einsum('bghqk,bgkd->bghqd', p_w, vwb)
        return gb[..., 0:1] * o_c + gb[..., 1:2] * o_s + gb[..., 2:3] * o_w

    o = lax.map(block, jnp.arange(T // Q_BLOCK) * Q_BLOCK)
    return o.transpose(1, 0, 4, 2, 3, 5).reshape(B, T, NSA_WIDTH)


def setup_inputs(seed: int = 0) -> dict:
    key = jax.random.key(seed)
    ks = jax.random.split(key, 32)
    L = DEPTH

    def nrm(k, shape, scale):
        return jax.random.normal(k, shape, jnp.float32) * scale

    return {
        'x': nrm(ks[0], (BATCH, SEQ, D_MODEL), 1.0),
        'positions': jnp.broadcast_to(jnp.arange(SEQ, dtype=jnp.int32), (BATCH, SEQ)),
        'norm1_w': 1.0 + nrm(ks[1], (L, D_MODEL), 0.02),
        'w_in': nrm(ks[2], (L, D_MODEL, IN_COLS), D_MODEL ** -0.5),
        'mu_rwkv': jax.random.uniform(ks[3], (L, RWKV_COLS), jnp.float32),
        'w0': jax.random.uniform(ks[4], (L, RWKV_WIDTH), jnp.float32, minval=-3.0, maxval=1.0),
        'w_lora_up': nrm(ks[5], (L, D_DECAY_LORA, RWKV_WIDTH), 0.1),
        'a0': nrm(ks[6], (L, RWKV_WIDTH), 0.1),
        'a_lora_up': nrm(ks[7], (L, D_AAA_LORA, RWKV_WIDTH), 0.5 * D_AAA_LORA ** -0.5),
        'g_lora_up': nrm(ks[8], (L, D_GATE_LORA, RWKV_WIDTH), D_GATE_LORA ** -0.5),
        'k_k': 0.85 + nrm(ks[9], (L, RWKV_WIDTH), 0.02),
        'k_a': 1.0 + nrm(ks[10], (L, RWKV_WIDTH), 0.02),
        'r_k': nrm(ks[11], (L, RWKV_HEADS, HEAD_DIM), 0.1),
        'lnx_w': 1.0 + nrm(ks[12], (L, RWKV_WIDTH), 0.02),
        'lnx_b': nrm(ks[13], (L, RWKV_WIDTH), 0.01),
        'cmp_pos_k': nrm(ks[14], (L, CMP_BLOCK, HEAD_DIM), 0.02),
        'cmp_pos_v': nrm(ks[15], (L, CMP_BLOCK, HEAD_DIM), 0.02),
        'cmp_k_w1': nrm(ks[16], (L, CMP_BLOCK * HEAD_DIM, CMP_HIDDEN), (CMP_BLOCK * HEAD_DIM) ** -0.5),
        'cmp_k_w2': nrm(ks[17], (L, CMP_HIDDEN, HEAD_DIM), CMP_HIDDEN ** -0.5),
        'cmp_v_w1': nrm(ks[18], (L, CMP_BLOCK * HEAD_DIM, CMP_HIDDEN), (CMP_BLOCK * HEAD_DIM) ** -0.5),
        'cmp_v_w2': nrm(ks[19], (L, CMP_HIDDEN, HEAD_DIM), CMP_HIDDEN ** -0.5),
        'w_out': nrm(ks[20], (L, D_MODEL, D_MODEL), D_MODEL ** -0.5),
        'norm2_w': 1.0 + nrm(ks[21], (L, D_MODEL), 0.02),
        'ffn_w1': nrm(ks[22], (L, D_MODEL, D_FF), D_MODEL ** -0.5),
        'ffn_w3': nrm(ks[23], (L, D_MODEL, D_FF), D_MODEL ** -0.5),
        'ffn_w2': nrm(ks[24], (L, D_FF, D_MODEL), D_FF ** -0.5),
        'final_norm_w': 1.0 + nrm(ks[25], (D_MODEL,), 0.02),
    }


def reference(x, positions, norm1_w, w_in, mu_rwkv, w0, w_lora_up, a0, a_lora_up, g_lora_up,
              k_k, k_a, r_k, lnx_w, lnx_b, cmp_pos_k, cmp_pos_v, cmp_k_w1, cmp_k_w2,
              cmp_v_w1, cmp_v_w2, w_out, norm2_w, ffn_w1, ffn_w3, ffn_w2, final_norm_w):
    h = x
    for l in range(DEPTH):
        y = (rmsnorm(h, norm1_w[l]) @ w_in[l]).astype(jnp.float32)
        o_rwkv = rwkv7_group(y[..., :RWKV_COLS], mu_rwkv[l], w0[l], w_lora_up[l], a0[l],
                             a_lora_up[l], g_lora_up[l], k_k[l], k_a[l], r_k[l], lnx_w[l], lnx_b[l])
        o_nsa = nsa_group(y[..., RWKV_COLS:], positions, cmp_pos_k[l], cmp_pos_v[l],
                          cmp_k_w1[l], cmp_k_w2[l], cmp_v_w1[l], cmp_v_w2[l])
        o = jnp.concatenate([o_rwkv, o_nsa], axis=-1).astype(h.dtype)
        h = h + o @ w_out[l]
        u = rmsnorm(h, norm2_w[l])
        h = h + (jax.nn.silu(u @ ffn_w1[l]) * (u @ ffn_w3[l])) @ ffn_w2[l]
    return rmsnorm(h, final_norm_w)
```

```python
import functools

import jax
import jax.numpy as jnp
from jax import lax
from jax.experimental import pallas as pl
from jax.experimental.pallas import tpu as pltpu

F32 = jnp.float32
BF16 = jnp.bfloat16
HI = lax.Precision.HIGHEST

D_MODEL = 1024
HEAD_DIM = 64
RWKV_WIDTH = 512
RWKV_HEADS = 8
D_DECAY_LORA = 64
D_AAA_LORA = 64
D_GATE_LORA = 128
RWKV_COLS = 3 * RWKV_WIDTH + D_DECAY_LORA + D_AAA_LORA + D_GATE_LORA
NSA_WIDTH = 512
NSA_Q_HEADS = 8
NSA_KV_HEADS = 2
NSA_HPG = 4
NSA_KV_WIDTH = 128
N_BRANCH = 3
CMP_BLOCK = 32
CMP_STRIDE = 16
CMP_HIDDEN = 256
SEL_BLOCK = 64
SEL_TOPK = 16
WINDOW = 512
Q_BLOCK = 64
ROPE_THETA = 10000.0
D_FF = 2816
RMS_EPS = 1e-6
LNX_EPS = 64e-5
FORCE_BONUS = 1e4
NEG_INF = -1e30

LANES = 128
CHUNK = 64
Q_PAD_COLS = NSA_Q_HEADS * LANES
PROJ_COLS = RWKV_COLS + Q_PAD_COLS + 6 * NSA_KV_WIDTH + LANES
VMEM_LIMIT = 56 * 1024 * 1024


def _dot(a, b, prec=None):
    return jnp.dot(a, b, preferred_element_type=F32, precision=prec)


def _dot_nt(a, b, prec=None):
    return lax.dot_general(a, b, (((1,), (1,)), ((), ())), preferred_element_type=F32, precision=prec)


def _dot_tn(a, b, prec=None):
    return lax.dot_general(a, b, (((0,), (0,)), ((), ())), preferred_element_type=F32, precision=prec)


def _params(*sem):
    return pltpu.CompilerParams(dimension_semantics=sem, vmem_limit_bytes=VMEM_LIMIT)


def _rope_table_kernel(pos_ref, invf_ref, sign_ref, cos_ref, sin_ref):
    ang = pos_ref[...].astype(F32) * invf_ref[...]
    cos_ref[...] = jnp.cos(ang)
    sin_ref[...] = jnp.sin(ang) * sign_ref[...]


def _rope_tables(positions):
    n = positions.size
    tm = 512
    half = HEAD_DIM // 2
    inv_freq = ROPE_THETA ** (-jnp.arange(half, dtype=F32) / half)
    invf = jnp.tile(inv_freq, LANES // half).reshape(1, LANES)
    lane = jnp.arange(LANES)
    sign = jnp.where((lane % HEAD_DIM) < half, -1.0, 1.0).astype(F32).reshape(1, LANES)
    row = pl.BlockSpec((tm, LANES), lambda i: (i, 0))
    const = pl.BlockSpec((1, LANES), lambda i: (0, 0))
    return pl.pallas_call(
        _rope_table_kernel,
        grid=(n // tm,),
        in_specs=[pl.BlockSpec((tm, 1), lambda i: (i, 0)), const, const],
        out_specs=[row, row],
        out_shape=[jax.ShapeDtypeStruct((n, LANES), F32)] * 2,
        compiler_params=_params("parallel"),
        name="rope_tables",
    )(positions.reshape(n, 1), invf, sign)


def _proj_kernel(x_ref, n1_ref, w_ref, cos_ref, sin_ref,
                 yr_ref, q_ref, kc_ref, vc_ref, ks_ref, vs_ref, kw_ref, vw_ref, gl_ref):
    x = x_ref[...]
    ms = jnp.mean(x * x, axis=-1, keepdims=True)
    xn = (x * lax.rsqrt(ms + RMS_EPS) * n1_ref[...]).astype(BF16)
    cos = cos_ref[...]
    sin = sin_ref[...]
    lane = lax.broadcasted_iota(jnp.int32, cos.shape, 1)
    first_half = (lane % HEAD_DIM) < (HEAD_DIM // 2)

    def rope(v):
        rot = jnp.where(first_half, pltpu.roll(v, LANES - HEAD_DIM // 2, 1), pltpu.roll(v, HEAD_DIM // 2, 1))
        return v * cos + rot * sin

    def cols(c0, width):
        return _dot(xn, w_ref[:, c0:c0 + width])

    for j in range(RWKV_COLS // 256):
        yr_ref[:, j * 256:(j + 1) * 256] = cols(j * 256, 256)
    scale = HEAD_DIM ** -0.5
    for hq in range(NSA_Q_HEADS):
        q = rope(cols(RWKV_COLS + hq * LANES, LANES)) * scale
        q_ref[:, hq * LANES:(hq + 1) * LANES] = q.astype(BF16)
    c0 = RWKV_COLS + Q_PAD_COLS
    kc_ref[...] = rope(cols(c0, LANES))
    vc_ref[...] = cols(c0 + LANES, LANES)
    ks_ref[...] = rope(cols(c0 + 2 * LANES, LANES)).astype(BF16)
    vs_ref[...] = cols(c0 + 3 * LANES, LANES).astype(BF16)
    kw_ref[...] = rope(cols(c0 + 4 * LANES, LANES)).astype(BF16)
    vw_ref[...] = cols(c0 + 5 * LANES, LANES).astype(BF16)
    gl_ref[...] = cols(c0 + 6 * LANES, LANES)


def _pack_w_in(w_in):
    w_r = w_in[:, :RWKV_COLS]
    w_q = w_in[:, RWKV_COLS:RWKV_COLS + NSA_WIDTH].reshape(D_MODEL, NSA_KV_HEADS, NSA_HPG, HEAD_DIM)
    zero = jnp.zeros_like(w_q)
    w_q = jnp.stack([jnp.where(jnp.arange(NSA_KV_HEADS)[None, :, None, None] == gg, w_q, zero)
                     for gg in range(NSA_KV_HEADS)], axis=3)
    w_q = w_q.reshape(D_MODEL, Q_PAD_COLS)
    c0 = RWKV_COLS + NSA_WIDTH
    w_kv = w_in[:, c0:c0 + 6 * NSA_KV_WIDTH]
    w_g = w_in[:, c0 + 6 * NSA_KV_WIDTH:]
    w_g = jnp.pad(w_g, ((0, 0), (0, LANES - w_g.shape[1])))
    return jnp.concatenate([w_r, w_q, w_kv, w_g], axis=1).astype(BF16)


def _in_proj(x2, norm1_w, w_packed, cos_t, sin_t):
    n = x2.shape[0]
    tm = 256
    row = lambda width: pl.BlockSpec((tm, width), lambda i: (i, 0))
    const = lambda shape: pl.BlockSpec(shape, lambda i: (0, 0))
    out_shape = [
        jax.ShapeDtypeStruct((n, RWKV_COLS), F32),
        jax.ShapeDtypeStruct((n, Q_PAD_COLS), BF16),
        jax.ShapeDtypeStruct((n, LANES), F32),
        jax.ShapeDtypeStruct((n, LANES), F32),
        jax.ShapeDtypeStruct((n, LANES), BF16),
        jax.ShapeDtypeStruct((n, LANES), BF16),
        jax.ShapeDtypeStruct((n, LANES), BF16),
        jax.ShapeDtypeStruct((n, LANES), BF16),
        jax.ShapeDtypeStruct((n, LANES), F32),
    ]
    out_specs = [row(RWKV_COLS), row(Q_PAD_COLS)] + [row(LANES)] * 7
    return pl.pallas_call(
        _proj_kernel,
        grid=(n // tm,),
        in_specs=[row(D_MODEL), const((1, D_MODEL)), const((D_MODEL, PROJ_COLS)), row(LANES), row(LANES)],
        out_specs=out_specs,
        out_shape=out_shape,
        compiler_params=_params("parallel"),
        name="in_proj",
    )(x2, norm1_w.reshape(1, D_MODEL), w_packed, cos_t, sin_t)


RWKV_TT = 256


def _rwkv_kernel(y_ref, mu_ref, w0_ref, wl_ref, a0_ref, gup_ref, kk_ref, ka_ref, rk_ref,
                 lw_ref, lb_ref, hs_ref, tri_ref, o_ref, prev_ref, s_ref, o_scr):
    i = pl.program_id(1)

    @pl.when(i == 0)
    def _():
        prev_ref[...] = jnp.zeros_like(prev_ref)
        s_ref[...] = jnp.zeros_like(s_ref)

    tt = y_ref.shape[0]
    y = y_ref[...]
    row = lax.broadcasted_iota(jnp.int32, (tt, 1), 0)
    y_prev = jnp.where(row == 0, prev_ref[...], pltpu.roll(y, 1, 0))
    prev_ref[...] = y[tt - 1:tt, :]
    ys = y + (y_prev - y) * mu_ref[...]

    w_ = RWKV_WIDTH
    r = ys[:, 0:w_]
    k = ys[:, w_:2 * w_]
    v = ys[:, 2 * w_:3 * w_]
    z = ys[:, 3 * w_:3 * w_ + LANES]
    gd = ys[:, 3 * w_ + LANES:3 * w_ + 2 * LANES]
    lane = lax.broadcasted_iota(jnp.int32, z.shape, 1)
    zt = jnp.where(lane < D_DECAY_LORA, jnp.tanh(z), z)
    wa = _dot(zt, wl_ref[...], HI)
    w_raw = w0_ref[...] + wa[:, :w_]
    a = jax.nn.sigmoid(a0_ref[...] + wa[:, w_:])
    logw = -jnp.exp(-jax.nn.softplus(-w_raw) - 0.5)
    g = _dot(jax.nn.sigmoid(gd), gup_ref[...], HI)
    hs = hs_ref[...]
    kk = k * kk_ref[...]
    kk = kk * lax.rsqrt(jnp.maximum(_dot(kk * kk, hs, HI), 1e-12))
    k2 = k * (1.0 + (a - 1.0) * ka_ref[...])
    alpha = -kk
    beta = kk * a

    cw = _dot(tri_ref[...], logw, HI)
    e_in = jnp.exp(cw)
    e_ex = jnp.exp(cw - logw)
    e_neg = jnp.exp(-cw)
    a_t = alpha * e_ex
    r_t = r * e_in
    b_t = beta * e_neg
    k_t = k2 * e_neg

    m0 = lax.broadcasted_iota(jnp.int32, (CHUNK, LANES), 1) < HEAD_DIM

    def bd(xc):
        return jnp.concatenate([jnp.where(m0, xc, 0.0), jnp.where(m0, 0.0, xc)], axis=0)

    n2 = 2 * CHUNK
    ri = lax.broadcasted_iota(jnp.int32, (n2, n2), 0) % CHUNK
    ci = lax.broadcasted_iota(jnp.int32, (n2, n2), 1) % CHUNK
    strict = ri > ci
    incl = ri >= ci
    eye = jnp.where(lax.broadcasted_iota(jnp.int32, (n2, n2), 0) == lax.broadcasted_iota(jnp.int32, (n2, n2), 1),
                    1.0, 0.0)

    for c in range(tt // CHUNK):
        rs = slice(c * CHUNK, (c + 1) * CHUNK)
        cw_last = cw[c * CHUNK + CHUNK - 1:c * CHUNK + CHUNK, :]
        e_tot = jnp.exp(cw_last - cw[rs, :])
        w_c = jnp.exp(cw_last)
        for p in range(RWKV_WIDTH // LANES):
            ls = slice(p * LANES, (p + 1) * LANES)
            a_bd = bd(a_t[rs, ls])
            r_bd = bd(r_t[rs, ls])
            b_bd = bd(b_t[rs, ls])
            k_bd = bd(k_t[rs, ls])
            v_bd = bd(v[rs, ls])
            bh_bd = bd(beta[rs, ls] * e_tot[:, ls])
            kh_bd = bd(k2[rs, ls] * e_tot[:, ls])
            gm = _dot_nt(jnp.concatenate([a_bd, r_bd], axis=0), jnp.concatenate([b_bd, k_bd], axis=0), HI)
            l_ab = jnp.where(strict, gm[0:n2, 0:n2], 0.0)
            l_ak = jnp.where(strict, gm[0:n2, n2:], 0.0)
            m_rb = jnp.where(incl, gm[n2:, 0:n2], 0.0)
            m_rk = jnp.where(incl, gm[n2:, n2:], 0.0)
            tm_ = eye + l_ab
            pw = l_ab
            for _ in range(5):
                pw = _dot(pw, pw, HI)
                tm_ = tm_ + _dot(tm_, pw, HI)
            a_hat = _dot(tm_, a_bd, HI)
            u0 = _dot(tm_, _dot(l_ak, v_bd, HI), HI)
            o0 = _dot(m_rk, v_bd, HI)
            s_old = s_ref[p]
            xs = _dot_nt(jnp.concatenate([a_hat, r_bd], axis=0), s_old, HI)
            u = xs[0:n2] + u0
            o_bd = xs[n2:] + _dot(m_rb, u, HI) + o0
            o_scr[rs, ls] = o_bd[0:CHUNK] + o_bd[CHUNK:]
            s_ref[p] = s_old * w_c[:, ls] + _dot_tn(jnp.concatenate([u, v_bd], axis=0),
                                                   jnp.concatenate([bh_bd, kh_bd], axis=0), HI)

    o = o_scr[...]
    inv_n = 1.0 / HEAD_DIM
    mean = _dot(o, hs, HI) * inv_n
    d = o - mean
    var = _dot(d * d, hs, HI) * inv_n
    on = d * lax.rsqrt(var + LNX_EPS) * lw_ref[...] + lb_ref[...]
    bonus = _dot(r * k2 * rk_ref[...], hs, HI) * v
    o_ref[...] = ((on + bonus) * g).astype(o_ref.dtype)


def _rwkv(yr, b, t, mu, w0, w_lora_up, a0, a_lora_up, g_lora_up, k_k, k_a, r_k, lnx_w, lnx_b):
    tt = RWKV_TT
    nt = t // tt
    w_ = RWKV_WIDTH
    wl = jnp.zeros((LANES, 2 * w_), F32)
    wl = wl.at[:D_DECAY_LORA, :w_].set(w_lora_up).at[D_DECAY_LORA:, w_:].set(a_lora_up)
    head = jnp.arange(w_) // HEAD_DIM
    hs = (head[:, None] == head[None, :]).astype(F32)
    ti = jnp.arange(tt)
    tri = ((ti[:, None] // CHUNK == ti[None, :] // CHUNK) & (ti[:, None] >= ti[None, :])).astype(F32)
    vec = lambda a_, width: a_.reshape(1, width)
    const = lambda shape: pl.BlockSpec(shape, lambda bi, i: (0, 0))
    return pl.pallas_call(
        _rwkv_kernel,
        grid=(b, nt),
        in_specs=[pl.BlockSpec((tt, RWKV_COLS), lambda bi, i: (bi * nt + i, 0)),
                  const((1, RWKV_COLS)), const((1, w_)), const((LANES, 2 * w_)), const((1, w_)),
                  const((D_GATE_LORA, w_)), const((1, w_)), const((1, w_)), const((1, w_)),
                  const((1, w_)), const((1, w_)), const((w_, w_)), const((tt, tt))],
        out_specs=pl.BlockSpec((tt, w_), lambda bi, i: (bi * nt + i, 0)),
        out_shape=jax.ShapeDtypeStruct((b * t, w_), BF16),
        scratch_shapes=[pltpu.VMEM((1, RWKV_COLS), F32),
                        pltpu.VMEM((w_ // LANES, LANES, LANES), F32),
                        pltpu.VMEM((tt, w_), F32)],
        compiler_params=_params("parallel", "arbitrary"),
        name="rwkv7",
    )(yr, vec(mu, RWKV_COLS), vec(w0, w_), wl, vec(a0, w_), g_lora_up, vec(k_k, w_), vec(k_a, w_),
      vec(r_k, w_), vec(lnx_w, w_), vec(lnx_b, w_), hs, tri)


def _compress_kernel(zk_ref, zv_ref, pek_ref, pev_ref, wk1_ref, wv1_ref, wk2_ref, wv2_ref, kc_ref, vc_ref):
    def one(z_ref, pe_ref, w1_ref, w2_ref, out_ref):
        z = z_ref[0]
        nrow = z.shape[0]
        za = z + pe_ref[0:1, :]
        zb = z + pe_ref[1:2, :]
        acc = jnp.zeros((nrow, LANES), F32)
        for gi in range(NSA_KV_HEADS):
            ha = _dot(za, w1_ref[gi, 0], HI)
            hb = _dot(zb, w1_ref[gi, 1], HI)
            hid = ha + pltpu.roll(hb, nrow - 1, 0)
            acc = acc + _dot(jax.nn.gelu(hid), w2_ref[gi], HI)
        out_ref[0] = acc

    one(zk_ref, pek_ref, wk1_ref, wk2_ref, kc_ref)
    one(zv_ref, pev_ref, wv1_ref, wv2_ref, vc_ref)


def _compress_weights(pos, w1, w2):
    half = CMP_BLOCK // 2
    w1r = w1.reshape(2, half, HEAD_DIM, CMP_HIDDEN)
    w1g = jnp.zeros((NSA_KV_HEADS, 2, half, NSA_KV_HEADS, HEAD_DIM, CMP_HIDDEN), F32)
    for gi in range(NSA_KV_HEADS):
        w1g = w1g.at[gi, :, :, gi].set(w1r)
    w1g = w1g.reshape(NSA_KV_HEADS, 2, half * NSA_KV_WIDTH, CMP_HIDDEN)
    w2g = jnp.zeros((NSA_KV_HEADS, CMP_HIDDEN, NSA_KV_HEADS, HEAD_DIM), F32)
    for gi in range(NSA_KV_HEADS):
        w2g = w2g.at[gi, :, gi].set(w2)
    w2g = w2g.reshape(NSA_KV_HEADS, CMP_HIDDEN, NSA_KV_WIDTH)
    pe = jnp.broadcast_to(pos.reshape(2, half, 1, HEAD_DIM), (2, half, NSA_KV_HEADS, HEAD_DIM))
    pe = pe.reshape(2, half * NSA_KV_WIDTH)
    return pe, w1g, w2g


def _compress(kc, vc, b, t, cmp_pos_k, cmp_pos_v, k_w1, k_w2, v_w1, v_w2):
    nrow = t // CMP_STRIDE
    zw = CMP_STRIDE * NSA_KV_WIDTH
    pek, wk1, wk2 = _compress_weights(cmp_pos_k, k_w1, k_w2)
    pev, wv1, wv2 = _compress_weights(cmp_pos_v, v_w1, v_w2)
    zspec = pl.BlockSpec((1, nrow, zw), lambda bi: (bi, 0, 0))
    ospec = pl.BlockSpec((1, nrow, LANES), lambda bi: (bi, 0, 0))
    c2 = lambda shape: pl.BlockSpec(shape, lambda bi: (0,) * len(shape))
    return pl.pallas_call(
        _compress_kernel,
        grid=(b,),
        in_specs=[zspec, zspec, c2((2, zw)), c2((2, zw)), c2(wk1.shape), c2(wv1.shape), c2(wk2.shape), c2(wv2.shape)],
        out_specs=[ospec, ospec],
        out_shape=[jax.ShapeDtypeStruct((b, nrow, LANES), F32)] * 2,
        compiler_params=_params("parallel"),
        name="nsa_compress",
    )(kc.reshape(b, nrow, zw), vc.reshape(b, nrow, zw), pek, pev, wk1, wv1, wk2, wv2)


KEY_TILE = 128


def _nsa_kernel(q_ref, kcmp_ref, vcmp_ref, ks_ref, vs_ref, kw_ref, vw_ref, gl_ref, ov_ref, exp_ref,
                o_ref, selx_scr, m_scr, l_scr, acc_scr):
    s = pl.program_id(1)
    qb = Q_BLOCK
    rows = NSA_HPG * qb
    t_len = ks_ref.shape[1]
    n_kt = t_len // KEY_TILE
    n_cmp = kcmp_ref.shape[1]

    tok64 = s * qb + lax.broadcasted_iota(jnp.int32, (qb, LANES), 0)
    lane64 = lax.broadcasted_iota(jnp.int32, (qb, LANES), 1)
    tok_r = s * qb + lax.broadcasted_iota(jnp.int32, (rows, n_cmp), 0) % qb
    cmp_i = lax.broadcasted_iota(jnp.int32, (rows, n_cmp), 1)
    cmask = (cmp_i * CMP_STRIDE + CMP_BLOCK - 1 <= tok_r) & (cmp_i < n_cmp - 1)
    sig = jax.nn.sigmoid(gl_ref[...])

    def flash(k_ref, v_ref, qg, kt_lo, kt_hi, mask_fn):
        m_scr[...] = jnp.full(m_scr.shape, NEG_INF, F32)
        l_scr[...] = jnp.zeros(l_scr.shape, F32)
        acc_scr[...] = jnp.zeros(acc_scr.shape, F32)

        def body(kt, carry):
            k0 = pl.multiple_of(kt * KEY_TILE, KEY_TILE)
            kb = k_ref[0, pl.ds(k0, KEY_TILE), :]
            vb = v_ref[0, pl.ds(k0, KEY_TILE), :]
            sc = _dot_nt(qg, kb)
            m64 = jnp.where(mask_fn(kt, k0 + lane64), 1.0, 0.0)
            msk = jnp.concatenate([m64] * NSA_HPG, axis=0) > 0.5
            sc = jnp.where(msk, sc, NEG_INF)
            m_old = m_scr[...]
            m_new = jnp.maximum(m_old, jnp.max(sc, axis=1, keepdims=True))
            p = jnp.where(msk, jnp.exp(sc - m_new), 0.0)
            corr = jnp.exp(m_old - m_new)
            l_scr[...] = corr * l_scr[...] + jnp.sum(p, axis=1, keepdims=True)
            acc_scr[...] = corr * acc_scr[...] + _dot(p.astype(BF16), vb)
            m_scr[...] = m_new
            return carry

        lax.fori_loop(kt_lo, kt_hi, body, 0)
        return acc_scr[...] / l_scr[...]

    for gi in range(NSA_KV_HEADS):
        qg = jnp.concatenate([q_ref[:, (gi * NSA_HPG + h) * LANES:(gi * NSA_HPG + h + 1) * LANES]
                              for h in range(NSA_HPG)], axis=0)

        sc = _dot_nt(qg.astype(F32), kcmp_ref[0], HI)
        scm = jnp.where(cmask, sc, NEG_INF)
        mx = jnp.max(scm, axis=1, keepdims=True)
        e = jnp.where(cmask, jnp.exp(scm - mx), 0.0)
        den = jnp.sum(e, axis=1, keepdims=True)
        p_c = e / jnp.where(den > 0.0, den, 1.0)
        o_cmp = _dot(p_c.astype(BF16), vcmp_ref[0].astype(BF16))

        pc_sum = p_c[0:qb]
        for h in range(1, NSA_HPG):
            pc_sum = pc_sum + p_c[h * qb:(h + 1) * qb]
        imp = _dot(pc_sum, ov_ref[...], HI)
        forced = (lane64 == 0) | (lane64 == s) | (lane64 == s - 1)
        imp = jnp.where(lane64 <= s, imp + jnp.where(forced, FORCE_BONUS, 0.0), NEG_INF)
        cnt = jnp.zeros((qb, LANES), F32)
        for i2 in range(t_len // SEL_BLOCK):
            vi = imp[:, i2:i2 + 1]
            tie = jnp.where(lane64 > i2, jnp.where(vi == imp, 1.0, 0.0), 0.0)
            cnt = cnt + jnp.where(vi > imp, 1.0, tie)
        n_sel = min(SEL_TOPK, t_len // SEL_BLOCK)
        sel = jnp.where((cnt < n_sel) & (lane64 <= s), 1.0, 0.0).astype(BF16)
        for kt in range(n_kt):
            selx_scr[kt] = _dot(sel, exp_ref[:, kt * KEY_TILE:(kt + 1) * KEY_TILE])

        def sel_mask(kt, kpos):
            return (selx_scr[kt] > 0.5) & (kpos <= tok64)

        def win_mask(kt, kpos):
            return (kpos <= tok64) & (kpos > tok64 - WINDOW)

        kt_hi = s // 2 + 1
        o_sel = flash(ks_ref, vs_ref, qg, 0, kt_hi, sel_mask)
        o_win = flash(kw_ref, vw_ref, qg, jnp.maximum(kt_hi - 1 - WINDOW // KEY_TILE, 0), kt_hi, win_mask)

        in_group = (lane64 >= gi * HEAD_DIM) & (lane64 < (gi + 1) * HEAD_DIM)
        for h in range(NSA_HPG):
            col = (gi * NSA_HPG + h) * N_BRANCH
            hr = slice(h * qb, (h + 1) * qb)
            o_h = (sig[:, col:col + 1] * o_cmp[hr] + sig[:, col + 1:col + 2] * o_sel[hr]
                   + sig[:, col + 2:col + 3] * o_win[hr])
            hq = gi * NSA_HPG + h
            o_ref[:, hq * LANES:(hq + 1) * LANES] = jnp.where(in_group, o_h, 0.0).astype(o_ref.dtype)


def _nsa_attention(q, k_cmp, v_cmp, ks, vs, kw, vw, gl, b, t):
    ns = t // Q_BLOCK
    n_sb = t // SEL_BLOCK
    n_cmp = t // CMP_STRIDE
    cmp_start = jnp.arange(n_cmp) * CMP_STRIDE
    sb = jnp.arange(LANES)
    ov = ((cmp_start[:, None] < (sb[None, :] + 1) * SEL_BLOCK)
          & (cmp_start[:, None] + CMP_BLOCK > sb[None, :] * SEL_BLOCK)
          & (sb[None, :] < n_sb) & (jnp.arange(n_cmp)[:, None] < n_cmp - 1)).astype(F32)
    expand = (jnp.arange(t)[None, :] // SEL_BLOCK == jnp.arange(LANES)[:, None]).astype(BF16)
    rows = NSA_HPG * Q_BLOCK
    qspec = pl.BlockSpec((Q_BLOCK, Q_PAD_COLS), lambda bi, si: (bi * ns + si, 0))
    seq = lambda: pl.BlockSpec((1, t, LANES), lambda bi, si: (bi, 0, 0))
    cmp_spec = lambda: pl.BlockSpec((1, n_cmp, LANES), lambda bi, si: (bi, 0, 0))
    k3 = lambda a_: a_.reshape(b, t, LANES)
    return pl.pallas_call(
        _nsa_kernel,
        grid=(b, ns),
        in_specs=[qspec, cmp_spec(), cmp_spec(), seq(), seq(), seq(), seq(),
                  pl.BlockSpec((Q_BLOCK, LANES), lambda bi, si: (bi * ns + si, 0)),
                  pl.BlockSpec((n_cmp, LANES), lambda bi, si: (0, 0)),
                  pl.BlockSpec((LANES, t), lambda bi, si: (0, 0))],
        out_specs=pl.BlockSpec((Q_BLOCK, Q_PAD_COLS), lambda bi, si: (bi * ns + si, 0)),
        out_shape=jax.ShapeDtypeStruct((b * t, Q_PAD_COLS), BF16),
        scratch_shapes=[pltpu.VMEM((t // KEY_TILE, Q_BLOCK, KEY_TILE), F32),
                        pltpu.VMEM((rows, LANES), F32),
                        pltpu.VMEM((rows, LANES), F32),
                        pltpu.VMEM((rows, LANES), F32)],
        compiler_params=_params("parallel", "arbitrary"),
        name="nsa_attention",
    )(q, k_cmp, v_cmp, k3(ks), k3(vs), k3(kw), k3(vw), gl, ov, expand)


FF_TILE = 256


def _ffn_kernel(x_ref, orw_ref, ons_ref, wor_ref, won_ref, n2_ref, w1_ref, w3_ref, w2_ref, nf_ref,
                out_ref, h_scr, u_scr, acc_scr):
    j = pl.program_id(1)

    @pl.when(j == 0)
    def _():
        h = x_ref[...] + _dot(orw_ref[...], wor_ref[...]) + _dot(ons_ref[...], won_ref[...])
        h_scr[...] = h
        ms = jnp.mean(h * h, axis=-1, keepdims=True)
        u_scr[...] = (h * lax.rsqrt(ms + RMS_EPS) * n2_ref[...]).astype(BF16)
        acc_scr[...] = jnp.zeros_like(acc_scr)

    u = u_scr[...]
    gate = _dot(u, w1_ref[...])
    up = _dot(u, w3_ref[...])
    act = (jax.nn.silu(gate) * up).astype(BF16)
    acc_scr[...] += _dot(act, w2_ref[...])

    @pl.when(j == pl.num_programs(1) - 1)
    def _():
        h = h_scr[...] + acc_scr[...]
        ms = jnp.mean(h * h, axis=-1, keepdims=True)
        out_ref[...] = h * lax.rsqrt(ms + RMS_EPS) * nf_ref[...]


def _out_ffn(x2, o_rwkv, o_nsa, w_out, norm2_w, ffn_w1, ffn_w3, ffn_w2, final_norm_w):
    n = x2.shape[0]
    tm = 512
    w_or = w_out[:RWKV_WIDTH].astype(BF16)
    w_n = w_out[RWKV_WIDTH:].reshape(NSA_KV_HEADS, NSA_HPG, HEAD_DIM, D_MODEL)
    w_on = jnp.zeros((NSA_KV_HEADS, NSA_HPG, NSA_KV_HEADS, HEAD_DIM, D_MODEL), F32)
    for gi in range(NSA_KV_HEADS):
        w_on = w_on.at[gi, :, gi].set(w_n[gi])
    w_on = w_on.reshape(Q_PAD_COLS, D_MODEL).astype(BF16)
    row = lambda width: pl.BlockSpec((tm, width), lambda i, j: (i, 0))
    const = lambda shape: pl.BlockSpec(shape, lambda i, j: (0, 0))
    return pl.pallas_call(
        _ffn_kernel,
        grid=(n // tm, D_FF // FF_TILE),
        in_specs=[row(D_MODEL), row(RWKV_WIDTH), row(Q_PAD_COLS),
                  const((RWKV_WIDTH, D_MODEL)), const((Q_PAD_COLS, D_MODEL)), const((1, D_MODEL)),
                  pl.BlockSpec((D_MODEL, FF_TILE), lambda i, j: (0, j)),
                  pl.BlockSpec((D_MODEL, FF_TILE), lambda i, j: (0, j)),
                  pl.BlockSpec((FF_TILE, D_MODEL), lambda i, j: (j, 0)),
                  const((1, D_MODEL))],
        out_specs=row(D_MODEL),
        out_shape=jax.ShapeDtypeStruct((n, D_MODEL), F32),
        scratch_shapes=[pltpu.VMEM((tm, D_MODEL), F32), pltpu.VMEM((tm, D_MODEL), BF16),
                        pltpu.VMEM((tm, D_MODEL), F32)],
        compiler_params=_params("parallel", "arbitrary"),
        name="out_ffn",
    )(x2, o_rwkv, o_nsa, w_or, w_on, norm2_w.reshape(1, D_MODEL), ffn_w1.astype(BF16), ffn_w3.astype(BF16),
      ffn_w2.astype(BF16), final_norm_w.reshape(1, D_MODEL))


def kernel(x, positions, norm1_w, w_in, mu_rwkv, w0, w_lora_up, a0, a_lora_up, g_lora_up, k_k, k_a, r_k, lnx_w, lnx_b, cmp_pos_k, cmp_pos_v, cmp_k_w1, cmp_k_w2, cmp_v_w1, cmp_v_w2, w_out, norm2_w, ffn_w1, ffn_w3, ffn_w2, final_norm_w):
    b, t, d = x.shape
    assert d == D_MODEL and norm1_w.shape[0] == 1, "single-layer block with d_model 1024"
    assert t % RWKV_TT == 0 and t % (2 * KEY_TILE) == 0 and t // SEL_BLOCK <= LANES
    x2 = x.reshape(b * t, d)
    cos_t, sin_t = _rope_tables(positions)
    yr, q, kc, vc, ks, vs, kw, vw, gl = _in_proj(x2, norm1_w[0], _pack_w_in(w_in[0]), cos_t, sin_t)
    o_rwkv = _rwkv(yr, b, t, mu_rwkv[0], w0[0], w_lora_up[0], a0[0], a_lora_up[0], g_lora_up[0],
                   k_k[0], k_a[0], r_k[0], lnx_w[0], lnx_b[0])
    k_cmp, v_cmp = _compress(kc, vc, b, t, cmp_pos_k[0], cmp_pos_v[0], cmp_k_w1[0], cmp_k_w2[0],
                             cmp_v_w1[0], cmp_v_w2[0])
    o_nsa = _nsa_attention(q, k_cmp, v_cmp, ks, vs, kw, vw, gl, b, t)
    out = _out_ffn(x2, o_rwkv, o_nsa, w_out[0], norm2_w[0], ffn_w1[0], ffn_w3[0], ffn_w2[0], final_norm_w)
    return out.reshape(b, t, d)
```

```python
import functools

import jax
import jax.numpy as jnp
from jax import lax
from jax.experimental import pallas as pl
from jax.experimental.pallas import tpu as pltpu

F32 = jnp.float32
BF16 = jnp.bfloat16
HI = lax.Precision.HIGHEST

D_MODEL = 1024
HEAD_DIM = 64
RWKV_WIDTH = 512
RWKV_HEADS = 8
D_DECAY_LORA = 64
D_AAA_LORA = 64
D_GATE_LORA = 128
RWKV_COLS = 3 * RWKV_WIDTH + D_DECAY_LORA + D_AAA_LORA + D_GATE_LORA
NSA_WIDTH = 512
NSA_Q_HEADS = 8
NSA_KV_HEADS = 2
NSA_HPG = 4
NSA_KV_WIDTH = 128
N_BRANCH = 3
CMP_BLOCK = 32
CMP_STRIDE = 16
CMP_HIDDEN = 256
SEL_BLOCK = 64
SEL_TOPK = 16
WINDOW = 512
Q_BLOCK = 64
ROPE_THETA = 10000.0
D_FF = 2816
RMS_EPS = 1e-6
LNX_EPS = 64e-5
FORCE_BONUS = 1e4
NEG_INF = -1e30

LANES = 128
CHUNK = 64
Q_PAD_COLS = NSA_Q_HEADS * LANES
PROJ_COLS = RWKV_COLS + Q_PAD_COLS + 6 * NSA_KV_WIDTH + LANES
VMEM_LIMIT = 56 * 1024 * 1024


def _dot(a, b, prec=None):
    return jnp.dot(a, b, preferred_element_type=F32, precision=prec)


def _dot_nt(a, b, prec=None):
    return lax.dot_general(a, b, (((1,), (1,)), ((), ())), preferred_element_type=F32, precision=prec)


def _dot_tn(a, b, prec=None):
    return lax.dot_general(a, b, (((0,), (0,)), ((), ())), preferred_element_type=F32, precision=prec)


def _split(a):
    hi = a.astype(BF16)
    return hi, (a - hi.astype(F32)).astype(BF16)


def _mm(a, b, mode, dot=_dot):
    if mode == "hi":
        return dot(a, b, HI)
    if mode == "b1":
        return dot(a.astype(BF16), b.astype(BF16))
    if mode == "xa":
        bh, bl = _split(b)
        ab = a.astype(BF16)
        return dot(ab, bh) + dot(ab, bl)
    if mode == "xb":
        ah, al = _split(a)
        bb = b.astype(BF16)
        return dot(ah, bb) + dot(al, bb)
    ah, al = _split(a)
    bh, bl = _split(b)
    return dot(ah, bh) + (dot(ah, bl) + dot(al, bh))


def _params(*sem):
    return pltpu.CompilerParams(dimension_semantics=sem, vmem_limit_bytes=VMEM_LIMIT)


def _rope_table_kernel(pos_ref, invf_ref, sign_ref, cos_ref, sin_ref):
    ang = pos_ref[...].astype(F32) * invf_ref[...]
    cos_ref[...] = jnp.cos(ang)
    sin_ref[...] = jnp.sin(ang) * sign_ref[...]


def _rope_tables(positions):
    n = positions.size
    tm = 512
    half = HEAD_DIM // 2
    inv_freq = ROPE_THETA ** (-jnp.arange(half, dtype=F32) / half)
    invf = jnp.tile(inv_freq, LANES // half).reshape(1, LANES)
    lane = jnp.arange(LANES)
    sign = jnp.where((lane % HEAD_DIM) < half, -1.0, 1.0).astype(F32).reshape(1, LANES)
    row = pl.BlockSpec((tm, LANES), lambda i: (i, 0))
    const = pl.BlockSpec((1, LANES), lambda i: (0, 0))
    return pl.pallas_call(
        _rope_table_kernel,
        grid=(n // tm,),
        in_specs=[pl.BlockSpec((tm, 1), lambda i: (i, 0)), const, const],
        out_specs=[row, row],
        out_shape=[jax.ShapeDtypeStruct((n, LANES), F32)] * 2,
        compiler_params=_params("parallel"),
        name="rope_tables",
    )(positions.reshape(n, 1), invf, sign)


def _proj_kernel(x_ref, n1_ref, w_ref, cos_ref, sin_ref,
                 yr_ref, q_ref, kc_ref, vc_ref, ks_ref, vs_ref, kw_ref, vw_ref, gl_ref):
    x = x_ref[...]
    ms = jnp.mean(x * x, axis=-1, keepdims=True)
    xn = (x * lax.rsqrt(ms + RMS_EPS) * n1_ref[...]).astype(BF16)
    cos = cos_ref[...]
    sin = sin_ref[...]
    lane = lax.broadcasted_iota(jnp.int32, cos.shape, 1)
    first_half = (lane % HEAD_DIM) < (HEAD_DIM // 2)

    def rope(v):
        rot = jnp.where(first_half, pltpu.roll(v, LANES - HEAD_DIM // 2, 1), pltpu.roll(v, HEAD_DIM // 2, 1))
        return v * cos + rot * sin

    def cols(c0, width):
        return _dot(xn, w_ref[:, c0:c0 + width])

    for j in range(RWKV_COLS // 256):
        yr_ref[:, j * 256:(j + 1) * 256] = cols(j * 256, 256)
    scale = HEAD_DIM ** -0.5
    for hq in range(NSA_Q_HEADS):
        q = rope(cols(RWKV_COLS + hq * LANES, LANES)) * scale
        q_ref[:, hq * LANES:(hq + 1) * LANES] = q.astype(BF16)
    c0 = RWKV_COLS + Q_PAD_COLS
    kc_ref[...] = rope(cols(c0, LANES))
    vc_ref[...] = cols(c0 + LANES, LANES)
    ks_ref[...] = rope(cols(c0 + 2 * LANES, LANES)).astype(BF16)
    vs_ref[...] = cols(c0 + 3 * LANES, LANES).astype(BF16)
    kw_ref[...] = rope(cols(c0 + 4 * LANES, LANES)).astype(BF16)
    vw_ref[...] = cols(c0 + 5 * LANES, LANES).astype(BF16)
    gl_ref[...] = cols(c0 + 6 * LANES, LANES)


def _pack_w_in(w_in):
    w_r = w_in[:, :RWKV_COLS]
    w_q = w_in[:, RWKV_COLS:RWKV_COLS + NSA_WIDTH].reshape(D_MODEL, NSA_KV_HEADS, NSA_HPG, HEAD_DIM)
    zero = jnp.zeros_like(w_q)
    w_q = jnp.stack([jnp.where(jnp.arange(NSA_KV_HEADS)[None, :, None, None] == gg, w_q, zero)
                     for gg in range(NSA_KV_HEADS)], axis=3)
    w_q = w_q.reshape(D_MODEL, Q_PAD_COLS)
    c0 = RWKV_COLS + NSA_WIDTH
    w_kv = w_in[:, c0:c0 + 6 * NSA_KV_WIDTH]
    w_g = w_in[:, c0 + 6 * NSA_KV_WIDTH:]
    w_g = jnp.pad(w_g, ((0, 0), (0, LANES - w_g.shape[1])))
    return jnp.concatenate([w_r, w_q, w_kv, w_g], axis=1).astype(BF16)


def _in_proj(x2, norm1_w, w_packed, cos_t, sin_t):
    n = x2.shape[0]
    tm = 256
    row = lambda width: pl.BlockSpec((tm, width), lambda i: (i, 0))
    const = lambda shape: pl.BlockSpec(shape, lambda i: (0, 0))
    out_shape = [
        jax.ShapeDtypeStruct((n, RWKV_COLS), F32),
        jax.ShapeDtypeStruct((n, Q_PAD_COLS), BF16),
        jax.ShapeDtypeStruct((n, LANES), F32),
        jax.ShapeDtypeStruct((n, LANES), F32),
        jax.ShapeDtypeStruct((n, LANES), BF16),
        jax.ShapeDtypeStruct((n, LANES), BF16),
        jax.ShapeDtypeStruct((n, LANES), BF16),
        jax.ShapeDtypeStruct((n, LANES), BF16),
        jax.ShapeDtypeStruct((n, LANES), F32),
    ]
    out_specs = [row(RWKV_COLS), row(Q_PAD_COLS)] + [row(LANES)] * 7
    return pl.pallas_call(
        _proj_kernel,
        grid=(n // tm,),
        in_specs=[row(D_MODEL), const((1, D_MODEL)), const((D_MODEL, PROJ_COLS)), row(LANES), row(LANES)],
        out_specs=out_specs,
        out_shape=out_shape,
        compiler_params=_params("parallel"),
        name="in_proj",
    )(x2, norm1_w.reshape(1, D_MODEL), w_packed, cos_t, sin_t)


RWKV_TT = 256
P_LORA = "b3"
P_HS = "xb"
P_CUM = "xa"
P_GM = "b1"
P_INV = "b1"
P_SEQ = "b1"


def _rwkv_kernel(y_ref, mu_ref, w0_ref, wl_ref, a0_ref, gup_ref, kk_ref, ka_ref, rk_ref,
                 lw_ref, lb_ref, hs_ref, tri_ref, o_ref, prev_ref, s_ref, o_scr):
    i = pl.program_id(1)

    @pl.when(i == 0)
    def _():
        prev_ref[...] = jnp.zeros_like(prev_ref)
        s_ref[...] = jnp.zeros_like(s_ref)

    tt = y_ref.shape[0]
    y = y_ref[...]
    row = lax.broadcasted_iota(jnp.int32, (tt, 1), 0)
    y_prev = jnp.where(row == 0, prev_ref[...], pltpu.roll(y, 1, 0))
    prev_ref[...] = y[tt - 1:tt, :]
    ys = y + (y_prev - y) * mu_ref[...]

    w_ = RWKV_WIDTH
    r = ys[:, 0:w_]
    k = ys[:, w_:2 * w_]
    v = ys[:, 2 * w_:3 * w_]
    z = ys[:, 3 * w_:3 * w_ + LANES]
    gd = ys[:, 3 * w_ + LANES:3 * w_ + 2 * LANES]
    lane = lax.broadcasted_iota(jnp.int32, z.shape, 1)
    zt = jnp.where(lane < D_DECAY_LORA, jnp.tanh(z), z)
    wa = _mm(zt, wl_ref[...], P_LORA)
    w_raw = w0_ref[...] + wa[:, :w_]
    a = jax.nn.sigmoid(a0_ref[...] + wa[:, w_:])
    logw = -jnp.exp(-jax.nn.softplus(-w_raw) - 0.5)
    g = _mm(jax.nn.sigmoid(gd), gup_ref[...], P_LORA)
    hs = hs_ref[...]
    kk = k * kk_ref[...]
    kk = kk * lax.rsqrt(jnp.maximum(_mm(kk * kk, hs, P_HS), 1e-12))
    k2 = k * (1.0 + (a - 1.0) * ka_ref[...])
    alpha = -kk
    beta = kk * a

    cw = _mm(tri_ref[...], logw, P_CUM)
    e_in = jnp.exp(cw)
    e_ex = jnp.exp(cw - logw)
    e_neg = jnp.exp(-cw)
    a_t = alpha * e_ex
    r_t = r * e_in
    b_t = beta * e_neg
    k_t = k2 * e_neg

    m0 = lax.broadcasted_iota(jnp.int32, (CHUNK, LANES), 1) < HEAD_DIM

    def bd(xc):
        return jnp.concatenate([jnp.where(m0, xc, 0.0), jnp.where(m0, 0.0, xc)], axis=0)

    n2 = 2 * CHUNK
    ri = lax.broadcasted_iota(jnp.int32, (n2, n2), 0) % CHUNK
    ci = lax.broadcasted_iota(jnp.int32, (n2, n2), 1) % CHUNK
    strict = ri > ci
    incl = ri >= ci
    eye = jnp.where(lax.broadcasted_iota(jnp.int32, (n2, n2), 0) == lax.broadcasted_iota(jnp.int32, (n2, n2), 1),
                    1.0, 0.0)

    for c in range(tt // CHUNK):
        rs = slice(c * CHUNK, (c + 1) * CHUNK)
        cw_last = cw[c * CHUNK + CHUNK - 1:c * CHUNK + CHUNK, :]
        e_tot = jnp.exp(cw_last - cw[rs, :])
        w_c = jnp.exp(cw_last)
        for p in range(RWKV_WIDTH // LANES):
            ls = slice(p * LANES, (p + 1) * LANES)
            a_bd = bd(a_t[rs, ls])
            r_bd = bd(r_t[rs, ls])
            b_bd = bd(b_t[rs, ls])
            k_bd = bd(k_t[rs, ls])
            v_bd = bd(v[rs, ls])
            bh_bd = bd(beta[rs, ls] * e_tot[:, ls])
            kh_bd = bd(k2[rs, ls] * e_tot[:, ls])
            gm = _mm(jnp.concatenate([a_bd, r_bd], axis=0), jnp.concatenate([b_bd, k_bd], axis=0), P_GM, _dot_nt)
            l_ab = jnp.where(strict, gm[0:n2, 0:n2], 0.0)
            l_ak = jnp.where(strict, gm[0:n2, n2:], 0.0)
            m_rb = jnp.where(incl, gm[n2:, 0:n2], 0.0)
            m_rk = jnp.where(incl, gm[n2:, n2:], 0.0)
            tm_ = eye + l_ab
            pw = l_ab
            for _ in range(5):
                pw = _mm(pw, pw, P_INV)
                tm_ = tm_ + _mm(tm_, pw, P_INV)
            a_hat = _mm(tm_, a_bd, P_SEQ)
            u0 = _mm(tm_, _mm(l_ak, v_bd, P_SEQ), P_SEQ)
            o0 = _mm(m_rk, v_bd, P_SEQ)
            s_old = s_ref[p]
            xs = _mm(jnp.concatenate([a_hat, r_bd], axis=0), s_old, P_SEQ, _dot_nt)
            u = xs[0:n2] + u0
            o_bd = xs[n2:] + _mm(m_rb, u, P_SEQ) + o0
            o_scr[rs, ls] = o_bd[0:CHUNK] + o_bd[CHUNK:]
            s_ref[p] = s_old * w_c[:, ls] + _mm(jnp.concatenate([u, v_bd], axis=0),
                                                jnp.concatenate([bh_bd, kh_bd], axis=0), P_SEQ, _dot_tn)

    o = o_scr[...]
    inv_n = 1.0 / HEAD_DIM
    mean = _mm(o, hs, P_HS) * inv_n
    d = o - mean
    var = _mm(d * d, hs, P_HS) * inv_n
    on = d * lax.rsqrt(var + LNX_EPS) * lw_ref[...] + lb_ref[...]
    bonus = _mm(r * k2 * rk_ref[...], hs, P_HS) * v
    o_ref[...] = ((on + bonus) * g).astype(o_ref.dtype)


def _rwkv(yr, b, t, mu, w0, w_lora_up, a0, a_lora_up, g_lora_up, k_k, k_a, r_k, lnx_w, lnx_b):
    tt = RWKV_TT
    nt = t // tt
    w_ = RWKV_WIDTH
    wl = jnp.zeros((LANES, 2 * w_), F32)
    wl = wl.at[:D_DECAY_LORA, :w_].set(w_lora_up).at[D_DECAY_LORA:, w_:].set(a_lora_up)
    head = jnp.arange(w_) // HEAD_DIM
    hs = (head[:, None] == head[None, :]).astype(F32)
    ti = jnp.arange(tt)
    tri = ((ti[:, None] // CHUNK == ti[None, :] // CHUNK) & (ti[:, None] >= ti[None, :])).astype(F32)
    vec = lambda a_, width: a_.reshape(1, width)
    const = lambda shape: pl.BlockSpec(shape, lambda bi, i: (0, 0))
    return pl.pallas_call(
        _rwkv_kernel,
        grid=(b, nt),
        in_specs=[pl.BlockSpec((tt, RWKV_COLS), lambda bi, i: (bi * nt + i, 0)),
                  const((1, RWKV_COLS)), const((1, w_)), const((LANES, 2 * w_)), const((1, w_)),
                  const((D_GATE_LORA, w_)), const((1, w_)), const((1, w_)), const((1, w_)),
                  const((1, w_)), const((1, w_)), const((w_, w_)), const((tt, tt))],
        out_specs=pl.BlockSpec((tt, w_), lambda bi, i: (bi * nt + i, 0)),
        out_shape=jax.ShapeDtypeStruct((b * t, w_), BF16),
        scratch_shapes=[pltpu.VMEM((1, RWKV_COLS), F32),
                        pltpu.VMEM((w_ // LANES, LANES, LANES), F32),
                        pltpu.VMEM((tt, w_), F32)],
        compiler_params=_params("parallel", "arbitrary"),
        name="rwkv7",
    )(yr, vec(mu, RWKV_COLS), vec(w0, w_), wl, vec(a0, w_), g_lora_up, vec(k_k, w_), vec(k_a, w_),
      vec(r_k, w_), vec(lnx_w, w_), vec(lnx_b, w_), hs, tri)


def _compress_kernel(zk_ref, zv_ref, pek_ref, pev_ref, wk1_ref, wv1_ref, wk2_ref, wv2_ref, kc_ref, vc_ref):
    def one(z_ref, pe_ref, w1_ref, w2_ref, out_ref):
        z = z_ref[0]
        nrow = z.shape[0]
        za = z + pe_ref[0:1, :]
        zb = z + pe_ref[1:2, :]
        acc = jnp.zeros((nrow, LANES), F32)
        for gi in range(NSA_KV_HEADS):
            ha = _dot(za, w1_ref[gi, 0], HI)
            hb = _dot(zb, w1_ref[gi, 1], HI)
            hid = ha + pltpu.roll(hb, nrow - 1, 0)
            acc = acc + _dot(jax.nn.gelu(hid), w2_ref[gi], HI)
        out_ref[0] = acc

    one(zk_ref, pek_ref, wk1_ref, wk2_ref, kc_ref)
    one(zv_ref, pev_ref, wv1_ref, wv2_ref, vc_ref)


def _compress_weights(pos, w1, w2):
    half = CMP_BLOCK // 2
    w1r = w1.reshape(2, half, HEAD_DIM, CMP_HIDDEN)
    w1g = jnp.zeros((NSA_KV_HEADS, 2, half, NSA_KV_HEADS, HEAD_DIM, CMP_HIDDEN), F32)
    for gi in range(NSA_KV_HEADS):
        w1g = w1g.at[gi, :, :, gi].set(w1r)
    w1g = w1g.reshape(NSA_KV_HEADS, 2, half * NSA_KV_WIDTH, CMP_HIDDEN)
    w2g = jnp.zeros((NSA_KV_HEADS, CMP_HIDDEN, NSA_KV_HEADS, HEAD_DIM), F32)
    for gi in range(NSA_KV_HEADS):
        w2g = w2g.at[gi, :, gi].set(w2)
    w2g = w2g.reshape(NSA_KV_HEADS, CMP_HIDDEN, NSA_KV_WIDTH)
    pe = jnp.broadcast_to(pos.reshape(2, half, 1, HEAD_DIM), (2, half, NSA_KV_HEADS, HEAD_DIM))
    pe = pe.reshape(2, half * NSA_KV_WIDTH)
    return pe, w1g, w2g


def _compress(kc, vc, b, t, cmp_pos_k, cmp_pos_v, k_w1, k_w2, v_w1, v_w2):
    nrow = t // CMP_STRIDE
    zw = CMP_STRIDE * NSA_KV_WIDTH
    pek, wk1, wk2 = _compress_weights(cmp_pos_k, k_w1, k_w2)
    pev, wv1, wv2 = _compress_weights(cmp_pos_v, v_w1, v_w2)
    zspec = pl.BlockSpec((1, nrow, zw), lambda bi: (bi, 0, 0))
    ospec = pl.BlockSpec((1, nrow, LANES), lambda bi: (bi, 0, 0))
    c2 = lambda shape: pl.BlockSpec(shape, lambda bi: (0,) * len(shape))
    return pl.pallas_call(
        _compress_kernel,
        grid=(b,),
        in_specs=[zspec, zspec, c2((2, zw)), c2((2, zw)), c2(wk1.shape), c2(wv1.shape), c2(wk2.shape), c2(wv2.shape)],
        out_specs=[ospec, ospec],
        out_shape=[jax.ShapeDtypeStruct((b, nrow, LANES), F32)] * 2,
        compiler_params=_params("parallel"),
        name="nsa_compress",
    )(kc.reshape(b, nrow, zw), vc.reshape(b, nrow, zw), pek, pev, wk1, wv1, wk2, wv2)


SEL_TILE = 256
SEL_UNROLL = 2
WIN_TILE = 128


def _nsa_kernel(q_ref, kcmp_ref, vcmp_ref, ks_ref, vs_ref, kw_ref, vw_ref, gl_ref, ovt_ref, exp_ref,
                o_ref, m_scr, l_scr, acc_scr, sc_scr):
    s = pl.program_id(1)
    qb = Q_BLOCK
    n_head = NSA_Q_HEADS
    rows = n_head * qb
    n_cmp = kcmp_ref.shape[1]
    n_sb = ovt_ref.shape[0]

    tok64 = s * qb + lax.broadcasted_iota(jnp.int32, (qb, LANES), 0)
    lane64 = lax.broadcasted_iota(jnp.int32, (qb, LANES), 1)
    sig = jax.nn.sigmoid(gl_ref[...])

    qq = jnp.concatenate([q_ref[:, hq * LANES:(hq + 1) * LANES] for hq in range(n_head)], axis=0)

    def per_head(x64):
        return jnp.concatenate([x64] * n_head, axis=0)

    def per_group_head(x128):
        return jnp.concatenate([x128[0:qb]] * NSA_HPG + [x128[qb:2 * qb]] * NSA_HPG, axis=0)

    tok_c = s * qb + lax.broadcasted_iota(jnp.int32, (qb, n_cmp), 0)
    cmp_i = lax.broadcasted_iota(jnp.int32, (qb, n_cmp), 1)
    cbias = per_head(jnp.where((cmp_i * CMP_STRIDE + CMP_BLOCK - 1 <= tok_c) & (cmp_i < n_cmp - 1), 0.0, NEG_INF))
    sc = _mm(qq, kcmp_ref[0], "xa", _dot_nt) + cbias
    mx = jnp.max(sc, axis=1, keepdims=True)
    e = jnp.where(cbias == 0.0, jnp.exp(sc - mx), 0.0)
    den = jnp.sum(e, axis=1, keepdims=True)
    p_c = e / jnp.where(den > 0.0, den, 1.0)
    o_cmp = _dot(p_c.astype(BF16), vcmp_ref[0].astype(BF16))

    pc_sums = []
    for gi in range(NSA_KV_HEADS):
        r0 = gi * NSA_HPG * qb
        acc = p_c[r0:r0 + qb]
        for h in range(1, NSA_HPG):
            acc = acc + p_c[r0 + h * qb:r0 + (h + 1) * qb]
        pc_sums.append(acc)
    pcs = jnp.concatenate(pc_sums, axis=0)
    imp = _dot_nt(ovt_ref[...], pcs, HI)
    blk = lax.broadcasted_iota(jnp.int32, (n_sb, LANES), 0)
    forced = (blk == 0) | (blk == s) | (blk == s - 1)
    imp = jnp.where(blk <= s, imp + jnp.where(forced, FORCE_BONUS, 0.0), NEG_INF)
    cnt = jnp.zeros((n_sb, LANES), F32)
    for i2 in range(n_sb):
        vi = imp[i2:i2 + 1, :]
        tie = jnp.where(blk > i2, jnp.where(vi == imp, 1.0, 0.0), 0.0)
        cnt = cnt + jnp.where(vi > imp, 1.0, tie)
    n_sel = min(SEL_TOPK, n_sb)
    sel_t = jnp.where((cnt < n_sel) & (blk <= s), 1.0, 0.0)
    sel = sel_t.T.astype(BF16)

    def attend(k_ref, v_ref, kt_w, n_steps, unroll, tile_bias):
        def loop(body, init):
            def step(it, c):
                for u in range(unroll):
                    c = body(it * unroll + u, c)
                return c
            if isinstance(n_steps, int):
                c = init
                for it in range(n_steps):
                    c = step(it, c)
                return c
            return lax.fori_loop(0, n_steps, step, init)

        def pass1(j, mx_):
            kt, bias = tile_bias(j)
            k0 = pl.multiple_of(kt * kt_w, kt_w)
            sc_ = _dot_nt(qq, k_ref[0, pl.ds(k0, kt_w), :]) + bias
            sc_scr[j, :, 0:kt_w] = sc_
            for c0 in range(0, kt_w, LANES):
                mx_ = jnp.maximum(mx_, sc_[:, c0:c0 + LANES])
            return mx_

        mx_ = loop(pass1, jnp.full((rows, LANES), NEG_INF, F32))
        m_scr[...] = jnp.broadcast_to(jnp.max(mx_, axis=1, keepdims=True), m_scr.shape)
        l_scr[...] = jnp.zeros(l_scr.shape, F32)
        acc_scr[...] = jnp.zeros(acc_scr.shape, F32)

        def pass2(j, c):
            kt, _ = tile_bias(j, False)
            k0 = pl.multiple_of(kt * kt_w, kt_w)
            m_b = jnp.concatenate([m_scr[...]] * (kt_w // LANES), axis=1)
            p = jnp.exp(sc_scr[j, :, 0:kt_w] - m_b)
            l_new = l_scr[...]
            for c0 in range(0, kt_w, LANES):
                l_new = l_new + p[:, c0:c0 + LANES]
            l_scr[...] = l_new
            acc_scr[...] += _dot(p.astype(BF16), v_ref[0, pl.ds(k0, kt_w), :])
            return c

        loop(pass2, 0)
        return acc_scr[...] / jnp.sum(l_scr[...], axis=1, keepdims=True)

    tok_s = s * qb + lax.broadcasted_iota(jnp.int32, (2 * qb, SEL_TILE), 0) % qb
    lane_s = lax.broadcasted_iota(jnp.int32, (2 * qb, SEL_TILE), 1)

    def sel_bias(j, want_bias=True):
        if not want_bias:
            return j, None
        chosen = _dot(sel, exp_ref[j])
        ok = (chosen > 0.5) & (j * SEL_TILE + lane_s <= tok_s)
        return j, per_group_head(jnp.where(ok, 0.0, NEG_INF))

    o_sel = attend(ks_ref, vs_ref, SEL_TILE, (s * qb) // (SEL_TILE * SEL_UNROLL) + 1, SEL_UNROLL, sel_bias)

    n_win = WINDOW // WIN_TILE + 1
    first = (s * qb) // WIN_TILE + 1 - n_win

    def win_bias(j, want_bias=True):
        kt = jnp.maximum(first + j, 0)
        if not want_bias:
            return kt, None
        kp = kt * WIN_TILE + lane64 + jnp.where(first + j >= 0, 0, 1 << 24)
        return kt, per_head(jnp.where((kp <= tok64) & (kp > tok64 - WINDOW), 0.0, NEG_INF))

    o_win = attend(kw_ref, vw_ref, WIN_TILE, 1, n_win, win_bias)

    for hq in range(n_head):
        gi = hq // NSA_HPG
        hr = slice(hq * qb, (hq + 1) * qb)
        col = hq * N_BRANCH
        o_h = (sig[:, col:col + 1] * o_cmp[hr] + sig[:, col + 1:col + 2] * o_sel[hr]
               + sig[:, col + 2:col + 3] * o_win[hr])
        in_group = (lane64 >= gi * HEAD_DIM) & (lane64 < (gi + 1) * HEAD_DIM)
        o_ref[:, hq * LANES:(hq + 1) * LANES] = jnp.where(in_group, o_h, 0.0).astype(o_ref.dtype)


def _nsa_attention(q, k_cmp, v_cmp, ks, vs, kw, vw, gl, b, t):
    ns = t // Q_BLOCK
    n_sb = t // SEL_BLOCK
    n_cmp = t // CMP_STRIDE
    n_kt = t // SEL_TILE
    cmp_start = jnp.arange(n_cmp) * CMP_STRIDE
    sb = jnp.arange(n_sb)
    ovt = ((cmp_start[None, :] < (sb[:, None] + 1) * SEL_BLOCK)
           & (cmp_start[None, :] + CMP_BLOCK > sb[:, None] * SEL_BLOCK)
           & (jnp.arange(n_cmp)[None, :] < n_cmp - 1)).astype(F32)
    key_blk = (jnp.arange(n_kt)[:, None] * SEL_TILE + jnp.arange(SEL_TILE)[None, :]) // SEL_BLOCK
    expand = (key_blk[:, None, :] == sb[None, :, None]).astype(BF16)
    rows = NSA_Q_HEADS * Q_BLOCK
    qspec = pl.BlockSpec((Q_BLOCK, Q_PAD_COLS), lambda bi, si: (bi * ns + si, 0))
    seq = lambda: pl.BlockSpec((1, t, LANES), lambda bi, si: (bi, 0, 0))
    cmp_spec = lambda: pl.BlockSpec((1, n_cmp, LANES), lambda bi, si: (bi, 0, 0))
    k3 = lambda a_: a_.reshape(b, t, LANES)
    return pl.pallas_call(
        _nsa_kernel,
        grid=(b, ns),
        in_specs=[qspec, cmp_spec(), cmp_spec(), seq(), seq(), seq(), seq(),
                  pl.BlockSpec((Q_BLOCK, LANES), lambda bi, si: (bi * ns + si, 0)),
                  pl.BlockSpec((n_sb, n_cmp), lambda bi, si: (0, 0)),
                  pl.BlockSpec((n_kt, n_sb, SEL_TILE), lambda bi, si: (0, 0, 0))],
        out_specs=pl.BlockSpec((Q_BLOCK, Q_PAD_COLS), lambda bi, si: (bi * ns + si, 0)),
        out_shape=jax.ShapeDtypeStruct((b * t, Q_PAD_COLS), BF16),
        scratch_shapes=[pltpu.VMEM((rows, LANES), F32),
                        pltpu.VMEM((rows, LANES), F32),
                        pltpu.VMEM((rows, LANES), F32),
                        pltpu.VMEM((n_kt, rows, SEL_TILE), F32)],
        compiler_params=_params("parallel", "arbitrary"),
        name="nsa_attention",
    )(q, k_cmp, v_cmp, k3(ks), k3(vs), k3(kw), k3(vw), gl, ovt, expand)


FF_TILE = 256


def _ffn_kernel(x_ref, orw_ref, ons_ref, wor_ref, won_ref, n2_ref, w1_ref, w3_ref, w2_ref, nf_ref,
                out_ref, h_scr, u_scr, acc_scr):
    j = pl.program_id(1)

    @pl.when(j == 0)
    def _():
        h = x_ref[...] + _dot(orw_ref[...], wor_ref[...]) + _dot(ons_ref[...], won_ref[...])
        h_scr[...] = h
        ms = jnp.mean(h * h, axis=-1, keepdims=True)
        u_scr[...] = (h * lax.rsqrt(ms + RMS_EPS) * n2_ref[...]).astype(BF16)
        acc_scr[...] = jnp.zeros_like(acc_scr)

    u = u_scr[...]
    gate = _dot(u, w1_ref[...])
    up = _dot(u, w3_ref[...])
    act = (jax.nn.silu(gate) * up).astype(BF16)
    acc_scr[...] += _dot(act, w2_ref[...])

    @pl.when(j == pl.num_programs(1) - 1)
    def _():
        h = h_scr[...] + acc_scr[...]
        ms = jnp.mean(h * h, axis=-1, keepdims=True)
        out_ref[...] = h * lax.rsqrt(ms + RMS_EPS) * nf_ref[...]


def _out_ffn(x2, o_rwkv, o_nsa, w_out, norm2_w, ffn_w1, ffn_w3, ffn_w2, final_norm_w):
    n = x2.shape[0]
    tm = 512
    w_or = w_out[:RWKV_WIDTH].astype(BF16)
    w_n = w_out[RWKV_WIDTH:].reshape(NSA_KV_HEADS, NSA_HPG, HEAD_DIM, D_MODEL)
    w_on = jnp.zeros((NSA_KV_HEADS, NSA_HPG, NSA_KV_HEADS, HEAD_DIM, D_MODEL), F32)
    for gi in range(NSA_KV_HEADS):
        w_on = w_on.at[gi, :, gi].set(w_n[gi])
    w_on = w_on.reshape(Q_PAD_COLS, D_MODEL).astype(BF16)
    row = lambda width: pl.BlockSpec((tm, width), lambda i, j: (i, 0))
    const = lambda shape: pl.BlockSpec(shape, lambda i, j: (0, 0))
    return pl.pallas_call(
        _ffn_kernel,
        grid=(n // tm, D_FF // FF_TILE),
        in_specs=[row(D_MODEL), row(RWKV_WIDTH), row(Q_PAD_COLS),
                  const((RWKV_WIDTH, D_MODEL)), const((Q_PAD_COLS, D_MODEL)), const((1, D_MODEL)),
                  pl.BlockSpec((D_MODEL, FF_TILE), lambda i, j: (0, j)),
                  pl.BlockSpec((D_MODEL, FF_TILE), lambda i, j: (0, j)),
                  pl.BlockSpec((FF_TILE, D_MODEL), lambda i, j: (j, 0)),
                  const((1, D_MODEL))],
        out_specs=row(D_MODEL),
        out_shape=jax.ShapeDtypeStruct((n, D_MODEL), F32),
        scratch_shapes=[pltpu.VMEM((tm, D_MODEL), F32), pltpu.VMEM((tm, D_MODEL), BF16),
                        pltpu.VMEM((tm, D_MODEL), F32)],
        compiler_params=_params("parallel", "arbitrary"),
        name="out_ffn",
    )(x2, o_rwkv, o_nsa, w_or, w_on, norm2_w.reshape(1, D_MODEL), ffn_w1.astype(BF16), ffn_w3.astype(BF16),
      ffn_w2.astype(BF16), final_norm_w.reshape(1, D_MODEL))


def kernel(x, positions, norm1_w, w_in, mu_rwkv, w0, w_lora_up, a0, a_lora_up, g_lora_up, k_k, k_a, r_k, lnx_w, lnx_b, cmp_pos_k, cmp_pos_v, cmp_k_w1, cmp_k_w2, cmp_v_w1, cmp_v_w2, w_out, norm2_w, ffn_w1, ffn_w3, ffn_w2, final_norm_w):
    b, t, d = x.shape
    assert d == D_MODEL and norm1_w.shape[0] == 1, "single-layer block with d_model 1024"
    assert t % RWKV_TT == 0 and t % (SEL_UNROLL * SEL_TILE) == 0 and t // SEL_BLOCK <= LANES
    assert t // SEL_TILE >= WINDOW // WIN_TILE + 1, "score scratch is shared by the selected and window branches"
    x2 = x.reshape(b * t, d)
    cos_t, sin_t = _rope_tables(positions)
    yr, q, kc, vc, ks, vs, kw, vw, gl = _in_proj(x2, norm1_w[0], _pack_w_in(w_in[0]), cos_t, sin_t)
    o_rwkv = _rwkv(yr, b, t, mu_rwkv[0], w0[0], w_lora_up[0], a0[0], a_lora_up[0], g_lora_up[0],
                   k_k[0], k_a[0], r_k[0], lnx_w[0], lnx_b[0])
    k_cmp, v_cmp = _compress(kc, vc, b, t, cmp_pos_k[0], cmp_pos_v[0], cmp_k_w1[0], cmp_k_w2[0],
                             cmp_v_w1[0], cmp_v_w2[0])
    o_nsa = _nsa_attention(q, k_cmp, v_cmp, ks, vs, kw, vw, gl, b, t)
    out = _out_ffn(x2, o_rwkv, o_nsa, w_out[0], norm2_w[0], ffn_w1[0], ffn_w3[0], ffn_w2[0], final_norm_w)
    return out.reshape(b, t, d)
```

```python
import functools

import jax
import jax.numpy as jnp
from jax import lax
from jax.experimental import pallas as pl
from jax.experimental.pallas import tpu as pltpu

F32 = jnp.float32
BF16 = jnp.bfloat16
HI = lax.Precision.HIGHEST

D_MODEL = 1024
HEAD_DIM = 64
RWKV_WIDTH = 512
RWKV_HEADS = 8
D_DECAY_LORA = 64
D_AAA_LORA = 64
D_GATE_LORA = 128
RWKV_COLS = 3 * RWKV_WIDTH + D_DECAY_LORA + D_AAA_LORA + D_GATE_LORA
NSA_WIDTH = 512
NSA_Q_HEADS = 8
NSA_KV_HEADS = 2
NSA_HPG = 4
NSA_KV_WIDTH = 128
N_BRANCH = 3
CMP_BLOCK = 32
CMP_STRIDE = 16
CMP_HIDDEN = 256
SEL_BLOCK = 64
SEL_TOPK = 16
WINDOW = 512
Q_BLOCK = 64
ROPE_THETA = 10000.0
D_FF = 2816
RMS_EPS = 1e-6
LNX_EPS = 64e-5
FORCE_BONUS = 1e4
NEG_INF = -1e30

LANES = 128
CHUNK = 64
Q_PAD_COLS = NSA_Q_HEADS * LANES
PROJ_COLS = RWKV_COLS + Q_PAD_COLS + 6 * NSA_KV_WIDTH + LANES
VMEM_LIMIT = 56 * 1024 * 1024


def _dot(a, b, prec=None):
    return jnp.dot(a, b, preferred_element_type=F32, precision=prec)


def _dot_nt(a, b, prec=None):
    return lax.dot_general(a, b, (((1,), (1,)), ((), ())), preferred_element_type=F32, precision=prec)


def _dot_tn(a, b, prec=None):
    return lax.dot_general(a, b, (((0,), (0,)), ((), ())), preferred_element_type=F32, precision=prec)


def _split(a):
    hi = a.astype(BF16)
    return hi, (a - hi.astype(F32)).astype(BF16)


def _mm(a, b, mode, dot=_dot):
    if mode == "hi":
        return dot(a, b, HI)
    if mode == "b1":
        return dot(a.astype(BF16), b.astype(BF16))
    if mode == "xa":
        bh, bl = _split(b)
        ab = a.astype(BF16)
        return dot(ab, bh) + dot(ab, bl)
    if mode == "xb":
        ah, al = _split(a)
        bb = b.astype(BF16)
        return dot(ah, bb) + dot(al, bb)
    ah, al = _split(a)
    bh, bl = _split(b)
    return dot(ah, bh) + (dot(ah, bl) + dot(al, bh))


def _params(*sem):
    return pltpu.CompilerParams(dimension_semantics=sem, vmem_limit_bytes=VMEM_LIMIT)


def _rope_table_kernel(pos_ref, invf_ref, sign_ref, cos_ref, sin_ref):
    ang = pos_ref[...].astype(F32) * invf_ref[...]
    cos_ref[...] = jnp.cos(ang)
    sin_ref[...] = jnp.sin(ang) * sign_ref[...]


def _rope_tables(positions):
    n = positions.size
    tm = 512
    half = HEAD_DIM // 2
    inv_freq = ROPE_THETA ** (-jnp.arange(half, dtype=F32) / half)
    invf = jnp.tile(inv_freq, LANES // half).reshape(1, LANES)
    lane = jnp.arange(LANES)
    sign = jnp.where((lane % HEAD_DIM) < half, -1.0, 1.0).astype(F32).reshape(1, LANES)
    row = pl.BlockSpec((tm, LANES), lambda i: (i, 0))
    const = pl.BlockSpec((1, LANES), lambda i: (0, 0))
    return pl.pallas_call(
        _rope_table_kernel,
        grid=(n // tm,),
        in_specs=[pl.BlockSpec((tm, 1), lambda i: (i, 0)), const, const],
        out_specs=[row, row],
        out_shape=[jax.ShapeDtypeStruct((n, LANES), F32)] * 2,
        compiler_params=_params("parallel"),
        name="rope_tables",
    )(positions.reshape(n, 1), invf, sign)


def _proj_kernel(x_ref, n1_ref, w_ref, cos_ref, sin_ref,
                 yr_ref, q_ref, kc_ref, vc_ref, ks_ref, vs_ref, kw_ref, vw_ref, gl_ref):
    x = x_ref[...]
    ms = jnp.mean(x * x, axis=-1, keepdims=True)
    xn = (x * lax.rsqrt(ms + RMS_EPS) * n1_ref[...]).astype(BF16)
    cos = cos_ref[...]
    sin = sin_ref[...]
    lane = lax.broadcasted_iota(jnp.int32, cos.shape, 1)
    first_half = (lane % HEAD_DIM) < (HEAD_DIM // 2)

    def rope(v):
        rot = jnp.where(first_half, pltpu.roll(v, LANES - HEAD_DIM // 2, 1), pltpu.roll(v, HEAD_DIM // 2, 1))
        return v * cos + rot * sin

    def cols(c0, width):
        return _dot(xn, w_ref[:, c0:c0 + width])

    for j in range(RWKV_COLS // 256):
        yr_ref[:, j * 256:(j + 1) * 256] = cols(j * 256, 256)
    scale = HEAD_DIM ** -0.5
    for hq in range(NSA_Q_HEADS):
        q = rope(cols(RWKV_COLS + hq * LANES, LANES)) * scale
        q_ref[:, hq * LANES:(hq + 1) * LANES] = q.astype(BF16)
    c0 = RWKV_COLS + Q_PAD_COLS
    kc_ref[...] = rope(cols(c0, LANES))
    vc_ref[...] = cols(c0 + LANES, LANES)
    ks_ref[...] = rope(cols(c0 + 2 * LANES, LANES)).astype(BF16)
    vs_ref[...] = cols(c0 + 3 * LANES, LANES).astype(BF16)
    kw_ref[...] = rope(cols(c0 + 4 * LANES, LANES)).astype(BF16)
    vw_ref[...] = cols(c0 + 5 * LANES, LANES).astype(BF16)
    gl_ref[...] = cols(c0 + 6 * LANES, LANES)


def _pack_w_in(w_in):
    w_r = w_in[:, :RWKV_COLS]
    w_q = w_in[:, RWKV_COLS:RWKV_COLS + NSA_WIDTH].reshape(D_MODEL, NSA_KV_HEADS, NSA_HPG, HEAD_DIM)
    zero = jnp.zeros_like(w_q)
    w_q = jnp.stack([jnp.where(jnp.arange(NSA_KV_HEADS)[None, :, None, None] == gg, w_q, zero)
                     for gg in range(NSA_KV_HEADS)], axis=3)
    w_q = w_q.reshape(D_MODEL, Q_PAD_COLS)
    c0 = RWKV_COLS + NSA_WIDTH
    w_kv = w_in[:, c0:c0 + 6 * NSA_KV_WIDTH]
    w_g = w_in[:, c0 + 6 * NSA_KV_WIDTH:]
    w_g = jnp.pad(w_g, ((0, 0), (0, LANES - w_g.shape[1])))
    return jnp.concatenate([w_r, w_q, w_kv, w_g], axis=1).astype(BF16)


def _in_proj(x2, norm1_w, w_packed, cos_t, sin_t):
    n = x2.shape[0]
    tm = 256
    row = lambda width: pl.BlockSpec((tm, width), lambda i: (i, 0))
    const = lambda shape: pl.BlockSpec(shape, lambda i: (0, 0))
    out_shape = [
        jax.ShapeDtypeStruct((n, RWKV_COLS), F32),
        jax.ShapeDtypeStruct((n, Q_PAD_COLS), BF16),
        jax.ShapeDtypeStruct((n, LANES), F32),
        jax.ShapeDtypeStruct((n, LANES), F32),
        jax.ShapeDtypeStruct((n, LANES), BF16),
        jax.ShapeDtypeStruct((n, LANES), BF16),
        jax.ShapeDtypeStruct((n, LANES), BF16),
        jax.ShapeDtypeStruct((n, LANES), BF16),
        jax.ShapeDtypeStruct((n, LANES), F32),
    ]
    out_specs = [row(RWKV_COLS), row(Q_PAD_COLS)] + [row(LANES)] * 7
    return pl.pallas_call(
        _proj_kernel,
        grid=(n // tm,),
        in_specs=[row(D_MODEL), const((1, D_MODEL)), const((D_MODEL, PROJ_COLS)), row(LANES), row(LANES)],
        out_specs=out_specs,
        out_shape=out_shape,
        compiler_params=_params("parallel"),
        name="in_proj",
    )(x2, norm1_w.reshape(1, D_MODEL), w_packed, cos_t, sin_t)


RWKV_TT = 256
P_LORA = "b3"
P_HS = "xb"
P_CUM = "xa"


def _rwkv_kernel(y_ref, mu_ref, w0_ref, wl_ref, a0_ref, gup_ref, kk_ref, ka_ref, rk_ref,
                 lw_ref, lb_ref, hs_ref, tri_ref, o_ref, prev_ref, s_ref, o_scr):
    i = pl.program_id(1)

    @pl.when(i == 0)
    def _():
        prev_ref[...] = jnp.zeros_like(prev_ref)
        s_ref[...] = jnp.zeros_like(s_ref)

    tt = y_ref.shape[0]
    y = y_ref[...]
    row = lax.broadcasted_iota(jnp.int32, (tt, 1), 0)
    y_prev = jnp.where(row == 0, prev_ref[...], pltpu.roll(y, 1, 0))
    prev_ref[...] = y[tt - 1:tt, :]
    ys = y + (y_prev - y) * mu_ref[...]

    w_ = RWKV_WIDTH
    r = ys[:, 0:w_]
    k = ys[:, w_:2 * w_]
    v = ys[:, 2 * w_:3 * w_]
    z = ys[:, 3 * w_:3 * w_ + LANES]
    gd = ys[:, 3 * w_ + LANES:3 * w_ + 2 * LANES]
    lane = lax.broadcasted_iota(jnp.int32, z.shape, 1)
    zt = jnp.where(lane < D_DECAY_LORA, jnp.tanh(z), z)
    wa = _mm(zt, wl_ref[...], P_LORA)
    w_raw = w0_ref[...] + wa[:, :w_]
    a = jax.nn.sigmoid(a0_ref[...] + wa[:, w_:])
    logw = -jnp.exp(-jax.nn.softplus(-w_raw) - 0.5)
    g = _mm(jax.nn.sigmoid(gd), gup_ref[...], P_LORA)
    hs = hs_ref[...]
    kk = k * kk_ref[...]
    kk = kk * lax.rsqrt(jnp.maximum(_mm(kk * kk, hs, P_HS), 1e-12))
    k2 = k * (1.0 + (a - 1.0) * ka_ref[...])
    alpha = -kk
    beta = kk * a

    cw = _mm(tri_ref[...], logw, P_CUM)
    e_in = jnp.exp(cw)
    e_ex = jnp.exp(cw - logw)
    e_neg = jnp.exp(-cw)
    a_t = alpha * e_ex
    r_t = r * e_in
    b_t = beta * e_neg
    k_t = k2 * e_neg

    m0 = lax.broadcasted_iota(jnp.int32, (CHUNK, LANES), 1) < HEAD_DIM

    def bd(xc):
        return jnp.concatenate([jnp.where(m0, xc, 0.0), jnp.where(m0, 0.0, xc)], axis=0)

    n2 = 2 * CHUNK
    ri = lax.broadcasted_iota(jnp.int32, (n2, n2), 0) % CHUNK
    ci = lax.broadcasted_iota(jnp.int32, (n2, n2), 1) % CHUNK
    strict = ri > ci
    incl = ri >= ci
    eye = jnp.where(lax.broadcasted_iota(jnp.int32, (n2, n2), 0) == lax.broadcasted_iota(jnp.int32, (n2, n2), 1),
                    1.0, 0.0)

    units = [(c, p) for c in range(tt // CHUNK) for p in range(w_ // LANES)]
    w_c = {}
    st = {}
    for c in range(tt // CHUNK):
        rs = slice(c * CHUNK, (c + 1) * CHUNK)
        cw_last = cw[c * CHUNK + CHUNK - 1:c * CHUNK + CHUNK, :]
        e_tot = jnp.exp(cw_last - cw[rs, :])
        w_c[c] = jnp.exp(cw_last)
        for p in range(w_ // LANES):
            ls = slice(p * LANES, (p + 1) * LANES)
            a_bd = bd(a_t[rs, ls]).astype(BF16)
            r_bd = bd(r_t[rs, ls])
            b_bd = bd(b_t[rs, ls]).astype(BF16)
            k_bd = bd(k_t[rs, ls]).astype(BF16)
            gm = _dot_nt(jnp.concatenate([a_bd, r_bd.astype(BF16)], axis=0), jnp.concatenate([b_bd, k_bd], axis=0))
            st[c, p] = dict(
                a_bd=a_bd, r_bd=r_bd,
                v_bd=bd(v[rs, ls]).astype(BF16),
                bh_bd=bd(beta[rs, ls] * e_tot[:, ls]).astype(BF16),
                kh_bd=bd(k2[rs, ls] * e_tot[:, ls]).astype(BF16),
                l_ab=jnp.where(strict, gm[0:n2, 0:n2], 0.0),
                l_ak=jnp.where(strict, gm[0:n2, n2:], 0.0).astype(BF16),
                m_rb=jnp.where(incl, gm[n2:, 0:n2], 0.0).astype(BF16),
                m_rk=jnp.where(incl, gm[n2:, n2:], 0.0).astype(BF16))

    pw, tm_ = {}, {}
    for u_ in units:
        l_ab = st[u_]["l_ab"]
        lb = l_ab.astype(BF16)
        pw[u_] = _dot(lb, lb)
        tm_[u_] = eye + l_ab
    for level in range(5):
        for u_ in units:
            pb = pw[u_].astype(BF16)
            if level < 4:
                y = _dot(pb, jnp.concatenate([pb, tm_[u_].astype(BF16)], axis=1))
                pw[u_] = y[:, 0:n2]
                tm_[u_] = tm_[u_] + y[:, n2:]
            else:
                tm_[u_] = tm_[u_] + _dot(pb, tm_[u_].astype(BF16))

    for u_ in units:
        d_ = st[u_]
        lv = _dot(d_["l_ak"], d_["v_bd"])
        au = _dot(tm_[u_].astype(BF16), jnp.concatenate([d_["a_bd"], lv.astype(BF16)], axis=1)).astype(BF16)
        mm_ = _dot(d_["m_rb"], au)
        d_["r_hat"] = (d_["r_bd"] + mm_[:, 0:n2]).astype(BF16)
        d_["o0"] = mm_[:, n2:] + _dot(d_["m_rk"], d_["v_bd"])
        gn = _dot_tn(au, d_["bh_bd"])
        d_["g"] = gn[0:n2].astype(BF16)
        d_["n"] = gn[n2:] + _dot_tn(d_["v_bd"], d_["kh_bd"])

    for c, p in units:
        d_ = st[c, p]
        rs = slice(c * CHUNK, (c + 1) * CHUNK)
        ls = slice(p * LANES, (p + 1) * LANES)
        s_old = s_ref[p]
        sb = s_old.astype(BF16)
        o_bd = _dot_nt(d_["r_hat"], sb) + d_["o0"]
        o_scr[rs, ls] = o_bd[0:CHUNK] + o_bd[CHUNK:]
        s_ref[p] = s_old * w_c[c][:, ls] + _dot(sb, d_["g"]) + d_["n"]

    o = o_scr[...]
    inv_n = 1.0 / HEAD_DIM
    mean = _mm(o, hs, P_HS) * inv_n
    d = o - mean
    var = _mm(d * d, hs, P_HS) * inv_n
    on = d * lax.rsqrt(var + LNX_EPS) * lw_ref[...] + lb_ref[...]
    bonus = _mm(r * k2 * rk_ref[...], hs, P_HS) * v
    o_ref[...] = ((on + bonus) * g).astype(o_ref.dtype)


def _rwkv(yr, b, t, mu, w0, w_lora_up, a0, a_lora_up, g_lora_up, k_k, k_a, r_k, lnx_w, lnx_b):
    tt = RWKV_TT
    nt = t // tt
    w_ = RWKV_WIDTH
    wl = jnp.zeros((LANES, 2 * w_), F32)
    wl = wl.at[:D_DECAY_LORA, :w_].set(w_lora_up).at[D_DECAY_LORA:, w_:].set(a_lora_up)
    head = jnp.arange(w_) // HEAD_DIM
    hs = (head[:, None] == head[None, :]).astype(F32)
    ti = jnp.arange(tt)
    tri = ((ti[:, None] // CHUNK == ti[None, :] // CHUNK) & (ti[:, None] >= ti[None, :])).astype(F32)
    vec = lambda a_, width: a_.reshape(1, width)
    const = lambda shape: pl.BlockSpec(shape, lambda bi, i: (0, 0))
    return pl.pallas_call(
        _rwkv_kernel,
        grid=(b, nt),
        in_specs=[pl.BlockSpec((tt, RWKV_COLS), lambda bi, i: (bi * nt + i, 0)),
                  const((1, RWKV_COLS)), const((1, w_)), const((LANES, 2 * w_)), const((1, w_)),
                  const((D_GATE_LORA, w_)), const((1, w_)), const((1, w_)), const((1, w_)),
                  const((1, w_)), const((1, w_)), const((w_, w_)), const((tt, tt))],
        out_specs=pl.BlockSpec((tt, w_), lambda bi, i: (bi * nt + i, 0)),
        out_shape=jax.ShapeDtypeStruct((b * t, w_), BF16),
        scratch_shapes=[pltpu.VMEM((1, RWKV_COLS), F32),
                        pltpu.VMEM((w_ // LANES, LANES, LANES), F32),
                        pltpu.VMEM((tt, w_), F32)],
        compiler_params=_params("parallel", "arbitrary"),
        name="rwkv7",
    )(yr, vec(mu, RWKV_COLS), vec(w0, w_), wl, vec(a0, w_), g_lora_up, vec(k_k, w_), vec(k_a, w_),
      vec(r_k, w_), vec(lnx_w, w_), vec(lnx_b, w_), hs, tri)


def _compress_kernel(zk_ref, zv_ref, pek_ref, pev_ref, wk1_ref, wv1_ref, wk2_ref, wv2_ref, kc_ref, vc_ref):
    def one(z_ref, pe_ref, w1_ref, w2_ref, out_ref):
        z = z_ref[0]
        nrow = z.shape[0]
        za = z + pe_ref[0:1, :]
        zb = z + pe_ref[1:2, :]
        acc = jnp.zeros((nrow, LANES), F32)
        for gi in range(NSA_KV_HEADS):
            ha = _dot(za, w1_ref[gi, 0], HI)
            hb = _dot(zb, w1_ref[gi, 1], HI)
            hid = ha + pltpu.roll(hb, nrow - 1, 0)
            acc = acc + _dot(jax.nn.gelu(hid), w2_ref[gi], HI)
        out_ref[0] = acc

    one(zk_ref, pek_ref, wk1_ref, wk2_ref, kc_ref)
    one(zv_ref, pev_ref, wv1_ref, wv2_ref, vc_ref)


def _compress_weights(pos, w1, w2):
    half = CMP_BLOCK // 2
    w1r = w1.reshape(2, half, HEAD_DIM, CMP_HIDDEN)
    w1g = jnp.zeros((NSA_KV_HEADS, 2, half, NSA_KV_HEADS, HEAD_DIM, CMP_HIDDEN), F32)
    for gi in range(NSA_KV_HEADS):
        w1g = w1g.at[gi, :, :, gi].set(w1r)
    w1g = w1g.reshape(NSA_KV_HEADS, 2, half * NSA_KV_WIDTH, CMP_HIDDEN)
    w2g = jnp.zeros((NSA_KV_HEADS, CMP_HIDDEN, NSA_KV_HEADS, HEAD_DIM), F32)
    for gi in range(NSA_KV_HEADS):
        w2g = w2g.at[gi, :, gi].set(w2)
    w2g = w2g.reshape(NSA_KV_HEADS, CMP_HIDDEN, NSA_KV_WIDTH)
    pe = jnp.broadcast_to(pos.reshape(2, half, 1, HEAD_DIM), (2, half, NSA_KV_HEADS, HEAD_DIM))
    pe = pe.reshape(2, half * NSA_KV_WIDTH)
    return pe, w1g, w2g


def _compress(kc, vc, b, t, cmp_pos_k, cmp_pos_v, k_w1, k_w2, v_w1, v_w2):
    nrow = t // CMP_STRIDE
    zw = CMP_STRIDE * NSA_KV_WIDTH
    pek, wk1, wk2 = _compress_weights(cmp_pos_k, k_w1, k_w2)
    pev, wv1, wv2 = _compress_weights(cmp_pos_v, v_w1, v_w2)
    zspec = pl.BlockSpec((1, nrow, zw), lambda bi: (bi, 0, 0))
    ospec = pl.BlockSpec((1, nrow, LANES), lambda bi: (bi, 0, 0))
    c2 = lambda shape: pl.BlockSpec(shape, lambda bi: (0,) * len(shape))
    return pl.pallas_call(
        _compress_kernel,
        grid=(b,),
        in_specs=[zspec, zspec, c2((2, zw)), c2((2, zw)), c2(wk1.shape), c2(wv1.shape), c2(wk2.shape), c2(wv2.shape)],
        out_specs=[ospec, ospec],
        out_shape=[jax.ShapeDtypeStruct((b, nrow, LANES), F32)] * 2,
        compiler_params=_params("parallel"),
        name="nsa_compress",
    )(kc.reshape(b, nrow, zw), vc.reshape(b, nrow, zw), pek, pev, wk1, wv1, wk2, wv2)


SEL_TILE = 256
SEL_UNROLL = 2
WIN_TILE = 128


def _nsa_kernel(q_ref, kcmp_ref, vcmp_ref, ks_ref, vs_ref, kw_ref, vw_ref, gl_ref, ovt_ref, exp_ref,
                o_ref, m_scr, l_scr, acc_scr, sc_scr):
    s = pl.program_id(1)
    qb = Q_BLOCK
    n_head = NSA_Q_HEADS
    rows = n_head * qb
    n_cmp = kcmp_ref.shape[1]
    n_sb = ovt_ref.shape[0]

    tok64 = s * qb + lax.broadcasted_iota(jnp.int32, (qb, LANES), 0)
    lane64 = lax.broadcasted_iota(jnp.int32, (qb, LANES), 1)
    sig = jax.nn.sigmoid(gl_ref[...])

    qq = jnp.concatenate([q_ref[:, hq * LANES:(hq + 1) * LANES] for hq in range(n_head)], axis=0)

    def per_head(x64):
        return jnp.concatenate([x64] * n_head, axis=0)

    def per_group_head(x128):
        return jnp.concatenate([x128[0:qb]] * NSA_HPG + [x128[qb:2 * qb]] * NSA_HPG, axis=0)

    tok_c = s * qb + lax.broadcasted_iota(jnp.int32, (qb, n_cmp), 0)
    cmp_i = lax.broadcasted_iota(jnp.int32, (qb, n_cmp), 1)
    cbias = per_head(jnp.where((cmp_i * CMP_STRIDE + CMP_BLOCK - 1 <= tok_c) & (cmp_i < n_cmp - 1), 0.0, NEG_INF))
    sc = _mm(qq, kcmp_ref[0], "xa", _dot_nt) + cbias
    mx = jnp.max(sc, axis=1, keepdims=True)
    e = jnp.where(cbias == 0.0, jnp.exp(sc - mx), 0.0)
    den = jnp.sum(e, axis=1, keepdims=True)
    p_c = e / jnp.where(den > 0.0, den, 1.0)
    o_cmp = _dot(p_c.astype(BF16), vcmp_ref[0].astype(BF16))

    pc_sums = []
    for gi in range(NSA_KV_HEADS):
        r0 = gi * NSA_HPG * qb
        acc = p_c[r0:r0 + qb]
        for h in range(1, NSA_HPG):
            acc = acc + p_c[r0 + h * qb:r0 + (h + 1) * qb]
        pc_sums.append(acc)
    pcs = jnp.concatenate(pc_sums, axis=0)
    imp = _dot_nt(ovt_ref[...], pcs, HI)
    blk = lax.broadcasted_iota(jnp.int32, (n_sb, LANES), 0)
    forced = (blk == 0) | (blk == s) | (blk == s - 1)
    imp = jnp.where(blk <= s, imp + jnp.where(forced, FORCE_BONUS, 0.0), NEG_INF)
    cnt = jnp.zeros((n_sb, LANES), F32)
    for i2 in range(n_sb):
        vi = imp[i2:i2 + 1, :]
        tie = jnp.where(blk > i2, jnp.where(vi == imp, 1.0, 0.0), 0.0)
        cnt = cnt + jnp.where(vi > imp, 1.0, tie)
    n_sel = min(SEL_TOPK, n_sb)
    sel_t = jnp.where((cnt < n_sel) & (blk <= s), 1.0, 0.0)
    sel = sel_t.T.astype(BF16)

    def attend(k_ref, v_ref, kt_w, n_steps, unroll, tile_bias):
        def loop(body, init):
            def step(it, c):
                for u in range(unroll):
                    c = body(it * unroll + u, c)
                return c
            if isinstance(n_steps, int):
                c = init
                for it in range(n_steps):
                    c = step(it, c)
                return c
            return lax.fori_loop(0, n_steps, step, init)

        def pass1(j, mx_):
            kt, bias = tile_bias(j)
            k0 = pl.multiple_of(kt * kt_w, kt_w)
            sc_ = _dot_nt(qq, k_ref[0, pl.ds(k0, kt_w), :]) + bias
            sc_scr[j, :, 0:kt_w] = sc_
            for c0 in range(0, kt_w, LANES):
                mx_ = jnp.maximum(mx_, sc_[:, c0:c0 + LANES])
            return mx_

        mx_ = loop(pass1, jnp.full((rows, LANES), NEG_INF, F32))
        m_scr[...] = jnp.broadcast_to(jnp.max(mx_, axis=1, keepdims=True), m_scr.shape)
        l_scr[...] = jnp.zeros(l_scr.shape, F32)
        acc_scr[...] = jnp.zeros(acc_scr.shape, F32)

        def pass2(j, c):
            kt, _ = tile_bias(j, False)
            k0 = pl.multiple_of(kt * kt_w, kt_w)
            m_b = jnp.concatenate([m_scr[...]] * (kt_w // LANES), axis=1)
            p = jnp.exp(sc_scr[j, :, 0:kt_w] - m_b)
            l_new = l_scr[...]
            for c0 in range(0, kt_w, LANES):
                l_new = l_new + p[:, c0:c0 + LANES]
            l_scr[...] = l_new
            acc_scr[...] += _dot(p.astype(BF16), v_ref[0, pl.ds(k0, kt_w), :])
            return c

        loop(pass2, 0)
        return acc_scr[...] / jnp.sum(l_scr[...], axis=1, keepdims=True)

    tok_s = s * qb + lax.broadcasted_iota(jnp.int32, (2 * qb, SEL_TILE), 0) % qb
    lane_s = lax.broadcasted_iota(jnp.int32, (2 * qb, SEL_TILE), 1)

    def sel_bias(j, want_bias=True):
        if not want_bias:
            return j, None
        chosen = _dot(sel, exp_ref[j])
        ok = (chosen > 0.5) & (j * SEL_TILE + lane_s <= tok_s)
        return j, per_group_head(jnp.where(ok, 0.0, NEG_INF))

    o_sel = attend(ks_ref, vs_ref, SEL_TILE, (s * qb) // (SEL_TILE * SEL_UNROLL) + 1, SEL_UNROLL, sel_bias)

    n_win = WINDOW // WIN_TILE + 1
    first = (s * qb) // WIN_TILE + 1 - n_win

    def win_bias(j, want_bias=True):
        kt = jnp.maximum(first + j, 0)
        if not want_bias:
            return kt, None
        kp = kt * WIN_TILE + lane64 + jnp.where(first + j >= 0, 0, 1 << 24)
        return kt, per_head(jnp.where((kp <= tok64) & (kp > tok64 - WINDOW), 0.0, NEG_INF))

    o_win = attend(kw_ref, vw_ref, WIN_TILE, 1, n_win, win_bias)

    for hq in range(n_head):
        gi = hq // NSA_HPG
        hr = slice(hq * qb, (hq + 1) * qb)
        col = hq * N_BRANCH
        o_h = (sig[:, col:col + 1] * o_cmp[hr] + sig[:, col + 1:col + 2] * o_sel[hr]
               + sig[:, col + 2:col + 3] * o_win[hr])
        in_group = (lane64 >= gi * HEAD_DIM) & (lane64 < (gi + 1) * HEAD_DIM)
        o_ref[:, hq * LANES:(hq + 1) * LANES] = jnp.where(in_group, o_h, 0.0).astype(o_ref.dtype)


def _nsa_attention(q, k_cmp, v_cmp, ks, vs, kw, vw, gl, b, t):
    ns = t // Q_BLOCK
    n_sb = t // SEL_BLOCK
    n_cmp = t // CMP_STRIDE
    n_kt = t // SEL_TILE
    cmp_start = jnp.arange(n_cmp) * CMP_STRIDE
    sb = jnp.arange(n_sb)
    ovt = ((cmp_start[None, :] < (sb[:, None] + 1) * SEL_BLOCK)
           & (cmp_start[None, :] + CMP_BLOCK > sb[:, None] * SEL_BLOCK)
           & (jnp.arange(n_cmp)[None, :] < n_cmp - 1)).astype(F32)
    key_blk = (jnp.arange(n_kt)[:, None] * SEL_TILE + jnp.arange(SEL_TILE)[None, :]) // SEL_BLOCK
    expand = (key_blk[:, None, :] == sb[None, :, None]).astype(BF16)
    rows = NSA_Q_HEADS * Q_BLOCK
    qspec = pl.BlockSpec((Q_BLOCK, Q_PAD_COLS), lambda bi, si: (bi * ns + si, 0))
    seq = lambda: pl.BlockSpec((1, t, LANES), lambda bi, si: (bi, 0, 0))
    cmp_spec = lambda: pl.BlockSpec((1, n_cmp, LANES), lambda bi, si: (bi, 0, 0))
    k3 = lambda a_: a_.reshape(b, t, LANES)
    return pl.pallas_call(
        _nsa_kernel,
        grid=(b, ns),
        in_specs=[qspec, cmp_spec(), cmp_spec(), seq(), seq(), seq(), seq(),
                  pl.BlockSpec((Q_BLOCK, LANES), lambda bi, si: (bi * ns + si, 0)),
                  pl.BlockSpec((n_sb, n_cmp), lambda bi, si: (0, 0)),
                  pl.BlockSpec((n_kt, n_sb, SEL_TILE), lambda bi, si: (0, 0, 0))],
        out_specs=pl.BlockSpec((Q_BLOCK, Q_PAD_COLS), lambda bi, si: (bi * ns + si, 0)),
        out_shape=jax.ShapeDtypeStruct((b * t, Q_PAD_COLS), BF16),
        scratch_shapes=[pltpu.VMEM((rows, LANES), F32),
                        pltpu.VMEM((rows, LANES), F32),
                        pltpu.VMEM((rows, LANES), F32),
                        pltpu.VMEM((n_kt, rows, SEL_TILE), F32)],
        compiler_params=_params("parallel", "arbitrary"),
        name="nsa_attention",
    )(q, k_cmp, v_cmp, k3(ks), k3(vs), k3(kw), k3(vw), gl, ovt, expand)


FF_TILE = 256


def _ffn_kernel(x_ref, orw_ref, ons_ref, wor_ref, won_ref, n2_ref, w1_ref, w3_ref, w2_ref, nf_ref,
                out_ref, h_scr, u_scr, acc_scr):
    j = pl.program_id(1)

    @pl.when(j == 0)
    def _():
        h = x_ref[...] + _dot(orw_ref[...], wor_ref[...]) + _dot(ons_ref[...], won_ref[...])
        h_scr[...] = h
        ms = jnp.mean(h * h, axis=-1, keepdims=True)
        u_scr[...] = (h * lax.rsqrt(ms + RMS_EPS) * n2_ref[...]).astype(BF16)
        acc_scr[...] = jnp.zeros_like(acc_scr)

    u = u_scr[...]
    gate = _dot(u, w1_ref[...])
    up = _dot(u, w3_ref[...])
    act = (jax.nn.silu(gate) * up).astype(BF16)
    acc_scr[...] += _dot(act, w2_ref[...])

    @pl.when(j == pl.num_programs(1) - 1)
    def _():
        h = h_scr[...] + acc_scr[...]
        ms = jnp.mean(h * h, axis=-1, keepdims=True)
        out_ref[...] = h * lax.rsqrt(ms + RMS_EPS) * nf_ref[...]


def _out_ffn(x2, o_rwkv, o_nsa, w_out, norm2_w, ffn_w1, ffn_w3, ffn_w2, final_norm_w):
    n = x2.shape[0]
    tm = 512
    w_or = w_out[:RWKV_WIDTH].astype(BF16)
    w_n = w_out[RWKV_WIDTH:].reshape(NSA_KV_HEADS, NSA_HPG, HEAD_DIM, D_MODEL)
    w_on = jnp.zeros((NSA_KV_HEADS, NSA_HPG, NSA_KV_HEADS, HEAD_DIM, D_MODEL), F32)
    for gi in range(NSA_KV_HEADS):
        w_on = w_on.at[gi, :, gi].set(w_n[gi])
    w_on = w_on.reshape(Q_PAD_COLS, D_MODEL).astype(BF16)
    row = lambda width: pl.BlockSpec((tm, width), lambda i, j: (i, 0))
    const = lambda shape: pl.BlockSpec(shape, lambda i, j: (0, 0))
    return pl.pallas_call(
        _ffn_kernel,
        grid=(n // tm, D_FF // FF_TILE),
        in_specs=[row(D_MODEL), row(RWKV_WIDTH), row(Q_PAD_COLS),
                  const((RWKV_WIDTH, D_MODEL)), const((Q_PAD_COLS, D_MODEL)), const((1, D_MODEL)),
                  pl.BlockSpec((D_MODEL, FF_TILE), lambda i, j: (0, j)),
                  pl.BlockSpec((D_MODEL, FF_TILE), lambda i, j: (0, j)),
                  pl.BlockSpec((FF_TILE, D_MODEL), lambda i, j: (j, 0)),
                  const((1, D_MODEL))],
        out_specs=row(D_MODEL),
        out_shape=jax.ShapeDtypeStruct((n, D_MODEL), F32),
        scratch_shapes=[pltpu.VMEM((tm, D_MODEL), F32), pltpu.VMEM((tm, D_MODEL), BF16),
                        pltpu.VMEM((tm, D_MODEL), F32)],
        compiler_params=_params("parallel", "arbitrary"),
        name="out_ffn",
    )(x2, o_rwkv, o_nsa, w_or, w_on, norm2_w.reshape(1, D_MODEL), ffn_w1.astype(BF16), ffn_w3.astype(BF16),
      ffn_w2.astype(BF16), final_norm_w.reshape(1, D_MODEL))


def kernel(x, positions, norm1_w, w_in, mu_rwkv, w0, w_lora_up, a0, a_lora_up, g_lora_up, k_k, k_a, r_k, lnx_w, lnx_b, cmp_pos_k, cmp_pos_v, cmp_k_w1, cmp_k_w2, cmp_v_w1, cmp_v_w2, w_out, norm2_w, ffn_w1, ffn_w3, ffn_w2, final_norm_w):
    b, t, d = x.shape
    assert d == D_MODEL and norm1_w.shape[0] == 1, "single-layer block with d_model 1024"
    assert t % RWKV_TT == 0 and t % (SEL_UNROLL * SEL_TILE) == 0 and t // SEL_BLOCK <= LANES
    assert t // SEL_TILE >= WINDOW // WIN_TILE + 1, "score scratch is shared by the selected and window branches"
    x2 = x.reshape(b * t, d)
    cos_t, sin_t = _rope_tables(positions)
    yr, q, kc, vc, ks, vs, kw, vw, gl = _in_proj(x2, norm1_w[0], _pack_w_in(w_in[0]), cos_t, sin_t)
    o_rwkv = _rwkv(yr, b, t, mu_rwkv[0], w0[0], w_lora_up[0], a0[0], a_lora_up[0], g_lora_up[0],
                   k_k[0], k_a[0], r_k[0], lnx_w[0], lnx_b[0])
    k_cmp, v_cmp = _compress(kc, vc, b, t, cmp_pos_k[0], cmp_pos_v[0], cmp_k_w1[0], cmp_k_w2[0],
                             cmp_v_w1[0], cmp_v_w2[0])
    o_nsa = _nsa_attention(q, k_cmp, v_cmp, ks, vs, kw, vw, gl, b, t)
    out = _out_ffn(x2, o_rwkv, o_nsa, w_out[0], norm2_w[0], ffn_w1[0], ffn_w3[0], ffn_w2[0], final_norm_w)
    return out.reshape(b, t, d)
```

```python
import functools

import jax
import jax.numpy as jnp
from jax import lax
from jax.experimental import pallas as pl
from jax.experimental.pallas import tpu as pltpu

F32 = jnp.float32
BF16 = jnp.bfloat16
HI = lax.Precision.HIGHEST

D_MODEL = 1024
HEAD_DIM = 64
RWKV_WIDTH = 512
RWKV_HEADS = 8
D_DECAY_LORA = 64
D_AAA_LORA = 64
D_GATE_LORA = 128
RWKV_COLS = 3 * RWKV_WIDTH + D_DECAY_LORA + D_AAA_LORA + D_GATE_LORA
NSA_WIDTH = 512
NSA_Q_HEADS = 8
NSA_KV_HEADS = 2
NSA_HPG = 4
NSA_KV_WIDTH = 128
N_BRANCH = 3
CMP_BLOCK = 32
CMP_STRIDE = 16
CMP_HIDDEN = 256
SEL_BLOCK = 64
SEL_TOPK = 16
WINDOW = 512
Q_BLOCK = 64
ROPE_THETA = 10000.0
D_FF = 2816
RMS_EPS = 1e-6
LNX_EPS = 64e-5
FORCE_BONUS = 1e4
NEG_INF = -1e30

LANES = 128
CHUNK = 64
Q_PAD_COLS = NSA_Q_HEADS * LANES
PROJ_COLS = RWKV_COLS + Q_PAD_COLS + 6 * NSA_KV_WIDTH + LANES
VMEM_LIMIT = 56 * 1024 * 1024


def _dot(a, b, prec=None):
    return jnp.dot(a, b, preferred_element_type=F32, precision=prec)


def _dot_nt(a, b, prec=None):
    return lax.dot_general(a, b, (((1,), (1,)), ((), ())), preferred_element_type=F32, precision=prec)


def _dot_tn(a, b, prec=None):
    return lax.dot_general(a, b, (((0,), (0,)), ((), ())), preferred_element_type=F32, precision=prec)


def _split(a):
    hi = a.astype(BF16)
    return hi, (a - hi.astype(F32)).astype(BF16)


def _mm(a, b, mode, dot=_dot):
    if mode == "hi":
        return dot(a, b, HI)
    if mode == "b1":
        return dot(a.astype(BF16), b.astype(BF16))
    if mode == "xa":
        bh, bl = _split(b)
        ab = a.astype(BF16)
        return dot(ab, bh) + dot(ab, bl)
    if mode == "xb":
        ah, al = _split(a)
        bb = b.astype(BF16)
        return dot(ah, bb) + dot(al, bb)
    ah, al = _split(a)
    bh, bl = _split(b)
    return dot(ah, bh) + (dot(ah, bl) + dot(al, bh))


def _params(*sem):
    return pltpu.CompilerParams(dimension_semantics=sem, vmem_limit_bytes=VMEM_LIMIT)


def _rope_table_kernel(pos_ref, invf_ref, sign_ref, cos_ref, sin_ref):
    ang = pos_ref[...].astype(F32) * invf_ref[...]
    cos_ref[...] = jnp.cos(ang)
    sin_ref[...] = jnp.sin(ang) * sign_ref[...]


def _rope_tables(positions):
    n = positions.size
    tm = 512
    half = HEAD_DIM // 2
    inv_freq = ROPE_THETA ** (-jnp.arange(half, dtype=F32) / half)
    invf = jnp.tile(inv_freq, LANES // half).reshape(1, LANES)
    lane = jnp.arange(LANES)
    sign = jnp.where((lane % HEAD_DIM) < half, -1.0, 1.0).astype(F32).reshape(1, LANES)
    row = pl.BlockSpec((tm, LANES), lambda i: (i, 0))
    const = pl.BlockSpec((1, LANES), lambda i: (0, 0))
    return pl.pallas_call(
        _rope_table_kernel,
        grid=(n // tm,),
        in_specs=[pl.BlockSpec((tm, 1), lambda i: (i, 0)), const, const],
        out_specs=[row, row],
        out_shape=[jax.ShapeDtypeStruct((n, LANES), F32)] * 2,
        compiler_params=_params("parallel"),
        name="rope_tables",
    )(positions.reshape(n, 1), invf, sign)


def _proj_kernel(x_ref, n1_ref, w_ref, cos_ref, sin_ref,
                 yr_ref, q_ref, kc_ref, vc_ref, ks_ref, vs_ref, kw_ref, vw_ref, gl_ref):
    x = x_ref[...]
    ms = jnp.mean(x * x, axis=-1, keepdims=True)
    xn = (x * lax.rsqrt(ms + RMS_EPS) * n1_ref[...]).astype(BF16)
    cos = cos_ref[...]
    sin = sin_ref[...]
    lane = lax.broadcasted_iota(jnp.int32, cos.shape, 1)
    first_half = (lane % HEAD_DIM) < (HEAD_DIM // 2)

    def rope(v):
        rot = jnp.where(first_half, pltpu.roll(v, LANES - HEAD_DIM // 2, 1), pltpu.roll(v, HEAD_DIM // 2, 1))
        return v * cos + rot * sin

    def cols(c0, width):
        return _dot(xn, w_ref[:, c0:c0 + width])

    for j in range(RWKV_COLS // 256):
        yr_ref[:, j * 256:(j + 1) * 256] = cols(j * 256, 256)
    scale = HEAD_DIM ** -0.5
    for hq in range(NSA_Q_HEADS):
        q = rope(cols(RWKV_COLS + hq * LANES, LANES)) * scale
        q_ref[:, hq * LANES:(hq + 1) * LANES] = q.astype(BF16)
    c0 = RWKV_COLS + Q_PAD_COLS
    kc_ref[...] = rope(cols(c0, LANES))
    vc_ref[...] = cols(c0 + LANES, LANES)
    ks_ref[...] = rope(cols(c0 + 2 * LANES, LANES)).astype(BF16)
    vs_ref[...] = cols(c0 + 3 * LANES, LANES).astype(BF16)
    kw_ref[...] = rope(cols(c0 + 4 * LANES, LANES)).astype(BF16)
    vw_ref[...] = cols(c0 + 5 * LANES, LANES).astype(BF16)
    gl_ref[...] = cols(c0 + 6 * LANES, LANES)


def _pack_w_in(w_in):
    w_r = w_in[:, :RWKV_COLS]
    w_q = w_in[:, RWKV_COLS:RWKV_COLS + NSA_WIDTH].reshape(D_MODEL, NSA_KV_HEADS, NSA_HPG, HEAD_DIM)
    zero = jnp.zeros_like(w_q)
    w_q = jnp.stack([jnp.where(jnp.arange(NSA_KV_HEADS)[None, :, None, None] == gg, w_q, zero)
                     for gg in range(NSA_KV_HEADS)], axis=3)
    w_q = w_q.reshape(D_MODEL, Q_PAD_COLS)
    c0 = RWKV_COLS + NSA_WIDTH
    w_kv = w_in[:, c0:c0 + 6 * NSA_KV_WIDTH]
    w_g = w_in[:, c0 + 6 * NSA_KV_WIDTH:]
    w_g = jnp.pad(w_g, ((0, 0), (0, LANES - w_g.shape[1])))
    return jnp.concatenate([w_r, w_q, w_kv, w_g], axis=1).astype(BF16)


def _in_proj(x2, norm1_w, w_packed, cos_t, sin_t):
    n = x2.shape[0]
    tm = 512
    row = lambda width: pl.BlockSpec((tm, width), lambda i: (i, 0))
    const = lambda shape: pl.BlockSpec(shape, lambda i: (0, 0))
    out_shape = [
        jax.ShapeDtypeStruct((n, RWKV_COLS), F32),
        jax.ShapeDtypeStruct((n, Q_PAD_COLS), BF16),
        jax.ShapeDtypeStruct((n, LANES), F32),
        jax.ShapeDtypeStruct((n, LANES), F32),
        jax.ShapeDtypeStruct((n, LANES), BF16),
        jax.ShapeDtypeStruct((n, LANES), BF16),
        jax.ShapeDtypeStruct((n, LANES), BF16),
        jax.ShapeDtypeStruct((n, LANES), BF16),
        jax.ShapeDtypeStruct((n, LANES), F32),
    ]
    out_specs = [row(RWKV_COLS), row(Q_PAD_COLS)] + [row(LANES)] * 7
    return pl.pallas_call(
        _proj_kernel,
        grid=(n // tm,),
        in_specs=[row(D_MODEL), const((1, D_MODEL)), const((D_MODEL, PROJ_COLS)), row(LANES), row(LANES)],
        out_specs=out_specs,
        out_shape=out_shape,
        compiler_params=_params("parallel"),
        name="in_proj",
    )(x2, norm1_w.reshape(1, D_MODEL), w_packed, cos_t, sin_t)


RWKV_TT = 256
P_LORA = "b3"
P_HS = "xb"
P_CUM = "xa"


def _rwkv_kernel(y_ref, mu_ref, w0_ref, wl_ref, a0_ref, gup_ref, kk_ref, ka_ref, rk_ref,
                 lw_ref, lb_ref, hs_ref, tri_ref, o_ref, prev_ref, s_ref, o_scr):
    i = pl.program_id(1)

    @pl.when(i == 0)
    def _():
        prev_ref[...] = jnp.zeros_like(prev_ref)
        s_ref[...] = jnp.zeros_like(s_ref)

    tt = y_ref.shape[0]
    y = y_ref[...]
    row = lax.broadcasted_iota(jnp.int32, (tt, 1), 0)
    y_prev = jnp.where(row == 0, prev_ref[...], pltpu.roll(y, 1, 0))
    prev_ref[...] = y[tt - 1:tt, :]
    ys = y + (y_prev - y) * mu_ref[...]

    w_ = RWKV_WIDTH
    r = ys[:, 0:w_]
    k = ys[:, w_:2 * w_]
    v = ys[:, 2 * w_:3 * w_]
    z = ys[:, 3 * w_:3 * w_ + LANES]
    gd = ys[:, 3 * w_ + LANES:3 * w_ + 2 * LANES]
    lane = lax.broadcasted_iota(jnp.int32, z.shape, 1)
    zt = jnp.where(lane < D_DECAY_LORA, jnp.tanh(z), z)
    wa = _mm(zt, wl_ref[...], P_LORA)
    w_raw = w0_ref[...] + wa[:, :w_]
    a = jax.nn.sigmoid(a0_ref[...] + wa[:, w_:])
    logw = -jnp.exp(-jax.nn.softplus(-w_raw) - 0.5)
    g = _mm(jax.nn.sigmoid(gd), gup_ref[...], P_LORA)
    hs = hs_ref[...]
    kk = k * kk_ref[...]
    kk = kk * lax.rsqrt(jnp.maximum(_mm(kk * kk, hs, P_HS), 1e-12))
    k2 = k * (1.0 + (a - 1.0) * ka_ref[...])
    alpha = -kk
    beta = kk * a

    cw = _mm(tri_ref[...], logw, P_CUM)
    e_in = jnp.exp(cw)
    e_ex = jnp.exp(cw - logw)
    e_neg = jnp.exp(-cw)
    a_t = alpha * e_ex
    r_t = r * e_in
    b_t = beta * e_neg
    k_t = k2 * e_neg

    m0 = lax.broadcasted_iota(jnp.int32, (CHUNK, LANES), 1) < HEAD_DIM

    def bd(xc):
        return jnp.concatenate([jnp.where(m0, xc, 0.0), jnp.where(m0, 0.0, xc)], axis=0)

    n2 = 2 * CHUNK
    ri = lax.broadcasted_iota(jnp.int32, (n2, n2), 0) % CHUNK
    ci = lax.broadcasted_iota(jnp.int32, (n2, n2), 1) % CHUNK
    strict = ri > ci
    incl = ri >= ci
    eye = jnp.where(lax.broadcasted_iota(jnp.int32, (n2, n2), 0) == lax.broadcasted_iota(jnp.int32, (n2, n2), 1),
                    1.0, 0.0)

    units = [(c, p) for c in range(tt // CHUNK) for p in range(w_ // LANES)]
    w_c = {}
    st = {}
    for c in range(tt // CHUNK):
        rs = slice(c * CHUNK, (c + 1) * CHUNK)
        cw_last = cw[c * CHUNK + CHUNK - 1:c * CHUNK + CHUNK, :]
        e_tot = jnp.exp(cw_last - cw[rs, :])
        w_c[c] = jnp.exp(cw_last)
        for p in range(w_ // LANES):
            ls = slice(p * LANES, (p + 1) * LANES)
            a_bd = bd(a_t[rs, ls]).astype(BF16)
            r_bd = bd(r_t[rs, ls])
            b_bd = bd(b_t[rs, ls]).astype(BF16)
            k_bd = bd(k_t[rs, ls]).astype(BF16)
            gm = _dot_nt(jnp.concatenate([a_bd, r_bd.astype(BF16)], axis=0), jnp.concatenate([b_bd, k_bd], axis=0))
            st[c, p] = dict(
                a_bd=a_bd, r_bd=r_bd,
                v_bd=bd(v[rs, ls]).astype(BF16),
                bh_bd=bd(beta[rs, ls] * e_tot[:, ls]).astype(BF16),
                kh_bd=bd(k2[rs, ls] * e_tot[:, ls]).astype(BF16),
                l_ab=jnp.where(strict, gm[0:n2, 0:n2], 0.0),
                l_ak=jnp.where(strict, gm[0:n2, n2:], 0.0).astype(BF16),
                m_rb=jnp.where(incl, gm[n2:, 0:n2], 0.0).astype(BF16),
                m_rk=jnp.where(incl, gm[n2:, n2:], 0.0).astype(BF16))

    pw, tm_ = {}, {}
    for u_ in units:
        l_ab = st[u_]["l_ab"]
        lb = l_ab.astype(BF16)
        pw[u_] = _dot(lb, lb)
        tm_[u_] = eye + l_ab
    for level in range(5):
        for u_ in units:
            pb = pw[u_].astype(BF16)
            if level < 4:
                y = _dot(pb, jnp.concatenate([pb, tm_[u_].astype(BF16)], axis=1))
                pw[u_] = y[:, 0:n2]
                tm_[u_] = tm_[u_] + y[:, n2:]
            else:
                tm_[u_] = tm_[u_] + _dot(pb, tm_[u_].astype(BF16))

    for u_ in units:
        d_ = st[u_]
        lv = _dot(d_["l_ak"], d_["v_bd"])
        au = _dot(tm_[u_].astype(BF16), jnp.concatenate([d_["a_bd"], lv.astype(BF16)], axis=1)).astype(BF16)
        mm_ = _dot(d_["m_rb"], au)
        d_["r_hat"] = (d_["r_bd"] + mm_[:, 0:n2]).astype(BF16)
        d_["o0"] = mm_[:, n2:] + _dot(d_["m_rk"], d_["v_bd"])
        gn = _dot_tn(au, d_["bh_bd"])
        d_["g"] = gn[0:n2].astype(BF16)
        d_["n"] = gn[n2:] + _dot_tn(d_["v_bd"], d_["kh_bd"])

    for c, p in units:
        d_ = st[c, p]
        rs = slice(c * CHUNK, (c + 1) * CHUNK)
        ls = slice(p * LANES, (p + 1) * LANES)
        s_old = s_ref[p]
        sb = s_old.astype(BF16)
        o_bd = _dot_nt(d_["r_hat"], sb) + d_["o0"]
        o_scr[rs, ls] = o_bd[0:CHUNK] + o_bd[CHUNK:]
        s_ref[p] = s_old * w_c[c][:, ls] + _dot(sb, d_["g"]) + d_["n"]

    o = o_scr[...]
    inv_n = 1.0 / HEAD_DIM
    mean = _mm(o, hs, P_HS) * inv_n
    d = o - mean
    var = _mm(d * d, hs, P_HS) * inv_n
    on = d * lax.rsqrt(var + LNX_EPS) * lw_ref[...] + lb_ref[...]
    bonus = _mm(r * k2 * rk_ref[...], hs, P_HS) * v
    o_ref[...] = ((on + bonus) * g).astype(o_ref.dtype)


def _rwkv(yr, b, t, mu, w0, w_lora_up, a0, a_lora_up, g_lora_up, k_k, k_a, r_k, lnx_w, lnx_b):
    tt = RWKV_TT
    nt = t // tt
    w_ = RWKV_WIDTH
    wl = jnp.zeros((LANES, 2 * w_), F32)
    wl = wl.at[:D_DECAY_LORA, :w_].set(w_lora_up).at[D_DECAY_LORA:, w_:].set(a_lora_up)
    head = jnp.arange(w_) // HEAD_DIM
    hs = (head[:, None] == head[None, :]).astype(F32)
    ti = jnp.arange(tt)
    tri = ((ti[:, None] // CHUNK == ti[None, :] // CHUNK) & (ti[:, None] >= ti[None, :])).astype(F32)
    vec = lambda a_, width: a_.reshape(1, width)
    const = lambda shape: pl.BlockSpec(shape, lambda bi, i: (0, 0))
    return pl.pallas_call(
        _rwkv_kernel,
        grid=(b, nt),
        in_specs=[pl.BlockSpec((tt, RWKV_COLS), lambda bi, i: (bi * nt + i, 0)),
                  const((1, RWKV_COLS)), const((1, w_)), const((LANES, 2 * w_)), const((1, w_)),
                  const((D_GATE_LORA, w_)), const((1, w_)), const((1, w_)), const((1, w_)),
                  const((1, w_)), const((1, w_)), const((w_, w_)), const((tt, tt))],
        out_specs=pl.BlockSpec((tt, w_), lambda bi, i: (bi * nt + i, 0)),
        out_shape=jax.ShapeDtypeStruct((b * t, w_), BF16),
        scratch_shapes=[pltpu.VMEM((1, RWKV_COLS), F32),
                        pltpu.VMEM((w_ // LANES, LANES, LANES), F32),
                        pltpu.VMEM((tt, w_), F32)],
        compiler_params=_params("parallel", "arbitrary"),
        name="rwkv7",
    )(yr, vec(mu, RWKV_COLS), vec(w0, w_), wl, vec(a0, w_), g_lora_up, vec(k_k, w_), vec(k_a, w_),
      vec(r_k, w_), vec(lnx_w, w_), vec(lnx_b, w_), hs, tri)


def _compress_kernel(zk_ref, zv_ref, pek_ref, pev_ref, wk1_ref, wv1_ref, wk2_ref, wv2_ref, kc_ref, vc_ref):
    def one(z_ref, pe_ref, w1_ref, w2_ref, out_ref):
        z = z_ref[0]
        nrow = z.shape[0]
        za = z + pe_ref[0:1, :]
        zb = z + pe_ref[1:2, :]
        acc = jnp.zeros((nrow, LANES), F32)
        for gi in range(NSA_KV_HEADS):
            ha = _mm(za, w1_ref[gi, 0], "b3")
            hb = _mm(zb, w1_ref[gi, 1], "b3")
            hid = ha + pltpu.roll(hb, nrow - 1, 0)
            acc = acc + _mm(jax.nn.gelu(hid), w2_ref[gi], "b3")
        out_ref[0] = acc

    one(zk_ref, pek_ref, wk1_ref, wk2_ref, kc_ref)
    one(zv_ref, pev_ref, wv1_ref, wv2_ref, vc_ref)


def _compress_weights(pos, w1, w2):
    half = CMP_BLOCK // 2
    w1r = w1.reshape(2, half, HEAD_DIM, CMP_HIDDEN)
    w1g = jnp.zeros((NSA_KV_HEADS, 2, half, NSA_KV_HEADS, HEAD_DIM, CMP_HIDDEN), F32)
    for gi in range(NSA_KV_HEADS):
        w1g = w1g.at[gi, :, :, gi].set(w1r)
    w1g = w1g.reshape(NSA_KV_HEADS, 2, half * NSA_KV_WIDTH, CMP_HIDDEN)
    w2g = jnp.zeros((NSA_KV_HEADS, CMP_HIDDEN, NSA_KV_HEADS, HEAD_DIM), F32)
    for gi in range(NSA_KV_HEADS):
        w2g = w2g.at[gi, :, gi].set(w2)
    w2g = w2g.reshape(NSA_KV_HEADS, CMP_HIDDEN, NSA_KV_WIDTH)
    pe = jnp.broadcast_to(pos.reshape(2, half, 1, HEAD_DIM), (2, half, NSA_KV_HEADS, HEAD_DIM))
    pe = pe.reshape(2, half * NSA_KV_WIDTH)
    return pe, w1g, w2g


def _compress(kc, vc, b, t, cmp_pos_k, cmp_pos_v, k_w1, k_w2, v_w1, v_w2):
    nrow = t // CMP_STRIDE
    zw = CMP_STRIDE * NSA_KV_WIDTH
    pek, wk1, wk2 = _compress_weights(cmp_pos_k, k_w1, k_w2)
    pev, wv1, wv2 = _compress_weights(cmp_pos_v, v_w1, v_w2)
    zspec = pl.BlockSpec((1, nrow, zw), lambda bi: (bi, 0, 0))
    ospec = pl.BlockSpec((1, nrow, LANES), lambda bi: (bi, 0, 0))
    c2 = lambda shape: pl.BlockSpec(shape, lambda bi: (0,) * len(shape))
    return pl.pallas_call(
        _compress_kernel,
        grid=(b,),
        in_specs=[zspec, zspec, c2((2, zw)), c2((2, zw)), c2(wk1.shape), c2(wv1.shape), c2(wk2.shape), c2(wv2.shape)],
        out_specs=[ospec, ospec],
        out_shape=[jax.ShapeDtypeStruct((b, nrow, LANES), F32)] * 2,
        compiler_params=_params("parallel"),
        name="nsa_compress",
    )(kc.reshape(b, nrow, zw), vc.reshape(b, nrow, zw), pek, pev, wk1, wv1, wk2, wv2)


SEL_TILE = 256
SEL_UNROLL = 2
WIN_TILE = 128


def _nsa_kernel(q_ref, kcmp_ref, vcmp_ref, ks_ref, vs_ref, kw_ref, vw_ref, gl_ref, ovt_ref, exp_ref,
                o_ref, m_scr, l_scr, acc_scr, sc_scr, imp_scr, cnt_scr, sel_scr):
    s = pl.program_id(1)
    qb = Q_BLOCK
    n_head = NSA_Q_HEADS
    rows = n_head * qb
    n_cmp = kcmp_ref.shape[1]
    n_sb = ovt_ref.shape[0]

    tok64 = s * qb + lax.broadcasted_iota(jnp.int32, (qb, LANES), 0)
    lane64 = lax.broadcasted_iota(jnp.int32, (qb, LANES), 1)
    sig = jax.nn.sigmoid(gl_ref[...])

    qq = jnp.concatenate([q_ref[:, hq * LANES:(hq + 1) * LANES] for hq in range(n_head)], axis=0)

    def per_head(x64):
        return jnp.concatenate([x64] * n_head, axis=0)

    def per_group_head(x128):
        return jnp.concatenate([x128[0:qb]] * NSA_HPG + [x128[qb:2 * qb]] * NSA_HPG, axis=0)

    n_win = WINDOW // WIN_TILE + 1
    first = (s * qb) // WIN_TILE + 1 - n_win
    win_sc, win_k0 = [], []
    for j in range(n_win):
        kt = jnp.maximum(first + j, 0)
        k0 = pl.multiple_of(kt * WIN_TILE, WIN_TILE)
        kp = k0 + lane64 + jnp.where(first + j >= 0, 0, 1 << 24)
        bias = per_head(jnp.where((kp <= tok64) & (kp > tok64 - WINDOW), 0.0, NEG_INF))
        win_sc.append(_dot_nt(qq, kw_ref[0, pl.ds(k0, WIN_TILE), :]) + bias)
        win_k0.append(k0)
    mxw = win_sc[0]
    for j in range(1, n_win):
        mxw = jnp.maximum(mxw, win_sc[j])
    m_w = jnp.max(mxw, axis=1, keepdims=True)
    l_w = jnp.zeros((rows, LANES), F32)
    acc_w = jnp.zeros((rows, LANES), F32)
    for j in range(n_win):
        p = jnp.exp(win_sc[j] - m_w)
        l_w = l_w + p
        acc_w = acc_w + _dot(p.astype(BF16), vw_ref[0, pl.ds(win_k0[j], WIN_TILE), :])
    o_win = acc_w / jnp.sum(l_w, axis=1, keepdims=True)

    tok_c = s * qb + lax.broadcasted_iota(jnp.int32, (qb, n_cmp), 0)
    cmp_i = lax.broadcasted_iota(jnp.int32, (qb, n_cmp), 1)
    cbias = per_head(jnp.where((cmp_i * CMP_STRIDE + CMP_BLOCK - 1 <= tok_c) & (cmp_i < n_cmp - 1), 0.0, NEG_INF))
    sc = _mm(qq, kcmp_ref[0], "xa", _dot_nt) + cbias
    mx = jnp.max(sc, axis=1, keepdims=True)
    e = jnp.where(cbias == 0.0, jnp.exp(sc - mx), 0.0)
    den = jnp.sum(e, axis=1, keepdims=True)
    p_c = e / jnp.where(den > 0.0, den, 1.0)
    o_cmp = _dot(p_c.astype(BF16), vcmp_ref[0].astype(BF16))

    pc_sums = []
    for gi in range(NSA_KV_HEADS):
        r0 = gi * NSA_HPG * qb
        acc = p_c[r0:r0 + qb]
        for h in range(1, NSA_HPG):
            acc = acc + p_c[r0 + h * qb:r0 + (h + 1) * qb]
        pc_sums.append(acc)
    pcs = jnp.concatenate(pc_sums, axis=0)
    imp = _dot_nt(ovt_ref[...], pcs, HI)
    blk = lax.broadcasted_iota(jnp.int32, (n_sb, LANES), 0)
    forced = (blk == 0) | (blk == s) | (blk == s - 1)
    imp = jnp.where(blk <= s, imp + jnp.where(forced, FORCE_BONUS, 0.0), NEG_INF)
    n_sel = min(SEL_TOPK, n_sb)
    sel_scr[...] = jnp.where(blk <= s, 1.0, 0.0)

    @pl.when(s >= n_sel)
    def _():
        imp_scr[...] = imp
        cnt_scr[...] = jnp.zeros(cnt_scr.shape, F32)
        sub = lax.broadcasted_iota(jnp.int32, (8, LANES), 0)
        for ig in range(n_sb // 8):
            @pl.when(ig * 8 <= s)
            def _(ig=ig):
                impv = imp_scr[...]
                cnt = [cnt_scr[rg * 8:(rg + 1) * 8, :] for rg in range(n_sb // 8)]
                for i2 in range(ig * 8, ig * 8 + 8):
                    vi = impv[i2:i2 + 1, :]
                    for rg in range(n_sb // 8):
                        vj = impv[rg * 8:(rg + 1) * 8, :]
                        if rg < ig:
                            one = jnp.where(vi > vj, 1.0, 0.0)
                        elif rg > ig:
                            one = jnp.where(vi >= vj, 1.0, 0.0)
                        else:
                            one = jnp.where(sub > i2 - ig * 8, jnp.where(vi >= vj, 1.0, 0.0),
                                            jnp.where(vi > vj, 1.0, 0.0))
                        cnt[rg] = cnt[rg] + one
                for rg in range(n_sb // 8):
                    cnt_scr[rg * 8:(rg + 1) * 8, :] = cnt[rg]
        sel_scr[...] = jnp.where((cnt_scr[...] < n_sel) & (blk <= s), 1.0, 0.0)

    sel = sel_scr[...].T.astype(BF16)

    tok_s = s * qb + lax.broadcasted_iota(jnp.int32, (2 * qb, SEL_TILE), 0) % qb
    lane_s = lax.broadcasted_iota(jnp.int32, (2 * qb, SEL_TILE), 1)
    n_steps = (s * qb) // (SEL_TILE * SEL_UNROLL) + 1

    def pass1(it, mx_):
        for u in range(SEL_UNROLL):
            j = it * SEL_UNROLL + u
            k0 = pl.multiple_of(j * SEL_TILE, SEL_TILE)
            chosen = _dot(sel, exp_ref[j])
            ok = (chosen > 0.5) & (k0 + lane_s <= tok_s)
            sc_ = _dot_nt(qq, ks_ref[0, pl.ds(k0, SEL_TILE), :]) + per_group_head(jnp.where(ok, 0.0, NEG_INF))
            sc_scr[j] = sc_
            for c0 in range(0, SEL_TILE, LANES):
                mx_ = jnp.maximum(mx_, sc_[:, c0:c0 + LANES])
        return mx_

    mx_ = lax.fori_loop(0, n_steps, pass1, jnp.full((rows, LANES), NEG_INF, F32))
    m_scr[...] = jnp.broadcast_to(jnp.max(mx_, axis=1, keepdims=True), m_scr.shape)
    l_scr[...] = jnp.zeros(l_scr.shape, F32)
    acc_scr[...] = jnp.zeros(acc_scr.shape, F32)

    def pass2(it, c):
        for u in range(SEL_UNROLL):
            j = it * SEL_UNROLL + u
            k0 = pl.multiple_of(j * SEL_TILE, SEL_TILE)
            m_b = jnp.concatenate([m_scr[...]] * (SEL_TILE // LANES), axis=1)
            p = jnp.exp(sc_scr[j] - m_b)
            l_new = l_scr[...]
            for c0 in range(0, SEL_TILE, LANES):
                l_new = l_new + p[:, c0:c0 + LANES]
            l_scr[...] = l_new
            acc_scr[...] += _dot(p.astype(BF16), vs_ref[0, pl.ds(k0, SEL_TILE), :])
        return c

    lax.fori_loop(0, n_steps, pass2, 0)
    o_sel = acc_scr[...] / jnp.sum(l_scr[...], axis=1, keepdims=True)

    for hq in range(n_head):
        gi = hq // NSA_HPG
        hr = slice(hq * qb, (hq + 1) * qb)
        col = hq * N_BRANCH
        o_h = (sig[:, col:col + 1] * o_cmp[hr] + sig[:, col + 1:col + 2] * o_sel[hr]
               + sig[:, col + 2:col + 3] * o_win[hr])
        in_group = (lane64 >= gi * HEAD_DIM) & (lane64 < (gi + 1) * HEAD_DIM)
        o_ref[:, hq * LANES:(hq + 1) * LANES] = jnp.where(in_group, o_h, 0.0).astype(o_ref.dtype)


def _nsa_attention(q, k_cmp, v_cmp, ks, vs, kw, vw, gl, b, t):
    ns = t // Q_BLOCK
    n_sb = t // SEL_BLOCK
    n_cmp = t // CMP_STRIDE
    n_kt = t // SEL_TILE
    cmp_start = jnp.arange(n_cmp) * CMP_STRIDE
    sb = jnp.arange(n_sb)
    ovt = ((cmp_start[None, :] < (sb[:, None] + 1) * SEL_BLOCK)
           & (cmp_start[None, :] + CMP_BLOCK > sb[:, None] * SEL_BLOCK)
           & (jnp.arange(n_cmp)[None, :] < n_cmp - 1)).astype(F32)
    key_blk = (jnp.arange(n_kt)[:, None] * SEL_TILE + jnp.arange(SEL_TILE)[None, :]) // SEL_BLOCK
    expand = (key_blk[:, None, :] == sb[None, :, None]).astype(BF16)
    rows = NSA_Q_HEADS * Q_BLOCK
    qspec = pl.BlockSpec((Q_BLOCK, Q_PAD_COLS), lambda bi, si: (bi * ns + si, 0))
    seq = lambda: pl.BlockSpec((1, t, LANES), lambda bi, si: (bi, 0, 0))
    cmp_spec = lambda: pl.BlockSpec((1, n_cmp, LANES), lambda bi, si: (bi, 0, 0))
    k3 = lambda a_: a_.reshape(b, t, LANES)
    return pl.pallas_call(
        _nsa_kernel,
        grid=(b, ns),
        in_specs=[qspec, cmp_spec(), cmp_spec(), seq(), seq(), seq(), seq(),
                  pl.BlockSpec((Q_BLOCK, LANES), lambda bi, si: (bi * ns + si, 0)),
                  pl.BlockSpec((n_sb, n_cmp), lambda bi, si: (0, 0)),
                  pl.BlockSpec((n_kt, n_sb, SEL_TILE), lambda bi, si: (0, 0, 0))],
        out_specs=pl.BlockSpec((Q_BLOCK, Q_PAD_COLS), lambda bi, si: (bi * ns + si, 0)),
        out_shape=jax.ShapeDtypeStruct((b * t, Q_PAD_COLS), BF16),
        scratch_shapes=[pltpu.VMEM((rows, LANES), F32),
                        pltpu.VMEM((rows, LANES), F32),
                        pltpu.VMEM((rows, LANES), F32),
                        pltpu.VMEM((n_kt, rows, SEL_TILE), F32),
                        pltpu.VMEM((n_sb, LANES), F32),
                        pltpu.VMEM((n_sb, LANES), F32),
                        pltpu.VMEM((n_sb, LANES), F32)],
        compiler_params=_params("parallel", "arbitrary"),
        name="nsa_attention",
    )(q, k_cmp, v_cmp, k3(ks), k3(vs), k3(kw), k3(vw), gl, ovt, expand)


FF_TILE = 256


def _ffn_kernel(x_ref, orw_ref, ons_ref, wor_ref, won_ref, n2_ref, w1_ref, w3_ref, w2_ref, nf_ref,
                out_ref, h_scr, u_scr, acc_scr):
    j = pl.program_id(1)

    @pl.when(j == 0)
    def _():
        h = x_ref[...] + _dot(orw_ref[...], wor_ref[...]) + _dot(ons_ref[...], won_ref[...])
        h_scr[...] = h
        ms = jnp.mean(h * h, axis=-1, keepdims=True)
        u_scr[...] = (h * lax.rsqrt(ms + RMS_EPS) * n2_ref[...]).astype(BF16)
        acc_scr[...] = jnp.zeros_like(acc_scr)

    u = u_scr[...]
    gate = _dot(u, w1_ref[...])
    up = _dot(u, w3_ref[...])
    act = (jax.nn.silu(gate) * up).astype(BF16)
    acc_scr[...] += _dot(act, w2_ref[...])

    @pl.when(j == pl.num_programs(1) - 1)
    def _():
        h = h_scr[...] + acc_scr[...]
        ms = jnp.mean(h * h, axis=-1, keepdims=True)
        out_ref[...] = h * lax.rsqrt(ms + RMS_EPS) * nf_ref[...]


def _out_ffn(x2, o_rwkv, o_nsa, w_out, norm2_w, ffn_w1, ffn_w3, ffn_w2, final_norm_w):
    n = x2.shape[0]
    tm = 1024
    w_or = w_out[:RWKV_WIDTH].astype(BF16)
    w_n = w_out[RWKV_WIDTH:].reshape(NSA_KV_HEADS, NSA_HPG, HEAD_DIM, D_MODEL)
    w_on = jnp.zeros((NSA_KV_HEADS, NSA_HPG, NSA_KV_HEADS, HEAD_DIM, D_MODEL), F32)
    for gi in range(NSA_KV_HEADS):
        w_on = w_on.at[gi, :, gi].set(w_n[gi])
    w_on = w_on.reshape(Q_PAD_COLS, D_MODEL).astype(BF16)
    row = lambda width: pl.BlockSpec((tm, width), lambda i, j: (i, 0))
    const = lambda shape: pl.BlockSpec(shape, lambda i, j: (0, 0))
    return pl.pallas_call(
        _ffn_kernel,
        grid=(n // tm, D_FF // FF_TILE),
        in_specs=[row(D_MODEL), row(RWKV_WIDTH), row(Q_PAD_COLS),
                  const((RWKV_WIDTH, D_MODEL)), const((Q_PAD_COLS, D_MODEL)), const((1, D_MODEL)),
                  pl.BlockSpec((D_MODEL, FF_TILE), lambda i, j: (0, j)),
                  pl.BlockSpec((D_MODEL, FF_TILE), lambda i, j: (0, j)),
                  pl.BlockSpec((FF_TILE, D_MODEL), lambda i, j: (j, 0)),
                  const((1, D_MODEL))],
        out_specs=row(D_MODEL),
        out_shape=jax.ShapeDtypeStruct((n, D_MODEL), F32),
        scratch_shapes=[pltpu.VMEM((tm, D_MODEL), F32), pltpu.VMEM((tm, D_MODEL), BF16),
                        pltpu.VMEM((tm, D_MODEL), F32)],
        compiler_params=_params("parallel", "arbitrary"),
        name="out_ffn",
    )(x2, o_rwkv, o_nsa, w_or, w_on, norm2_w.reshape(1, D_MODEL), ffn_w1.astype(BF16), ffn_w3.astype(BF16),
      ffn_w2.astype(BF16), final_norm_w.reshape(1, D_MODEL))


def kernel(x, positions, norm1_w, w_in, mu_rwkv, w0, w_lora_up, a0, a_lora_up, g_lora_up, k_k, k_a, r_k, lnx_w, lnx_b, cmp_pos_k, cmp_pos_v, cmp_k_w1, cmp_k_w2, cmp_v_w1, cmp_v_w2, w_out, norm2_w, ffn_w1, ffn_w3, ffn_w2, final_norm_w):
    b, t, d = x.shape
    assert d == D_MODEL and norm1_w.shape[0] == 1, "single-layer block with d_model 1024"
    assert t % RWKV_TT == 0 and t % (SEL_UNROLL * SEL_TILE) == 0 and t // SEL_BLOCK <= LANES
    x2 = x.reshape(b * t, d)
    cos_t, sin_t = _rope_tables(positions)
    yr, q, kc, vc, ks, vs, kw, vw, gl = _in_proj(x2, norm1_w[0], _pack_w_in(w_in[0]), cos_t, sin_t)
    o_rwkv = _rwkv(yr, b, t, mu_rwkv[0], w0[0], w_lora_up[0], a0[0], a_lora_up[0], g_lora_up[0],
                   k_k[0], k_a[0], r_k[0], lnx_w[0], lnx_b[0])
    k_cmp, v_cmp = _compress(kc, vc, b, t, cmp_pos_k[0], cmp_pos_v[0], cmp_k_w1[0], cmp_k_w2[0],
                             cmp_v_w1[0], cmp_v_w2[0])
    o_nsa = _nsa_attention(q, k_cmp, v_cmp, ks, vs, kw, vw, gl, b, t)
    out = _out_ffn(x2, o_rwkv, o_nsa, w_out[0], norm2_w[0], ffn_w1[0], ffn_w3[0], ffn_w2[0], final_norm_w)
    return out.reshape(b, t, d)
```

```python
import functools

import jax
import jax.numpy as jnp
from jax import lax
from jax.experimental import pallas as pl
from jax.experimental.pallas import tpu as pltpu

F32 = jnp.float32
BF16 = jnp.bfloat16
HI = lax.Precision.HIGHEST

D_MODEL = 1024
HEAD_DIM = 64
RWKV_WIDTH = 512
RWKV_HEADS = 8
D_DECAY_LORA = 64
D_AAA_LORA = 64
D_GATE_LORA = 128
RWKV_COLS = 3 * RWKV_WIDTH + D_DECAY_LORA + D_AAA_LORA + D_GATE_LORA
NSA_WIDTH = 512
NSA_Q_HEADS = 8
NSA_KV_HEADS = 2
NSA_HPG = 4
NSA_KV_WIDTH = 128
N_BRANCH = 3
CMP_BLOCK = 32
CMP_STRIDE = 16
CMP_HIDDEN = 256
SEL_BLOCK = 64
SEL_TOPK = 16
WINDOW = 512
Q_BLOCK = 64
ROPE_THETA = 10000.0
D_FF = 2816
RMS_EPS = 1e-6
LNX_EPS = 64e-5
FORCE_BONUS = 1e4
NEG_INF = -1e30

LANES = 128
CHUNK = 64
Q_PAD_COLS = NSA_Q_HEADS * LANES
PROJ_COLS = RWKV_COLS + Q_PAD_COLS + 6 * NSA_KV_WIDTH + LANES
VMEM_LIMIT = 56 * 1024 * 1024


def _dot(a, b, prec=None):
    return jnp.dot(a, b, preferred_element_type=F32, precision=prec)


def _dot_nt(a, b, prec=None):
    return lax.dot_general(a, b, (((1,), (1,)), ((), ())), preferred_element_type=F32, precision=prec)


def _dot_tn(a, b, prec=None):
    return lax.dot_general(a, b, (((0,), (0,)), ((), ())), preferred_element_type=F32, precision=prec)


def _split(a):
    hi = a.astype(BF16)
    return hi, (a - hi.astype(F32)).astype(BF16)


def _mm(a, b, mode, dot=_dot):
    if mode == "hi":
        return dot(a, b, HI)
    if mode == "b1":
        return dot(a.astype(BF16), b.astype(BF16))
    if mode == "xa":
        bh, bl = _split(b)
        ab = a.astype(BF16)
        return dot(ab, bh) + dot(ab, bl)
    if mode == "xb":
        ah, al = _split(a)
        bb = b.astype(BF16)
        return dot(ah, bb) + dot(al, bb)
    ah, al = _split(a)
    bh, bl = _split(b)
    return dot(ah, bh) + (dot(ah, bl) + dot(al, bh))


def _params(*sem):
    return pltpu.CompilerParams(dimension_semantics=sem, vmem_limit_bytes=VMEM_LIMIT)


def _rope_table_kernel(pos_ref, invf_ref, sign_ref, cos_ref, sin_ref):
    ang = pos_ref[...].astype(F32) * invf_ref[...]
    cos_ref[...] = jnp.cos(ang)
    sin_ref[...] = jnp.sin(ang) * sign_ref[...]


def _rope_tables(positions):
    n = positions.size
    tm = 512
    half = HEAD_DIM // 2
    inv_freq = ROPE_THETA ** (-jnp.arange(half, dtype=F32) / half)
    invf = jnp.tile(inv_freq, LANES // half).reshape(1, LANES)
    lane = jnp.arange(LANES)
    sign = jnp.where((lane % HEAD_DIM) < half, -1.0, 1.0).astype(F32).reshape(1, LANES)
    row = pl.BlockSpec((tm, LANES), lambda i: (i, 0))
    const = pl.BlockSpec((1, LANES), lambda i: (0, 0))
    return pl.pallas_call(
        _rope_table_kernel,
        grid=(n // tm,),
        in_specs=[pl.BlockSpec((tm, 1), lambda i: (i, 0)), const, const],
        out_specs=[row, row],
        out_shape=[jax.ShapeDtypeStruct((n, LANES), F32)] * 2,
        compiler_params=_params("parallel"),
        name="rope_tables",
    )(positions.reshape(n, 1), invf, sign)


def _proj_kernel(x_ref, n1_ref, w_ref, cos_ref, sin_ref,
                 yr_ref, q_ref, kc_ref, vc_ref, ks_ref, vs_ref, kw_ref, vw_ref, gl_ref):
    x = x_ref[...]
    ms = jnp.mean(x * x, axis=-1, keepdims=True)
    xn = (x * lax.rsqrt(ms + RMS_EPS) * n1_ref[...]).astype(BF16)
    cos = cos_ref[...]
    sin = sin_ref[...]
    lane = lax.broadcasted_iota(jnp.int32, cos.shape, 1)
    first_half = (lane % HEAD_DIM) < (HEAD_DIM // 2)

    def rope(v):
        rot = jnp.where(first_half, pltpu.roll(v, LANES - HEAD_DIM // 2, 1), pltpu.roll(v, HEAD_DIM // 2, 1))
        return v * cos + rot * sin

    def cols(c0, width):
        return _dot(xn, w_ref[:, c0:c0 + width])

    for j in range(RWKV_COLS // 256):
        yr_ref[:, j * 256:(j + 1) * 256] = cols(j * 256, 256)
    scale = HEAD_DIM ** -0.5
    for hq in range(0, NSA_Q_HEADS, 2):
        q2 = cols(RWKV_COLS + hq * LANES, 2 * LANES)
        for e in range(2):
            q = rope(q2[:, e * LANES:(e + 1) * LANES]) * scale
            q_ref[:, (hq + e) * LANES:(hq + e + 1) * LANES] = q.astype(BF16)
    c0 = RWKV_COLS + Q_PAD_COLS
    kv = cols(c0, 2 * LANES)
    kc_ref[...] = rope(kv[:, :LANES])
    vc_ref[...] = kv[:, LANES:]
    kv = cols(c0 + 2 * LANES, 2 * LANES)
    ks_ref[...] = rope(kv[:, :LANES]).astype(BF16)
    vs_ref[...] = kv[:, LANES:].astype(BF16)
    kv = cols(c0 + 4 * LANES, 2 * LANES)
    kw_ref[...] = rope(kv[:, :LANES]).astype(BF16)
    vw_ref[...] = kv[:, LANES:].astype(BF16)
    gl_ref[...] = cols(c0 + 6 * LANES, LANES)


def _pack_w_in(w_in):
    w_r = w_in[:, :RWKV_COLS]
    w_q = w_in[:, RWKV_COLS:RWKV_COLS + NSA_WIDTH].reshape(D_MODEL, NSA_KV_HEADS, NSA_HPG, HEAD_DIM)
    zero = jnp.zeros_like(w_q)
    w_q = jnp.stack([jnp.where(jnp.arange(NSA_KV_HEADS)[None, :, None, None] == gg, w_q, zero)
                     for gg in range(NSA_KV_HEADS)], axis=3)
    w_q = w_q.reshape(D_MODEL, Q_PAD_COLS)
    c0 = RWKV_COLS + NSA_WIDTH
    w_kv = w_in[:, c0:c0 + 6 * NSA_KV_WIDTH]
    w_g = w_in[:, c0 + 6 * NSA_KV_WIDTH:]
    w_g = jnp.pad(w_g, ((0, 0), (0, LANES - w_g.shape[1])))
    return jnp.concatenate([w_r, w_q, w_kv, w_g], axis=1).astype(BF16)


def _in_proj(x2, norm1_w, w_packed, cos_t, sin_t):
    n = x2.shape[0]
    tm = 512
    row = lambda width: pl.BlockSpec((tm, width), lambda i: (i, 0))
    const = lambda shape: pl.BlockSpec(shape, lambda i: (0, 0))
    out_shape = [
        jax.ShapeDtypeStruct((n, RWKV_COLS), F32),
        jax.ShapeDtypeStruct((n, Q_PAD_COLS), BF16),
        jax.ShapeDtypeStruct((n, LANES), F32),
        jax.ShapeDtypeStruct((n, LANES), F32),
        jax.ShapeDtypeStruct((n, LANES), BF16),
        jax.ShapeDtypeStruct((n, LANES), BF16),
        jax.ShapeDtypeStruct((n, LANES), BF16),
        jax.ShapeDtypeStruct((n, LANES), BF16),
        jax.ShapeDtypeStruct((n, LANES), F32),
    ]
    out_specs = [row(RWKV_COLS), row(Q_PAD_COLS)] + [row(LANES)] * 7
    return pl.pallas_call(
        _proj_kernel,
        grid=(n // tm,),
        in_specs=[row(D_MODEL), const((1, D_MODEL)), const((D_MODEL, PROJ_COLS)), row(LANES), row(LANES)],
        out_specs=out_specs,
        out_shape=out_shape,
        compiler_params=_params("parallel"),
        name="in_proj",
    )(x2, norm1_w.reshape(1, D_MODEL), w_packed, cos_t, sin_t)


RWKV_TT = 256
P_LORA = "b3"
P_HS = "xb"
P_CUM = "xa"


def _rwkv_kernel(y_ref, mu_ref, w0_ref, wl_ref, a0_ref, gup_ref, kk_ref, ka_ref, rk_ref,
                 lw_ref, lb_ref, hs_ref, tri_ref, o_ref, prev_ref, s_ref, o_scr):
    i = pl.program_id(1)

    @pl.when(i == 0)
    def _():
        prev_ref[...] = jnp.zeros_like(prev_ref)
        s_ref[...] = jnp.zeros_like(s_ref)

    tt = y_ref.shape[0]
    y = y_ref[...]
    row = lax.broadcasted_iota(jnp.int32, (tt, 1), 0)
    y_prev = jnp.where(row == 0, prev_ref[...], pltpu.roll(y, 1, 0))
    prev_ref[...] = y[tt - 1:tt, :]
    ys = y + (y_prev - y) * mu_ref[...]

    w_ = RWKV_WIDTH
    r = ys[:, 0:w_]
    k = ys[:, w_:2 * w_]
    v = ys[:, 2 * w_:3 * w_]
    z = ys[:, 3 * w_:3 * w_ + LANES]
    gd = ys[:, 3 * w_ + LANES:3 * w_ + 2 * LANES]
    lane = lax.broadcasted_iota(jnp.int32, z.shape, 1)
    zt = jnp.where(lane < D_DECAY_LORA, jnp.tanh(z), z)
    wa = _mm(zt, wl_ref[...], P_LORA)
    w_raw = w0_ref[...] + wa[:, :w_]
    a = jax.nn.sigmoid(a0_ref[...] + wa[:, w_:])
    logw = -jnp.exp(-jax.nn.softplus(-w_raw) - 0.5)
    g = _mm(jax.nn.sigmoid(gd), gup_ref[...], P_LORA)
    hs = hs_ref[...]
    kk = k * kk_ref[...]
    kk = kk * lax.rsqrt(jnp.maximum(_mm(kk * kk, hs, P_HS), 1e-12))
    k2 = k * (1.0 + (a - 1.0) * ka_ref[...])
    alpha = -kk
    beta = kk * a

    cw = _mm(tri_ref[...], logw, P_CUM)
    e_in = jnp.exp(cw)
    e_ex = jnp.exp(cw - logw)
    e_neg = jnp.exp(-cw)
    a_t = alpha * e_ex
    r_t = r * e_in
    b_t = beta * e_neg
    k_t = k2 * e_neg

    m0 = lax.broadcasted_iota(jnp.int32, (CHUNK, LANES), 1) < HEAD_DIM

    def bd(xc):
        return jnp.concatenate([jnp.where(m0, xc, 0.0), jnp.where(m0, 0.0, xc)], axis=0)

    n2 = 2 * CHUNK
    ri = lax.broadcasted_iota(jnp.int32, (n2, n2), 0) % CHUNK
    ci = lax.broadcasted_iota(jnp.int32, (n2, n2), 1) % CHUNK
    strict = ri > ci
    incl = ri >= ci
    eye = jnp.where(lax.broadcasted_iota(jnp.int32, (n2, n2), 0) == lax.broadcasted_iota(jnp.int32, (n2, n2), 1),
                    1.0, 0.0)

    units = [(c, p) for c in range(tt // CHUNK) for p in range(w_ // LANES)]
    w_c = {}
    st = {}
    for c in range(tt // CHUNK):
        rs = slice(c * CHUNK, (c + 1) * CHUNK)
        cw_last = cw[c * CHUNK + CHUNK - 1:c * CHUNK + CHUNK, :]
        e_tot = jnp.exp(cw_last - cw[rs, :])
        w_c[c] = jnp.exp(cw_last)
        for p in range(w_ // LANES):
            ls = slice(p * LANES, (p + 1) * LANES)
            a_bd = bd(a_t[rs, ls]).astype(BF16)
            r_bd = bd(r_t[rs, ls])
            b_bd = bd(b_t[rs, ls]).astype(BF16)
            k_bd = bd(k_t[rs, ls]).astype(BF16)
            gm = _dot_nt(jnp.concatenate([a_bd, r_bd.astype(BF16)], axis=0), jnp.concatenate([b_bd, k_bd], axis=0))
            st[c, p] = dict(
                a_bd=a_bd, r_bd=r_bd,
                v_bd=bd(v[rs, ls]).astype(BF16),
                bh_bd=bd(beta[rs, ls] * e_tot[:, ls]).astype(BF16),
                kh_bd=bd(k2[rs, ls] * e_tot[:, ls]).astype(BF16),
                l_ab=jnp.where(strict, gm[0:n2, 0:n2], 0.0),
                l_ak=jnp.where(strict, gm[0:n2, n2:], 0.0).astype(BF16),
                m_rb=jnp.where(incl, gm[n2:, 0:n2], 0.0).astype(BF16),
                m_rk=jnp.where(incl, gm[n2:, n2:], 0.0).astype(BF16))

    pw, tm_ = {}, {}
    for u_ in units:
        l_ab = st[u_]["l_ab"]
        lb = l_ab.astype(BF16)
        pw[u_] = _dot(lb, lb)
        tm_[u_] = eye + l_ab
    for level in range(5):
        for u_ in units:
            pb = pw[u_].astype(BF16)
            if level < 4:
                y = _dot(pb, jnp.concatenate([pb, tm_[u_].astype(BF16)], axis=1))
                pw[u_] = y[:, 0:n2]
                tm_[u_] = tm_[u_] + y[:, n2:]
            else:
                tm_[u_] = tm_[u_] + _dot(pb, tm_[u_].astype(BF16))

    zero_bd = jnp.zeros((n2, n2), BF16)
    lv = {u_: _dot(st[u_]["l_ak"], st[u_]["v_bd"]).astype(BF16) for u_ in units}
    au = {u_: _dot(tm_[u_].astype(BF16), jnp.concatenate([st[u_]["a_bd"], lv[u_]], axis=1)).astype(BF16)
          for u_ in units}
    for u_ in units:
        d_ = st[u_]
        mo = _dot(jnp.concatenate([d_["m_rb"], d_["m_rk"]], axis=1),
                  jnp.concatenate([au[u_], jnp.concatenate([zero_bd, d_["v_bd"]], axis=1)], axis=0))
        d_["r_hat"] = (d_["r_bd"] + mo[:, 0:n2]).astype(BF16)
        d_["o0"] = mo[:, n2:]
    for u_ in units:
        st[u_]["g"] = _dot_tn(au[u_][:, 0:n2], st[u_]["bh_bd"]).astype(BF16)
    for u_ in units:
        d_ = st[u_]
        d_["n"] = _dot_tn(jnp.concatenate([au[u_][:, n2:], d_["v_bd"]], axis=0),
                          jnp.concatenate([d_["bh_bd"], d_["kh_bd"]], axis=0))

    for c, p in units:
        d_ = st[c, p]
        rs = slice(c * CHUNK, (c + 1) * CHUNK)
        ls = slice(p * LANES, (p + 1) * LANES)
        s_old = s_ref[p]
        sb = s_old.astype(BF16)
        o_bd = _dot_nt(d_["r_hat"], sb) + d_["o0"]
        o_scr[rs, ls] = o_bd[0:CHUNK] + o_bd[CHUNK:]
        s_ref[p] = s_old * w_c[c][:, ls] + _dot(sb, d_["g"]) + d_["n"]

    o = o_scr[...]
    inv_n = 1.0 / HEAD_DIM
    mean = _mm(o, hs, P_HS) * inv_n
    d = o - mean
    var = _mm(d * d, hs, P_HS) * inv_n
    on = d * lax.rsqrt(var + LNX_EPS) * lw_ref[...] + lb_ref[...]
    bonus = _mm(r * k2 * rk_ref[...], hs, P_HS) * v
    o_ref[...] = ((on + bonus) * g).astype(o_ref.dtype)


def _rwkv(yr, b, t, mu, w0, w_lora_up, a0, a_lora_up, g_lora_up, k_k, k_a, r_k, lnx_w, lnx_b):
    tt = RWKV_TT
    nt = t // tt
    w_ = RWKV_WIDTH
    wl = jnp.zeros((LANES, 2 * w_), F32)
    wl = wl.at[:D_DECAY_LORA, :w_].set(w_lora_up).at[D_DECAY_LORA:, w_:].set(a_lora_up)
    head = jnp.arange(w_) // HEAD_DIM
    hs = (head[:, None] == head[None, :]).astype(F32)
    ti = jnp.arange(tt)
    tri = ((ti[:, None] // CHUNK == ti[None, :] // CHUNK) & (ti[:, None] >= ti[None, :])).astype(F32)
    vec = lambda a_, width: a_.reshape(1, width)
    const = lambda shape: pl.BlockSpec(shape, lambda bi, i: (0, 0))
    return pl.pallas_call(
        _rwkv_kernel,
        grid=(b, nt),
        in_specs=[pl.BlockSpec((tt, RWKV_COLS), lambda bi, i: (bi * nt + i, 0)),
                  const((1, RWKV_COLS)), const((1, w_)), const((LANES, 2 * w_)), const((1, w_)),
                  const((D_GATE_LORA, w_)), const((1, w_)), const((1, w_)), const((1, w_)),
                  const((1, w_)), const((1, w_)), const((w_, w_)), const((tt, tt))],
        out_specs=pl.BlockSpec((tt, w_), lambda bi, i: (bi * nt + i, 0)),
        out_shape=jax.ShapeDtypeStruct((b * t, w_), BF16),
        scratch_shapes=[pltpu.VMEM((1, RWKV_COLS), F32),
                        pltpu.VMEM((w_ // LANES, LANES, LANES), F32),
                        pltpu.VMEM((tt, w_), F32)],
        compiler_params=_params("parallel", "arbitrary"),
        name="rwkv7",
    )(yr, vec(mu, RWKV_COLS), vec(w0, w_), wl, vec(a0, w_), g_lora_up, vec(k_k, w_), vec(k_a, w_),
      vec(r_k, w_), vec(lnx_w, w_), vec(lnx_b, w_), hs, tri)


def _compress_kernel(zk_ref, zv_ref, pek_ref, pev_ref, wk1_ref, wv1_ref, wk2_ref, wv2_ref, kc_ref, vc_ref):
    def one(z_ref, pe_ref, w1_ref, w2_ref, out_ref):
        z = z_ref[0]
        nrow = z.shape[0]
        za = z + pe_ref[0:1, :]
        zb = z + pe_ref[1:2, :]
        acc = jnp.zeros((nrow, LANES), F32)
        for gi in range(NSA_KV_HEADS):
            ha = _mm(za, w1_ref[gi, 0], "b3")
            hb = _mm(zb, w1_ref[gi, 1], "b3")
            hid = ha + pltpu.roll(hb, nrow - 1, 0)
            acc = acc + _mm(jax.nn.gelu(hid), w2_ref[gi], "b3")
        out_ref[0] = acc

    one(zk_ref, pek_ref, wk1_ref, wk2_ref, kc_ref)
    one(zv_ref, pev_ref, wv1_ref, wv2_ref, vc_ref)


def _compress_weights(pos, w1, w2):
    half = CMP_BLOCK // 2
    w1r = w1.reshape(2, half, HEAD_DIM, CMP_HIDDEN)
    w1g = jnp.zeros((NSA_KV_HEADS, 2, half, NSA_KV_HEADS, HEAD_DIM, CMP_HIDDEN), F32)
    for gi in range(NSA_KV_HEADS):
        w1g = w1g.at[gi, :, :, gi].set(w1r)
    w1g = w1g.reshape(NSA_KV_HEADS, 2, half * NSA_KV_WIDTH, CMP_HIDDEN)
    w2g = jnp.zeros((NSA_KV_HEADS, CMP_HIDDEN, NSA_KV_HEADS, HEAD_DIM), F32)
    for gi in range(NSA_KV_HEADS):
        w2g = w2g.at[gi, :, gi].set(w2)
    w2g = w2g.reshape(NSA_KV_HEADS, CMP_HIDDEN, NSA_KV_WIDTH)
    pe = jnp.broadcast_to(pos.reshape(2, half, 1, HEAD_DIM), (2, half, NSA_KV_HEADS, HEAD_DIM))
    pe = pe.reshape(2, half * NSA_KV_WIDTH)
    return pe, w1g, w2g


def _compress(kc, vc, b, t, cmp_pos_k, cmp_pos_v, k_w1, k_w2, v_w1, v_w2):
    nrow = t // CMP_STRIDE
    zw = CMP_STRIDE * NSA_KV_WIDTH
    pek, wk1, wk2 = _compress_weights(cmp_pos_k, k_w1, k_w2)
    pev, wv1, wv2 = _compress_weights(cmp_pos_v, v_w1, v_w2)
    zspec = pl.BlockSpec((1, nrow, zw), lambda bi: (bi, 0, 0))
    ospec = pl.BlockSpec((1, nrow, LANES), lambda bi: (bi, 0, 0))
    c2 = lambda shape: pl.BlockSpec(shape, lambda bi: (0,) * len(shape))
    return pl.pallas_call(
        _compress_kernel,
        grid=(b,),
        in_specs=[zspec, zspec, c2((2, zw)), c2((2, zw)), c2(wk1.shape), c2(wv1.shape), c2(wk2.shape), c2(wv2.shape)],
        out_specs=[ospec, ospec],
        out_shape=[jax.ShapeDtypeStruct((b, nrow, LANES), F32)] * 2,
        compiler_params=_params("parallel"),
        name="nsa_compress",
    )(kc.reshape(b, nrow, zw), vc.reshape(b, nrow, zw), pek, pev, wk1, wv1, wk2, wv2)


SEL_TILE = 256
SEL_UNROLL = 4
WIN_TILE = 128


def _nsa_kernel(q_ref, kcmp_ref, vcmp_ref, ks_ref, vs_ref, kw_ref, vw_ref, gl_ref, ovt_ref, exp_ref,
                o_ref, m_scr, l_scr, acc_scr, sc_scr, imp_scr, cnt_scr, sel_scr):
    s = pl.program_id(1)
    qb = Q_BLOCK
    n_head = NSA_Q_HEADS
    rows = n_head * qb
    n_cmp = kcmp_ref.shape[1]
    n_sb = ovt_ref.shape[0]

    tok64 = s * qb + lax.broadcasted_iota(jnp.int32, (qb, LANES), 0)
    lane64 = lax.broadcasted_iota(jnp.int32, (qb, LANES), 1)
    sig = jax.nn.sigmoid(gl_ref[...])

    qq = jnp.concatenate([q_ref[:, hq * LANES:(hq + 1) * LANES] for hq in range(n_head)], axis=0)

    def per_head(x64):
        return jnp.concatenate([x64] * n_head, axis=0)

    def per_group_head(x128):
        return jnp.concatenate([x128[0:qb]] * NSA_HPG + [x128[qb:2 * qb]] * NSA_HPG, axis=0)

    n_win = WINDOW // WIN_TILE + 1
    first = (s * qb) // WIN_TILE + 1 - n_win
    tok_w = s * qb + lax.broadcasted_iota(jnp.int32, (qb, WIN_TILE), 0)
    lane_w = lax.broadcasted_iota(jnp.int32, (qb, WIN_TILE), 1)
    win_sc, win_k0 = [], []
    mxw = jnp.full((rows, LANES), NEG_INF, F32)
    for j in range(n_win):
        kt = jnp.maximum(first + j, 0)
        k0 = pl.multiple_of(kt * WIN_TILE, WIN_TILE)
        kp = k0 + lane_w + jnp.where(first + j >= 0, 0, 1 << 24)
        bias = per_head(jnp.where((kp <= tok_w) & (kp > tok_w - WINDOW), 0.0, NEG_INF))
        sc_ = _dot_nt(qq, kw_ref[0, pl.ds(k0, WIN_TILE), :]) + bias
        for c0 in range(0, WIN_TILE, LANES):
            mxw = jnp.maximum(mxw, sc_[:, c0:c0 + LANES])
        win_sc.append(sc_)
        win_k0.append(k0)
    m_w = jnp.max(mxw, axis=1, keepdims=True)
    l_w = jnp.zeros((rows, LANES), F32)
    acc_w = jnp.zeros((rows, LANES), F32)
    for j in range(n_win):
        p = jnp.exp(win_sc[j] - m_w)
        for c0 in range(0, WIN_TILE, LANES):
            l_w = l_w + p[:, c0:c0 + LANES]
        acc_w = acc_w + _dot(p.astype(BF16), vw_ref[0, pl.ds(win_k0[j], WIN_TILE), :])
    o_win = acc_w / jnp.sum(l_w, axis=1, keepdims=True)

    tok_c = s * qb + lax.broadcasted_iota(jnp.int32, (qb, n_cmp), 0)
    cmp_i = lax.broadcasted_iota(jnp.int32, (qb, n_cmp), 1)
    cbias = per_head(jnp.where((cmp_i * CMP_STRIDE + CMP_BLOCK - 1 <= tok_c) & (cmp_i < n_cmp - 1), 0.0, NEG_INF))
    sc = _mm(qq, kcmp_ref[0], "xa", _dot_nt) + cbias
    mx = jnp.max(sc, axis=1, keepdims=True)
    e = jnp.where(cbias == 0.0, jnp.exp(sc - mx), 0.0)
    den = jnp.sum(e, axis=1, keepdims=True)
    p_c = e / jnp.where(den > 0.0, den, 1.0)
    o_cmp = _dot(p_c.astype(BF16), vcmp_ref[0].astype(BF16))

    pc_sums = []
    for gi in range(NSA_KV_HEADS):
        r0 = gi * NSA_HPG * qb
        acc = p_c[r0:r0 + qb]
        for h in range(1, NSA_HPG):
            acc = acc + p_c[r0 + h * qb:r0 + (h + 1) * qb]
        pc_sums.append(acc)
    pcs = jnp.concatenate(pc_sums, axis=0)
    imp = _dot_nt(ovt_ref[...], pcs, HI)
    blk = lax.broadcasted_iota(jnp.int32, (n_sb, LANES), 0)
    forced = (blk == 0) | (blk == s) | (blk == s - 1)
    imp = jnp.where(blk <= s, imp + jnp.where(forced, FORCE_BONUS, 0.0), NEG_INF)
    n_sel = min(SEL_TOPK, n_sb)
    sel_scr[...] = jnp.where(blk <= s, 1.0, 0.0)

    @pl.when(s >= n_sel)
    def _():
        imp_scr[...] = imp
        cnt_scr[...] = jnp.zeros(cnt_scr.shape, F32)
        sub = lax.broadcasted_iota(jnp.int32, (8, LANES), 0)
        for ig in range(n_sb // 8):
            @pl.when(ig * 8 <= s)
            def _(ig=ig):
                impv = imp_scr[...]
                cnt = [cnt_scr[rg * 8:(rg + 1) * 8, :] for rg in range(n_sb // 8)]
                for i2 in range(ig * 8, ig * 8 + 8):
                    vi = impv[i2:i2 + 1, :]
                    for rg in range(n_sb // 8):
                        vj = impv[rg * 8:(rg + 1) * 8, :]
                        if rg < ig:
                            one = jnp.where(vi > vj, 1.0, 0.0)
                        elif rg > ig:
                            one = jnp.where(vi >= vj, 1.0, 0.0)
                        else:
                            one = jnp.where(sub > i2 - ig * 8, jnp.where(vi >= vj, 1.0, 0.0),
                                            jnp.where(vi > vj, 1.0, 0.0))
                        cnt[rg] = cnt[rg] + one
                for rg in range(n_sb // 8):
                    cnt_scr[rg * 8:(rg + 1) * 8, :] = cnt[rg]
        sel_scr[...] = jnp.where((cnt_scr[...] < n_sel) & (blk <= s), 1.0, 0.0)

    sel = sel_scr[...].T.astype(BF16)

    tok_s = s * qb + lax.broadcasted_iota(jnp.int32, (2 * qb, SEL_TILE), 0) % qb
    lane_s = lax.broadcasted_iota(jnp.int32, (2 * qb, SEL_TILE), 1)
    n_steps = (s * qb) // (SEL_TILE * SEL_UNROLL) + 1

    def pass1(it, mx_):
        for u in range(SEL_UNROLL):
            j = it * SEL_UNROLL + u
            k0 = pl.multiple_of(j * SEL_TILE, SEL_TILE)
            chosen = _dot(sel, exp_ref[j])
            ok = (chosen > 0.5) & (k0 + lane_s <= tok_s)
            sc_ = _dot_nt(qq, ks_ref[0, pl.ds(k0, SEL_TILE), :]) + per_group_head(jnp.where(ok, 0.0, NEG_INF))
            sc_scr[j] = sc_
            for c0 in range(0, SEL_TILE, LANES):
                mx_ = jnp.maximum(mx_, sc_[:, c0:c0 + LANES])
        return mx_

    mx_ = lax.fori_loop(0, n_steps, pass1, jnp.full((rows, LANES), NEG_INF, F32))
    m_scr[...] = jnp.broadcast_to(jnp.max(mx_, axis=1, keepdims=True), m_scr.shape)
    l_scr[...] = jnp.zeros(l_scr.shape, F32)
    acc_scr[...] = jnp.zeros(acc_scr.shape, F32)

    def pass2(it, c):
        for u in range(SEL_UNROLL):
            j = it * SEL_UNROLL + u
            k0 = pl.multiple_of(j * SEL_TILE, SEL_TILE)
            m_b = jnp.concatenate([m_scr[...]] * (SEL_TILE // LANES), axis=1)
            p = jnp.exp(sc_scr[j] - m_b)
            l_new = l_scr[...]
            for c0 in range(0, SEL_TILE, LANES):
                l_new = l_new + p[:, c0:c0 + LANES]
            l_scr[...] = l_new
            acc_scr[...] += _dot(p.astype(BF16), vs_ref[0, pl.ds(k0, SEL_TILE), :])
        return c

    lax.fori_loop(0, n_steps, pass2, 0)
    o_sel = acc_scr[...] / jnp.sum(l_scr[...], axis=1, keepdims=True)

    for hq in range(n_head):
        gi = hq // NSA_HPG
        hr = slice(hq * qb, (hq + 1) * qb)
        col = hq * N_BRANCH
        o_h = (sig[:, col:col + 1] * o_cmp[hr] + sig[:, col + 1:col + 2] * o_sel[hr]
               + sig[:, col + 2:col + 3] * o_win[hr])
        in_group = (lane64 >= gi * HEAD_DIM) & (lane64 < (gi + 1) * HEAD_DIM)
        o_ref[:, hq * LANES:(hq + 1) * LANES] = jnp.where(in_group, o_h, 0.0).astype(o_ref.dtype)


def _nsa_attention(q, k_cmp, v_cmp, ks, vs, kw, vw, gl, b, t):
    ns = t // Q_BLOCK
    n_sb = t // SEL_BLOCK
    n_cmp = t // CMP_STRIDE
    n_kt = t // SEL_TILE
    cmp_start = jnp.arange(n_cmp) * CMP_STRIDE
    sb = jnp.arange(n_sb)
    ovt = ((cmp_start[None, :] < (sb[:, None] + 1) * SEL_BLOCK)
           & (cmp_start[None, :] + CMP_BLOCK > sb[:, None] * SEL_BLOCK)
           & (jnp.arange(n_cmp)[None, :] < n_cmp - 1)).astype(F32)
    key_blk = (jnp.arange(n_kt)[:, None] * SEL_TILE + jnp.arange(SEL_TILE)[None, :]) // SEL_BLOCK
    expand = (key_blk[:, None, :] == sb[None, :, None]).astype(BF16)
    rows = NSA_Q_HEADS * Q_BLOCK
    qspec = pl.BlockSpec((Q_BLOCK, Q_PAD_COLS), lambda bi, si: (bi * ns + si, 0))
    seq = lambda: pl.BlockSpec((1, t, LANES), lambda bi, si: (bi, 0, 0))
    cmp_spec = lambda: pl.BlockSpec((1, n_cmp, LANES), lambda bi, si: (bi, 0, 0))
    k3 = lambda a_: a_.reshape(b, t, LANES)
    return pl.pallas_call(
        _nsa_kernel,
        grid=(b, ns),
        in_specs=[qspec, cmp_spec(), cmp_spec(), seq(), seq(), seq(), seq(),
                  pl.BlockSpec((Q_BLOCK, LANES), lambda bi, si: (bi * ns + si, 0)),
                  pl.BlockSpec((n_sb, n_cmp), lambda bi, si: (0, 0)),
                  pl.BlockSpec((n_kt, n_sb, SEL_TILE), lambda bi, si: (0, 0, 0))],
        out_specs=pl.BlockSpec((Q_BLOCK, Q_PAD_COLS), lambda bi, si: (bi * ns + si, 0)),
        out_shape=jax.ShapeDtypeStruct((b * t, Q_PAD_COLS), BF16),
        scratch_shapes=[pltpu.VMEM((rows, LANES), F32),
                        pltpu.VMEM((rows, LANES), F32),
                        pltpu.VMEM((rows, LANES), F32),
                        pltpu.VMEM((n_kt, rows, SEL_TILE), F32),
                        pltpu.VMEM((n_sb, LANES), F32),
                        pltpu.VMEM((n_sb, LANES), F32),
                        pltpu.VMEM((n_sb, LANES), F32)],
        compiler_params=_params("parallel", "arbitrary"),
        name="nsa_attention",
    )(q, k_cmp, v_cmp, k3(ks), k3(vs), k3(kw), k3(vw), gl, ovt, expand)


FF_TILE = 256


def _ffn_kernel(x_ref, orw_ref, ons_ref, wor_ref, won_ref, n2_ref, w1_ref, w3_ref, w2_ref, nf_ref,
                out_ref, h_scr, u_scr, acc_scr):
    j = pl.program_id(1)

    @pl.when(j == 0)
    def _():
        h = x_ref[...] + _dot(orw_ref[...], wor_ref[...]) + _dot(ons_ref[...], won_ref[...])
        h_scr[...] = h
        ms = jnp.mean(h * h, axis=-1, keepdims=True)
        u_scr[...] = (h * lax.rsqrt(ms + RMS_EPS) * n2_ref[...]).astype(BF16)
        acc_scr[...] = jnp.zeros_like(acc_scr)

    u = u_scr[...]
    gate = _dot(u, w1_ref[...])
    up = _dot(u, w3_ref[...])
    act = (jax.nn.silu(gate) * up).astype(BF16)
    acc_scr[...] += _dot(act, w2_ref[...])

    @pl.when(j == pl.num_programs(1) - 1)
    def _():
        h = h_scr[...] + acc_scr[...]
        ms = jnp.mean(h * h, axis=-1, keepdims=True)
        out_ref[...] = h * lax.rsqrt(ms + RMS_EPS) * nf_ref[...]


def _out_ffn(x2, o_rwkv, o_nsa, w_out, norm2_w, ffn_w1, ffn_w3, ffn_w2, final_norm_w):
    n = x2.shape[0]
    tm = 1024
    w_or = w_out[:RWKV_WIDTH].astype(BF16)
    w_n = w_out[RWKV_WIDTH:].reshape(NSA_KV_HEADS, NSA_HPG, HEAD_DIM, D_MODEL)
    w_on = jnp.zeros((NSA_KV_HEADS, NSA_HPG, NSA_KV_HEADS, HEAD_DIM, D_MODEL), F32)
    for gi in range(NSA_KV_HEADS):
        w_on = w_on.at[gi, :, gi].set(w_n[gi])
    w_on = w_on.reshape(Q_PAD_COLS, D_MODEL).astype(BF16)
    row = lambda width: pl.BlockSpec((tm, width), lambda i, j: (i, 0))
    const = lambda shape: pl.BlockSpec(shape, lambda i, j: (0, 0))
    return pl.pallas_call(
        _ffn_kernel,
        grid=(n // tm, D_FF // FF_TILE),
        in_specs=[row(D_MODEL), row(RWKV_WIDTH), row(Q_PAD_COLS),
                  const((RWKV_WIDTH, D_MODEL)), const((Q_PAD_COLS, D_MODEL)), const((1, D_MODEL)),
                  pl.BlockSpec((D_MODEL, FF_TILE), lambda i, j: (0, j)),
                  pl.BlockSpec((D_MODEL, FF_TILE), lambda i, j: (0, j)),
                  pl.BlockSpec((FF_TILE, D_MODEL), lambda i, j: (j, 0)),
                  const((1, D_MODEL))],
        out_specs=row(D_MODEL),
        out_shape=jax.ShapeDtypeStruct((n, D_MODEL), F32),
        scratch_shapes=[pltpu.VMEM((tm, D_MODEL), F32), pltpu.VMEM((tm, D_MODEL), BF16),
                        pltpu.VMEM((tm, D_MODEL), F32)],
        compiler_params=_params("parallel", "arbitrary"),
        name="out_ffn",
    )(x2, o_rwkv, o_nsa, w_or, w_on, norm2_w.reshape(1, D_MODEL), ffn_w1.astype(BF16), ffn_w3.astype(BF16),
      ffn_w2.astype(BF16), final_norm_w.reshape(1, D_MODEL))


def kernel(x, positions, norm1_w, w_in, mu_rwkv, w0, w_lora_up, a0, a_lora_up, g_lora_up, k_k, k_a, r_k, lnx_w, lnx_b, cmp_pos_k, cmp_pos_v, cmp_k_w1, cmp_k_w2, cmp_v_w1, cmp_v_w2, w_out, norm2_w, ffn_w1, ffn_w3, ffn_w2, final_norm_w):
    b, t, d = x.shape
    assert d == D_MODEL and norm1_w.shape[0] == 1, "single-layer block with d_model 1024"
    assert t % RWKV_TT == 0 and t % (SEL_UNROLL * SEL_TILE) == 0 and t // SEL_BLOCK <= LANES
    x2 = x.reshape(b * t, d)
    cos_t, sin_t = _rope_tables(positions)
    yr, q, kc, vc, ks, vs, kw, vw, gl = _in_proj(x2, norm1_w[0], _pack_w_in(w_in[0]), cos_t, sin_t)
    o_rwkv = _rwkv(yr, b, t, mu_rwkv[0], w0[0], w_lora_up[0], a0[0], a_lora_up[0], g_lora_up[0],
                   k_k[0], k_a[0], r_k[0], lnx_w[0], lnx_b[0])
    k_cmp, v_cmp = _compress(kc, vc, b, t, cmp_pos_k[0], cmp_pos_v[0], cmp_k_w1[0], cmp_k_w2[0],
                             cmp_v_w1[0], cmp_v_w2[0])
    o_nsa = _nsa_attention(q, k_cmp, v_cmp, ks, vs, kw, vw, gl, b, t)
    out = _out_ffn(x2, o_rwkv, o_nsa, w_out[0], norm2_w[0], ffn_w1[0], ffn_w3[0], ffn_w2[0], final_norm_w)
    return out.reshape(b, t, d)
```

```python
import functools

import jax
import jax.numpy as jnp
from jax import lax
from jax.experimental import pallas as pl
from jax.experimental.pallas import tpu as pltpu

F32 = jnp.float32
BF16 = jnp.bfloat16
HI = lax.Precision.HIGHEST

D_MODEL = 1024
HEAD_DIM = 64
RWKV_WIDTH = 512
RWKV_HEADS = 8
D_DECAY_LORA = 64
D_AAA_LORA = 64
D_GATE_LORA = 128
RWKV_COLS = 3 * RWKV_WIDTH + D_DECAY_LORA + D_AAA_LORA + D_GATE_LORA
NSA_WIDTH = 512
NSA_Q_HEADS = 8
NSA_KV_HEADS = 2
NSA_HPG = 4
NSA_KV_WIDTH = 128
N_BRANCH = 3
CMP_BLOCK = 32
CMP_STRIDE = 16
CMP_HIDDEN = 256
SEL_BLOCK = 64
SEL_TOPK = 16
WINDOW = 512
Q_BLOCK = 64
ROPE_THETA = 10000.0
D_FF = 2816
RMS_EPS = 1e-6
LNX_EPS = 64e-5
FORCE_BONUS = 1e4
NEG_INF = -1e30

LANES = 128
CHUNK = 64
Q_PAD_COLS = NSA_Q_HEADS * LANES
PROJ_COLS = RWKV_COLS + Q_PAD_COLS + 6 * NSA_KV_WIDTH + LANES
VMEM_LIMIT = 56 * 1024 * 1024


def _dot(a, b, prec=None):
    return jnp.dot(a, b, preferred_element_type=F32, precision=prec)


def _dot_nt(a, b, prec=None):
    return lax.dot_general(a, b, (((1,), (1,)), ((), ())), preferred_element_type=F32, precision=prec)


def _dot_tn(a, b, prec=None):
    return lax.dot_general(a, b, (((0,), (0,)), ((), ())), preferred_element_type=F32, precision=prec)


def _split(a):
    hi = a.astype(BF16)
    return hi, (a - hi.astype(F32)).astype(BF16)


def _mm(a, b, mode, dot=_dot):
    if mode == "hi":
        return dot(a, b, HI)
    if mode == "b1":
        return dot(a.astype(BF16), b.astype(BF16))
    if mode == "xa":
        bh, bl = _split(b)
        ab = a.astype(BF16)
        return dot(ab, bh) + dot(ab, bl)
    if mode == "xb":
        ah, al = _split(a)
        bb = b.astype(BF16)
        return dot(ah, bb) + dot(al, bb)
    ah, al = _split(a)
    bh, bl = _split(b)
    return dot(ah, bh) + (dot(ah, bl) + dot(al, bh))


def _params(*sem):
    return pltpu.CompilerParams(dimension_semantics=sem, vmem_limit_bytes=VMEM_LIMIT)


def _proj_kernel(x_ref, n1_ref, w_ref, pos_ref, invf_ref, sign_ref,
                 yr_ref, q_ref, kc_ref, vc_ref, ks_ref, vs_ref, kw_ref, vw_ref, gl_ref, cos_scr, sin_scr):
    x = x_ref[...]
    ms = jnp.mean(x * x, axis=-1, keepdims=True)
    xn = (x * lax.rsqrt(ms + RMS_EPS) * n1_ref[...]).astype(BF16)
    ang = pos_ref[...].astype(F32) * invf_ref[...]
    cos_scr[...] = jnp.cos(ang)
    sin_scr[...] = jnp.sin(ang) * sign_ref[...]
    lane = lax.broadcasted_iota(jnp.int32, ang.shape, 1)
    first_half = (lane % HEAD_DIM) < (HEAD_DIM // 2)

    def rope(v):
        rot = jnp.where(first_half, pltpu.roll(v, LANES - HEAD_DIM // 2, 1), pltpu.roll(v, HEAD_DIM // 2, 1))
        return v * cos_scr[...] + rot * sin_scr[...]

    def cols(c0, width):
        return _dot(xn, w_ref[:, c0:c0 + width])

    for j in range(RWKV_COLS // 256):
        yr_ref[:, j * 256:(j + 1) * 256] = cols(j * 256, 256)
    scale = HEAD_DIM ** -0.5
    for hq in range(0, NSA_Q_HEADS, 2):
        q2 = cols(RWKV_COLS + hq * LANES, 2 * LANES)
        for e in range(2):
            q = rope(q2[:, e * LANES:(e + 1) * LANES]) * scale
            q_ref[:, (hq + e) * LANES:(hq + e + 1) * LANES] = q.astype(BF16)
    c0 = RWKV_COLS + Q_PAD_COLS
    kv = cols(c0, 2 * LANES)
    kc_ref[...] = rope(kv[:, :LANES])
    vc_ref[...] = kv[:, LANES:]
    kv = cols(c0 + 2 * LANES, 2 * LANES)
    ks_ref[...] = rope(kv[:, :LANES]).astype(BF16)
    vs_ref[...] = kv[:, LANES:].astype(BF16)
    kv = cols(c0 + 4 * LANES, 2 * LANES)
    kw_ref[...] = rope(kv[:, :LANES]).astype(BF16)
    vw_ref[...] = kv[:, LANES:].astype(BF16)
    gl_ref[...] = cols(c0 + 6 * LANES, LANES)


def _pack_w_in(w_in):
    w_r = w_in[:, :RWKV_COLS]
    w_q = w_in[:, RWKV_COLS:RWKV_COLS + NSA_WIDTH].reshape(D_MODEL, NSA_KV_HEADS, NSA_HPG, HEAD_DIM)
    zero = jnp.zeros_like(w_q)
    w_q = jnp.stack([jnp.where(jnp.arange(NSA_KV_HEADS)[None, :, None, None] == gg, w_q, zero)
                     for gg in range(NSA_KV_HEADS)], axis=3)
    w_q = w_q.reshape(D_MODEL, Q_PAD_COLS)
    c0 = RWKV_COLS + NSA_WIDTH
    w_kv = w_in[:, c0:c0 + 6 * NSA_KV_WIDTH]
    w_g = w_in[:, c0 + 6 * NSA_KV_WIDTH:]
    w_g = jnp.pad(w_g, ((0, 0), (0, LANES - w_g.shape[1])))
    return jnp.concatenate([w_r, w_q, w_kv, w_g], axis=1).astype(BF16)


def _in_proj(x2, norm1_w, w_packed, positions):
    n = x2.shape[0]
    tm = 512
    half = HEAD_DIM // 2
    inv_freq = ROPE_THETA ** (-jnp.arange(half, dtype=F32) / half)
    invf = jnp.tile(inv_freq, LANES // half).reshape(1, LANES)
    lane = jnp.arange(LANES)
    sign = jnp.where((lane % HEAD_DIM) < half, -1.0, 1.0).astype(F32).reshape(1, LANES)
    row = lambda width: pl.BlockSpec((tm, width), lambda i: (i, 0))
    const = lambda shape: pl.BlockSpec(shape, lambda i: (0, 0))
    out_shape = [
        jax.ShapeDtypeStruct((n, RWKV_COLS), F32),
        jax.ShapeDtypeStruct((n, Q_PAD_COLS), BF16),
        jax.ShapeDtypeStruct((n, LANES), F32),
        jax.ShapeDtypeStruct((n, LANES), F32),
        jax.ShapeDtypeStruct((n, LANES), BF16),
        jax.ShapeDtypeStruct((n, LANES), BF16),
        jax.ShapeDtypeStruct((n, LANES), BF16),
        jax.ShapeDtypeStruct((n, LANES), BF16),
        jax.ShapeDtypeStruct((n, LANES), F32),
    ]
    out_specs = [row(RWKV_COLS), row(Q_PAD_COLS)] + [row(LANES)] * 7
    return pl.pallas_call(
        _proj_kernel,
        grid=(n // tm,),
        in_specs=[row(D_MODEL), const((1, D_MODEL)), const((D_MODEL, PROJ_COLS)), row(1),
                  const((1, LANES)), const((1, LANES))],
        out_specs=out_specs,
        out_shape=out_shape,
        scratch_shapes=[pltpu.VMEM((tm, LANES), F32), pltpu.VMEM((tm, LANES), F32)],
        compiler_params=_params("parallel"),
        name="in_proj",
    )(x2, norm1_w.reshape(1, D_MODEL), w_packed, positions.reshape(n, 1), invf, sign)


RWKV_TT = 256
P_LORA = "b3"
P_HS = "xb"
P_CUM = "xa"


def _rwkv_kernel(y_ref, mu_ref, w0_ref, wl_ref, a0_ref, gup_ref, kk_ref, ka_ref, rk_ref,
                 lw_ref, lb_ref, hs_ref, tri_ref, o_ref, prev_ref, s_ref, o_scr):
    i = pl.program_id(1)

    @pl.when(i == 0)
    def _():
        prev_ref[...] = jnp.zeros_like(prev_ref)
        s_ref[...] = jnp.zeros_like(s_ref)

    tt = y_ref.shape[0]
    y = y_ref[...]
    row = lax.broadcasted_iota(jnp.int32, (tt, 1), 0)
    y_prev = jnp.where(row == 0, prev_ref[...], pltpu.roll(y, 1, 0))
    prev_ref[...] = y[tt - 1:tt, :]
    ys = y + (y_prev - y) * mu_ref[...]

    w_ = RWKV_WIDTH
    r = ys[:, 0:w_]
    k = ys[:, w_:2 * w_]
    v = ys[:, 2 * w_:3 * w_]
    z = ys[:, 3 * w_:3 * w_ + LANES]
    gd = ys[:, 3 * w_ + LANES:3 * w_ + 2 * LANES]
    lane = lax.broadcasted_iota(jnp.int32, z.shape, 1)
    zt = jnp.where(lane < D_DECAY_LORA, jnp.tanh(z), z)
    wa = _mm(zt, wl_ref[...], P_LORA)
    w_raw = w0_ref[...] + wa[:, :w_]
    a = jax.nn.sigmoid(a0_ref[...] + wa[:, w_:])
    logw = -jnp.exp(-jax.nn.softplus(-w_raw) - 0.5)
    g = _mm(jax.nn.sigmoid(gd), gup_ref[...], P_LORA)
    hs = hs_ref[...]
    kk = k * kk_ref[...]
    kk = kk * lax.rsqrt(jnp.maximum(_mm(kk * kk, hs, P_HS), 1e-12))
    k2 = k * (1.0 + (a - 1.0) * ka_ref[...])
    alpha = -kk
    beta = kk * a

    cw = _mm(tri_ref[...], logw, P_CUM)
    e_in = jnp.exp(cw)
    e_ex = jnp.exp(cw - logw)
    e_neg = jnp.exp(-cw)
    a_t = alpha * e_ex
    r_t = r * e_in
    b_t = beta * e_neg
    k_t = k2 * e_neg

    m0 = lax.broadcasted_iota(jnp.int32, (CHUNK, LANES), 1) < HEAD_DIM

    def bd(xc):
        return jnp.concatenate([jnp.where(m0, xc, 0.0), jnp.where(m0, 0.0, xc)], axis=0)

    n2 = 2 * CHUNK
    ri = lax.broadcasted_iota(jnp.int32, (n2, n2), 0) % CHUNK
    ci = lax.broadcasted_iota(jnp.int32, (n2, n2), 1) % CHUNK
    strict = ri > ci
    incl = ri >= ci
    eye = jnp.where(lax.broadcasted_iota(jnp.int32, (n2, n2), 0) == lax.broadcasted_iota(jnp.int32, (n2, n2), 1),
                    1.0, 0.0)

    units = [(c, p) for c in range(tt // CHUNK) for p in range(w_ // LANES)]
    w_c = {}
    st = {}
    for c in range(tt // CHUNK):
        rs = slice(c * CHUNK, (c + 1) * CHUNK)
        cw_last = cw[c * CHUNK + CHUNK - 1:c * CHUNK + CHUNK, :]
        e_tot = jnp.exp(cw_last - cw[rs, :])
        w_c[c] = jnp.exp(cw_last)
        for p in range(w_ // LANES):
            ls = slice(p * LANES, (p + 1) * LANES)
            a_bd = bd(a_t[rs, ls]).astype(BF16)
            r_bd = bd(r_t[rs, ls])
            b_bd = bd(b_t[rs, ls]).astype(BF16)
            k_bd = bd(k_t[rs, ls]).astype(BF16)
            gm = _dot_nt(jnp.concatenate([a_bd, r_bd.astype(BF16)], axis=0), jnp.concatenate([b_bd, k_bd], axis=0))
            st[c, p] = dict(
                a_bd=a_bd, r_bd=r_bd,
                v_bd=bd(v[rs, ls]).astype(BF16),
                bh_bd=bd(beta[rs, ls] * e_tot[:, ls]).astype(BF16),
                kh_bd=bd(k2[rs, ls] * e_tot[:, ls]).astype(BF16),
                l_ab=jnp.where(strict, gm[0:n2, 0:n2], 0.0),
                l_ak=jnp.where(strict, gm[0:n2, n2:], 0.0).astype(BF16),
                m_rb=jnp.where(incl, gm[n2:, 0:n2], 0.0).astype(BF16),
                m_rk=jnp.where(incl, gm[n2:, n2:], 0.0).astype(BF16))

    pw, tm_ = {}, {}
    for u_ in units:
        l_ab = st[u_]["l_ab"]
        lb = l_ab.astype(BF16)
        pw[u_] = _dot(lb, lb)
        tm_[u_] = eye + l_ab
    for level in range(5):
        for u_ in units:
            pb = pw[u_].astype(BF16)
            if level < 4:
                y = _dot(pb, jnp.concatenate([pb, tm_[u_].astype(BF16)], axis=1))
                pw[u_] = y[:, 0:n2]
                tm_[u_] = tm_[u_] + y[:, n2:]
            else:
                tm_[u_] = tm_[u_] + _dot(pb, tm_[u_].astype(BF16))

    zero_bd = jnp.zeros((n2, n2), BF16)
    lv = {u_: _dot(st[u_]["l_ak"], st[u_]["v_bd"]).astype(BF16) for u_ in units}
    au = {u_: _dot(tm_[u_].astype(BF16), jnp.concatenate([st[u_]["a_bd"], lv[u_]], axis=1)).astype(BF16)
          for u_ in units}
    for u_ in units:
        d_ = st[u_]
        mo = _dot(jnp.concatenate([d_["m_rb"], d_["m_rk"]], axis=1),
                  jnp.concatenate([au[u_], jnp.concatenate([zero_bd, d_["v_bd"]], axis=1)], axis=0))
        d_["r_hat"] = (d_["r_bd"] + mo[:, 0:n2]).astype(BF16)
        d_["o0"] = mo[:, n2:]
    for u_ in units:
        st[u_]["g"] = _dot_tn(au[u_][:, 0:n2], st[u_]["bh_bd"]).astype(BF16)
    for u_ in units:
        d_ = st[u_]
        d_["n"] = _dot_tn(jnp.concatenate([au[u_][:, n2:], d_["v_bd"]], axis=0),
                          jnp.concatenate([d_["bh_bd"], d_["kh_bd"]], axis=0))

    for c, p in units:
        d_ = st[c, p]
        rs = slice(c * CHUNK, (c + 1) * CHUNK)
        ls = slice(p * LANES, (p + 1) * LANES)
        s_old = s_ref[p]
        sb = s_old.astype(BF16)
        o_bd = _dot_nt(d_["r_hat"], sb) + d_["o0"]
        o_scr[rs, ls] = o_bd[0:CHUNK] + o_bd[CHUNK:]
        s_ref[p] = s_old * w_c[c][:, ls] + _dot(sb, d_["g"]) + d_["n"]

    o = o_scr[...]
    inv_n = 1.0 / HEAD_DIM
    mean = _mm(o, hs, P_HS) * inv_n
    d = o - mean
    var = _mm(d * d, hs, P_HS) * inv_n
    on = d * lax.rsqrt(var + LNX_EPS) * lw_ref[...] + lb_ref[...]
    bonus = _mm(r * k2 * rk_ref[...], hs, P_HS) * v
    o_ref[...] = ((on + bonus) * g).astype(o_ref.dtype)


def _rwkv(yr, b, t, mu, w0, w_lora_up, a0, a_lora_up, g_lora_up, k_k, k_a, r_k, lnx_w, lnx_b):
    tt = RWKV_TT
    nt = t // tt
    w_ = RWKV_WIDTH
    wl = jnp.zeros((LANES, 2 * w_), F32)
    wl = wl.at[:D_DECAY_LORA, :w_].set(w_lora_up).at[D_DECAY_LORA:, w_:].set(a_lora_up)
    head = jnp.arange(w_) // HEAD_DIM
    hs = (head[:, None] == head[None, :]).astype(F32)
    ti = jnp.arange(tt)
    tri = ((ti[:, None] // CHUNK == ti[None, :] // CHUNK) & (ti[:, None] >= ti[None, :])).astype(F32)
    vec = lambda a_, width: a_.reshape(1, width)
    const = lambda shape: pl.BlockSpec(shape, lambda bi, i: (0, 0))
    return pl.pallas_call(
        _rwkv_kernel,
        grid=(b, nt),
        in_specs=[pl.BlockSpec((tt, RWKV_COLS), lambda bi, i: (bi * nt + i, 0)),
                  const((1, RWKV_COLS)), const((1, w_)), const((LANES, 2 * w_)), const((1, w_)),
                  const((D_GATE_LORA, w_)), const((1, w_)), const((1, w_)), const((1, w_)),
                  const((1, w_)), const((1, w_)), const((w_, w_)), const((tt, tt))],
        out_specs=pl.BlockSpec((tt, w_), lambda bi, i: (bi * nt + i, 0)),
        out_shape=jax.ShapeDtypeStruct((b * t, w_), BF16),
        scratch_shapes=[pltpu.VMEM((1, RWKV_COLS), F32),
                        pltpu.VMEM((w_ // LANES, LANES, LANES), F32),
                        pltpu.VMEM((tt, w_), F32)],
        compiler_params=_params("parallel", "arbitrary"),
        name="rwkv7",
    )(yr, vec(mu, RWKV_COLS), vec(w0, w_), wl, vec(a0, w_), g_lora_up, vec(k_k, w_), vec(k_a, w_),
      vec(r_k, w_), vec(lnx_w, w_), vec(lnx_b, w_), hs, tri)


def _compress_kernel(zk_ref, zv_ref, pek_ref, pev_ref, wk1_ref, wv1_ref, wk2_ref, wv2_ref, kc_ref, vc_ref):
    def one(z_ref, pe_ref, w1_ref, w2_ref, out_ref):
        z = z_ref[0]
        nrow = z.shape[0]
        za = z + pe_ref[0:1, :]
        zb = z + pe_ref[1:2, :]
        acc = jnp.zeros((nrow, LANES), F32)
        for gi in range(NSA_KV_HEADS):
            ha = _mm(za, w1_ref[gi, 0], "b3")
            hb = _mm(zb, w1_ref[gi, 1], "b3")
            hid = ha + pltpu.roll(hb, nrow - 1, 0)
            acc = acc + _mm(jax.nn.gelu(hid), w2_ref[gi], "b3")
        out_ref[0] = acc

    one(zk_ref, pek_ref, wk1_ref, wk2_ref, kc_ref)
    one(zv_ref, pev_ref, wv1_ref, wv2_ref, vc_ref)


def _compress_weights(pos, w1, w2):
    half = CMP_BLOCK // 2
    w1r = w1.reshape(2, half, HEAD_DIM, CMP_HIDDEN)
    w1g = jnp.zeros((NSA_KV_HEADS, 2, half, NSA_KV_HEADS, HEAD_DIM, CMP_HIDDEN), F32)
    for gi in range(NSA_KV_HEADS):
        w1g = w1g.at[gi, :, :, gi].set(w1r)
    w1g = w1g.reshape(NSA_KV_HEADS, 2, half * NSA_KV_WIDTH, CMP_HIDDEN)
    w2g = jnp.zeros((NSA_KV_HEADS, CMP_HIDDEN, NSA_KV_HEADS, HEAD_DIM), F32)
    for gi in range(NSA_KV_HEADS):
        w2g = w2g.at[gi, :, gi].set(w2)
    w2g = w2g.reshape(NSA_KV_HEADS, CMP_HIDDEN, NSA_KV_WIDTH)
    pe = jnp.broadcast_to(pos.reshape(2, half, 1, HEAD_DIM), (2, half, NSA_KV_HEADS, HEAD_DIM))
    pe = pe.reshape(2, half * NSA_KV_WIDTH)
    return pe, w1g, w2g


def _compress(kc, vc, b, t, cmp_pos_k, cmp_pos_v, k_w1, k_w2, v_w1, v_w2):
    nrow = t // CMP_STRIDE
    zw = CMP_STRIDE * NSA_KV_WIDTH
    pek, wk1, wk2 = _compress_weights(cmp_pos_k, k_w1, k_w2)
    pev, wv1, wv2 = _compress_weights(cmp_pos_v, v_w1, v_w2)
    zspec = pl.BlockSpec((1, nrow, zw), lambda bi: (bi, 0, 0))
    ospec = pl.BlockSpec((1, nrow, LANES), lambda bi: (bi, 0, 0))
    c2 = lambda shape: pl.BlockSpec(shape, lambda bi: (0,) * len(shape))
    return pl.pallas_call(
        _compress_kernel,
        grid=(b,),
        in_specs=[zspec, zspec, c2((2, zw)), c2((2, zw)), c2(wk1.shape), c2(wv1.shape), c2(wk2.shape), c2(wv2.shape)],
        out_specs=[ospec, ospec],
        out_shape=[jax.ShapeDtypeStruct((b, nrow, LANES), F32)] * 2,
        compiler_params=_params("parallel"),
        name="nsa_compress",
    )(kc.reshape(b, nrow, zw), vc.reshape(b, nrow, zw), pek, pev, wk1, wv1, wk2, wv2)


SEL_TILE = 256
SEL_UNROLL = 4
WIN_TILE = 128


def _nsa_kernel(q_ref, kcmp_ref, vcmp_ref, ks_ref, vs_ref, kw_ref, vw_ref, gl_ref, ovt_ref, exp_ref,
                o_ref, m_scr, l_scr, acc_scr, sc_scr, imp_scr, cnt_scr, sel_scr, g_scr):
    s = pl.program_id(1)
    qb = Q_BLOCK
    n_head = NSA_Q_HEADS
    rows = n_head * qb
    n_cmp = kcmp_ref.shape[1]
    n_sb = ovt_ref.shape[0]

    tok64 = s * qb + lax.broadcasted_iota(jnp.int32, (qb, LANES), 0)
    lane64 = lax.broadcasted_iota(jnp.int32, (qb, LANES), 1)
    sig = jax.nn.sigmoid(gl_ref[...])
    for col in range(n_head * N_BRANCH):
        g_scr[col] = jnp.broadcast_to(sig[:, col:col + 1], (qb, LANES))

    qq = jnp.concatenate([q_ref[:, hq * LANES:(hq + 1) * LANES] for hq in range(n_head)], axis=0)

    def per_head(x64):
        return jnp.concatenate([x64] * n_head, axis=0)

    def per_group_head(x128):
        return jnp.concatenate([x128[0:qb]] * NSA_HPG + [x128[qb:2 * qb]] * NSA_HPG, axis=0)

    n_win = WINDOW // WIN_TILE + 1
    first = (s * qb) // WIN_TILE + 1 - n_win
    tok_w = s * qb + lax.broadcasted_iota(jnp.int32, (qb, WIN_TILE), 0)
    lane_w = lax.broadcasted_iota(jnp.int32, (qb, WIN_TILE), 1)
    win_sc, win_k0 = [], []
    mxw = jnp.full((rows, LANES), NEG_INF, F32)
    for j in range(n_win):
        kt = jnp.maximum(first + j, 0)
        k0 = pl.multiple_of(kt * WIN_TILE, WIN_TILE)
        kp = k0 + lane_w + jnp.where(first + j >= 0, 0, 1 << 24)
        bias = per_head(jnp.where((kp <= tok_w) & (kp > tok_w - WINDOW), 0.0, NEG_INF))
        sc_ = _dot_nt(qq, kw_ref[0, pl.ds(k0, WIN_TILE), :]) + bias
        for c0 in range(0, WIN_TILE, LANES):
            mxw = jnp.maximum(mxw, sc_[:, c0:c0 + LANES])
        win_sc.append(sc_)
        win_k0.append(k0)
    m_w = jnp.max(mxw, axis=1, keepdims=True)
    l_w = jnp.zeros((rows, LANES), F32)
    acc_w = jnp.zeros((rows, LANES), F32)
    for j in range(n_win):
        p = jnp.exp(win_sc[j] - m_w)
        for c0 in range(0, WIN_TILE, LANES):
            l_w = l_w + p[:, c0:c0 + LANES]
        acc_w = acc_w + _dot(p.astype(BF16), vw_ref[0, pl.ds(win_k0[j], WIN_TILE), :])
    o_win = acc_w / jnp.sum(l_w, axis=1, keepdims=True)

    tok_c = s * qb + lax.broadcasted_iota(jnp.int32, (qb, n_cmp), 0)
    cmp_i = lax.broadcasted_iota(jnp.int32, (qb, n_cmp), 1)
    cbias = per_head(jnp.where((cmp_i * CMP_STRIDE + CMP_BLOCK - 1 <= tok_c) & (cmp_i < n_cmp - 1), 0.0, NEG_INF))
    sc = _mm(qq, kcmp_ref[0], "xa", _dot_nt) + cbias
    mx = jnp.max(sc, axis=1, keepdims=True)
    e = jnp.where(cbias == 0.0, jnp.exp(sc - mx), 0.0)
    den = jnp.sum(e, axis=1, keepdims=True)
    p_c = e / jnp.where(den > 0.0, den, 1.0)
    o_cmp = _dot(p_c.astype(BF16), vcmp_ref[0].astype(BF16))

    pc_sums = []
    for gi in range(NSA_KV_HEADS):
        r0 = gi * NSA_HPG * qb
        acc = p_c[r0:r0 + qb]
        for h in range(1, NSA_HPG):
            acc = acc + p_c[r0 + h * qb:r0 + (h + 1) * qb]
        pc_sums.append(acc)
    pcs = jnp.concatenate(pc_sums, axis=0)
    imp = _dot_nt(ovt_ref[...], pcs, HI)
    blk = lax.broadcasted_iota(jnp.int32, (n_sb, LANES), 0)
    forced = (blk == 0) | (blk == s) | (blk == s - 1)
    imp = jnp.where(blk <= s, imp + jnp.where(forced, FORCE_BONUS, 0.0), NEG_INF)
    n_sel = min(SEL_TOPK, n_sb)
    sel_scr[...] = jnp.where(blk <= s, 1.0, 0.0)

    @pl.when(s >= n_sel)
    def _():
        imp_scr[...] = imp
        cnt_scr[...] = jnp.zeros(cnt_scr.shape, F32)
        sub = lax.broadcasted_iota(jnp.int32, (8, LANES), 0)
        for ig in range(n_sb // 8):
            @pl.when(ig * 8 <= s)
            def _(ig=ig):
                impv = imp_scr[...]
                cnt = [cnt_scr[rg * 8:(rg + 1) * 8, :] for rg in range(n_sb // 8)]
                for i2 in range(ig * 8, ig * 8 + 8):
                    vi = impv[i2:i2 + 1, :]
                    for rg in range(n_sb // 8):
                        vj = impv[rg * 8:(rg + 1) * 8, :]
                        if rg < ig:
                            one = jnp.where(vi > vj, 1.0, 0.0)
                        elif rg > ig:
                            one = jnp.where(vi >= vj, 1.0, 0.0)
                        else:
                            one = jnp.where(sub > i2 - ig * 8, jnp.where(vi >= vj, 1.0, 0.0),
                                            jnp.where(vi > vj, 1.0, 0.0))
                        cnt[rg] = cnt[rg] + one
                for rg in range(n_sb // 8):
                    cnt_scr[rg * 8:(rg + 1) * 8, :] = cnt[rg]
        sel_scr[...] = jnp.where((cnt_scr[...] < n_sel) & (blk <= s), 1.0, 0.0)

    sel = sel_scr[...].T.astype(BF16)

    tok_s = s * qb + lax.broadcasted_iota(jnp.int32, (2 * qb, SEL_TILE), 0) % qb
    lane_s = lax.broadcasted_iota(jnp.int32, (2 * qb, SEL_TILE), 1)
    n_steps = (s * qb) // (SEL_TILE * SEL_UNROLL) + 1

    def pass1(it, mx_):
        for u in range(SEL_UNROLL):
            j = it * SEL_UNROLL + u
            k0 = pl.multiple_of(j * SEL_TILE, SEL_TILE)
            chosen = _dot(sel, exp_ref[j])
            ok = (chosen > 0.5) & (k0 + lane_s <= tok_s)
            sc_ = _dot_nt(qq, ks_ref[0, pl.ds(k0, SEL_TILE), :]) + per_group_head(jnp.where(ok, 0.0, NEG_INF))
            sc_scr[j] = sc_
            for c0 in range(0, SEL_TILE, LANES):
                mx_ = jnp.maximum(mx_, sc_[:, c0:c0 + LANES])
        return mx_

    mx_ = lax.fori_loop(0, n_steps, pass1, jnp.full((rows, LANES), NEG_INF, F32))
    m_scr[...] = jnp.broadcast_to(jnp.max(mx_, axis=1, keepdims=True), m_scr.shape)
    l_scr[...] = jnp.zeros(l_scr.shape, F32)
    acc_scr[...] = jnp.zeros(acc_scr.shape, F32)

    def pass2(it, c):
        for u in range(SEL_UNROLL):
            j = it * SEL_UNROLL + u
            k0 = pl.multiple_of(j * SEL_TILE, SEL_TILE)
            m_b = jnp.concatenate([m_scr[...]] * (SEL_TILE // LANES), axis=1)
            p = jnp.exp(sc_scr[j] - m_b)
            l_new = l_scr[...]
            for c0 in range(0, SEL_TILE, LANES):
                l_new = l_new + p[:, c0:c0 + LANES]
            l_scr[...] = l_new
            acc_scr[...] += _dot(p.astype(BF16), vs_ref[0, pl.ds(k0, SEL_TILE), :])
        return c

    lax.fori_loop(0, n_steps, pass2, 0)
    o_sel = acc_scr[...] / jnp.sum(l_scr[...], axis=1, keepdims=True)

    for hq in range(n_head):
        gi = hq // NSA_HPG
        hr = slice(hq * qb, (hq + 1) * qb)
        col = hq * N_BRANCH
        o_h = g_scr[col] * o_cmp[hr] + g_scr[col + 1] * o_sel[hr] + g_scr[col + 2] * o_win[hr]
        in_group = (lane64 >= gi * HEAD_DIM) & (lane64 < (gi + 1) * HEAD_DIM)
        o_ref[:, hq * LANES:(hq + 1) * LANES] = jnp.where(in_group, o_h, 0.0).astype(o_ref.dtype)


def _nsa_attention(q, k_cmp, v_cmp, ks, vs, kw, vw, gl, b, t):
    ns = t // Q_BLOCK
    n_sb = t // SEL_BLOCK
    n_cmp = t // CMP_STRIDE
    n_kt = t // SEL_TILE
    cmp_start = jnp.arange(n_cmp) * CMP_STRIDE
    sb = jnp.arange(n_sb)
    ovt = ((cmp_start[None, :] < (sb[:, None] + 1) * SEL_BLOCK)
           & (cmp_start[None, :] + CMP_BLOCK > sb[:, None] * SEL_BLOCK)
           & (jnp.arange(n_cmp)[None, :] < n_cmp - 1)).astype(F32)
    key_blk = (jnp.arange(n_kt)[:, None] * SEL_TILE + jnp.arange(SEL_TILE)[None, :]) // SEL_BLOCK
    expand = (key_blk[:, None, :] == sb[None, :, None]).astype(BF16)
    rows = NSA_Q_HEADS * Q_BLOCK
    qspec = pl.BlockSpec((Q_BLOCK, Q_PAD_COLS), lambda bi, si: (bi * ns + si, 0))
    seq = lambda: pl.BlockSpec((1, t, LANES), lambda bi, si: (bi, 0, 0))
    cmp_spec = lambda: pl.BlockSpec((1, n_cmp, LANES), lambda bi, si: (bi, 0, 0))
    k3 = lambda a_: a_.reshape(b, t, LANES)
    return pl.pallas_call(
        _nsa_kernel,
        grid=(b, ns),
        in_specs=[qspec, cmp_spec(), cmp_spec(), seq(), seq(), seq(), seq(),
                  pl.BlockSpec((Q_BLOCK, LANES), lambda bi, si: (bi * ns + si, 0)),
                  pl.BlockSpec((n_sb, n_cmp), lambda bi, si: (0, 0)),
                  pl.BlockSpec((n_kt, n_sb, SEL_TILE), lambda bi, si: (0, 0, 0))],
        out_specs=pl.BlockSpec((Q_BLOCK, Q_PAD_COLS), lambda bi, si: (bi * ns + si, 0)),
        out_shape=jax.ShapeDtypeStruct((b * t, Q_PAD_COLS), BF16),
        scratch_shapes=[pltpu.VMEM((rows, LANES), F32),
                        pltpu.VMEM((rows, LANES), F32),
                        pltpu.VMEM((rows, LANES), F32),
                        pltpu.VMEM((n_kt, rows, SEL_TILE), F32),
                        pltpu.VMEM((n_sb, LANES), F32),
                        pltpu.VMEM((n_sb, LANES), F32),
                        pltpu.VMEM((n_sb, LANES), F32),
                        pltpu.VMEM((NSA_Q_HEADS * N_BRANCH, Q_BLOCK, LANES), F32)],
        compiler_params=_params("parallel", "arbitrary"),
        name="nsa_attention",
    )(q, k_cmp, v_cmp, k3(ks), k3(vs), k3(kw), k3(vw), gl, ovt, expand)


FF_TILE = 256


def _ffn_kernel(x_ref, orw_ref, ons_ref, wor_ref, won_ref, n2_ref, w1_ref, w3_ref, w2_ref, nf_ref,
                out_ref, h_scr, u_scr, acc_scr):
    j = pl.program_id(1)

    @pl.when(j == 0)
    def _():
        h = x_ref[...] + _dot(orw_ref[...], wor_ref[...]) + _dot(ons_ref[...], won_ref[...])
        h_scr[...] = h
        ms = jnp.mean(h * h, axis=-1, keepdims=True)
        u_scr[...] = (h * lax.rsqrt(ms + RMS_EPS) * n2_ref[...]).astype(BF16)
        acc_scr[...] = jnp.zeros_like(acc_scr)

    u = u_scr[...]
    gate = _dot(u, w1_ref[...])
    up = _dot(u, w3_ref[...])
    act = (jax.nn.silu(gate) * up).astype(BF16)
    acc_scr[...] += _dot(act, w2_ref[...])

    @pl.when(j == pl.num_programs(1) - 1)
    def _():
        h = h_scr[...] + acc_scr[...]
        ms = jnp.mean(h * h, axis=-1, keepdims=True)
        out_ref[...] = h * lax.rsqrt(ms + RMS_EPS) * nf_ref[...]


def _out_ffn(x2, o_rwkv, o_nsa, w_out, norm2_w, ffn_w1, ffn_w3, ffn_w2, final_norm_w):
    n = x2.shape[0]
    tm = 1024
    w_or = w_out[:RWKV_WIDTH].astype(BF16)
    w_n = w_out[RWKV_WIDTH:].reshape(NSA_KV_HEADS, NSA_HPG, HEAD_DIM, D_MODEL)
    w_on = jnp.zeros((NSA_KV_HEADS, NSA_HPG, NSA_KV_HEADS, HEAD_DIM, D_MODEL), F32)
    for gi in range(NSA_KV_HEADS):
        w_on = w_on.at[gi, :, gi].set(w_n[gi])
    w_on = w_on.reshape(Q_PAD_COLS, D_MODEL).astype(BF16)
    row = lambda width: pl.BlockSpec((tm, width), lambda i, j: (i, 0))
    const = lambda shape: pl.BlockSpec(shape, lambda i, j: (0, 0))
    return pl.pallas_call(
        _ffn_kernel,
        grid=(n // tm, D_FF // FF_TILE),
        in_specs=[row(D_MODEL), row(RWKV_WIDTH), row(Q_PAD_COLS),
                  const((RWKV_WIDTH, D_MODEL)), const((Q_PAD_COLS, D_MODEL)), const((1, D_MODEL)),
                  pl.BlockSpec((D_MODEL, FF_TILE), lambda i, j: (0, j)),
                  pl.BlockSpec((D_MODEL, FF_TILE), lambda i, j: (0, j)),
                  pl.BlockSpec((FF_TILE, D_MODEL), lambda i, j: (j, 0)),
                  const((1, D_MODEL))],
        out_specs=row(D_MODEL),
        out_shape=jax.ShapeDtypeStruct((n, D_MODEL), F32),
        scratch_shapes=[pltpu.VMEM((tm, D_MODEL), F32), pltpu.VMEM((tm, D_MODEL), BF16),
                        pltpu.VMEM((tm, D_MODEL), F32)],
        compiler_params=_params("parallel", "arbitrary"),
        name="out_ffn",
    )(x2, o_rwkv, o_nsa, w_or, w_on, norm2_w.reshape(1, D_MODEL), ffn_w1.astype(BF16), ffn_w3.astype(BF16),
      ffn_w2.astype(BF16), final_norm_w.reshape(1, D_MODEL))


def kernel(x, positions, norm1_w, w_in, mu_rwkv, w0, w_lora_up, a0, a_lora_up, g_lora_up, k_k, k_a, r_k, lnx_w, lnx_b, cmp_pos_k, cmp_pos_v, cmp_k_w1, cmp_k_w2, cmp_v_w1, cmp_v_w2, w_out, norm2_w, ffn_w1, ffn_w3, ffn_w2, final_norm_w):
    b, t, d = x.shape
    assert d == D_MODEL and norm1_w.shape[0] == 1, "single-layer block with d_model 1024"
    assert t % RWKV_TT == 0 and t % (SEL_UNROLL * SEL_TILE) == 0 and t // SEL_BLOCK <= LANES
    x2 = x.reshape(b * t, d)
    yr, q, kc, vc, ks, vs, kw, vw, gl = _in_proj(x2, norm1_w[0], _pack_w_in(w_in[0]), positions)
    o_rwkv = _rwkv(yr, b, t, mu_rwkv[0], w0[0], w_lora_up[0], a0[0], a_lora_up[0], g_lora_up[0],
                   k_k[0], k_a[0], r_k[0], lnx_w[0], lnx_b[0])
    k_cmp, v_cmp = _compress(kc, vc, b, t, cmp_pos_k[0], cmp_pos_v[0], cmp_k_w1[0], cmp_k_w2[0],
                             cmp_v_w1[0], cmp_v_w2[0])
    o_nsa = _nsa_attention(q, k_cmp, v_cmp, ks, vs, kw, vw, gl, b, t)
    out = _out_ffn(x2, o_rwkv, o_nsa, w_out[0], norm2_w[0], ffn_w1[0], ffn_w3[0], ffn_w2[0], final_norm_w)
    return out.reshape(b, t, d)
```

```python
import jax
import jax.numpy as jnp
from jax import lax
from jax.experimental import pallas as pl
from jax.experimental.pallas import tpu as pltpu

F32 = jnp.float32
BF16 = jnp.bfloat16
HI = lax.Precision.HIGHEST

D_MODEL = 1024
HEAD_DIM = 64
RWKV_WIDTH = 512
RWKV_HEADS = 8
D_DECAY_LORA = 64
D_AAA_LORA = 64
D_GATE_LORA = 128
RWKV_COLS = 3 * RWKV_WIDTH + D_DECAY_LORA + D_AAA_LORA + D_GATE_LORA
NSA_WIDTH = 512
NSA_Q_HEADS = 8
NSA_KV_HEADS = 2
NSA_HPG = 4
NSA_KV_WIDTH = 128
N_BRANCH = 3
CMP_BLOCK = 32
CMP_STRIDE = 16
CMP_HIDDEN = 256
SEL_BLOCK = 64
SEL_TOPK = 16
WINDOW = 512
Q_BLOCK = 64
ROPE_THETA = 10000.0
D_FF = 2816
RMS_EPS = 1e-6
LNX_EPS = 64e-5
FORCE_BONUS = 1e4
NEG_INF = -1e30

LANES = 128
CHUNK = 64
Q_PAD_COLS = NSA_Q_HEADS * LANES
PROJ_COLS = RWKV_COLS + Q_PAD_COLS + 6 * NSA_KV_WIDTH + LANES
VMEM_LIMIT = 56 * 1024 * 1024


def _dot(a, b, prec=None):
    return jnp.dot(a, b, preferred_element_type=F32, precision=prec)


def _dot_nt(a, b, prec=None):
    return lax.dot_general(a, b, (((1,), (1,)), ((), ())), preferred_element_type=F32, precision=prec)


def _dot_tn(a, b, prec=None):
    return lax.dot_general(a, b, (((0,), (0,)), ((), ())), preferred_element_type=F32, precision=prec)


def _split(a):
    hi = a.astype(BF16)
    return hi, (a - hi.astype(F32)).astype(BF16)


def _mm(a, b, mode, dot=_dot):
    if mode == "hi":
        return dot(a, b, HI)
    if mode == "b1":
        return dot(a.astype(BF16), b.astype(BF16))
    if mode == "xa":
        bh, bl = _split(b)
        ab = a.astype(BF16)
        return dot(ab, bh) + dot(ab, bl)
    if mode == "xb":
        ah, al = _split(a)
        bb = b.astype(BF16)
        return dot(ah, bb) + dot(al, bb)
    ah, al = _split(a)
    bh, bl = _split(b)
    return dot(ah, bh) + (dot(ah, bl) + dot(al, bh))


def _params(*sem):
    return pltpu.CompilerParams(dimension_semantics=sem, vmem_limit_bytes=VMEM_LIMIT)


def _proj_kernel(x_ref, n1_ref, w_ref, pos_ref, invf_ref, sign_ref,
                 yr_ref, q_ref, kc_ref, vc_ref, ks_ref, vs_ref, kw_ref, vw_ref, gl_ref, cos_scr, sin_scr):
    x = x_ref[...]
    ms = jnp.mean(x * x, axis=-1, keepdims=True)
    xn = (x * lax.rsqrt(ms + RMS_EPS) * n1_ref[...]).astype(BF16)
    ang = pos_ref[...].astype(F32) * invf_ref[...]
    cos_scr[...] = jnp.cos(ang)
    sin_scr[...] = jnp.sin(ang) * sign_ref[...]
    lane = lax.broadcasted_iota(jnp.int32, ang.shape, 1)
    first_half = (lane % HEAD_DIM) < (HEAD_DIM // 2)

    def rope(v):
        rot = jnp.where(first_half, pltpu.roll(v, LANES - HEAD_DIM // 2, 1), pltpu.roll(v, HEAD_DIM // 2, 1))
        return v * cos_scr[...] + rot * sin_scr[...]

    def cols(c0, width):
        return _dot(xn, w_ref[:, c0:c0 + width])

    for j in range(RWKV_COLS // 256):
        yr_ref[:, j * 256:(j + 1) * 256] = cols(j * 256, 256)
    scale = HEAD_DIM ** -0.5
    for hq in range(0, NSA_Q_HEADS, 2):
        q2 = cols(RWKV_COLS + hq * LANES, 2 * LANES)
        for e in range(2):
            q = rope(q2[:, e * LANES:(e + 1) * LANES]) * scale
            q_ref[:, (hq + e) * LANES:(hq + e + 1) * LANES] = q.astype(BF16)
    c0 = RWKV_COLS + Q_PAD_COLS
    kv = cols(c0, 2 * LANES)
    kc_ref[...] = rope(kv[:, :LANES])
    vc_ref[...] = kv[:, LANES:]
    kv = cols(c0 + 2 * LANES, 2 * LANES)
    ks_ref[...] = rope(kv[:, :LANES]).astype(BF16)
    vs_ref[...] = kv[:, LANES:].astype(BF16)
    kv = cols(c0 + 4 * LANES, 2 * LANES)
    kw_ref[...] = rope(kv[:, :LANES]).astype(BF16)
    vw_ref[...] = kv[:, LANES:].astype(BF16)
    gl_ref[...] = cols(c0 + 6 * LANES, LANES)


def _pack_w_in(w_in):
    w_r = w_in[:, :RWKV_COLS]
    w_q = w_in[:, RWKV_COLS:RWKV_COLS + NSA_WIDTH].reshape(D_MODEL, NSA_KV_HEADS, NSA_HPG, HEAD_DIM)
    zero = jnp.zeros_like(w_q)
    w_q = jnp.stack([jnp.where(jnp.arange(NSA_KV_HEADS)[None, :, None, None] == gg, w_q, zero)
                     for gg in range(NSA_KV_HEADS)], axis=3)
    w_q = w_q.reshape(D_MODEL, Q_PAD_COLS)
    c0 = RWKV_COLS + NSA_WIDTH
    w_kv = w_in[:, c0:c0 + 6 * NSA_KV_WIDTH]
    w_g = w_in[:, c0 + 6 * NSA_KV_WIDTH:]
    w_g = jnp.pad(w_g, ((0, 0), (0, LANES - w_g.shape[1])))
    return jnp.concatenate([w_r, w_q, w_kv, w_g], axis=1).astype(BF16)


def _in_proj(x2, norm1_w, w_packed, positions):
    n = x2.shape[0]
    tm = 512
    half = HEAD_DIM // 2
    inv_freq = ROPE_THETA ** (-jnp.arange(half, dtype=F32) / half)
    invf = jnp.tile(inv_freq, LANES // half).reshape(1, LANES)
    lane = jnp.arange(LANES)
    sign = jnp.where((lane % HEAD_DIM) < half, -1.0, 1.0).astype(F32).reshape(1, LANES)
    row = lambda width: pl.BlockSpec((tm, width), lambda i: (i, 0))
    const = lambda shape: pl.BlockSpec(shape, lambda i: (0, 0))
    out_shape = [
        jax.ShapeDtypeStruct((n, RWKV_COLS), F32),
        jax.ShapeDtypeStruct((n, Q_PAD_COLS), BF16),
        jax.ShapeDtypeStruct((n, LANES), F32),
        jax.ShapeDtypeStruct((n, LANES), F32),
        jax.ShapeDtypeStruct((n, LANES), BF16),
        jax.ShapeDtypeStruct((n, LANES), BF16),
        jax.ShapeDtypeStruct((n, LANES), BF16),
        jax.ShapeDtypeStruct((n, LANES), BF16),
        jax.ShapeDtypeStruct((n, LANES), F32),
    ]
    out_specs = [row(RWKV_COLS), row(Q_PAD_COLS)] + [row(LANES)] * 7
    return pl.pallas_call(
        _proj_kernel,
        grid=(n // tm,),
        in_specs=[row(D_MODEL), const((1, D_MODEL)), const((D_MODEL, PROJ_COLS)), row(1),
                  const((1, LANES)), const((1, LANES))],
        out_specs=out_specs,
        out_shape=out_shape,
        scratch_shapes=[pltpu.VMEM((tm, LANES), F32), pltpu.VMEM((tm, LANES), F32)],
        compiler_params=_params("parallel"),
        name="in_proj",
    )(x2, norm1_w.reshape(1, D_MODEL), w_packed, positions.reshape(n, 1), invf, sign)


RWKV_TT = 256
P_LORA = "b3"
P_HS = "xb"
P_CUM = "xa"


def _rwkv_kernel(y_ref, mu_ref, w0_ref, wl_ref, a0_ref, gup_ref, kk_ref, ka_ref, rk_ref,
                 lw_ref, lb_ref, hs_ref, tri_ref, o_ref, prev_ref, s_ref, o_scr):
    i = pl.program_id(1)

    @pl.when(i == 0)
    def _():
        prev_ref[...] = jnp.zeros_like(prev_ref)
        s_ref[...] = jnp.zeros_like(s_ref)

    tt = y_ref.shape[0]
    y = y_ref[...]
    row = lax.broadcasted_iota(jnp.int32, (tt, 1), 0)
    y_prev = jnp.where(row == 0, prev_ref[...], pltpu.roll(y, 1, 0))
    prev_ref[...] = y[tt - 1:tt, :]
    ys = y + (y_prev - y) * mu_ref[...]

    w_ = RWKV_WIDTH
    r = ys[:, 0:w_]
    k = ys[:, w_:2 * w_]
    v = ys[:, 2 * w_:3 * w_]
    z = ys[:, 3 * w_:3 * w_ + LANES]
    gd = ys[:, 3 * w_ + LANES:3 * w_ + 2 * LANES]
    lane = lax.broadcasted_iota(jnp.int32, z.shape, 1)
    zt = jnp.where(lane < D_DECAY_LORA, jnp.tanh(z), z)
    wa = _mm(zt, wl_ref[...], P_LORA)
    w_raw = w0_ref[...] + wa[:, :w_]
    a = jax.nn.sigmoid(a0_ref[...] + wa[:, w_:])
    logw = -jnp.exp(-jax.nn.softplus(-w_raw) - 0.5)
    g = _mm(jax.nn.sigmoid(gd), gup_ref[...], P_LORA)
    hs = hs_ref[...]
    kk = k * kk_ref[...]
    kk = kk * lax.rsqrt(jnp.maximum(_mm(kk * kk, hs, P_HS), 1e-12))
    k2 = k * (1.0 + (a - 1.0) * ka_ref[...])
    alpha = -kk
    beta = kk * a

    cw = _mm(tri_ref[...], logw, P_CUM)
    e_in = jnp.exp(cw)
    e_ex = jnp.exp(cw - logw)
    e_neg = jnp.exp(-cw)
    a_t = alpha * e_ex
    r_t = r * e_in
    b_t = beta * e_neg
    k_t = k2 * e_neg

    m0 = lax.broadcasted_iota(jnp.int32, (CHUNK, LANES), 1) < HEAD_DIM

    def bd(xc):
        return jnp.concatenate([jnp.where(m0, xc, 0.0), jnp.where(m0, 0.0, xc)], axis=0)

    n2 = 2 * CHUNK
    ri = lax.broadcasted_iota(jnp.int32, (n2, n2), 0) % CHUNK
    ci = lax.broadcasted_iota(jnp.int32, (n2, n2), 1) % CHUNK
    strict = ri > ci
    incl = ri >= ci
    eye = jnp.where(lax.broadcasted_iota(jnp.int32, (n2, n2), 0) == lax.broadcasted_iota(jnp.int32, (n2, n2), 1),
                    1.0, 0.0)

    units = [(c, p) for c in range(tt // CHUNK) for p in range(w_ // LANES)]
    w_c = {}
    st = {}
    for c in range(tt // CHUNK):
        rs = slice(c * CHUNK, (c + 1) * CHUNK)
        cw_last = cw[c * CHUNK + CHUNK - 1:c * CHUNK + CHUNK, :]
        e_tot = jnp.exp(cw_last - cw[rs, :])
        w_c[c] = jnp.exp(cw_last)
        for p in range(w_ // LANES):
            ls = slice(p * LANES, (p + 1) * LANES)
            a_bd = bd(a_t[rs, ls]).astype(BF16)
            r_bd = bd(r_t[rs, ls])
            b_bd = bd(b_t[rs, ls]).astype(BF16)
            k_bd = bd(k_t[rs, ls]).astype(BF16)
            gm = _dot_nt(jnp.concatenate([a_bd, r_bd.astype(BF16)], axis=0), jnp.concatenate([b_bd, k_bd], axis=0))
            st[c, p] = dict(
                a_bd=a_bd, r_bd=r_bd,
                v_bd=bd(v[rs, ls]).astype(BF16),
                bh_bd=bd(beta[rs, ls] * e_tot[:, ls]).astype(BF16),
                kh_bd=bd(k2[rs, ls] * e_tot[:, ls]).astype(BF16),
                l_ab=jnp.where(strict, gm[0:n2, 0:n2], 0.0),
                l_ak=jnp.where(strict, gm[0:n2, n2:], 0.0).astype(BF16),
                m_rb=jnp.where(incl, gm[n2:, 0:n2], 0.0).astype(BF16),
                m_rk=jnp.where(incl, gm[n2:, n2:], 0.0).astype(BF16))

    pw, tm_ = {}, {}
    for u_ in units:
        l_ab = st[u_]["l_ab"]
        lb = l_ab.astype(BF16)
        pw[u_] = _dot(lb, lb)
        tm_[u_] = eye + l_ab
    for level in range(5):
        for u_ in units:
            pb = pw[u_].astype(BF16)
            if level < 4:
                y = _dot(pb, jnp.concatenate([pb, tm_[u_].astype(BF16)], axis=1))
                pw[u_] = y[:, 0:n2]
                tm_[u_] = tm_[u_] + y[:, n2:]
            else:
                tm_[u_] = tm_[u_] + _dot(pb, tm_[u_].astype(BF16))

    zero_bd = jnp.zeros((n2, n2), BF16)
    lv = {u_: _dot(st[u_]["l_ak"], st[u_]["v_bd"]).astype(BF16) for u_ in units}
    au = {u_: _dot(tm_[u_].astype(BF16), jnp.concatenate([st[u_]["a_bd"], lv[u_]], axis=1)).astype(BF16)
          for u_ in units}
    for u_ in units:
        d_ = st[u_]
        mo = _dot(jnp.concatenate([d_["m_rb"], d_["m_rk"]], axis=1),
                  jnp.concatenate([au[u_], jnp.concatenate([zero_bd, d_["v_bd"]], axis=1)], axis=0))
        d_["r_hat"] = (d_["r_bd"] + mo[:, 0:n2]).astype(BF16)
        d_["o0"] = mo[:, n2:]
    for u_ in units:
        st[u_]["g"] = _dot_tn(au[u_][:, 0:n2], st[u_]["bh_bd"]).astype(BF16)
    for u_ in units:
        d_ = st[u_]
        d_["n"] = _dot_tn(jnp.concatenate([au[u_][:, n2:], d_["v_bd"]], axis=0),
                          jnp.concatenate([d_["bh_bd"], d_["kh_bd"]], axis=0))

    for c, p in units:
        d_ = st[c, p]
        rs = slice(c * CHUNK, (c + 1) * CHUNK)
        ls = slice(p * LANES, (p + 1) * LANES)
        s_old = s_ref[p]
        sb = s_old.astype(BF16)
        o_bd = _dot_nt(d_["r_hat"], sb) + d_["o0"]
        o_scr[rs, ls] = o_bd[0:CHUNK] + o_bd[CHUNK:]
        s_ref[p] = s_old * w_c[c][:, ls] + _dot(sb, d_["g"]) + d_["n"]

    o = o_scr[...]
    inv_n = 1.0 / HEAD_DIM
    mean = _mm(o, hs, P_HS) * inv_n
    d = o - mean
    var = _mm(d * d, hs, P_HS) * inv_n
    on = d * lax.rsqrt(var + LNX_EPS) * lw_ref[...] + lb_ref[...]
    bonus = _mm(r * k2 * rk_ref[...], hs, P_HS) * v
    o_ref[...] = ((on + bonus) * g).astype(o_ref.dtype)


def _rwkv(yr, b, t, mu, w0, w_lora_up, a0, a_lora_up, g_lora_up, k_k, k_a, r_k, lnx_w, lnx_b):
    tt = RWKV_TT
    nt = t // tt
    w_ = RWKV_WIDTH
    wl = jnp.zeros((LANES, 2 * w_), F32)
    wl = wl.at[:D_DECAY_LORA, :w_].set(w_lora_up).at[D_DECAY_LORA:, w_:].set(a_lora_up)
    head = jnp.arange(w_) // HEAD_DIM
    hs = (head[:, None] == head[None, :]).astype(F32)
    ti = jnp.arange(tt)
    tri = ((ti[:, None] // CHUNK == ti[None, :] // CHUNK) & (ti[:, None] >= ti[None, :])).astype(F32)
    vec = lambda a_, width: a_.reshape(1, width)
    const = lambda shape: pl.BlockSpec(shape, lambda bi, i: (0, 0))
    return pl.pallas_call(
        _rwkv_kernel,
        grid=(b, nt),
        in_specs=[pl.BlockSpec((tt, RWKV_COLS), lambda bi, i: (bi * nt + i, 0)),
                  const((1, RWKV_COLS)), const((1, w_)), const((LANES, 2 * w_)), const((1, w_)),
                  const((D_GATE_LORA, w_)), const((1, w_)), const((1, w_)), const((1, w_)),
                  const((1, w_)), const((1, w_)), const((w_, w_)), const((tt, tt))],
        out_specs=pl.BlockSpec((tt, w_), lambda bi, i: (bi * nt + i, 0)),
        out_shape=jax.ShapeDtypeStruct((b * t, w_), BF16),
        scratch_shapes=[pltpu.VMEM((1, RWKV_COLS), F32),
                        pltpu.VMEM((w_ // LANES, LANES, LANES), F32),
                        pltpu.VMEM((tt, w_), F32)],
        compiler_params=_params("parallel", "arbitrary"),
        name="rwkv7",
    )(yr, vec(mu, RWKV_COLS), vec(w0, w_), wl, vec(a0, w_), g_lora_up, vec(k_k, w_), vec(k_a, w_),
      vec(r_k, w_), vec(lnx_w, w_), vec(lnx_b, w_), hs, tri)


def _compress_kernel(zk_ref, zv_ref, pek_ref, pev_ref, wk1_ref, wv1_ref, wk2_ref, wv2_ref, kc_ref, vc_ref):
    def one(z_ref, pe_ref, w1_ref, w2_ref, out_ref):
        z = z_ref[0]
        nrow = z.shape[0]
        za = z + pe_ref[0:1, :]
        zb = z + pe_ref[1:2, :]
        acc = jnp.zeros((nrow, LANES), F32)
        for gi in range(NSA_KV_HEADS):
            ha = _mm(za, w1_ref[gi, 0], "b3")
            hb = _mm(zb, w1_ref[gi, 1], "b3")
            hid = ha + pltpu.roll(hb, nrow - 1, 0)
            acc = acc + _mm(jax.nn.gelu(hid), w2_ref[gi], "b3")
        out_ref[0] = acc

    one(zk_ref, pek_ref, wk1_ref, wk2_ref, kc_ref)
    one(zv_ref, pev_ref, wv1_ref, wv2_ref, vc_ref)


def _compress_weights(pos, w1, w2):
    half = CMP_BLOCK // 2
    w1r = w1.reshape(2, half, HEAD_DIM, CMP_HIDDEN)
    w1g = jnp.zeros((NSA_KV_HEADS, 2, half, NSA_KV_HEADS, HEAD_DIM, CMP_HIDDEN), F32)
    for gi in range(NSA_KV_HEADS):
        w1g = w1g.at[gi, :, :, gi].set(w1r)
    w1g = w1g.reshape(NSA_KV_HEADS, 2, half * NSA_KV_WIDTH, CMP_HIDDEN)
    w2g = jnp.zeros((NSA_KV_HEADS, CMP_HIDDEN, NSA_KV_HEADS, HEAD_DIM), F32)
    for gi in range(NSA_KV_HEADS):
        w2g = w2g.at[gi, :, gi].set(w2)
    w2g = w2g.reshape(NSA_KV_HEADS, CMP_HIDDEN, NSA_KV_WIDTH)
    pe = jnp.broadcast_to(pos.reshape(2, half, 1, HEAD_DIM), (2, half, NSA_KV_HEADS, HEAD_DIM))
    pe = pe.reshape(2, half * NSA_KV_WIDTH)
    return pe, w1g, w2g


def _compress(kc, vc, b, t, cmp_pos_k, cmp_pos_v, k_w1, k_w2, v_w1, v_w2):
    nrow = t // CMP_STRIDE
    zw = CMP_STRIDE * NSA_KV_WIDTH
    pek, wk1, wk2 = _compress_weights(cmp_pos_k, k_w1, k_w2)
    pev, wv1, wv2 = _compress_weights(cmp_pos_v, v_w1, v_w2)
    zspec = pl.BlockSpec((1, nrow, zw), lambda bi: (bi, 0, 0))
    ospec = pl.BlockSpec((1, nrow, LANES), lambda bi: (bi, 0, 0))
    c2 = lambda shape: pl.BlockSpec(shape, lambda bi: (0,) * len(shape))
    return pl.pallas_call(
        _compress_kernel,
        grid=(b,),
        in_specs=[zspec, zspec, c2((2, zw)), c2((2, zw)), c2(wk1.shape), c2(wv1.shape), c2(wk2.shape), c2(wv2.shape)],
        out_specs=[ospec, ospec],
        out_shape=[jax.ShapeDtypeStruct((b, nrow, LANES), F32)] * 2,
        compiler_params=_params("parallel"),
        name="nsa_compress",
    )(kc.reshape(b, nrow, zw), vc.reshape(b, nrow, zw), pek, pev, wk1, wv1, wk2, wv2)


SEL_TILE = 256
SEL_UNROLL = 4
WIN_TILE = 128
NSA_QBLK = 2


def _nsa_kernel(q_ref, kcmp_ref, vcmp_ref, ks_ref, vs_ref, kw_ref, vw_ref, gl_ref, ovt_ref, exp_ref,
                o_ref, m_scr, l_scr, acc_scr, sc_scr, imp_scr, cnt_scr, sel_scr, g_scr):
    s2 = pl.program_id(1)
    qb = Q_BLOCK
    nq = NSA_QBLK
    n_head = NSA_Q_HEADS
    rows = nq * n_head * qb
    n_cmp = kcmp_ref.shape[1]
    n_sb = ovt_ref.shape[0]
    s_of = [s2 * nq + qi for qi in range(nq)]
    s_last = s_of[-1]

    def tok(qi, shape):
        return s_of[qi] * qb + lax.broadcasted_iota(jnp.int32, shape, 0)

    def per_head(x64s):
        return jnp.concatenate([x for x in x64s for _ in range(n_head)], axis=0)

    def per_group_head(x):
        parts = []
        for qi in range(nq):
            for gi in range(NSA_KV_HEADS):
                r0 = (qi * NSA_KV_HEADS + gi) * qb
                parts += [x[r0:r0 + qb]] * NSA_HPG
        return jnp.concatenate(parts, axis=0)

    lane64 = lax.broadcasted_iota(jnp.int32, (qb, LANES), 1)
    sig = jax.nn.sigmoid(gl_ref[...])
    for col in range(n_head * N_BRANCH):
        g_scr[col] = jnp.broadcast_to(sig[:, col:col + 1], (nq * qb, LANES))

    qq = jnp.concatenate([q_ref[qi * qb:(qi + 1) * qb, hq * LANES:(hq + 1) * LANES]
                          for qi in range(nq) for hq in range(n_head)], axis=0)

    n_win = (WINDOW + nq * qb) // WIN_TILE
    first = (s2 * nq * qb - WINDOW) // WIN_TILE
    lane_w = lax.broadcasted_iota(jnp.int32, (qb, WIN_TILE), 1)
    win_sc, win_k0 = [], []
    mxw = jnp.full((rows, LANES), NEG_INF, F32)
    for j in range(n_win):
        kt = jnp.maximum(first + j, 0)
        k0 = pl.multiple_of(kt * WIN_TILE, WIN_TILE)
        kp = k0 + lane_w + jnp.where(first + j >= 0, 0, 1 << 24)
        bias = per_head([jnp.where((kp <= tok(qi, kp.shape)) & (kp > tok(qi, kp.shape) - WINDOW), 0.0, NEG_INF)
                         for qi in range(nq)])
        sc_ = _dot_nt(qq, kw_ref[0, pl.ds(k0, WIN_TILE), :]) + bias
        for c0 in range(0, WIN_TILE, LANES):
            mxw = jnp.maximum(mxw, sc_[:, c0:c0 + LANES])
        win_sc.append(sc_)
        win_k0.append(k0)
    m_w = jnp.max(mxw, axis=1, keepdims=True)
    l_w = jnp.zeros((rows, LANES), F32)
    acc_w = jnp.zeros((rows, LANES), F32)
    for j in range(n_win):
        p = jnp.exp(win_sc[j] - m_w)
        for c0 in range(0, WIN_TILE, LANES):
            l_w = l_w + p[:, c0:c0 + LANES]
        acc_w = acc_w + _dot(p.astype(BF16), vw_ref[0, pl.ds(win_k0[j], WIN_TILE), :])
    o_win = acc_w / jnp.sum(l_w, axis=1, keepdims=True)

    cmp_i = lax.broadcasted_iota(jnp.int32, (qb, n_cmp), 1)
    cbias = per_head([jnp.where((cmp_i * CMP_STRIDE + CMP_BLOCK - 1 <= tok(qi, cmp_i.shape)) & (cmp_i < n_cmp - 1),
                                0.0, NEG_INF) for qi in range(nq)])
    sc = _mm(qq, kcmp_ref[0], "xa", _dot_nt) + cbias
    mx = jnp.max(sc, axis=1, keepdims=True)
    e = jnp.where(cbias == 0.0, jnp.exp(sc - mx), 0.0)
    den = jnp.sum(e, axis=1, keepdims=True)
    p_c = e / jnp.where(den > 0.0, den, 1.0)
    o_cmp = _dot(p_c.astype(BF16), vcmp_ref[0].astype(BF16))

    pc_sums = []
    for qg in range(nq * NSA_KV_HEADS):
        r0 = qg * NSA_HPG * qb
        acc = p_c[r0:r0 + qb]
        for h in range(1, NSA_HPG):
            acc = acc + p_c[r0 + h * qb:r0 + (h + 1) * qb]
        pc_sums.append(acc)
    pcs = jnp.concatenate(pc_sums, axis=0)
    imp = _dot_nt(ovt_ref[...], pcs, HI)
    nl = nq * NSA_KV_HEADS * qb
    blk = lax.broadcasted_iota(jnp.int32, (n_sb, nl), 0)
    s_lane = s2 * nq + lax.broadcasted_iota(jnp.int32, (n_sb, nl), 1) // (NSA_KV_HEADS * qb)
    forced = (blk == 0) | (blk == s_lane) | (blk == s_lane - 1)
    imp = jnp.where(blk <= s_lane, imp + jnp.where(forced, FORCE_BONUS, 0.0), NEG_INF)
    n_sel = min(SEL_TOPK, n_sb)
    sel_scr[...] = jnp.where(blk <= s_lane, 1.0, 0.0)

    @pl.when(s_last >= n_sel)
    def _():
        imp_scr[...] = imp
        cnt_scr[...] = jnp.zeros(cnt_scr.shape, F32)
        sub = lax.broadcasted_iota(jnp.int32, (8, nl), 0)
        for ig in range(n_sb // 8):
            @pl.when(ig * 8 <= s_last)
            def _(ig=ig):
                impv = imp_scr[...]
                cnt = [cnt_scr[rg * 8:(rg + 1) * 8, :] for rg in range(n_sb // 8)]
                for i2 in range(ig * 8, ig * 8 + 8):
                    vi = impv[i2:i2 + 1, :]
                    for rg in range(n_sb // 8):
                        vj = impv[rg * 8:(rg + 1) * 8, :]
                        if rg < ig:
                            one = jnp.where(vi > vj, 1.0, 0.0)
                        elif rg > ig:
                            one = jnp.where(vi >= vj, 1.0, 0.0)
                        else:
                            one = jnp.where(sub > i2 - ig * 8, jnp.where(vi >= vj, 1.0, 0.0),
                                            jnp.where(vi > vj, 1.0, 0.0))
                        cnt[rg] = cnt[rg] + one
                for rg in range(n_sb // 8):
                    cnt_scr[rg * 8:(rg + 1) * 8, :] = cnt[rg]
        sel_scr[...] = jnp.where((cnt_scr[...] < n_sel) & (blk <= s_lane), 1.0, 0.0)

    sel = sel_scr[...].T.astype(BF16)

    r_s = lax.broadcasted_iota(jnp.int32, (nl, SEL_TILE), 0)
    tok_s = (s2 * nq + r_s // (NSA_KV_HEADS * qb)) * qb + r_s % qb
    lane_s = lax.broadcasted_iota(jnp.int32, (nl, SEL_TILE), 1)
    n_steps = (s_last * qb) // (SEL_TILE * SEL_UNROLL) + 1

    def pass1(it, mx_):
        for u in range(SEL_UNROLL):
            j = it * SEL_UNROLL + u
            k0 = pl.multiple_of(j * SEL_TILE, SEL_TILE)
            chosen = _dot(sel, exp_ref[j])
            ok = (chosen > 0.5) & (k0 + lane_s <= tok_s)
            sc_ = _dot_nt(qq, ks_ref[0, pl.ds(k0, SEL_TILE), :]) + per_group_head(jnp.where(ok, 0.0, NEG_INF))
            sc_scr[j] = sc_
            for c0 in range(0, SEL_TILE, LANES):
                mx_ = jnp.maximum(mx_, sc_[:, c0:c0 + LANES])
        return mx_

    mx_ = lax.fori_loop(0, n_steps, pass1, jnp.full((rows, LANES), NEG_INF, F32))
    m_scr[...] = jnp.broadcast_to(jnp.max(mx_, axis=1, keepdims=True), m_scr.shape)
    l_scr[...] = jnp.zeros(l_scr.shape, F32)
    acc_scr[...] = jnp.zeros(acc_scr.shape, F32)

    def pass2(it, c):
        for u in range(SEL_UNROLL):
            j = it * SEL_UNROLL + u
            k0 = pl.multiple_of(j * SEL_TILE, SEL_TILE)
            m_b = jnp.concatenate([m_scr[...]] * (SEL_TILE // LANES), axis=1)
            p = jnp.exp(sc_scr[j] - m_b)
            l_new = l_scr[...]
            for c0 in range(0, SEL_TILE, LANES):
                l_new = l_new + p[:, c0:c0 + LANES]
            l_scr[...] = l_new
            acc_scr[...] += _dot(p.astype(BF16), vs_ref[0, pl.ds(k0, SEL_TILE), :])
        return c

    lax.fori_loop(0, n_steps, pass2, 0)
    o_sel = acc_scr[...] / jnp.sum(l_scr[...], axis=1, keepdims=True)

    low_half = lane64 < HEAD_DIM
    for qi in range(nq):
        ts = slice(qi * qb, (qi + 1) * qb)
        for h in range(NSA_HPG):
            halves = []
            for gi in range(NSA_KV_HEADS):
                hq = gi * NSA_HPG + h
                r0 = (qi * n_head + hq) * qb
                hr = slice(r0, r0 + qb)
                col = hq * N_BRANCH
                halves.append(g_scr[col, ts] * o_cmp[hr] + g_scr[col + 1, ts] * o_sel[hr] + g_scr[col + 2, ts] * o_win[hr])
            o_ref[ts, h * LANES:(h + 1) * LANES] = jnp.where(low_half, halves[0], halves[1]).astype(o_ref.dtype)


def _nsa_attention(q, k_cmp, v_cmp, ks, vs, kw, vw, gl, b, t):
    tq = NSA_QBLK * Q_BLOCK
    ns = t // tq
    n_sb = t // SEL_BLOCK
    n_cmp = t // CMP_STRIDE
    n_kt = t // SEL_TILE
    cmp_start = jnp.arange(n_cmp) * CMP_STRIDE
    sb = jnp.arange(n_sb)
    ovt = ((cmp_start[None, :] < (sb[:, None] + 1) * SEL_BLOCK)
           & (cmp_start[None, :] + CMP_BLOCK > sb[:, None] * SEL_BLOCK)
           & (jnp.arange(n_cmp)[None, :] < n_cmp - 1)).astype(F32)
    key_blk = (jnp.arange(n_kt)[:, None] * SEL_TILE + jnp.arange(SEL_TILE)[None, :]) // SEL_BLOCK
    expand = (key_blk[:, None, :] == sb[None, :, None]).astype(BF16)
    rows = NSA_QBLK * NSA_Q_HEADS * Q_BLOCK
    nl = NSA_QBLK * NSA_KV_HEADS * Q_BLOCK
    qspec = pl.BlockSpec((tq, Q_PAD_COLS), lambda bi, si: (bi * ns + si, 0))
    seq = lambda: pl.BlockSpec((1, t, LANES), lambda bi, si: (bi, 0, 0))
    cmp_spec = lambda: pl.BlockSpec((1, n_cmp, LANES), lambda bi, si: (bi, 0, 0))
    k3 = lambda a_: a_.reshape(b, t, LANES)
    return pl.pallas_call(
        _nsa_kernel,
        grid=(b, ns),
        in_specs=[qspec, cmp_spec(), cmp_spec(), seq(), seq(), seq(), seq(),
                  pl.BlockSpec((tq, LANES), lambda bi, si: (bi * ns + si, 0)),
                  pl.BlockSpec((n_sb, n_cmp), lambda bi, si: (0, 0)),
                  pl.BlockSpec((n_kt, n_sb, SEL_TILE), lambda bi, si: (0, 0, 0))],
        out_specs=pl.BlockSpec((tq, NSA_WIDTH), lambda bi, si: (bi * ns + si, 0)),
        out_shape=jax.ShapeDtypeStruct((b * t, NSA_WIDTH), BF16),
        scratch_shapes=[pltpu.VMEM((rows, LANES), F32),
                        pltpu.VMEM((rows, LANES), F32),
                        pltpu.VMEM((rows, LANES), F32),
                        pltpu.VMEM((n_kt, rows, SEL_TILE), F32),
                        pltpu.VMEM((n_sb, nl), F32),
                        pltpu.VMEM((n_sb, nl), F32),
                        pltpu.VMEM((n_sb, nl), F32),
                        pltpu.VMEM((NSA_Q_HEADS * N_BRANCH, tq, LANES), F32)],
        compiler_params=_params("parallel", "arbitrary"),
        name="nsa_attention",
    )(q, k_cmp, v_cmp, k3(ks), k3(vs), k3(kw), k3(vw), gl, ovt, expand)


FF_TILE = 256


def _ffn_kernel(x_ref, orw_ref, ons_ref, wor_ref, won_ref, n2_ref, w1_ref, w3_ref, w2_ref, nf_ref,
                out_ref, h_scr, u_scr, acc_scr):
    j = pl.program_id(1)

    @pl.when(j == 0)
    def _():
        h = x_ref[...] + _dot(orw_ref[...], wor_ref[...]) + _dot(ons_ref[...], won_ref[...])
        h_scr[...] = h
        ms = jnp.mean(h * h, axis=-1, keepdims=True)
        u_scr[...] = (h * lax.rsqrt(ms + RMS_EPS) * n2_ref[...]).astype(BF16)
        acc_scr[...] = jnp.zeros_like(acc_scr)

    u = u_scr[...]
    gate = _dot(u, w1_ref[...])
    up = _dot(u, w3_ref[...])
    act = (jax.nn.silu(gate) * up).astype(BF16)
    acc_scr[...] += _dot(act, w2_ref[...])

    @pl.when(j == pl.num_programs(1) - 1)
    def _():
        h = h_scr[...] + acc_scr[...]
        ms = jnp.mean(h * h, axis=-1, keepdims=True)
        out_ref[...] = h * lax.rsqrt(ms + RMS_EPS) * nf_ref[...]


def _out_ffn(x2, o_rwkv, o_nsa, w_out, norm2_w, ffn_w1, ffn_w3, ffn_w2, final_norm_w):
    n = x2.shape[0]
    tm = 1024
    w_or = w_out[:RWKV_WIDTH].astype(BF16)
    w_n = w_out[RWKV_WIDTH:].reshape(NSA_KV_HEADS, NSA_HPG, HEAD_DIM, D_MODEL)
    w_on = w_n.transpose(1, 0, 2, 3).reshape(NSA_WIDTH, D_MODEL).astype(BF16)
    row = lambda width: pl.BlockSpec((tm, width), lambda i, j: (i, 0))
    const = lambda shape: pl.BlockSpec(shape, lambda i, j: (0, 0))
    return pl.pallas_call(
        _ffn_kernel,
        grid=(n // tm, D_FF // FF_TILE),
        in_specs=[row(D_MODEL), row(RWKV_WIDTH), row(NSA_WIDTH),
                  const((RWKV_WIDTH, D_MODEL)), const((NSA_WIDTH, D_MODEL)), const((1, D_MODEL)),
                  pl.BlockSpec((D_MODEL, FF_TILE), lambda i, j: (0, j)),
                  pl.BlockSpec((D_MODEL, FF_TILE), lambda i, j: (0, j)),
                  pl.BlockSpec((FF_TILE, D_MODEL), lambda i, j: (j, 0)),
                  const((1, D_MODEL))],
        out_specs=row(D_MODEL),
        out_shape=jax.ShapeDtypeStruct((n, D_MODEL), F32),
        scratch_shapes=[pltpu.VMEM((tm, D_MODEL), F32), pltpu.VMEM((tm, D_MODEL), BF16),
                        pltpu.VMEM((tm, D_MODEL), F32)],
        compiler_params=_params("parallel", "arbitrary"),
        name="out_ffn",
    )(x2, o_rwkv, o_nsa, w_or, w_on, norm2_w.reshape(1, D_MODEL), ffn_w1.astype(BF16), ffn_w3.astype(BF16),
      ffn_w2.astype(BF16), final_norm_w.reshape(1, D_MODEL))


def kernel(x, positions, norm1_w, w_in, mu_rwkv, w0, w_lora_up, a0, a_lora_up, g_lora_up, k_k, k_a, r_k, lnx_w, lnx_b, cmp_pos_k, cmp_pos_v, cmp_k_w1, cmp_k_w2, cmp_v_w1, cmp_v_w2, w_out, norm2_w, ffn_w1, ffn_w3, ffn_w2, final_norm_w):
    b, t, d = x.shape
    assert d == D_MODEL and norm1_w.shape[0] == 1, "single-layer block with d_model 1024"
    assert t % RWKV_TT == 0 and t % (SEL_UNROLL * SEL_TILE) == 0 and t % (NSA_QBLK * Q_BLOCK) == 0
    x2 = x.reshape(b * t, d)
    yr, q, kc, vc, ks, vs, kw, vw, gl = _in_proj(x2, norm1_w[0], _pack_w_in(w_in[0]), positions)
    o_rwkv = _rwkv(yr, b, t, mu_rwkv[0], w0[0], w_lora_up[0], a0[0], a_lora_up[0], g_lora_up[0],
                   k_k[0], k_a[0], r_k[0], lnx_w[0], lnx_b[0])
    k_cmp, v_cmp = _compress(kc, vc, b, t, cmp_pos_k[0], cmp_pos_v[0], cmp_k_w1[0], cmp_k_w2[0],
                             cmp_v_w1[0], cmp_v_w2[0])
    o_nsa = _nsa_attention(q, k_cmp, v_cmp, ks, vs, kw, vw, gl, b, t)
    out = _out_ffn(x2, o_rwkv, o_nsa, w_out[0], norm2_w[0], ffn_w1[0], ffn_w3[0], ffn_w2[0], final_norm_w)
    return out.reshape(b, t, d)
```

```python
import math

import jax
import jax.numpy as jnp
from jax import lax
from jax.experimental import pallas as pl
from jax.experimental.pallas import tpu as pltpu

F32 = jnp.float32
BF16 = jnp.bfloat16
HI = lax.Precision.HIGHEST

D_MODEL = 1024
HEAD_DIM = 64
RWKV_WIDTH = 512
RWKV_HEADS = 8
D_DECAY_LORA = 64
D_AAA_LORA = 64
D_GATE_LORA = 128
RWKV_COLS = 3 * RWKV_WIDTH + D_DECAY_LORA + D_AAA_LORA + D_GATE_LORA
NSA_WIDTH = 512
NSA_Q_HEADS = 8
NSA_KV_HEADS = 2
NSA_HPG = 4
NSA_KV_WIDTH = 128
N_BRANCH = 3
CMP_BLOCK = 32
CMP_STRIDE = 16
CMP_HIDDEN = 256
SEL_BLOCK = 64
SEL_TOPK = 16
WINDOW = 512
Q_BLOCK = 64
ROPE_THETA = 10000.0
D_FF = 2816
RMS_EPS = 1e-6
LNX_EPS = 64e-5
FORCE_BONUS = 1e4
NEG_INF = -1e30

LANES = 128
CHUNK = 64
Q_PAD_COLS = NSA_Q_HEADS * LANES
PROJ_COLS = RWKV_COLS + Q_PAD_COLS + 6 * NSA_KV_WIDTH + LANES
VMEM_LIMIT = 56 * 1024 * 1024


def _dot(a, b, prec=None):
    return jnp.dot(a, b, preferred_element_type=F32, precision=prec)


def _dot_nt(a, b, prec=None):
    return lax.dot_general(a, b, (((1,), (1,)), ((), ())), preferred_element_type=F32, precision=prec)


def _dot_tn(a, b, prec=None):
    return lax.dot_general(a, b, (((0,), (0,)), ((), ())), preferred_element_type=F32, precision=prec)


def _split(a):
    hi = a.astype(BF16)
    return hi, (a - hi.astype(F32)).astype(BF16)


def _mm(a, b, mode, dot=_dot):
    if mode == "hi":
        return dot(a, b, HI)
    if mode == "b1":
        return dot(a.astype(BF16), b.astype(BF16))
    if mode == "xa":
        bh, bl = _split(b)
        ab = a.astype(BF16)
        return dot(ab, bh) + dot(ab, bl)
    if mode == "xb":
        ah, al = _split(a)
        bb = b.astype(BF16)
        return dot(ah, bb) + dot(al, bb)
    ah, al = _split(a)
    bh, bl = _split(b)
    return dot(ah, bh) + (dot(ah, bl) + dot(al, bh))


def _params(*sem):
    return pltpu.CompilerParams(dimension_semantics=sem, vmem_limit_bytes=VMEM_LIMIT)


def _proj_kernel(x_ref, n1_ref, w_ref, pos_ref, invf_ref, sign_ref,
                 yr_ref, q_ref, kc_ref, vc_ref, ks_ref, vs_ref, kw_ref, vw_ref, gl_ref, cos_scr, sin_scr):
    x = x_ref[...]
    ms = jnp.mean(x * x, axis=-1, keepdims=True)
    xn = (x * lax.rsqrt(ms + RMS_EPS) * n1_ref[...]).astype(BF16)
    ang = pos_ref[...].astype(F32) * invf_ref[...]
    cos_scr[...] = jnp.cos(ang)
    sin_scr[...] = jnp.sin(ang) * sign_ref[...]
    lane = lax.broadcasted_iota(jnp.int32, ang.shape, 1)
    first_half = (lane % HEAD_DIM) < (HEAD_DIM // 2)

    def rope(v):
        rot = jnp.where(first_half, pltpu.roll(v, LANES - HEAD_DIM // 2, 1), pltpu.roll(v, HEAD_DIM // 2, 1))
        return v * cos_scr[...] + rot * sin_scr[...]

    def cols(c0, width):
        return _dot(xn, w_ref[:, c0:c0 + width])

    for j in range(RWKV_COLS // 256):
        yr_ref[:, j * 256:(j + 1) * 256] = cols(j * 256, 256)
    scale = HEAD_DIM ** -0.5
    for hq in range(0, NSA_Q_HEADS, 2):
        q2 = cols(RWKV_COLS + hq * LANES, 2 * LANES)
        for e in range(2):
            q = rope(q2[:, e * LANES:(e + 1) * LANES]) * scale
            q_ref[:, (hq + e) * LANES:(hq + e + 1) * LANES] = q.astype(BF16)
    c0 = RWKV_COLS + Q_PAD_COLS
    kv = cols(c0, 2 * LANES)
    kc_ref[...] = rope(kv[:, :LANES])
    vc_ref[...] = kv[:, LANES:]
    kv = cols(c0 + 2 * LANES, 2 * LANES)
    ks_ref[...] = rope(kv[:, :LANES]).astype(BF16)
    vs_ref[...] = kv[:, LANES:].astype(BF16)
    kv = cols(c0 + 4 * LANES, 2 * LANES)
    kw_ref[...] = rope(kv[:, :LANES]).astype(BF16)
    vw_ref[...] = kv[:, LANES:].astype(BF16)
    gl_ref[...] = cols(c0 + 6 * LANES, LANES)


def _pack_w_in(w_in):
    w_r = w_in[:, :RWKV_COLS]
    w_q = w_in[:, RWKV_COLS:RWKV_COLS + NSA_WIDTH].reshape(D_MODEL, NSA_KV_HEADS, NSA_HPG, HEAD_DIM)
    zero = jnp.zeros_like(w_q)
    w_q = jnp.stack([jnp.where(jnp.arange(NSA_KV_HEADS)[None, :, None, None] == gg, w_q, zero)
                     for gg in range(NSA_KV_HEADS)], axis=3)
    w_q = w_q.reshape(D_MODEL, Q_PAD_COLS)
    c0 = RWKV_COLS + NSA_WIDTH
    w_kv = w_in[:, c0:c0 + 6 * NSA_KV_WIDTH]
    w_g = w_in[:, c0 + 6 * NSA_KV_WIDTH:]
    w_g = jnp.pad(w_g, ((0, 0), (0, LANES - w_g.shape[1])))
    return jnp.concatenate([w_r, w_q, w_kv, w_g], axis=1).astype(BF16)


def _in_proj(x2, norm1_w, w_packed, positions):
    n = x2.shape[0]
    tm = 512
    half = HEAD_DIM // 2
    inv_freq = ROPE_THETA ** (-jnp.arange(half, dtype=F32) / half)
    invf = jnp.tile(inv_freq, LANES // half).reshape(1, LANES)
    lane = jnp.arange(LANES)
    sign = jnp.where((lane % HEAD_DIM) < half, -1.0, 1.0).astype(F32).reshape(1, LANES)
    row = lambda width: pl.BlockSpec((tm, width), lambda i: (i, 0))
    const = lambda shape: pl.BlockSpec(shape, lambda i: (0, 0))
    out_shape = [
        jax.ShapeDtypeStruct((n, RWKV_COLS), F32),
        jax.ShapeDtypeStruct((n, Q_PAD_COLS), BF16),
        jax.ShapeDtypeStruct((n, LANES), F32),
        jax.ShapeDtypeStruct((n, LANES), F32),
        jax.ShapeDtypeStruct((n, LANES), BF16),
        jax.ShapeDtypeStruct((n, LANES), BF16),
        jax.ShapeDtypeStruct((n, LANES), BF16),
        jax.ShapeDtypeStruct((n, LANES), BF16),
        jax.ShapeDtypeStruct((n, LANES), F32),
    ]
    out_specs = [row(RWKV_COLS), row(Q_PAD_COLS)] + [row(LANES)] * 7
    return pl.pallas_call(
        _proj_kernel,
        grid=(n // tm,),
        in_specs=[row(D_MODEL), const((1, D_MODEL)), const((D_MODEL, PROJ_COLS)), row(1),
                  const((1, LANES)), const((1, LANES))],
        out_specs=out_specs,
        out_shape=out_shape,
        scratch_shapes=[pltpu.VMEM((tm, LANES), F32), pltpu.VMEM((tm, LANES), F32)],
        compiler_params=_params("parallel"),
        name="in_proj",
    )(x2, norm1_w.reshape(1, D_MODEL), w_packed, positions.reshape(n, 1), invf, sign)


RWKV_TT = 256
P_LORA = "b3"
P_GATE = "b1"
P_HS = "b1"
P_CUM = "xa"


def _rwkv_kernel(y_ref, mu_ref, w0_ref, wl_ref, a0_ref, gup_ref, kk_ref, ka_ref, rk_ref,
                 lw_ref, lb_ref, hs_ref, tri_ref, o_ref, prev_ref, s_ref, o_scr):
    i = pl.program_id(1)

    @pl.when(i == 0)
    def _():
        prev_ref[...] = jnp.zeros_like(prev_ref)
        s_ref[...] = jnp.zeros_like(s_ref)

    tt = y_ref.shape[0]
    y = y_ref[...]
    row = lax.broadcasted_iota(jnp.int32, (tt, 1), 0)
    y_prev = jnp.where(row == 0, prev_ref[...], pltpu.roll(y, 1, 0))
    prev_ref[...] = y[tt - 1:tt, :]
    ys = y + (y_prev - y) * mu_ref[...]

    w_ = RWKV_WIDTH
    r = ys[:, 0:w_]
    k = ys[:, w_:2 * w_]
    v = ys[:, 2 * w_:3 * w_]
    z = ys[:, 3 * w_:3 * w_ + LANES]
    gd = ys[:, 3 * w_ + LANES:3 * w_ + 2 * LANES]
    lane = lax.broadcasted_iota(jnp.int32, z.shape, 1)
    zt = jnp.where(lane < D_DECAY_LORA, jnp.tanh(z), z)
    wa = _mm(zt, wl_ref[...], P_LORA)
    w_raw = w0_ref[...] + wa[:, :w_]
    a = jax.nn.sigmoid(a0_ref[...] + wa[:, w_:])
    logw = -jnp.exp(-jax.nn.softplus(-w_raw) - 0.5)
    g = _mm(jax.nn.sigmoid(gd), gup_ref[...], P_GATE)
    hs = hs_ref[...]
    kk = k * kk_ref[...]
    kk = kk * lax.rsqrt(jnp.maximum(_mm(kk * kk, hs, P_HS), 1e-12))
    k2 = k * (1.0 + (a - 1.0) * ka_ref[...])
    alpha = -kk
    beta = kk * a

    cw = _mm(tri_ref[...], logw, P_CUM)
    e_in = jnp.exp(cw)
    e_ex = jnp.exp(cw - logw)
    e_neg = jnp.exp(-cw)
    a_t = alpha * e_ex
    r_t = r * e_in
    b_t = beta * e_neg
    k_t = k2 * e_neg

    m0 = lax.broadcasted_iota(jnp.int32, (CHUNK, LANES), 1) < HEAD_DIM

    def bd(xc):
        return jnp.concatenate([jnp.where(m0, xc, 0.0), jnp.where(m0, 0.0, xc)], axis=0)

    n2 = 2 * CHUNK
    ri = lax.broadcasted_iota(jnp.int32, (n2, n2), 0) % CHUNK
    ci = lax.broadcasted_iota(jnp.int32, (n2, n2), 1) % CHUNK
    strict = ri > ci
    incl = ri >= ci
    eye = jnp.where(lax.broadcasted_iota(jnp.int32, (n2, n2), 0) == lax.broadcasted_iota(jnp.int32, (n2, n2), 1),
                    1.0, 0.0)

    units = [(c, p) for c in range(tt // CHUNK) for p in range(w_ // LANES)]
    w_c = {}
    st = {}
    for c in range(tt // CHUNK):
        rs = slice(c * CHUNK, (c + 1) * CHUNK)
        cw_last = cw[c * CHUNK + CHUNK - 1:c * CHUNK + CHUNK, :]
        e_tot = jnp.exp(cw_last - cw[rs, :])
        w_c[c] = jnp.exp(cw_last)
        for p in range(w_ // LANES):
            ls = slice(p * LANES, (p + 1) * LANES)
            a_bd = bd(a_t[rs, ls]).astype(BF16)
            r_bd = bd(r_t[rs, ls])
            b_bd = bd(b_t[rs, ls]).astype(BF16)
            k_bd = bd(k_t[rs, ls]).astype(BF16)
            gm = _dot_nt(jnp.concatenate([a_bd, r_bd.astype(BF16)], axis=0), jnp.concatenate([b_bd, k_bd], axis=0))
            st[c, p] = dict(
                a_bd=a_bd, r_bd=r_bd,
                v_bd=bd(v[rs, ls]).astype(BF16),
                bh_bd=bd(beta[rs, ls] * e_tot[:, ls]).astype(BF16),
                kh_bd=bd(k2[rs, ls] * e_tot[:, ls]).astype(BF16),
                l_ab=jnp.where(strict, gm[0:n2, 0:n2], 0.0),
                l_ak=jnp.where(strict, gm[0:n2, n2:], 0.0).astype(BF16),
                m_rb=jnp.where(incl, gm[n2:, 0:n2], 0.0).astype(BF16),
                m_rk=jnp.where(incl, gm[n2:, n2:], 0.0).astype(BF16))

    pw, tm_ = {}, {}
    for u_ in units:
        l_ab = st[u_]["l_ab"]
        lb = l_ab.astype(BF16)
        pw[u_] = _dot(lb, lb)
        tm_[u_] = eye + l_ab
    for level in range(5):
        for u_ in units:
            pb = pw[u_].astype(BF16)
            if level < 4:
                y = _dot(pb, jnp.concatenate([pb, tm_[u_].astype(BF16)], axis=1))
                pw[u_] = y[:, 0:n2]
                tm_[u_] = tm_[u_] + y[:, n2:]
            else:
                tm_[u_] = tm_[u_] + _dot(pb, tm_[u_].astype(BF16))

    zero_bd = jnp.zeros((n2, n2), BF16)
    lv = {u_: _dot(st[u_]["l_ak"], st[u_]["v_bd"]).astype(BF16) for u_ in units}
    au = {u_: _dot(tm_[u_].astype(BF16), jnp.concatenate([st[u_]["a_bd"], lv[u_]], axis=1)).astype(BF16)
          for u_ in units}
    for u_ in units:
        d_ = st[u_]
        mo = _dot(jnp.concatenate([d_["m_rb"], d_["m_rk"]], axis=1),
                  jnp.concatenate([au[u_], jnp.concatenate([zero_bd, d_["v_bd"]], axis=1)], axis=0))
        d_["r_hat"] = (d_["r_bd"] + mo[:, 0:n2]).astype(BF16)
        d_["o0"] = mo[:, n2:]
    for u_ in units:
        st[u_]["g"] = _dot_tn(au[u_][:, 0:n2], st[u_]["bh_bd"]).astype(BF16)
    for u_ in units:
        d_ = st[u_]
        d_["n"] = _dot_tn(jnp.concatenate([au[u_][:, n2:], d_["v_bd"]], axis=0),
                          jnp.concatenate([d_["bh_bd"], d_["kh_bd"]], axis=0))

    for c, p in units:
        d_ = st[c, p]
        rs = slice(c * CHUNK, (c + 1) * CHUNK)
        ls = slice(p * LANES, (p + 1) * LANES)
        s_old = s_ref[p]
        sb = s_old.astype(BF16)
        o_bd = _dot_nt(d_["r_hat"], sb) + d_["o0"]
        o_scr[rs, ls] = o_bd[0:CHUNK] + o_bd[CHUNK:]
        s_ref[p] = s_old * w_c[c][:, ls] + _dot(sb, d_["g"]) + d_["n"]

    o = o_scr[...]
    inv_n = 1.0 / HEAD_DIM
    mean = _mm(o, hs, P_HS) * inv_n
    d = o - mean
    var = _mm(d * d, hs, P_HS) * inv_n
    on = d * lax.rsqrt(var + LNX_EPS) * lw_ref[...] + lb_ref[...]
    bonus = _mm(r * k2 * rk_ref[...], hs, P_HS) * v
    o_ref[...] = ((on + bonus) * g).astype(o_ref.dtype)


def _rwkv(yr, b, t, mu, w0, w_lora_up, a0, a_lora_up, g_lora_up, k_k, k_a, r_k, lnx_w, lnx_b):
    tt = RWKV_TT
    nt = t // tt
    w_ = RWKV_WIDTH
    wl = jnp.zeros((LANES, 2 * w_), F32)
    wl = wl.at[:D_DECAY_LORA, :w_].set(w_lora_up).at[D_DECAY_LORA:, w_:].set(a_lora_up)
    head = jnp.arange(w_) // HEAD_DIM
    hs = (head[:, None] == head[None, :]).astype(F32)
    ti = jnp.arange(tt)
    tri = ((ti[:, None] // CHUNK == ti[None, :] // CHUNK) & (ti[:, None] >= ti[None, :])).astype(F32)
    vec = lambda a_, width: a_.reshape(1, width)
    const = lambda shape: pl.BlockSpec(shape, lambda bi, i: (0, 0))
    return pl.pallas_call(
        _rwkv_kernel,
        grid=(b, nt),
        in_specs=[pl.BlockSpec((tt, RWKV_COLS), lambda bi, i: (bi * nt + i, 0)),
                  const((1, RWKV_COLS)), const((1, w_)), const((LANES, 2 * w_)), const((1, w_)),
                  const((D_GATE_LORA, w_)), const((1, w_)), const((1, w_)), const((1, w_)),
                  const((1, w_)), const((1, w_)), const((w_, w_)), const((tt, tt))],
        out_specs=pl.BlockSpec((tt, w_), lambda bi, i: (bi * nt + i, 0)),
        out_shape=jax.ShapeDtypeStruct((b * t, w_), BF16),
        scratch_shapes=[pltpu.VMEM((1, RWKV_COLS), F32),
                        pltpu.VMEM((w_ // LANES, LANES, LANES), F32),
                        pltpu.VMEM((tt, w_), F32)],
        compiler_params=_params("parallel", "arbitrary"),
        name="rwkv7",
    )(yr, vec(mu, RWKV_COLS), vec(w0, w_), wl, vec(a0, w_), g_lora_up, vec(k_k, w_), vec(k_a, w_),
      vec(r_k, w_), vec(lnx_w, w_), vec(lnx_b, w_), hs, tri)


def _compress_kernel(zk_ref, zv_ref, pek_ref, pev_ref, wk1_ref, wv1_ref, wk2_ref, wv2_ref, kc_ref, vc_ref):
    def one(z_ref, pe_ref, w1_ref, w2_ref, out_ref):
        z = z_ref[0]
        nrow = z.shape[0]
        za = z + pe_ref[0:1, :]
        zb = z + pe_ref[1:2, :]
        acc = jnp.zeros((nrow, LANES), F32)
        for gi in range(NSA_KV_HEADS):
            ha = _mm(za, w1_ref[gi, 0], "b3")
            hb = _mm(zb, w1_ref[gi, 1], "b3")
            hid = ha + pltpu.roll(hb, nrow - 1, 0)
            acc = acc + _mm(jax.nn.gelu(hid), w2_ref[gi], "b3")
        out_ref[0] = acc

    one(zk_ref, pek_ref, wk1_ref, wk2_ref, kc_ref)
    one(zv_ref, pev_ref, wv1_ref, wv2_ref, vc_ref)


def _compress_weights(pos, w1, w2):
    half = CMP_BLOCK // 2
    w1r = w1.reshape(2, half, HEAD_DIM, CMP_HIDDEN)
    w1g = jnp.zeros((NSA_KV_HEADS, 2, half, NSA_KV_HEADS, HEAD_DIM, CMP_HIDDEN), F32)
    for gi in range(NSA_KV_HEADS):
        w1g = w1g.at[gi, :, :, gi].set(w1r)
    w1g = w1g.reshape(NSA_KV_HEADS, 2, half * NSA_KV_WIDTH, CMP_HIDDEN)
    w2g = jnp.zeros((NSA_KV_HEADS, CMP_HIDDEN, NSA_KV_HEADS, HEAD_DIM), F32)
    for gi in range(NSA_KV_HEADS):
        w2g = w2g.at[gi, :, gi].set(w2)
    w2g = w2g.reshape(NSA_KV_HEADS, CMP_HIDDEN, NSA_KV_WIDTH)
    pe = jnp.broadcast_to(pos.reshape(2, half, 1, HEAD_DIM), (2, half, NSA_KV_HEADS, HEAD_DIM))
    pe = pe.reshape(2, half * NSA_KV_WIDTH)
    return pe, w1g, w2g


def _compress(kc, vc, b, t, cmp_pos_k, cmp_pos_v, k_w1, k_w2, v_w1, v_w2):
    nrow = t // CMP_STRIDE
    zw = CMP_STRIDE * NSA_KV_WIDTH
    pek, wk1, wk2 = _compress_weights(cmp_pos_k, k_w1, k_w2)
    pev, wv1, wv2 = _compress_weights(cmp_pos_v, v_w1, v_w2)
    zspec = pl.BlockSpec((1, nrow, zw), lambda bi: (bi, 0, 0))
    ospec = pl.BlockSpec((1, nrow, LANES), lambda bi: (bi, 0, 0))
    c2 = lambda shape: pl.BlockSpec(shape, lambda bi: (0,) * len(shape))
    return pl.pallas_call(
        _compress_kernel,
        grid=(b,),
        in_specs=[zspec, zspec, c2((2, zw)), c2((2, zw)), c2(wk1.shape), c2(wv1.shape), c2(wk2.shape), c2(wv2.shape)],
        out_specs=[ospec, ospec],
        out_shape=[jax.ShapeDtypeStruct((b, nrow, LANES), F32)] * 2,
        compiler_params=_params("parallel"),
        name="nsa_compress",
    )(kc.reshape(b, nrow, zw), vc.reshape(b, nrow, zw), pek, pev, wk1, wv1, wk2, wv2)


SEL_TILE = 256
SEL_UNROLL = 4
WIN_TILE = 128
NSA_QBLK = 2


def _n_win_tiles():
    tq = NSA_QBLK * Q_BLOCK
    span = WINDOW + tq + WIN_TILE - math.gcd(WIN_TILE, tq)
    return -(-span // WIN_TILE)


def _nsa_kernel(q_ref, kcmp_ref, vcmp_ref, ks_ref, vs_ref, kw_ref, vw_ref, gl_ref, ovt_ref, exp_ref,
                o_ref, m_scr, l_scr, acc_scr, sc_scr, imp_scr, cnt_scr, sel_scr, g_scr):
    s2 = pl.program_id(1)
    qb = Q_BLOCK
    nq = NSA_QBLK
    n_head = NSA_Q_HEADS
    rows = nq * n_head * qb
    n_cmp = kcmp_ref.shape[1]
    n_sb = ovt_ref.shape[0]
    s_of = [s2 * nq + qi for qi in range(nq)]
    s_last = s_of[-1]

    def tok(qi, shape):
        return s_of[qi] * qb + lax.broadcasted_iota(jnp.int32, shape, 0)

    def per_head(x64s):
        return jnp.concatenate([x for x in x64s for _ in range(n_head)], axis=0)

    def per_group_head(x):
        parts = []
        for qi in range(nq):
            for gi in range(NSA_KV_HEADS):
                r0 = (qi * NSA_KV_HEADS + gi) * qb
                parts += [x[r0:r0 + qb]] * NSA_HPG
        return jnp.concatenate(parts, axis=0)

    lane64 = lax.broadcasted_iota(jnp.int32, (qb, LANES), 1)
    sig = jax.nn.sigmoid(gl_ref[...])
    for col in range(n_head * N_BRANCH):
        g_scr[col] = jnp.broadcast_to(sig[:, col:col + 1], (nq * qb, LANES))

    qq = jnp.concatenate([q_ref[qi * qb:(qi + 1) * qb, hq * LANES:(hq + 1) * LANES]
                          for qi in range(nq) for hq in range(n_head)], axis=0)

    n_win = _n_win_tiles()
    first = (s2 * nq * qb - WINDOW) // WIN_TILE
    lane_w = lax.broadcasted_iota(jnp.int32, (qb, WIN_TILE), 1)
    win_sc, win_k0 = [], []
    mxw = jnp.full((rows, LANES), NEG_INF, F32)
    for j in range(n_win):
        kt = jnp.maximum(first + j, 0)
        k0 = pl.multiple_of(kt * WIN_TILE, WIN_TILE)
        kp = k0 + lane_w + jnp.where(first + j >= 0, 0, 1 << 24)
        bias = per_head([jnp.where((kp <= tok(qi, kp.shape)) & (kp > tok(qi, kp.shape) - WINDOW), 0.0, NEG_INF)
                         for qi in range(nq)])
        sc_ = _dot_nt(qq, kw_ref[0, pl.ds(k0, WIN_TILE), :]) + bias
        for c0 in range(0, WIN_TILE, LANES):
            mxw = jnp.maximum(mxw, sc_[:, c0:c0 + LANES])
        win_sc.append(sc_)
        win_k0.append(k0)
    m_w = jnp.max(mxw, axis=1, keepdims=True)
    l_w = jnp.zeros((rows, LANES), F32)
    acc_w = jnp.zeros((rows, LANES), F32)
    for j in range(n_win):
        p = jnp.exp(win_sc[j] - m_w)
        for c0 in range(0, WIN_TILE, LANES):
            l_w = l_w + p[:, c0:c0 + LANES]
        acc_w = acc_w + _dot(p.astype(BF16), vw_ref[0, pl.ds(win_k0[j], WIN_TILE), :])
    o_win = acc_w / jnp.sum(l_w, axis=1, keepdims=True)

    cmp_i = lax.broadcasted_iota(jnp.int32, (qb, n_cmp), 1)
    cbias = per_head([jnp.where((cmp_i * CMP_STRIDE + CMP_BLOCK - 1 <= tok(qi, cmp_i.shape)) & (cmp_i < n_cmp - 1),
                                0.0, NEG_INF) for qi in range(nq)])
    sc = _mm(qq, kcmp_ref[0], "xa", _dot_nt) + cbias
    mx = jnp.max(sc, axis=1, keepdims=True)
    e = jnp.exp(sc - mx)
    den = jnp.sum(e, axis=1, keepdims=True)
    p_c = e * jnp.where(mx > 0.5 * NEG_INF, 1.0 / den, 0.0)
    o_cmp = _dot(p_c.astype(BF16), vcmp_ref[0].astype(BF16))

    pc_sums = []
    for qg in range(nq * NSA_KV_HEADS):
        r0 = qg * NSA_HPG * qb
        acc = p_c[r0:r0 + qb]
        for h in range(1, NSA_HPG):
            acc = acc + p_c[r0 + h * qb:r0 + (h + 1) * qb]
        pc_sums.append(acc)
    pcs = jnp.concatenate(pc_sums, axis=0)
    imp = _dot_nt(ovt_ref[...], pcs, HI)
    nl = nq * NSA_KV_HEADS * qb
    blk = lax.broadcasted_iota(jnp.int32, (n_sb, nl), 0)
    s_lane = s2 * nq + lax.broadcasted_iota(jnp.int32, (n_sb, nl), 1) // (NSA_KV_HEADS * qb)
    forced = (blk == 0) | (blk == s_lane) | (blk == s_lane - 1)
    imp = jnp.where(blk <= s_lane, imp + jnp.where(forced, FORCE_BONUS, 0.0), NEG_INF)
    n_sel = min(SEL_TOPK, n_sb)
    sel_scr[...] = jnp.where(blk <= s_lane, 1.0, 0.0)

    @pl.when(s_last >= n_sel)
    def _():
        imp_scr[...] = imp
        cnt_scr[...] = jnp.zeros(cnt_scr.shape, F32)
        sub = lax.broadcasted_iota(jnp.int32, (8, nl), 0)
        for ig in range(n_sb // 8):
            @pl.when(ig * 8 <= s_last)
            def _(ig=ig):
                impv = imp_scr[...]
                cnt = [cnt_scr[rg * 8:(rg + 1) * 8, :] for rg in range(n_sb // 8)]
                for i2 in range(ig * 8, ig * 8 + 8):
                    vi = impv[i2:i2 + 1, :]
                    for rg in range(n_sb // 8):
                        vj = impv[rg * 8:(rg + 1) * 8, :]
                        if rg < ig:
                            one = jnp.where(vi > vj, 1.0, 0.0)
                        elif rg > ig:
                            one = jnp.where(vi >= vj, 1.0, 0.0)
                        else:
                            one = jnp.where(sub > i2 - ig * 8, jnp.where(vi >= vj, 1.0, 0.0),
                                            jnp.where(vi > vj, 1.0, 0.0))
                        cnt[rg] = cnt[rg] + one
                for rg in range(n_sb // 8):
                    cnt_scr[rg * 8:(rg + 1) * 8, :] = cnt[rg]
        sel_scr[...] = jnp.where((cnt_scr[...] < n_sel) & (blk <= s_lane), 1.0, 0.0)

    sel = sel_scr[...].T.astype(BF16)

    r_s = lax.broadcasted_iota(jnp.int32, (nl, SEL_TILE), 0)
    tok_s = (s2 * nq + r_s // (NSA_KV_HEADS * qb)) * qb + r_s % qb
    lane_s = lax.broadcasted_iota(jnp.int32, (nl, SEL_TILE), 1)
    n_steps = (s_last * qb) // (SEL_TILE * SEL_UNROLL) + 1

    def pass1(it, mx_):
        for u in range(SEL_UNROLL):
            j = it * SEL_UNROLL + u
            k0 = pl.multiple_of(j * SEL_TILE, SEL_TILE)
            chosen = _dot(sel, exp_ref[j])
            ok = (chosen > 0.5) & (k0 + lane_s <= tok_s)
            sc_ = _dot_nt(qq, ks_ref[0, pl.ds(k0, SEL_TILE), :]) + per_group_head(jnp.where(ok, 0.0, NEG_INF))
            sc_scr[j] = sc_
            for c0 in range(0, SEL_TILE, LANES):
                mx_ = jnp.maximum(mx_, sc_[:, c0:c0 + LANES])
        return mx_

    mx_ = lax.fori_loop(0, n_steps, pass1, jnp.full((rows, LANES), NEG_INF, F32))
    m_scr[...] = jnp.broadcast_to(jnp.max(mx_, axis=1, keepdims=True), m_scr.shape)
    l_scr[...] = jnp.zeros(l_scr.shape, F32)
    acc_scr[...] = jnp.zeros(acc_scr.shape, F32)

    def pass2(it, c):
        for u in range(SEL_UNROLL):
            j = it * SEL_UNROLL + u
            k0 = pl.multiple_of(j * SEL_TILE, SEL_TILE)
            m_b = jnp.concatenate([m_scr[...]] * (SEL_TILE // LANES), axis=1)
            p = jnp.exp(sc_scr[j] - m_b)
            l_new = l_scr[...]
            for c0 in range(0, SEL_TILE, LANES):
                l_new = l_new + p[:, c0:c0 + LANES]
            l_scr[...] = l_new
            acc_scr[...] += _dot(p.astype(BF16), vs_ref[0, pl.ds(k0, SEL_TILE), :])
        return c

    lax.fori_loop(0, n_steps, pass2, 0)
    o_sel = acc_scr[...] / jnp.sum(l_scr[...], axis=1, keepdims=True)

    low_half = lane64 < HEAD_DIM
    for qi in range(nq):
        ts = slice(qi * qb, (qi + 1) * qb)
        for h in range(NSA_HPG):
            halves = []
            for gi in range(NSA_KV_HEADS):
                hq = gi * NSA_HPG + h
                r0 = (qi * n_head + hq) * qb
                hr = slice(r0, r0 + qb)
                col = hq * N_BRANCH
                halves.append(g_scr[col, ts] * o_cmp[hr] + g_scr[col + 1, ts] * o_sel[hr] + g_scr[col + 2, ts] * o_win[hr])
            o_ref[ts, h * LANES:(h + 1) * LANES] = jnp.where(low_half, halves[0], halves[1]).astype(o_ref.dtype)


def _nsa_attention(q, k_cmp, v_cmp, ks, vs, kw, vw, gl, b, t):
    tq = NSA_QBLK * Q_BLOCK
    ns = t // tq
    n_sb = t // SEL_BLOCK
    n_cmp = t // CMP_STRIDE
    n_kt = t // SEL_TILE
    cmp_start = jnp.arange(n_cmp) * CMP_STRIDE
    sb = jnp.arange(n_sb)
    ovt = ((cmp_start[None, :] < (sb[:, None] + 1) * SEL_BLOCK)
           & (cmp_start[None, :] + CMP_BLOCK > sb[:, None] * SEL_BLOCK)
           & (jnp.arange(n_cmp)[None, :] < n_cmp - 1)).astype(F32)
    key_blk = (jnp.arange(n_kt)[:, None] * SEL_TILE + jnp.arange(SEL_TILE)[None, :]) // SEL_BLOCK
    expand = (key_blk[:, None, :] == sb[None, :, None]).astype(BF16)
    rows = NSA_QBLK * NSA_Q_HEADS * Q_BLOCK
    nl = NSA_QBLK * NSA_KV_HEADS * Q_BLOCK
    qspec = pl.BlockSpec((tq, Q_PAD_COLS), lambda bi, si: (bi * ns + si, 0))
    seq = lambda: pl.BlockSpec((1, t, LANES), lambda bi, si: (bi, 0, 0))
    cmp_spec = lambda: pl.BlockSpec((1, n_cmp, LANES), lambda bi, si: (bi, 0, 0))
    k3 = lambda a_: a_.reshape(b, t, LANES)
    return pl.pallas_call(
        _nsa_kernel,
        grid=(b, ns),
        in_specs=[qspec, cmp_spec(), cmp_spec(), seq(), seq(), seq(), seq(),
                  pl.BlockSpec((tq, LANES), lambda bi, si: (bi * ns + si, 0)),
                  pl.BlockSpec((n_sb, n_cmp), lambda bi, si: (0, 0)),
                  pl.BlockSpec((n_kt, n_sb, SEL_TILE), lambda bi, si: (0, 0, 0))],
        out_specs=pl.BlockSpec((tq, NSA_WIDTH), lambda bi, si: (bi * ns + si, 0)),
        out_shape=jax.ShapeDtypeStruct((b * t, NSA_WIDTH), BF16),
        scratch_shapes=[pltpu.VMEM((rows, LANES), F32),
                        pltpu.VMEM((rows, LANES), F32),
                        pltpu.VMEM((rows, LANES), F32),
                        pltpu.VMEM((n_kt, rows, SEL_TILE), F32),
                        pltpu.VMEM((n_sb, nl), F32),
                        pltpu.VMEM((n_sb, nl), F32),
                        pltpu.VMEM((n_sb, nl), F32),
                        pltpu.VMEM((NSA_Q_HEADS * N_BRANCH, tq, LANES), F32)],
        compiler_params=_params("parallel", "arbitrary"),
        name="nsa_attention",
    )(q, k_cmp, v_cmp, k3(ks), k3(vs), k3(kw), k3(vw), gl, ovt, expand)


FF_TILE = 256


def _ffn_kernel(x_ref, orw_ref, ons_ref, wor_ref, won_ref, n2_ref, w1_ref, w3_ref, w2_ref, nf_ref,
                out_ref, h_scr, u_scr, acc_scr):
    j = pl.program_id(1)

    @pl.when(j == 0)
    def _():
        h = x_ref[...] + _dot(orw_ref[...], wor_ref[...]) + _dot(ons_ref[...], won_ref[...])
        h_scr[...] = h
        ms = jnp.mean(h * h, axis=-1, keepdims=True)
        u_scr[...] = (h * lax.rsqrt(ms + RMS_EPS) * n2_ref[...]).astype(BF16)
        acc_scr[...] = jnp.zeros_like(acc_scr)

    u = u_scr[...]
    gate = _dot(u, w1_ref[...])
    up = _dot(u, w3_ref[...])
    act = (jax.nn.silu(gate) * up).astype(BF16)
    acc_scr[...] += _dot(act, w2_ref[...])

    @pl.when(j == pl.num_programs(1) - 1)
    def _():
        h = h_scr[...] + acc_scr[...]
        ms = jnp.mean(h * h, axis=-1, keepdims=True)
        out_ref[...] = h * lax.rsqrt(ms + RMS_EPS) * nf_ref[...]


def _out_ffn(x2, o_rwkv, o_nsa, w_out, norm2_w, ffn_w1, ffn_w3, ffn_w2, final_norm_w):
    n = x2.shape[0]
    tm = 1024
    w_or = w_out[:RWKV_WIDTH].astype(BF16)
    w_n = w_out[RWKV_WIDTH:].reshape(NSA_KV_HEADS, NSA_HPG, HEAD_DIM, D_MODEL)
    w_on = w_n.transpose(1, 0, 2, 3).reshape(NSA_WIDTH, D_MODEL).astype(BF16)
    row = lambda width: pl.BlockSpec((tm, width), lambda i, j: (i, 0))
    const = lambda shape: pl.BlockSpec(shape, lambda i, j: (0, 0))
    return pl.pallas_call(
        _ffn_kernel,
        grid=(n // tm, D_FF // FF_TILE),
        in_specs=[row(D_MODEL), row(RWKV_WIDTH), row(NSA_WIDTH),
                  const((RWKV_WIDTH, D_MODEL)), const((NSA_WIDTH, D_MODEL)), const((1, D_MODEL)),
                  pl.BlockSpec((D_MODEL, FF_TILE), lambda i, j: (0, j)),
                  pl.BlockSpec((D_MODEL, FF_TILE), lambda i, j: (0, j)),
                  pl.BlockSpec((FF_TILE, D_MODEL), lambda i, j: (j, 0)),
                  const((1, D_MODEL))],
        out_specs=row(D_MODEL),
        out_shape=jax.ShapeDtypeStruct((n, D_MODEL), F32),
        scratch_shapes=[pltpu.VMEM((tm, D_MODEL), F32), pltpu.VMEM((tm, D_MODEL), BF16),
                        pltpu.VMEM((tm, D_MODEL), F32)],
        compiler_params=_params("parallel", "arbitrary"),
        name="out_ffn",
    )(x2, o_rwkv, o_nsa, w_or, w_on, norm2_w.reshape(1, D_MODEL), ffn_w1.astype(BF16), ffn_w3.astype(BF16),
      ffn_w2.astype(BF16), final_norm_w.reshape(1, D_MODEL))


def kernel(x, positions, norm1_w, w_in, mu_rwkv, w0, w_lora_up, a0, a_lora_up, g_lora_up, k_k, k_a, r_k, lnx_w, lnx_b, cmp_pos_k, cmp_pos_v, cmp_k_w1, cmp_k_w2, cmp_v_w1, cmp_v_w2, w_out, norm2_w, ffn_w1, ffn_w3, ffn_w2, final_norm_w):
    b, t, d = x.shape
    assert d == D_MODEL and norm1_w.shape[0] == 1, "single-layer block with d_model 1024"
    assert t % RWKV_TT == 0 and t % (SEL_UNROLL * SEL_TILE) == 0 and t % (NSA_QBLK * Q_BLOCK) == 0
    x2 = x.reshape(b * t, d)
    yr, q, kc, vc, ks, vs, kw, vw, gl = _in_proj(x2, norm1_w[0], _pack_w_in(w_in[0]), positions)
    o_rwkv = _rwkv(yr, b, t, mu_rwkv[0], w0[0], w_lora_up[0], a0[0], a_lora_up[0], g_lora_up[0],
                   k_k[0], k_a[0], r_k[0], lnx_w[0], lnx_b[0])
    k_cmp, v_cmp = _compress(kc, vc, b, t, cmp_pos_k[0], cmp_pos_v[0], cmp_k_w1[0], cmp_k_w2[0],
                             cmp_v_w1[0], cmp_v_w2[0])
    o_nsa = _nsa_attention(q, k_cmp, v_cmp, ks, vs, kw, vw, gl, b, t)
    out = _out_ffn(x2, o_rwkv, o_nsa, w_out[0], norm2_w[0], ffn_w1[0], ffn_w3[0], ffn_w2[0], final_norm_w)
    return out.reshape(b, t, d)
```

```python
import math

import jax
import jax.numpy as jnp
from jax import lax
from jax.experimental import pallas as pl
from jax.experimental.pallas import tpu as pltpu

F32 = jnp.float32
BF16 = jnp.bfloat16
HI = lax.Precision.HIGHEST

D_MODEL = 1024
HEAD_DIM = 64
RWKV_WIDTH = 512
RWKV_HEADS = 8
D_DECAY_LORA = 64
D_AAA_LORA = 64
D_GATE_LORA = 128
RWKV_COLS = 3 * RWKV_WIDTH + D_DECAY_LORA + D_AAA_LORA + D_GATE_LORA
NSA_WIDTH = 512
NSA_Q_HEADS = 8
NSA_KV_HEADS = 2
NSA_HPG = 4
NSA_KV_WIDTH = 128
N_BRANCH = 3
CMP_BLOCK = 32
CMP_STRIDE = 16
CMP_HIDDEN = 256
SEL_BLOCK = 64
SEL_TOPK = 16
WINDOW = 512
Q_BLOCK = 64
ROPE_THETA = 10000.0
D_FF = 2816
RMS_EPS = 1e-6
LNX_EPS = 64e-5
FORCE_BONUS = 1e4
NEG_INF = -1e30

LANES = 128
CHUNK = 64
Q_PAD_COLS = NSA_Q_HEADS * LANES
PROJ_COLS = RWKV_COLS + Q_PAD_COLS + 6 * NSA_KV_WIDTH + LANES
VMEM_LIMIT = 56 * 1024 * 1024


def _dot(a, b, prec=None):
    return jnp.dot(a, b, preferred_element_type=F32, precision=prec)


def _dot_nt(a, b, prec=None):
    return lax.dot_general(a, b, (((1,), (1,)), ((), ())), preferred_element_type=F32, precision=prec)


def _dot_tn(a, b, prec=None):
    return lax.dot_general(a, b, (((0,), (0,)), ((), ())), preferred_element_type=F32, precision=prec)


def _split(a):
    hi = a.astype(BF16)
    return hi, (a - hi.astype(F32)).astype(BF16)


def _mm(a, b, mode, dot=_dot):
    if mode == "hi":
        return dot(a, b, HI)
    if mode == "b1":
        return dot(a.astype(BF16), b.astype(BF16))
    if mode == "xa":
        bh, bl = _split(b)
        ab = a.astype(BF16)
        return dot(ab, bh) + dot(ab, bl)
    if mode == "xb":
        ah, al = _split(a)
        bb = b.astype(BF16)
        return dot(ah, bb) + dot(al, bb)
    ah, al = _split(a)
    bh, bl = _split(b)
    return dot(ah, bh) + (dot(ah, bl) + dot(al, bh))


def _params(*sem):
    return pltpu.CompilerParams(dimension_semantics=sem, vmem_limit_bytes=VMEM_LIMIT)


def _proj_kernel(x_ref, n1_ref, w_ref, pos_ref, invf_ref, sign_ref,
                 yr_ref, q_ref, kc_ref, vc_ref, ks_ref, vs_ref, kw_ref, vw_ref, gl_ref, cos_scr, sin_scr):
    x = x_ref[...]
    ms = jnp.mean(x * x, axis=-1, keepdims=True)
    xn = (x * lax.rsqrt(ms + RMS_EPS) * n1_ref[...]).astype(BF16)
    ang = pos_ref[...].astype(F32) * invf_ref[...]
    cos_scr[...] = jnp.cos(ang)
    sin_scr[...] = jnp.sin(ang) * sign_ref[...]
    lane = lax.broadcasted_iota(jnp.int32, ang.shape, 1)
    first_half = (lane % HEAD_DIM) < (HEAD_DIM // 2)

    def rope(v):
        rot = jnp.where(first_half, pltpu.roll(v, LANES - HEAD_DIM // 2, 1), pltpu.roll(v, HEAD_DIM // 2, 1))
        return v * cos_scr[...] + rot * sin_scr[...]

    def cols(c0, width):
        return _dot(xn, w_ref[:, c0:c0 + width])

    for j in range(RWKV_COLS // 256):
        yr_ref[:, j * 256:(j + 1) * 256] = cols(j * 256, 256)
    scale = HEAD_DIM ** -0.5
    for hq in range(0, NSA_Q_HEADS, 2):
        q2 = cols(RWKV_COLS + hq * LANES, 2 * LANES)
        for e in range(2):
            q = rope(q2[:, e * LANES:(e + 1) * LANES]) * scale
            q_ref[:, (hq + e) * LANES:(hq + e + 1) * LANES] = q.astype(BF16)
    c0 = RWKV_COLS + Q_PAD_COLS
    kv = cols(c0, 2 * LANES)
    kc_ref[...] = rope(kv[:, :LANES])
    vc_ref[...] = kv[:, LANES:]
    kv = cols(c0 + 2 * LANES, 2 * LANES)
    ks_ref[...] = rope(kv[:, :LANES]).astype(BF16)
    vs_ref[...] = kv[:, LANES:].astype(BF16)
    kv = cols(c0 + 4 * LANES, 2 * LANES)
    kw_ref[...] = rope(kv[:, :LANES]).astype(BF16)
    vw_ref[...] = kv[:, LANES:].astype(BF16)
    gl_ref[...] = cols(c0 + 6 * LANES, LANES)


def _pack_w_in(w_in):
    w_r = w_in[:, :RWKV_COLS]
    w_q = w_in[:, RWKV_COLS:RWKV_COLS + NSA_WIDTH].reshape(D_MODEL, NSA_KV_HEADS, NSA_HPG, HEAD_DIM)
    zero = jnp.zeros_like(w_q)
    w_q = jnp.stack([jnp.where(jnp.arange(NSA_KV_HEADS)[None, :, None, None] == gg, w_q, zero)
                     for gg in range(NSA_KV_HEADS)], axis=3)
    w_q = w_q.reshape(D_MODEL, Q_PAD_COLS)
    c0 = RWKV_COLS + NSA_WIDTH
    w_kv = w_in[:, c0:c0 + 6 * NSA_KV_WIDTH]
    w_g = w_in[:, c0 + 6 * NSA_KV_WIDTH:]
    w_g = jnp.pad(w_g, ((0, 0), (0, LANES - w_g.shape[1])))
    return jnp.concatenate([w_r, w_q, w_kv, w_g], axis=1).astype(BF16)


def _in_proj(x2, norm1_w, w_packed, positions):
    n = x2.shape[0]
    tm = 512
    half = HEAD_DIM // 2
    inv_freq = ROPE_THETA ** (-jnp.arange(half, dtype=F32) / half)
    invf = jnp.tile(inv_freq, LANES // half).reshape(1, LANES)
    lane = jnp.arange(LANES)
    sign = jnp.where((lane % HEAD_DIM) < half, -1.0, 1.0).astype(F32).reshape(1, LANES)
    row = lambda width: pl.BlockSpec((tm, width), lambda i: (i, 0))
    const = lambda shape: pl.BlockSpec(shape, lambda i: (0, 0))
    out_shape = [
        jax.ShapeDtypeStruct((n, RWKV_COLS), F32),
        jax.ShapeDtypeStruct((n, Q_PAD_COLS), BF16),
        jax.ShapeDtypeStruct((n, LANES), F32),
        jax.ShapeDtypeStruct((n, LANES), F32),
        jax.ShapeDtypeStruct((n, LANES), BF16),
        jax.ShapeDtypeStruct((n, LANES), BF16),
        jax.ShapeDtypeStruct((n, LANES), BF16),
        jax.ShapeDtypeStruct((n, LANES), BF16),
        jax.ShapeDtypeStruct((n, LANES), F32),
    ]
    out_specs = [row(RWKV_COLS), row(Q_PAD_COLS)] + [row(LANES)] * 7
    return pl.pallas_call(
        _proj_kernel,
        grid=(n // tm,),
        in_specs=[row(D_MODEL), const((1, D_MODEL)), const((D_MODEL, PROJ_COLS)), row(1),
                  const((1, LANES)), const((1, LANES))],
        out_specs=out_specs,
        out_shape=out_shape,
        scratch_shapes=[pltpu.VMEM((tm, LANES), F32), pltpu.VMEM((tm, LANES), F32)],
        compiler_params=_params("parallel"),
        name="in_proj",
    )(x2, norm1_w.reshape(1, D_MODEL), w_packed, positions.reshape(n, 1), invf, sign)


RWKV_TT = 256
P_LORA = "b3"
P_GATE = "b1"
P_HS = "b1"
P_CUM = "xa"


def _rwkv_kernel(y_ref, mu_ref, w0_ref, wl_ref, a0_ref, gup_ref, kk_ref, ka_ref, rk_ref,
                 lw_ref, lb_ref, hs_ref, tri_ref, o_ref, prev_ref, s_ref, o_scr):
    i = pl.program_id(1)

    @pl.when(i == 0)
    def _():
        prev_ref[...] = jnp.zeros_like(prev_ref)
        s_ref[...] = jnp.zeros_like(s_ref)

    tt = y_ref.shape[0]
    y = y_ref[...]
    row = lax.broadcasted_iota(jnp.int32, (tt, 1), 0)
    y_prev = jnp.where(row == 0, prev_ref[...], pltpu.roll(y, 1, 0))
    prev_ref[...] = y[tt - 1:tt, :]
    ys = y + (y_prev - y) * mu_ref[...]

    w_ = RWKV_WIDTH
    r = ys[:, 0:w_]
    k = ys[:, w_:2 * w_]
    v = ys[:, 2 * w_:3 * w_]
    z = ys[:, 3 * w_:3 * w_ + LANES]
    gd = ys[:, 3 * w_ + LANES:3 * w_ + 2 * LANES]
    lane = lax.broadcasted_iota(jnp.int32, z.shape, 1)
    zt = jnp.where(lane < D_DECAY_LORA, jnp.tanh(z), z)
    wa = _mm(zt, wl_ref[...], P_LORA)
    w_raw = w0_ref[...] + wa[:, :w_]
    a = jax.nn.sigmoid(a0_ref[...] + wa[:, w_:])
    logw = -jnp.exp(-jax.nn.softplus(-w_raw) - 0.5)
    g = _mm(jax.nn.sigmoid(gd), gup_ref[...], P_GATE)
    hs = hs_ref[...]
    kk = k * kk_ref[...]
    kk = kk * lax.rsqrt(jnp.maximum(_mm(kk * kk, hs, P_HS), 1e-12))
    k2 = k * (1.0 + (a - 1.0) * ka_ref[...])
    alpha = -kk
    beta = kk * a

    cw = _mm(tri_ref[...], logw, P_CUM)
    e_in = jnp.exp(cw)
    e_ex = jnp.exp(cw - logw)
    e_neg = jnp.exp(-cw)
    a_t = alpha * e_ex
    r_t = r * e_in
    b_t = beta * e_neg
    k_t = k2 * e_neg

    m0 = lax.broadcasted_iota(jnp.int32, (CHUNK, LANES), 1) < HEAD_DIM

    def bd(xc):
        return jnp.concatenate([jnp.where(m0, xc, 0.0), jnp.where(m0, 0.0, xc)], axis=0)

    n2 = 2 * CHUNK
    ri = lax.broadcasted_iota(jnp.int32, (n2, n2), 0) % CHUNK
    ci = lax.broadcasted_iota(jnp.int32, (n2, n2), 1) % CHUNK
    strict = ri > ci
    incl = ri >= ci
    eye = jnp.where(lax.broadcasted_iota(jnp.int32, (n2, n2), 0) == lax.broadcasted_iota(jnp.int32, (n2, n2), 1),
                    1.0, 0.0)

    units = [(c, p) for c in range(tt // CHUNK) for p in range(w_ // LANES)]
    w_c = {}
    st = {}
    for c in range(tt // CHUNK):
        rs = slice(c * CHUNK, (c + 1) * CHUNK)
        cw_last = cw[c * CHUNK + CHUNK - 1:c * CHUNK + CHUNK, :]
        e_tot = jnp.exp(cw_last - cw[rs, :])
        w_c[c] = jnp.exp(cw_last)
        for p in range(w_ // LANES):
            ls = slice(p * LANES, (p + 1) * LANES)
            a_bd = bd(a_t[rs, ls]).astype(BF16)
            r_bd = bd(r_t[rs, ls])
            b_bd = bd(b_t[rs, ls]).astype(BF16)
            k_bd = bd(k_t[rs, ls]).astype(BF16)
            gm = _dot_nt(jnp.concatenate([a_bd, r_bd.astype(BF16)], axis=0), jnp.concatenate([b_bd, k_bd], axis=0))
            st[c, p] = dict(
                a_bd=a_bd, r_bd=r_bd,
                v_bd=bd(v[rs, ls]).astype(BF16),
                bh_bd=bd(beta[rs, ls] * e_tot[:, ls]).astype(BF16),
                kh_bd=bd(k2[rs, ls] * e_tot[:, ls]).astype(BF16),
                l_ab=jnp.where(strict, gm[0:n2, 0:n2], 0.0),
                l_ak=jnp.where(strict, gm[0:n2, n2:], 0.0).astype(BF16),
                m_rb=jnp.where(incl, gm[n2:, 0:n2], 0.0).astype(BF16),
                m_rk=jnp.where(incl, gm[n2:, n2:], 0.0).astype(BF16))

    pw, tm_ = {}, {}
    for u_ in units:
        l_ab = st[u_]["l_ab"]
        lb = l_ab.astype(BF16)
        pw[u_] = _dot(lb, lb)
        tm_[u_] = eye + l_ab
    for level in range(5):
        for u_ in units:
            pb = pw[u_].astype(BF16)
            if level < 4:
                y = _dot(pb, jnp.concatenate([pb, tm_[u_].astype(BF16)], axis=1))
                pw[u_] = y[:, 0:n2]
                tm_[u_] = tm_[u_] + y[:, n2:]
            else:
                tm_[u_] = tm_[u_] + _dot(pb, tm_[u_].astype(BF16))

    zero_bd = jnp.zeros((n2, n2), BF16)
    lv = {u_: _dot(st[u_]["l_ak"], st[u_]["v_bd"]).astype(BF16) for u_ in units}
    au = {u_: _dot(tm_[u_].astype(BF16), jnp.concatenate([st[u_]["a_bd"], lv[u_]], axis=1)).astype(BF16)
          for u_ in units}
    for u_ in units:
        d_ = st[u_]
        mo = _dot(jnp.concatenate([d_["m_rb"], d_["m_rk"]], axis=1),
                  jnp.concatenate([au[u_], jnp.concatenate([zero_bd, d_["v_bd"]], axis=1)], axis=0))
        d_["r_hat"] = (d_["r_bd"] + mo[:, 0:n2]).astype(BF16)
        d_["o0"] = mo[:, n2:]
    for u_ in units:
        st[u_]["g"] = _dot_tn(au[u_][:, 0:n2], st[u_]["bh_bd"]).astype(BF16)
    for u_ in units:
        d_ = st[u_]
        d_["n"] = _dot_tn(jnp.concatenate([au[u_][:, n2:], d_["v_bd"]], axis=0),
                          jnp.concatenate([d_["bh_bd"], d_["kh_bd"]], axis=0))

    for c, p in units:
        d_ = st[c, p]
        rs = slice(c * CHUNK, (c + 1) * CHUNK)
        ls = slice(p * LANES, (p + 1) * LANES)
        s_old = s_ref[p]
        sb = s_old.astype(BF16)
        o_bd = _dot_nt(d_["r_hat"], sb) + d_["o0"]
        o_scr[rs, ls] = o_bd[0:CHUNK] + o_bd[CHUNK:]
        s_ref[p] = s_old * w_c[c][:, ls] + _dot(sb, d_["g"]) + d_["n"]

    o = o_scr[...]
    inv_n = 1.0 / HEAD_DIM
    mean = _mm(o, hs, P_HS) * inv_n
    d = o - mean
    var = _mm(d * d, hs, P_HS) * inv_n
    on = d * lax.rsqrt(var + LNX_EPS) * lw_ref[...] + lb_ref[...]
    bonus = _mm(r * k2 * rk_ref[...], hs, P_HS) * v
    o_ref[...] = ((on + bonus) * g).astype(o_ref.dtype)


def _rwkv(yr, b, t, mu, w0, w_lora_up, a0, a_lora_up, g_lora_up, k_k, k_a, r_k, lnx_w, lnx_b):
    tt = RWKV_TT
    nt = t // tt
    w_ = RWKV_WIDTH
    wl = jnp.zeros((LANES, 2 * w_), F32)
    wl = wl.at[:D_DECAY_LORA, :w_].set(w_lora_up).at[D_DECAY_LORA:, w_:].set(a_lora_up)
    head = jnp.arange(w_) // HEAD_DIM
    hs = (head[:, None] == head[None, :]).astype(F32)
    ti = jnp.arange(tt)
    tri = ((ti[:, None] // CHUNK == ti[None, :] // CHUNK) & (ti[:, None] >= ti[None, :])).astype(F32)
    vec = lambda a_, width: a_.reshape(1, width)
    const = lambda shape: pl.BlockSpec(shape, lambda bi, i: (0, 0))
    return pl.pallas_call(
        _rwkv_kernel,
        grid=(b, nt),
        in_specs=[pl.BlockSpec((tt, RWKV_COLS), lambda bi, i: (bi * nt + i, 0)),
                  const((1, RWKV_COLS)), const((1, w_)), const((LANES, 2 * w_)), const((1, w_)),
                  const((D_GATE_LORA, w_)), const((1, w_)), const((1, w_)), const((1, w_)),
                  const((1, w_)), const((1, w_)), const((w_, w_)), const((tt, tt))],
        out_specs=pl.BlockSpec((tt, w_), lambda bi, i: (bi * nt + i, 0)),
        out_shape=jax.ShapeDtypeStruct((b * t, w_), BF16),
        scratch_shapes=[pltpu.VMEM((1, RWKV_COLS), F32),
                        pltpu.VMEM((w_ // LANES, LANES, LANES), F32),
                        pltpu.VMEM((tt, w_), F32)],
        compiler_params=_params("parallel", "arbitrary"),
        name="rwkv7",
    )(yr, vec(mu, RWKV_COLS), vec(w0, w_), wl, vec(a0, w_), g_lora_up, vec(k_k, w_), vec(k_a, w_),
      vec(r_k, w_), vec(lnx_w, w_), vec(lnx_b, w_), hs, tri)


def _compress_kernel(zk_ref, zv_ref, pek_ref, pev_ref, wk1_ref, wv1_ref, wk2_ref, wv2_ref, kc_ref, vc_ref):
    def one(z_ref, pe_ref, w1_ref, w2_ref, out_ref):
        z = z_ref[0]
        nrow = z.shape[0]
        za = z + pe_ref[0:1, :]
        zb = z + pe_ref[1:2, :]
        acc = jnp.zeros((nrow, LANES), F32)
        for gi in range(NSA_KV_HEADS):
            ha = _mm(za, w1_ref[gi, 0], "b3")
            hb = _mm(zb, w1_ref[gi, 1], "b3")
            hid = ha + pltpu.roll(hb, nrow - 1, 0)
            acc = acc + _mm(jax.nn.gelu(hid), w2_ref[gi], "b3")
        out_ref[0] = acc

    one(zk_ref, pek_ref, wk1_ref, wk2_ref, kc_ref)
    one(zv_ref, pev_ref, wv1_ref, wv2_ref, vc_ref)


def _compress_weights(pos, w1, w2):
    half = CMP_BLOCK // 2
    w1r = w1.reshape(2, half, HEAD_DIM, CMP_HIDDEN)
    w1g = jnp.zeros((NSA_KV_HEADS, 2, half, NSA_KV_HEADS, HEAD_DIM, CMP_HIDDEN), F32)
    for gi in range(NSA_KV_HEADS):
        w1g = w1g.at[gi, :, :, gi].set(w1r)
    w1g = w1g.reshape(NSA_KV_HEADS, 2, half * NSA_KV_WIDTH, CMP_HIDDEN)
    w2g = jnp.zeros((NSA_KV_HEADS, CMP_HIDDEN, NSA_KV_HEADS, HEAD_DIM), F32)
    for gi in range(NSA_KV_HEADS):
        w2g = w2g.at[gi, :, gi].set(w2)
    w2g = w2g.reshape(NSA_KV_HEADS, CMP_HIDDEN, NSA_KV_WIDTH)
    pe = jnp.broadcast_to(pos.reshape(2, half, 1, HEAD_DIM), (2, half, NSA_KV_HEADS, HEAD_DIM))
    pe = pe.reshape(2, half * NSA_KV_WIDTH)
    return pe, w1g, w2g


def _compress(kc, vc, b, t, cmp_pos_k, cmp_pos_v, k_w1, k_w2, v_w1, v_w2):
    nrow = t // CMP_STRIDE
    zw = CMP_STRIDE * NSA_KV_WIDTH
    pek, wk1, wk2 = _compress_weights(cmp_pos_k, k_w1, k_w2)
    pev, wv1, wv2 = _compress_weights(cmp_pos_v, v_w1, v_w2)
    zspec = pl.BlockSpec((1, nrow, zw), lambda bi: (bi, 0, 0))
    ospec = pl.BlockSpec((1, nrow, LANES), lambda bi: (bi, 0, 0))
    c2 = lambda shape: pl.BlockSpec(shape, lambda bi: (0,) * len(shape))
    return pl.pallas_call(
        _compress_kernel,
        grid=(b,),
        in_specs=[zspec, zspec, c2((2, zw)), c2((2, zw)), c2(wk1.shape), c2(wv1.shape), c2(wk2.shape), c2(wv2.shape)],
        out_specs=[ospec, ospec],
        out_shape=[jax.ShapeDtypeStruct((b, nrow, LANES), F32)] * 2,
        compiler_params=_params("parallel"),
        name="nsa_compress",
    )(kc.reshape(b, nrow, zw), vc.reshape(b, nrow, zw), pek, pev, wk1, wv1, wk2, wv2)


SEL_TILE = 256
SEL_UNROLL = 4
WIN_TILE = 128
NSA_QBLK = 2


def _n_win_tiles():
    tq = NSA_QBLK * Q_BLOCK
    span = WINDOW + tq + WIN_TILE - math.gcd(WIN_TILE, tq)
    return -(-span // WIN_TILE)


def _nsa_kernel(q_ref, kcmp_ref, vcmp_ref, ks_ref, vs_ref, kw_ref, vw_ref, gl_ref, ovt_ref, oh_ref,
                o_ref, m_scr, l_scr, acc_scr, sc_scr, imp_scr, cnt_scr, sel_scr, g_scr):
    s2 = pl.program_id(1)
    qb = Q_BLOCK
    nq = NSA_QBLK
    n_head = NSA_Q_HEADS
    rows = nq * n_head * qb
    n_cmp = kcmp_ref.shape[1]
    n_sb = ovt_ref.shape[0]
    s_of = [s2 * nq + qi for qi in range(nq)]
    s_last = s_of[-1]

    def tok(qi, shape):
        return s_of[qi] * qb + lax.broadcasted_iota(jnp.int32, shape, 0)

    def per_head(x64s):
        return jnp.concatenate([x for x in x64s for _ in range(n_head)], axis=0)

    def per_group_head(x):
        parts = []
        for qi in range(nq):
            for gi in range(NSA_KV_HEADS):
                r0 = (qi * NSA_KV_HEADS + gi) * qb
                parts += [x[r0:r0 + qb]] * NSA_HPG
        return jnp.concatenate(parts, axis=0)

    lane64 = lax.broadcasted_iota(jnp.int32, (qb, LANES), 1)
    sig = jax.nn.sigmoid(gl_ref[...])
    for col in range(n_head * N_BRANCH):
        g_scr[col] = jnp.broadcast_to(sig[:, col:col + 1], (nq * qb, LANES))

    qq = jnp.concatenate([q_ref[qi * qb:(qi + 1) * qb, hq * LANES:(hq + 1) * LANES]
                          for qi in range(nq) for hq in range(n_head)], axis=0)

    n_win = _n_win_tiles()
    first = (s2 * nq * qb - WINDOW) // WIN_TILE
    lane_w = lax.broadcasted_iota(jnp.int32, (qb, WIN_TILE), 1)
    win_sc, win_k0 = [], []
    mxw = jnp.full((rows, LANES), NEG_INF, F32)
    for j in range(n_win):
        kt = jnp.maximum(first + j, 0)
        k0 = pl.multiple_of(kt * WIN_TILE, WIN_TILE)
        kp = k0 + lane_w + jnp.where(first + j >= 0, 0, 1 << 24)
        bias = per_head([jnp.where((kp <= tok(qi, kp.shape)) & (kp > tok(qi, kp.shape) - WINDOW), 0.0, NEG_INF)
                         for qi in range(nq)])
        sc_ = _dot_nt(qq, kw_ref[0, pl.ds(k0, WIN_TILE), :]) + bias
        for c0 in range(0, WIN_TILE, LANES):
            mxw = jnp.maximum(mxw, sc_[:, c0:c0 + LANES])
        win_sc.append(sc_)
        win_k0.append(k0)
    m_w = jnp.max(mxw, axis=1, keepdims=True)
    l_w = jnp.zeros((rows, LANES), F32)
    acc_w = jnp.zeros((rows, LANES), F32)
    for j in range(n_win):
        p = jnp.exp(win_sc[j] - m_w)
        for c0 in range(0, WIN_TILE, LANES):
            l_w = l_w + p[:, c0:c0 + LANES]
        acc_w = acc_w + _dot(p.astype(BF16), vw_ref[0, pl.ds(win_k0[j], WIN_TILE), :])
    o_win = acc_w / jnp.sum(l_w, axis=1, keepdims=True)

    cmp_i = lax.broadcasted_iota(jnp.int32, (qb, n_cmp), 1)
    cbias = per_head([jnp.where((cmp_i * CMP_STRIDE + CMP_BLOCK - 1 <= tok(qi, cmp_i.shape)) & (cmp_i < n_cmp - 1),
                                0.0, NEG_INF) for qi in range(nq)])
    sc = _mm(qq, kcmp_ref[0], "xa", _dot_nt) + cbias
    mx = jnp.max(sc, axis=1, keepdims=True)
    e = jnp.exp(sc - mx)
    den = jnp.sum(e, axis=1, keepdims=True)
    p_c = e * jnp.where(mx > 0.5 * NEG_INF, 1.0 / den, 0.0)
    o_cmp = _dot(p_c.astype(BF16), vcmp_ref[0].astype(BF16))

    pc_sums = []
    for qg in range(nq * NSA_KV_HEADS):
        r0 = qg * NSA_HPG * qb
        acc = p_c[r0:r0 + qb]
        for h in range(1, NSA_HPG):
            acc = acc + p_c[r0 + h * qb:r0 + (h + 1) * qb]
        pc_sums.append(acc)
    pcs = jnp.concatenate(pc_sums, axis=0)
    imp = _dot_nt(ovt_ref[...], pcs, HI)
    nl = nq * NSA_KV_HEADS * qb
    blk = lax.broadcasted_iota(jnp.int32, (n_sb, nl), 0)
    s_lane = s2 * nq + lax.broadcasted_iota(jnp.int32, (n_sb, nl), 1) // (NSA_KV_HEADS * qb)
    forced = (blk == 0) | (blk == s_lane) | (blk == s_lane - 1)
    imp = jnp.where(blk <= s_lane, imp + jnp.where(forced, FORCE_BONUS, 0.0), NEG_INF)
    n_sel = min(SEL_TOPK, n_sb)
    sel_scr[...] = jnp.where(blk <= s_lane, 1.0, 0.0)

    @pl.when(s_last >= n_sel)
    def _():
        imp_scr[...] = imp
        cnt_scr[...] = jnp.zeros(cnt_scr.shape, F32)
        sub = lax.broadcasted_iota(jnp.int32, (8, nl), 0)
        for ig in range(n_sb // 8):
            @pl.when(ig * 8 <= s_last)
            def _(ig=ig):
                impv = imp_scr[...]
                cnt = [cnt_scr[rg * 8:(rg + 1) * 8, :] for rg in range(n_sb // 8)]
                for i2 in range(ig * 8, ig * 8 + 8):
                    vi = impv[i2:i2 + 1, :]
                    for rg in range(n_sb // 8):
                        vj = impv[rg * 8:(rg + 1) * 8, :]
                        if rg < ig:
                            one = jnp.where(vi > vj, 1.0, 0.0)
                        elif rg > ig:
                            one = jnp.where(vi >= vj, 1.0, 0.0)
                        else:
                            one = jnp.where(sub > i2 - ig * 8, jnp.where(vi >= vj, 1.0, 0.0),
                                            jnp.where(vi > vj, 1.0, 0.0))
                        cnt[rg] = cnt[rg] + one
                for rg in range(n_sb // 8):
                    cnt_scr[rg * 8:(rg + 1) * 8, :] = cnt[rg]
        sel_scr[...] = jnp.where((cnt_scr[...] < n_sel) & (blk <= s_lane), 1.0, 0.0)

    sel_t = sel_scr[...]
    if n_sb < LANES:
        sel_t = jnp.concatenate([sel_t, jnp.zeros((LANES - n_sb, nl), F32)], axis=0)
    unsel = ((sel_t.T - 1.0) * -NEG_INF).astype(BF16)
    q_ext = jnp.concatenate([qq, per_group_head(unsel)], axis=1)

    lane_s = lax.broadcasted_iota(jnp.int32, (qb, SEL_TILE), 1)
    n_steps = (s_last * qb) // (SEL_TILE * SEL_UNROLL) + 1

    def pass1(it, mx_):
        for u in range(SEL_UNROLL):
            j = it * SEL_UNROLL + u
            k0 = pl.multiple_of(j * SEL_TILE, SEL_TILE)
            causal = per_head([jnp.where(k0 + lane_s <= tok(qi, lane_s.shape), 0.0, NEG_INF) for qi in range(nq)])
            k_ext = jnp.concatenate([ks_ref[0, pl.ds(k0, SEL_TILE), :], oh_ref[j]], axis=1)
            sc_ = _dot_nt(q_ext, k_ext) + causal
            sc_scr[j] = sc_
            for c0 in range(0, SEL_TILE, LANES):
                mx_ = jnp.maximum(mx_, sc_[:, c0:c0 + LANES])
        return mx_

    mx_ = lax.fori_loop(0, n_steps, pass1, jnp.full((rows, LANES), NEG_INF, F32))
    m_scr[...] = jnp.broadcast_to(jnp.max(mx_, axis=1, keepdims=True), m_scr.shape)
    l_scr[...] = jnp.zeros(l_scr.shape, F32)
    acc_scr[...] = jnp.zeros(acc_scr.shape, F32)

    def pass2(it, c):
        for u in range(SEL_UNROLL):
            j = it * SEL_UNROLL + u
            k0 = pl.multiple_of(j * SEL_TILE, SEL_TILE)
            m_b = jnp.concatenate([m_scr[...]] * (SEL_TILE // LANES), axis=1)
            p = jnp.exp(sc_scr[j] - m_b)
            l_new = l_scr[...]
            for c0 in range(0, SEL_TILE, LANES):
                l_new = l_new + p[:, c0:c0 + LANES]
            l_scr[...] = l_new
            acc_scr[...] += _dot(p.astype(BF16), vs_ref[0, pl.ds(k0, SEL_TILE), :])
        return c

    lax.fori_loop(0, n_steps, pass2, 0)
    o_sel = acc_scr[...] / jnp.sum(l_scr[...], axis=1, keepdims=True)

    low_half = lane64 < HEAD_DIM
    for qi in range(nq):
        ts = slice(qi * qb, (qi + 1) * qb)
        for h in range(NSA_HPG):
            halves = []
            for gi in range(NSA_KV_HEADS):
                hq = gi * NSA_HPG + h
                r0 = (qi * n_head + hq) * qb
                hr = slice(r0, r0 + qb)
                col = hq * N_BRANCH
                halves.append(g_scr[col, ts] * o_cmp[hr] + g_scr[col + 1, ts] * o_sel[hr] + g_scr[col + 2, ts] * o_win[hr])
            o_ref[ts, h * LANES:(h + 1) * LANES] = jnp.where(low_half, halves[0], halves[1]).astype(o_ref.dtype)


def _nsa_attention(q, k_cmp, v_cmp, ks, vs, kw, vw, gl, b, t):
    tq = NSA_QBLK * Q_BLOCK
    ns = t // tq
    n_sb = t // SEL_BLOCK
    n_cmp = t // CMP_STRIDE
    n_kt = t // SEL_TILE
    cmp_start = jnp.arange(n_cmp) * CMP_STRIDE
    sb = jnp.arange(n_sb)
    ovt = ((cmp_start[None, :] < (sb[:, None] + 1) * SEL_BLOCK)
           & (cmp_start[None, :] + CMP_BLOCK > sb[:, None] * SEL_BLOCK)
           & (jnp.arange(n_cmp)[None, :] < n_cmp - 1)).astype(F32)
    key_blk = (jnp.arange(n_kt)[:, None] * SEL_TILE + jnp.arange(SEL_TILE)[None, :]) // SEL_BLOCK
    onehot = (key_blk[:, :, None] == jnp.arange(LANES)[None, None, :]).astype(BF16)
    rows = NSA_QBLK * NSA_Q_HEADS * Q_BLOCK
    nl = NSA_QBLK * NSA_KV_HEADS * Q_BLOCK
    qspec = pl.BlockSpec((tq, Q_PAD_COLS), lambda bi, si: (bi * ns + si, 0))
    seq = lambda: pl.BlockSpec((1, t, LANES), lambda bi, si: (bi, 0, 0))
    cmp_spec = lambda: pl.BlockSpec((1, n_cmp, LANES), lambda bi, si: (bi, 0, 0))
    k3 = lambda a_: a_.reshape(b, t, LANES)
    return pl.pallas_call(
        _nsa_kernel,
        grid=(b, ns),
        in_specs=[qspec, cmp_spec(), cmp_spec(), seq(), seq(), seq(), seq(),
                  pl.BlockSpec((tq, LANES), lambda bi, si: (bi * ns + si, 0)),
                  pl.BlockSpec((n_sb, n_cmp), lambda bi, si: (0, 0)),
                  pl.BlockSpec((n_kt, SEL_TILE, LANES), lambda bi, si: (0, 0, 0))],
        out_specs=pl.BlockSpec((tq, NSA_WIDTH), lambda bi, si: (bi * ns + si, 0)),
        out_shape=jax.ShapeDtypeStruct((b * t, NSA_WIDTH), BF16),
        scratch_shapes=[pltpu.VMEM((rows, LANES), F32),
                        pltpu.VMEM((rows, LANES), F32),
                        pltpu.VMEM((rows, LANES), F32),
                        pltpu.VMEM((n_kt, rows, SEL_TILE), F32),
                        pltpu.VMEM((n_sb, nl), F32),
                        pltpu.VMEM((n_sb, nl), F32),
                        pltpu.VMEM((n_sb, nl), F32),
                        pltpu.VMEM((NSA_Q_HEADS * N_BRANCH, tq, LANES), F32)],
        compiler_params=_params("parallel", "arbitrary"),
        name="nsa_attention",
    )(q, k_cmp, v_cmp, k3(ks), k3(vs), k3(kw), k3(vw), gl, ovt, onehot)


FF_TILE = 256


def _ffn_kernel(x_ref, orw_ref, ons_ref, wor_ref, won_ref, n2_ref, w1_ref, w3_ref, w2_ref, nf_ref,
                out_ref, h_scr, u_scr, acc_scr):
    j = pl.program_id(1)

    @pl.when(j == 0)
    def _():
        h = x_ref[...] + _dot(orw_ref[...], wor_ref[...]) + _dot(ons_ref[...], won_ref[...])
        h_scr[...] = h
        ms = jnp.mean(h * h, axis=-1, keepdims=True)
        u_scr[...] = (h * lax.rsqrt(ms + RMS_EPS) * n2_ref[...]).astype(BF16)
        acc_scr[...] = jnp.zeros_like(acc_scr)

    u = u_scr[...]
    gate = _dot(u, w1_ref[...])
    up = _dot(u, w3_ref[...])
    act = (jax.nn.silu(gate) * up).astype(BF16)
    acc_scr[...] += _dot(act, w2_ref[...])

    @pl.when(j == pl.num_programs(1) - 1)
    def _():
        h = h_scr[...] + acc_scr[...]
        ms = jnp.mean(h * h, axis=-1, keepdims=True)
        out_ref[...] = h * lax.rsqrt(ms + RMS_EPS) * nf_ref[...]


def _out_ffn(x2, o_rwkv, o_nsa, w_out, norm2_w, ffn_w1, ffn_w3, ffn_w2, final_norm_w):
    n = x2.shape[0]
    tm = 1024
    w_or = w_out[:RWKV_WIDTH].astype(BF16)
    w_n = w_out[RWKV_WIDTH:].reshape(NSA_KV_HEADS, NSA_HPG, HEAD_DIM, D_MODEL)
    w_on = w_n.transpose(1, 0, 2, 3).reshape(NSA_WIDTH, D_MODEL).astype(BF16)
    row = lambda width: pl.BlockSpec((tm, width), lambda i, j: (i, 0))
    const = lambda shape: pl.BlockSpec(shape, lambda i, j: (0, 0))
    return pl.pallas_call(
        _ffn_kernel,
        grid=(n // tm, D_FF // FF_TILE),
        in_specs=[row(D_MODEL), row(RWKV_WIDTH), row(NSA_WIDTH),
                  const((RWKV_WIDTH, D_MODEL)), const((NSA_WIDTH, D_MODEL)), const((1, D_MODEL)),
                  pl.BlockSpec((D_MODEL, FF_TILE), lambda i, j: (0, j)),
                  pl.BlockSpec((D_MODEL, FF_TILE), lambda i, j: (0, j)),
                  pl.BlockSpec((FF_TILE, D_MODEL), lambda i, j: (j, 0)),
                  const((1, D_MODEL))],
        out_specs=row(D_MODEL),
        out_shape=jax.ShapeDtypeStruct((n, D_MODEL), F32),
        scratch_shapes=[pltpu.VMEM((tm, D_MODEL), F32), pltpu.VMEM((tm, D_MODEL), BF16),
                        pltpu.VMEM((tm, D_MODEL), F32)],
        compiler_params=_params("parallel", "arbitrary"),
        name="out_ffn",
    )(x2, o_rwkv, o_nsa, w_or, w_on, norm2_w.reshape(1, D_MODEL), ffn_w1.astype(BF16), ffn_w3.astype(BF16),
      ffn_w2.astype(BF16), final_norm_w.reshape(1, D_MODEL))


def kernel(x, positions, norm1_w, w_in, mu_rwkv, w0, w_lora_up, a0, a_lora_up, g_lora_up, k_k, k_a, r_k, lnx_w, lnx_b, cmp_pos_k, cmp_pos_v, cmp_k_w1, cmp_k_w2, cmp_v_w1, cmp_v_w2, w_out, norm2_w, ffn_w1, ffn_w3, ffn_w2, final_norm_w):
    b, t, d = x.shape
    assert d == D_MODEL and norm1_w.shape[0] == 1, "single-layer block with d_model 1024"
    assert t % RWKV_TT == 0 and t % (SEL_UNROLL * SEL_TILE) == 0 and t % (NSA_QBLK * Q_BLOCK) == 0
    x2 = x.reshape(b * t, d)
    yr, q, kc, vc, ks, vs, kw, vw, gl = _in_proj(x2, norm1_w[0], _pack_w_in(w_in[0]), positions)
    o_rwkv = _rwkv(yr, b, t, mu_rwkv[0], w0[0], w_lora_up[0], a0[0], a_lora_up[0], g_lora_up[0],
                   k_k[0], k_a[0], r_k[0], lnx_w[0], lnx_b[0])
    k_cmp, v_cmp = _compress(kc, vc, b, t, cmp_pos_k[0], cmp_pos_v[0], cmp_k_w1[0], cmp_k_w2[0],
                             cmp_v_w1[0], cmp_v_w2[0])
    o_nsa = _nsa_attention(q, k_cmp, v_cmp, ks, vs, kw, vw, gl, b, t)
    out = _out_ffn(x2, o_rwkv, o_nsa, w_out[0], norm2_w[0], ffn_w1[0], ffn_w3[0], ffn_w2[0], final_norm_w)
    return out.reshape(b, t, d)
```

```python
import math

import jax
import jax.numpy as jnp
from jax import lax
from jax.experimental import pallas as pl
from jax.experimental.pallas import tpu as pltpu

F32 = jnp.float32
BF16 = jnp.bfloat16
HI = lax.Precision.HIGHEST

D_MODEL = 1024
HEAD_DIM = 64
RWKV_WIDTH = 512
RWKV_HEADS = 8
D_DECAY_LORA = 64
D_AAA_LORA = 64
D_GATE_LORA = 128
RWKV_COLS = 3 * RWKV_WIDTH + D_DECAY_LORA + D_AAA_LORA + D_GATE_LORA
NSA_WIDTH = 512
NSA_Q_HEADS = 8
NSA_KV_HEADS = 2
NSA_HPG = 4
NSA_KV_WIDTH = 128
N_BRANCH = 3
CMP_BLOCK = 32
CMP_STRIDE = 16
CMP_HIDDEN = 256
SEL_BLOCK = 64
SEL_TOPK = 16
WINDOW = 512
Q_BLOCK = 64
ROPE_THETA = 10000.0
D_FF = 2816
RMS_EPS = 1e-6
LNX_EPS = 64e-5
FORCE_BONUS = 1e4
NEG_INF = -1e30

LANES = 128
CHUNK = 64
Q_PAD_COLS = NSA_Q_HEADS * LANES
PROJ_COLS = RWKV_COLS + Q_PAD_COLS + 6 * NSA_KV_WIDTH + LANES
VMEM_LIMIT = 56 * 1024 * 1024


def _dot(a, b, prec=None):
    return jnp.dot(a, b, preferred_element_type=F32, precision=prec)


def _dot_nt(a, b, prec=None):
    return lax.dot_general(a, b, (((1,), (1,)), ((), ())), preferred_element_type=F32, precision=prec)


def _dot_tn(a, b, prec=None):
    return lax.dot_general(a, b, (((0,), (0,)), ((), ())), preferred_element_type=F32, precision=prec)


def _split(a):
    hi = a.astype(BF16)
    return hi, (a - hi.astype(F32)).astype(BF16)


def _mm(a, b, mode, dot=_dot):
    if mode == "hi":
        return dot(a, b, HI)
    if mode == "b1":
        return dot(a.astype(BF16), b.astype(BF16))
    if mode == "xa":
        bh, bl = _split(b)
        ab = a.astype(BF16)
        return dot(ab, bh) + dot(ab, bl)
    if mode == "xb":
        ah, al = _split(a)
        bb = b.astype(BF16)
        return dot(ah, bb) + dot(al, bb)
    ah, al = _split(a)
    bh, bl = _split(b)
    return dot(ah, bh) + (dot(ah, bl) + dot(al, bh))


def _params(*sem):
    return pltpu.CompilerParams(dimension_semantics=sem, vmem_limit_bytes=VMEM_LIMIT)


def _proj_kernel(x_ref, n1_ref, w_ref, pos_ref, invf_ref, sign_ref,
                 yr_ref, q_ref, kc_ref, vc_ref, ks_ref, vs_ref, kw_ref, vw_ref, gl_ref, cos_scr, sin_scr):
    x = x_ref[...]
    ms = jnp.mean(x * x, axis=-1, keepdims=True)
    xn = (x * lax.rsqrt(ms + RMS_EPS) * n1_ref[...]).astype(BF16)
    ang = pos_ref[...].astype(F32) * invf_ref[...]
    cos_scr[...] = jnp.cos(ang)
    sin_scr[...] = jnp.sin(ang) * sign_ref[...]
    lane = lax.broadcasted_iota(jnp.int32, ang.shape, 1)
    first_half = (lane % HEAD_DIM) < (HEAD_DIM // 2)

    def rope(v):
        rot = jnp.where(first_half, pltpu.roll(v, LANES - HEAD_DIM // 2, 1), pltpu.roll(v, HEAD_DIM // 2, 1))
        return v * cos_scr[...] + rot * sin_scr[...]

    def cols(c0, width):
        return _dot(xn, w_ref[:, c0:c0 + width])

    for j in range(RWKV_COLS // 256):
        yr_ref[:, j * 256:(j + 1) * 256] = cols(j * 256, 256)
    scale = HEAD_DIM ** -0.5
    for hq in range(0, NSA_Q_HEADS, 2):
        q2 = cols(RWKV_COLS + hq * LANES, 2 * LANES)
        for e in range(2):
            q = rope(q2[:, e * LANES:(e + 1) * LANES]) * scale
            q_ref[:, (hq + e) * LANES:(hq + e + 1) * LANES] = q.astype(BF16)
    c0 = RWKV_COLS + Q_PAD_COLS
    kv = cols(c0, 2 * LANES)
    kc_ref[...] = rope(kv[:, :LANES])
    vc_ref[...] = kv[:, LANES:]
    kv = cols(c0 + 2 * LANES, 2 * LANES)
    ks_ref[...] = rope(kv[:, :LANES]).astype(BF16)
    vs_ref[...] = kv[:, LANES:].astype(BF16)
    kv = cols(c0 + 4 * LANES, 2 * LANES)
    kw_ref[...] = rope(kv[:, :LANES]).astype(BF16)
    vw_ref[...] = kv[:, LANES:].astype(BF16)
    gl_ref[...] = cols(c0 + 6 * LANES, LANES)


def _pack_w_in(w_in):
    w_r = w_in[:, :RWKV_COLS]
    w_q = w_in[:, RWKV_COLS:RWKV_COLS + NSA_WIDTH].reshape(D_MODEL, NSA_KV_HEADS, NSA_HPG, HEAD_DIM)
    zero = jnp.zeros_like(w_q)
    w_q = jnp.stack([jnp.where(jnp.arange(NSA_KV_HEADS)[None, :, None, None] == gg, w_q, zero)
                     for gg in range(NSA_KV_HEADS)], axis=3)
    w_q = w_q.reshape(D_MODEL, Q_PAD_COLS)
    c0 = RWKV_COLS + NSA_WIDTH
    w_kv = w_in[:, c0:c0 + 6 * NSA_KV_WIDTH]
    w_g = w_in[:, c0 + 6 * NSA_KV_WIDTH:]
    w_g = jnp.pad(w_g, ((0, 0), (0, LANES - w_g.shape[1])))
    return jnp.concatenate([w_r, w_q, w_kv, w_g], axis=1).astype(BF16)


def _in_proj(x2, norm1_w, w_packed, positions):
    n = x2.shape[0]
    tm = 512
    half = HEAD_DIM // 2
    inv_freq = ROPE_THETA ** (-jnp.arange(half, dtype=F32) / half)
    invf = jnp.tile(inv_freq, LANES // half).reshape(1, LANES)
    lane = jnp.arange(LANES)
    sign = jnp.where((lane % HEAD_DIM) < half, -1.0, 1.0).astype(F32).reshape(1, LANES)
    row = lambda width: pl.BlockSpec((tm, width), lambda i: (i, 0))
    const = lambda shape: pl.BlockSpec(shape, lambda i: (0, 0))
    out_shape = [
        jax.ShapeDtypeStruct((n, RWKV_COLS), F32),
        jax.ShapeDtypeStruct((n, Q_PAD_COLS), BF16),
        jax.ShapeDtypeStruct((n, LANES), F32),
        jax.ShapeDtypeStruct((n, LANES), F32),
        jax.ShapeDtypeStruct((n, LANES), BF16),
        jax.ShapeDtypeStruct((n, LANES), BF16),
        jax.ShapeDtypeStruct((n, LANES), BF16),
        jax.ShapeDtypeStruct((n, LANES), BF16),
        jax.ShapeDtypeStruct((n, LANES), F32),
    ]
    out_specs = [row(RWKV_COLS), row(Q_PAD_COLS)] + [row(LANES)] * 7
    return pl.pallas_call(
        _proj_kernel,
        grid=(n // tm,),
        in_specs=[row(D_MODEL), const((1, D_MODEL)), const((D_MODEL, PROJ_COLS)), row(1),
                  const((1, LANES)), const((1, LANES))],
        out_specs=out_specs,
        out_shape=out_shape,
        scratch_shapes=[pltpu.VMEM((tm, LANES), F32), pltpu.VMEM((tm, LANES), F32)],
        compiler_params=_params("parallel"),
        name="in_proj",
    )(x2, norm1_w.reshape(1, D_MODEL), w_packed, positions.reshape(n, 1), invf, sign)


RWKV_TT = 256
P_LORA = "b3"
P_GATE = "b1"
P_HS = "b1"
P_CUM = "xa"


def _rwkv_kernel(y_ref, mu_ref, w0_ref, wl_ref, a0_ref, gup_ref, kk_ref, ka_ref, rk_ref,
                 lw_ref, lb_ref, hs_ref, tri_ref, o_ref, prev_ref, s_ref, o_scr):
    i = pl.program_id(1)

    @pl.when(i == 0)
    def _():
        prev_ref[...] = jnp.zeros_like(prev_ref)
        s_ref[...] = jnp.zeros_like(s_ref)

    tt = y_ref.shape[0]
    y = y_ref[...]
    row = lax.broadcasted_iota(jnp.int32, (tt, 1), 0)
    y_prev = jnp.where(row == 0, prev_ref[...], pltpu.roll(y, 1, 0))
    prev_ref[...] = y[tt - 1:tt, :]
    ys = y + (y_prev - y) * mu_ref[...]

    w_ = RWKV_WIDTH
    r = ys[:, 0:w_]
    k = ys[:, w_:2 * w_]
    v = ys[:, 2 * w_:3 * w_]
    z = ys[:, 3 * w_:3 * w_ + LANES]
    gd = ys[:, 3 * w_ + LANES:3 * w_ + 2 * LANES]
    lane = lax.broadcasted_iota(jnp.int32, z.shape, 1)
    zt = jnp.where(lane < D_DECAY_LORA, jnp.tanh(z), z)
    wa = _mm(zt, wl_ref[...], P_LORA)
    w_raw = w0_ref[...] + wa[:, :w_]
    a = jax.nn.sigmoid(a0_ref[...] + wa[:, w_:])
    logw = -jnp.exp(-jax.nn.softplus(-w_raw) - 0.5)
    g = _mm(jax.nn.sigmoid(gd), gup_ref[...], P_GATE)
    hs = hs_ref[...]

    def head_sum(x):
        hw = hs.shape[0]
        return jnp.concatenate([_mm(x[:, c0:c0 + hw], hs, P_HS) for c0 in range(0, w_, hw)], axis=1)

    kk = k * kk_ref[...]
    kk = kk * lax.rsqrt(jnp.maximum(head_sum(kk * kk), 1e-12))
    k2 = k * (1.0 + (a - 1.0) * ka_ref[...])
    alpha = -kk
    beta = kk * a

    cw = _mm(tri_ref[...], logw, P_CUM)
    e_in = jnp.exp(cw)
    e_ex = jnp.exp(cw - logw)
    e_neg = jnp.exp(-cw)
    a_t = alpha * e_ex
    r_t = r * e_in
    b_t = beta * e_neg
    k_t = k2 * e_neg

    m0 = lax.broadcasted_iota(jnp.int32, (CHUNK, LANES), 1) < HEAD_DIM

    def bd(xc):
        return jnp.concatenate([jnp.where(m0, xc, 0.0), jnp.where(m0, 0.0, xc)], axis=0)

    n2 = 2 * CHUNK
    ri = lax.broadcasted_iota(jnp.int32, (n2, n2), 0) % CHUNK
    ci = lax.broadcasted_iota(jnp.int32, (n2, n2), 1) % CHUNK
    strict = ri > ci
    incl = ri >= ci
    eye = jnp.where(lax.broadcasted_iota(jnp.int32, (n2, n2), 0) == lax.broadcasted_iota(jnp.int32, (n2, n2), 1),
                    1.0, 0.0)

    units = [(c, p) for c in range(tt // CHUNK) for p in range(w_ // LANES)]
    w_c = {}
    st = {}
    for c in range(tt // CHUNK):
        rs = slice(c * CHUNK, (c + 1) * CHUNK)
        cw_last = cw[c * CHUNK + CHUNK - 1:c * CHUNK + CHUNK, :]
        e_tot = jnp.exp(cw_last - cw[rs, :])
        w_c[c] = jnp.exp(cw_last)
        for p in range(w_ // LANES):
            ls = slice(p * LANES, (p + 1) * LANES)
            a_bd = bd(a_t[rs, ls]).astype(BF16)
            r_bd = bd(r_t[rs, ls])
            b_bd = bd(b_t[rs, ls]).astype(BF16)
            k_bd = bd(k_t[rs, ls]).astype(BF16)
            gm = _dot_nt(jnp.concatenate([a_bd, r_bd.astype(BF16)], axis=0), jnp.concatenate([b_bd, k_bd], axis=0))
            st[c, p] = dict(
                a_bd=a_bd, r_bd=r_bd,
                v_bd=bd(v[rs, ls]).astype(BF16),
                bh_bd=bd(beta[rs, ls] * e_tot[:, ls]).astype(BF16),
                kh_bd=bd(k2[rs, ls] * e_tot[:, ls]).astype(BF16),
                l_ab=jnp.where(strict, gm[0:n2, 0:n2], 0.0),
                l_ak=jnp.where(strict, gm[0:n2, n2:], 0.0).astype(BF16),
                m_rb=jnp.where(incl, gm[n2:, 0:n2], 0.0).astype(BF16),
                m_rk=jnp.where(incl, gm[n2:, n2:], 0.0).astype(BF16))

    pw, tm_ = {}, {}
    for u_ in units:
        l_ab = st[u_]["l_ab"]
        lb = l_ab.astype(BF16)
        pw[u_] = _dot(lb, lb)
        tm_[u_] = eye + l_ab
    for level in range(5):
        for u_ in units:
            pb = pw[u_].astype(BF16)
            if level < 4:
                y = _dot(pb, jnp.concatenate([pb, tm_[u_].astype(BF16)], axis=1))
                pw[u_] = y[:, 0:n2]
                tm_[u_] = tm_[u_] + y[:, n2:]
            else:
                tm_[u_] = tm_[u_] + _dot(pb, tm_[u_].astype(BF16))

    zero_bd = jnp.zeros((n2, n2), BF16)
    lv = {u_: _dot(st[u_]["l_ak"], st[u_]["v_bd"]).astype(BF16) for u_ in units}
    au = {u_: _dot(tm_[u_].astype(BF16), jnp.concatenate([st[u_]["a_bd"], lv[u_]], axis=1)).astype(BF16)
          for u_ in units}
    for u_ in units:
        d_ = st[u_]
        mo = _dot(jnp.concatenate([d_["m_rb"], d_["m_rk"]], axis=1),
                  jnp.concatenate([au[u_], jnp.concatenate([zero_bd, d_["v_bd"]], axis=1)], axis=0))
        d_["r_hat"] = (d_["r_bd"] + mo[:, 0:n2]).astype(BF16)
        d_["o0"] = mo[:, n2:]
    for u_ in units:
        st[u_]["g"] = _dot_tn(au[u_][:, 0:n2], st[u_]["bh_bd"]).astype(BF16)
    for u_ in units:
        d_ = st[u_]
        d_["n"] = _dot_tn(jnp.concatenate([au[u_][:, n2:], d_["v_bd"]], axis=0),
                          jnp.concatenate([d_["bh_bd"], d_["kh_bd"]], axis=0))

    for c, p in units:
        d_ = st[c, p]
        rs = slice(c * CHUNK, (c + 1) * CHUNK)
        ls = slice(p * LANES, (p + 1) * LANES)
        s_old = s_ref[p]
        sb = s_old.astype(BF16)
        o_bd = _dot_nt(d_["r_hat"], sb) + d_["o0"]
        o_scr[rs, ls] = o_bd[0:CHUNK] + o_bd[CHUNK:]
        s_ref[p] = s_old * w_c[c][:, ls] + _dot(sb, d_["g"]) + d_["n"]

    o = o_scr[...]
    inv_n = 1.0 / HEAD_DIM
    mean = head_sum(o) * inv_n
    d = o - mean
    var = head_sum(d * d) * inv_n
    on = d * lax.rsqrt(var + LNX_EPS) * lw_ref[...] + lb_ref[...]
    bonus = head_sum(r * k2 * rk_ref[...]) * v
    o_ref[...] = ((on + bonus) * g).astype(o_ref.dtype)


def _rwkv(yr, b, t, mu, w0, w_lora_up, a0, a_lora_up, g_lora_up, k_k, k_a, r_k, lnx_w, lnx_b):
    tt = RWKV_TT
    nt = t // tt
    w_ = RWKV_WIDTH
    wl = jnp.zeros((LANES, 2 * w_), F32)
    wl = wl.at[:D_DECAY_LORA, :w_].set(w_lora_up).at[D_DECAY_LORA:, w_:].set(a_lora_up)
    hw = 2 * LANES
    head = jnp.arange(hw) // HEAD_DIM
    hs = (head[:, None] == head[None, :]).astype(F32)
    ti = jnp.arange(tt)
    tri = ((ti[:, None] // CHUNK == ti[None, :] // CHUNK) & (ti[:, None] >= ti[None, :])).astype(F32)
    vec = lambda a_, width: a_.reshape(1, width)
    const = lambda shape: pl.BlockSpec(shape, lambda bi, i: (0, 0))
    return pl.pallas_call(
        _rwkv_kernel,
        grid=(b, nt),
        in_specs=[pl.BlockSpec((tt, RWKV_COLS), lambda bi, i: (bi * nt + i, 0)),
                  const((1, RWKV_COLS)), const((1, w_)), const((LANES, 2 * w_)), const((1, w_)),
                  const((D_GATE_LORA, w_)), const((1, w_)), const((1, w_)), const((1, w_)),
                  const((1, w_)), const((1, w_)), const((hw, hw)), const((tt, tt))],
        out_specs=pl.BlockSpec((tt, w_), lambda bi, i: (bi * nt + i, 0)),
        out_shape=jax.ShapeDtypeStruct((b * t, w_), BF16),
        scratch_shapes=[pltpu.VMEM((1, RWKV_COLS), F32),
                        pltpu.VMEM((w_ // LANES, LANES, LANES), F32),
                        pltpu.VMEM((tt, w_), F32)],
        compiler_params=_params("parallel", "arbitrary"),
        name="rwkv7",
    )(yr, vec(mu, RWKV_COLS), vec(w0, w_), wl, vec(a0, w_), g_lora_up, vec(k_k, w_), vec(k_a, w_),
      vec(r_k, w_), vec(lnx_w, w_), vec(lnx_b, w_), hs, tri)


def _compress_kernel(zk_ref, zv_ref, pek_ref, pev_ref, wk1_ref, wv1_ref, wk2_ref, wv2_ref, kc_ref, vc_ref):
    def one(z_ref, pe_ref, w1_ref, w2_ref, out_ref):
        z = z_ref[0]
        nrow = z.shape[0]
        za = z + pe_ref[0:1, :]
        zb = z + pe_ref[1:2, :]
        acc = jnp.zeros((nrow, LANES), F32)
        for gi in range(NSA_KV_HEADS):
            ha = _mm(za, w1_ref[gi, 0], "b3")
            hb = _mm(zb, w1_ref[gi, 1], "b3")
            hid = ha + pltpu.roll(hb, nrow - 1, 0)
            acc = acc + _mm(jax.nn.gelu(hid), w2_ref[gi], "b3")
        out_ref[0] = acc

    one(zk_ref, pek_ref, wk1_ref, wk2_ref, kc_ref)
    one(zv_ref, pev_ref, wv1_ref, wv2_ref, vc_ref)


def _compress_weights(pos, w1, w2):
    half = CMP_BLOCK // 2
    w1r = w1.reshape(2, half, HEAD_DIM, CMP_HIDDEN)
    w1g = jnp.zeros((NSA_KV_HEADS, 2, half, NSA_KV_HEADS, HEAD_DIM, CMP_HIDDEN), F32)
    for gi in range(NSA_KV_HEADS):
        w1g = w1g.at[gi, :, :, gi].set(w1r)
    w1g = w1g.reshape(NSA_KV_HEADS, 2, half * NSA_KV_WIDTH, CMP_HIDDEN)
    w2g = jnp.zeros((NSA_KV_HEADS, CMP_HIDDEN, NSA_KV_HEADS, HEAD_DIM), F32)
    for gi in range(NSA_KV_HEADS):
        w2g = w2g.at[gi, :, gi].set(w2)
    w2g = w2g.reshape(NSA_KV_HEADS, CMP_HIDDEN, NSA_KV_WIDTH)
    pe = jnp.broadcast_to(pos.reshape(2, half, 1, HEAD_DIM), (2, half, NSA_KV_HEADS, HEAD_DIM))
    pe = pe.reshape(2, half * NSA_KV_WIDTH)
    return pe, w1g, w2g


def _compress(kc, vc, b, t, cmp_pos_k, cmp_pos_v, k_w1, k_w2, v_w1, v_w2):
    nrow = t // CMP_STRIDE
    zw = CMP_STRIDE * NSA_KV_WIDTH
    pek, wk1, wk2 = _compress_weights(cmp_pos_k, k_w1, k_w2)
    pev, wv1, wv2 = _compress_weights(cmp_pos_v, v_w1, v_w2)
    zspec = pl.BlockSpec((1, nrow, zw), lambda bi: (bi, 0, 0))
    ospec = pl.BlockSpec((1, nrow, LANES), lambda bi: (bi, 0, 0))
    c2 = lambda shape: pl.BlockSpec(shape, lambda bi: (0,) * len(shape))
    return pl.pallas_call(
        _compress_kernel,
        grid=(b,),
        in_specs=[zspec, zspec, c2((2, zw)), c2((2, zw)), c2(wk1.shape), c2(wv1.shape), c2(wk2.shape), c2(wv2.shape)],
        out_specs=[ospec, ospec],
        out_shape=[jax.ShapeDtypeStruct((b, nrow, LANES), F32)] * 2,
        compiler_params=_params("parallel"),
        name="nsa_compress",
    )(kc.reshape(b, nrow, zw), vc.reshape(b, nrow, zw), pek, pev, wk1, wv1, wk2, wv2)


SEL_TILE = 256
SEL_UNROLL = 4
WIN_TILE = 128
NSA_QBLK = 2


def _n_win_tiles():
    tq = NSA_QBLK * Q_BLOCK
    span = WINDOW + tq + WIN_TILE - math.gcd(WIN_TILE, tq)
    return -(-span // WIN_TILE)


def _nsa_kernel(q_ref, kcmp_ref, vcmp_ref, ks_ref, vs_ref, kw_ref, vw_ref, gl_ref, ovt_ref, oh_ref,
                o_ref, m_scr, l_scr, acc_scr, sc_scr, imp_scr, cnt_scr, sel_scr, g_scr):
    s2 = pl.program_id(1)
    qb = Q_BLOCK
    nq = NSA_QBLK
    n_head = NSA_Q_HEADS
    rows = nq * n_head * qb
    n_cmp = kcmp_ref.shape[1]
    n_sb = ovt_ref.shape[0]
    s_of = [s2 * nq + qi for qi in range(nq)]
    s_last = s_of[-1]

    def tok(qi, shape):
        return s_of[qi] * qb + lax.broadcasted_iota(jnp.int32, shape, 0)

    def per_head(x64s):
        return jnp.concatenate([x for x in x64s for _ in range(n_head)], axis=0)

    def per_group_head(x):
        parts = []
        for qi in range(nq):
            for gi in range(NSA_KV_HEADS):
                r0 = (qi * NSA_KV_HEADS + gi) * qb
                parts += [x[r0:r0 + qb]] * NSA_HPG
        return jnp.concatenate(parts, axis=0)

    lane64 = lax.broadcasted_iota(jnp.int32, (qb, LANES), 1)
    sig = jax.nn.sigmoid(gl_ref[...])
    for col in range(n_head * N_BRANCH):
        g_scr[col] = jnp.broadcast_to(sig[:, col:col + 1], (nq * qb, LANES))

    qq = jnp.concatenate([q_ref[qi * qb:(qi + 1) * qb, hq * LANES:(hq + 1) * LANES]
                          for qi in range(nq) for hq in range(n_head)], axis=0)

    n_win = _n_win_tiles()
    first = (s2 * nq * qb - WINDOW) // WIN_TILE
    lane_w = lax.broadcasted_iota(jnp.int32, (qb, WIN_TILE), 1)
    win_sc, win_k0 = [], []
    mxw = jnp.full((rows, LANES), NEG_INF, F32)
    for j in range(n_win):
        kt = jnp.maximum(first + j, 0)
        k0 = pl.multiple_of(kt * WIN_TILE, WIN_TILE)
        kp = k0 + lane_w + jnp.where(first + j >= 0, 0, 1 << 24)
        bias = per_head([jnp.where((kp <= tok(qi, kp.shape)) & (kp > tok(qi, kp.shape) - WINDOW), 0.0, NEG_INF)
                         for qi in range(nq)])
        sc_ = _dot_nt(qq, kw_ref[0, pl.ds(k0, WIN_TILE), :]) + bias
        for c0 in range(0, WIN_TILE, LANES):
            mxw = jnp.maximum(mxw, sc_[:, c0:c0 + LANES])
        win_sc.append(sc_)
        win_k0.append(k0)
    m_w = jnp.max(mxw, axis=1, keepdims=True)
    l_w = jnp.zeros((rows, LANES), F32)
    acc_w = jnp.zeros((rows, LANES), F32)
    for j in range(n_win):
        p = jnp.exp(win_sc[j] - m_w)
        for c0 in range(0, WIN_TILE, LANES):
            l_w = l_w + p[:, c0:c0 + LANES]
        acc_w = acc_w + _dot(p.astype(BF16), vw_ref[0, pl.ds(win_k0[j], WIN_TILE), :])
    o_win = acc_w / jnp.sum(l_w, axis=1, keepdims=True)

    cmp_i = lax.broadcasted_iota(jnp.int32, (qb, n_cmp), 1)
    cbias = per_head([jnp.where((cmp_i * CMP_STRIDE + CMP_BLOCK - 1 <= tok(qi, cmp_i.shape)) & (cmp_i < n_cmp - 1),
                                0.0, NEG_INF) for qi in range(nq)])
    sc = _mm(qq, kcmp_ref[0], "xa", _dot_nt) + cbias
    mx = jnp.max(sc, axis=1, keepdims=True)
    e = jnp.exp(sc - mx)
    den = jnp.sum(e, axis=1, keepdims=True)
    p_c = e * jnp.where(mx > 0.5 * NEG_INF, 1.0 / den, 0.0)
    o_cmp = _dot(p_c.astype(BF16), vcmp_ref[0].astype(BF16))

    pc_sums = []
    for qg in range(nq * NSA_KV_HEADS):
        r0 = qg * NSA_HPG * qb
        acc = p_c[r0:r0 + qb]
        for h in range(1, NSA_HPG):
            acc = acc + p_c[r0 + h * qb:r0 + (h + 1) * qb]
        pc_sums.append(acc)
    pcs = jnp.concatenate(pc_sums, axis=0)
    imp = _dot_nt(ovt_ref[...], pcs, HI)
    nl = nq * NSA_KV_HEADS * qb
    blk = lax.broadcasted_iota(jnp.int32, (n_sb, nl), 0)
    s_lane = s2 * nq + lax.broadcasted_iota(jnp.int32, (n_sb, nl), 1) // (NSA_KV_HEADS * qb)
    forced = (blk == 0) | (blk == s_lane) | (blk == s_lane - 1)
    imp = jnp.where(blk <= s_lane, imp + jnp.where(forced, FORCE_BONUS, 0.0), NEG_INF)
    n_sel = min(SEL_TOPK, n_sb)
    sel_scr[...] = jnp.where(blk <= s_lane, 1.0, 0.0)

    @pl.when(s_last >= n_sel)
    def _():
        imp_scr[...] = imp
        cnt_scr[...] = jnp.zeros(cnt_scr.shape, F32)
        sub = lax.broadcasted_iota(jnp.int32, (8, nl), 0)
        for ig in range(n_sb // 8):
            @pl.when(ig * 8 <= s_last)
            def _(ig=ig):
                impv = imp_scr[...]
                cnt = [cnt_scr[rg * 8:(rg + 1) * 8, :] for rg in range(n_sb // 8)]
                for i2 in range(ig * 8, ig * 8 + 8):
                    vi = impv[i2:i2 + 1, :]
                    for rg in range(n_sb // 8):
                        vj = impv[rg * 8:(rg + 1) * 8, :]
                        if rg < ig:
                            one = jnp.where(vi > vj, 1.0, 0.0)
                        elif rg > ig:
                            one = jnp.where(vi >= vj, 1.0, 0.0)
                        else:
                            one = jnp.where(sub > i2 - ig * 8, jnp.where(vi >= vj, 1.0, 0.0),
                                            jnp.where(vi > vj, 1.0, 0.0))
                        cnt[rg] = cnt[rg] + one
                for rg in range(n_sb // 8):
                    cnt_scr[rg * 8:(rg + 1) * 8, :] = cnt[rg]
        sel_scr[...] = jnp.where((cnt_scr[...] < n_sel) & (blk <= s_lane), 1.0, 0.0)

    sel_t = sel_scr[...]
    if n_sb < LANES:
        sel_t = jnp.concatenate([sel_t, jnp.zeros((LANES - n_sb, nl), F32)], axis=0)
    unsel = ((sel_t.T - 1.0) * -NEG_INF).astype(BF16)
    q_ext = jnp.concatenate([qq, per_group_head(unsel)], axis=1)

    lane_s = lax.broadcasted_iota(jnp.int32, (qb, SEL_TILE), 1)
    n_tiles = ((s_last + 1) * qb + SEL_TILE - 1) // SEL_TILE
    n_full = n_tiles // SEL_UNROLL
    rem = n_tiles % SEL_UNROLL
    rem_blocks = [w for w in (SEL_UNROLL >> i for i in range(1, SEL_UNROLL.bit_length())) if w > 0]

    def rem_start(w):
        return n_full * SEL_UNROLL + (rem & ~(2 * w - 1))

    def tile_scores(j, mx_):
        k0 = pl.multiple_of(j * SEL_TILE, SEL_TILE)
        causal = per_head([jnp.where(k0 + lane_s <= tok(qi, lane_s.shape), 0.0, NEG_INF) for qi in range(nq)])
        k_ext = jnp.concatenate([ks_ref[0, pl.ds(k0, SEL_TILE), :], oh_ref[j]], axis=1)
        sc_ = _dot_nt(q_ext, k_ext) + causal
        sc_scr[j] = sc_
        for c0 in range(0, SEL_TILE, LANES):
            mx_ = jnp.maximum(mx_, sc_[:, c0:c0 + LANES])
        return mx_

    def pass1(it, mx_):
        for u in range(SEL_UNROLL):
            mx_ = tile_scores(it * SEL_UNROLL + u, mx_)
        return mx_

    m_scr[...] = lax.fori_loop(0, n_full, pass1, jnp.full((rows, LANES), NEG_INF, F32))
    for w in rem_blocks:
        @pl.when((rem & w) != 0)
        def _(w=w):
            mx_ = m_scr[...]
            for u in range(w):
                mx_ = tile_scores(rem_start(w) + u, mx_)
            m_scr[...] = mx_
    m_scr[...] = jnp.broadcast_to(jnp.max(m_scr[...], axis=1, keepdims=True), m_scr.shape)
    l_scr[...] = jnp.zeros(l_scr.shape, F32)
    acc_scr[...] = jnp.zeros(acc_scr.shape, F32)

    def tile_pv(j):
        k0 = pl.multiple_of(j * SEL_TILE, SEL_TILE)
        m_b = jnp.concatenate([m_scr[...]] * (SEL_TILE // LANES), axis=1)
        p = jnp.exp(sc_scr[j] - m_b)
        l_new = l_scr[...]
        for c0 in range(0, SEL_TILE, LANES):
            l_new = l_new + p[:, c0:c0 + LANES]
        l_scr[...] = l_new
        acc_scr[...] += _dot(p.astype(BF16), vs_ref[0, pl.ds(k0, SEL_TILE), :])

    def pass2(it, c):
        for u in range(SEL_UNROLL):
            tile_pv(it * SEL_UNROLL + u)
        return c

    lax.fori_loop(0, n_full, pass2, 0)
    for w in rem_blocks:
        @pl.when((rem & w) != 0)
        def _(w=w):
            for u in range(w):
                tile_pv(rem_start(w) + u)
    o_sel = acc_scr[...] / jnp.sum(l_scr[...], axis=1, keepdims=True)

    low_half = lane64 < HEAD_DIM
    for qi in range(nq):
        ts = slice(qi * qb, (qi + 1) * qb)
        for h in range(NSA_HPG):
            halves = []
            for gi in range(NSA_KV_HEADS):
                hq = gi * NSA_HPG + h
                r0 = (qi * n_head + hq) * qb
                hr = slice(r0, r0 + qb)
                col = hq * N_BRANCH
                halves.append(g_scr[col, ts] * o_cmp[hr] + g_scr[col + 1, ts] * o_sel[hr] + g_scr[col + 2, ts] * o_win[hr])
            o_ref[ts, h * LANES:(h + 1) * LANES] = jnp.where(low_half, halves[0], halves[1]).astype(o_ref.dtype)


def _nsa_attention(q, k_cmp, v_cmp, ks, vs, kw, vw, gl, b, t):
    tq = NSA_QBLK * Q_BLOCK
    ns = t // tq
    n_sb = t // SEL_BLOCK
    n_cmp = t // CMP_STRIDE
    n_kt = t // SEL_TILE
    cmp_start = jnp.arange(n_cmp) * CMP_STRIDE
    sb = jnp.arange(n_sb)
    ovt = ((cmp_start[None, :] < (sb[:, None] + 1) * SEL_BLOCK)
           & (cmp_start[None, :] + CMP_BLOCK > sb[:, None] * SEL_BLOCK)
           & (jnp.arange(n_cmp)[None, :] < n_cmp - 1)).astype(F32)
    key_blk = (jnp.arange(n_kt)[:, None] * SEL_TILE + jnp.arange(SEL_TILE)[None, :]) // SEL_BLOCK
    onehot = (key_blk[:, :, None] == jnp.arange(LANES)[None, None, :]).astype(BF16)
    rows = NSA_QBLK * NSA_Q_HEADS * Q_BLOCK
    nl = NSA_QBLK * NSA_KV_HEADS * Q_BLOCK
    qspec = pl.BlockSpec((tq, Q_PAD_COLS), lambda bi, si: (bi * ns + si, 0))
    seq = lambda: pl.BlockSpec((1, t, LANES), lambda bi, si: (bi, 0, 0))
    cmp_spec = lambda: pl.BlockSpec((1, n_cmp, LANES), lambda bi, si: (bi, 0, 0))
    k3 = lambda a_: a_.reshape(b, t, LANES)
    return pl.pallas_call(
        _nsa_kernel,
        grid=(b, ns),
        in_specs=[qspec, cmp_spec(), cmp_spec(), seq(), seq(), seq(), seq(),
                  pl.BlockSpec((tq, LANES), lambda bi, si: (bi * ns + si, 0)),
                  pl.BlockSpec((n_sb, n_cmp), lambda bi, si: (0, 0)),
                  pl.BlockSpec((n_kt, SEL_TILE, LANES), lambda bi, si: (0, 0, 0))],
        out_specs=pl.BlockSpec((tq, NSA_WIDTH), lambda bi, si: (bi * ns + si, 0)),
        out_shape=jax.ShapeDtypeStruct((b * t, NSA_WIDTH), BF16),
        scratch_shapes=[pltpu.VMEM((rows, LANES), F32),
                        pltpu.VMEM((rows, LANES), F32),
                        pltpu.VMEM((rows, LANES), F32),
                        pltpu.VMEM((n_kt, rows, SEL_TILE), F32),
                        pltpu.VMEM((n_sb, nl), F32),
                        pltpu.VMEM((n_sb, nl), F32),
                        pltpu.VMEM((n_sb, nl), F32),
                        pltpu.VMEM((NSA_Q_HEADS * N_BRANCH, tq, LANES), F32)],
        compiler_params=_params("parallel", "arbitrary"),
        name="nsa_attention",
    )(q, k_cmp, v_cmp, k3(ks), k3(vs), k3(kw), k3(vw), gl, ovt, onehot)


FF_TILE = 256


def _ffn_kernel(x_ref, orw_ref, ons_ref, wor_ref, won_ref, n2_ref, w1_ref, w3_ref, w2_ref, nf_ref,
                out_ref, h_scr, u_scr, acc_scr):
    j = pl.program_id(1)

    @pl.when(j == 0)
    def _():
        h = x_ref[...] + _dot(orw_ref[...], wor_ref[...]) + _dot(ons_ref[...], won_ref[...])
        h_scr[...] = h
        ms = jnp.mean(h * h, axis=-1, keepdims=True)
        u_scr[...] = (h * lax.rsqrt(ms + RMS_EPS) * n2_ref[...]).astype(BF16)
        acc_scr[...] = jnp.zeros_like(acc_scr)

    u = u_scr[...]
    gate = _dot(u, w1_ref[...])
    up = _dot(u, w3_ref[...])
    act = (jax.nn.silu(gate) * up).astype(BF16)
    acc_scr[...] += _dot(act, w2_ref[...])

    @pl.when(j == pl.num_programs(1) - 1)
    def _():
        h = h_scr[...] + acc_scr[...]
        ms = jnp.mean(h * h, axis=-1, keepdims=True)
        out_ref[...] = h * lax.rsqrt(ms + RMS_EPS) * nf_ref[...]


def _out_ffn(x2, o_rwkv, o_nsa, w_out, norm2_w, ffn_w1, ffn_w3, ffn_w2, final_norm_w):
    n = x2.shape[0]
    tm = 1024
    w_or = w_out[:RWKV_WIDTH].astype(BF16)
    w_n = w_out[RWKV_WIDTH:].reshape(NSA_KV_HEADS, NSA_HPG, HEAD_DIM, D_MODEL)
    w_on = w_n.transpose(1, 0, 2, 3).reshape(NSA_WIDTH, D_MODEL).astype(BF16)
    row = lambda width: pl.BlockSpec((tm, width), lambda i, j: (i, 0))
    const = lambda shape: pl.BlockSpec(shape, lambda i, j: (0, 0))
    return pl.pallas_call(
        _ffn_kernel,
        grid=(n // tm, D_FF // FF_TILE),
        in_specs=[row(D_MODEL), row(RWKV_WIDTH), row(NSA_WIDTH),
                  const((RWKV_WIDTH, D_MODEL)), const((NSA_WIDTH, D_MODEL)), const((1, D_MODEL)),
                  pl.BlockSpec((D_MODEL, FF_TILE), lambda i, j: (0, j)),
                  pl.BlockSpec((D_MODEL, FF_TILE), lambda i, j: (0, j)),
                  pl.BlockSpec((FF_TILE, D_MODEL), lambda i, j: (j, 0)),
                  const((1, D_MODEL))],
        out_specs=row(D_MODEL),
        out_shape=jax.ShapeDtypeStruct((n, D_MODEL), F32),
        scratch_shapes=[pltpu.VMEM((tm, D_MODEL), F32), pltpu.VMEM((tm, D_MODEL), BF16),
                        pltpu.VMEM((tm, D_MODEL), F32)],
        compiler_params=_params("parallel", "arbitrary"),
        name="out_ffn",
    )(x2, o_rwkv, o_nsa, w_or, w_on, norm2_w.reshape(1, D_MODEL), ffn_w1.astype(BF16), ffn_w3.astype(BF16),
      ffn_w2.astype(BF16), final_norm_w.reshape(1, D_MODEL))


def kernel(x, positions, norm1_w, w_in, mu_rwkv, w0, w_lora_up, a0, a_lora_up, g_lora_up, k_k, k_a, r_k, lnx_w, lnx_b, cmp_pos_k, cmp_pos_v, cmp_k_w1, cmp_k_w2, cmp_v_w1, cmp_v_w2, w_out, norm2_w, ffn_w1, ffn_w3, ffn_w2, final_norm_w):
    b, t, d = x.shape
    assert d == D_MODEL and norm1_w.shape[0] == 1, "single-layer block with d_model 1024"
    assert t % RWKV_TT == 0 and t % (SEL_UNROLL * SEL_TILE) == 0 and t % (NSA_QBLK * Q_BLOCK) == 0
    x2 = x.reshape(b * t, d)
    yr, q, kc, vc, ks, vs, kw, vw, gl = _in_proj(x2, norm1_w[0], _pack_w_in(w_in[0]), positions)
    o_rwkv = _rwkv(yr, b, t, mu_rwkv[0], w0[0], w_lora_up[0], a0[0], a_lora_up[0], g_lora_up[0],
                   k_k[0], k_a[0], r_k[0], lnx_w[0], lnx_b[0])
    k_cmp, v_cmp = _compress(kc, vc, b, t, cmp_pos_k[0], cmp_pos_v[0], cmp_k_w1[0], cmp_k_w2[0],
                             cmp_v_w1[0], cmp_v_w2[0])
    o_nsa = _nsa_attention(q, k_cmp, v_cmp, ks, vs, kw, vw, gl, b, t)
    out = _out_ffn(x2, o_rwkv, o_nsa, w_out[0], norm2_w[0], ffn_w1[0], ffn_w3[0], ffn_w2[0], final_norm_w)
    return out.reshape(b, t, d)
```

```python
import math

import jax
import jax.numpy as jnp
from jax import lax
from jax.experimental import pallas as pl
from jax.experimental.pallas import tpu as pltpu

F32 = jnp.float32
BF16 = jnp.bfloat16
HI = lax.Precision.HIGHEST

D_MODEL = 1024
HEAD_DIM = 64
RWKV_WIDTH = 512
RWKV_HEADS = 8
D_DECAY_LORA = 64
D_AAA_LORA = 64
D_GATE_LORA = 128
RWKV_COLS = 3 * RWKV_WIDTH + D_DECAY_LORA + D_AAA_LORA + D_GATE_LORA
NSA_WIDTH = 512
NSA_Q_HEADS = 8
NSA_KV_HEADS = 2
NSA_HPG = 4
NSA_KV_WIDTH = 128
N_BRANCH = 3
CMP_BLOCK = 32
CMP_STRIDE = 16
CMP_HIDDEN = 256
SEL_BLOCK = 64
SEL_TOPK = 16
WINDOW = 512
Q_BLOCK = 64
ROPE_THETA = 10000.0
D_FF = 2816
RMS_EPS = 1e-6
LNX_EPS = 64e-5
FORCE_BONUS = 1e4
NEG_INF = -1e30

LANES = 128
CHUNK = 64
Q_PAD_COLS = NSA_Q_HEADS * LANES
PROJ_COLS = RWKV_COLS + Q_PAD_COLS + 6 * NSA_KV_WIDTH + LANES
VMEM_LIMIT = 56 * 1024 * 1024


def _dot(a, b, prec=None):
    return jnp.dot(a, b, preferred_element_type=F32, precision=prec)


def _dot_nt(a, b, prec=None):
    return lax.dot_general(a, b, (((1,), (1,)), ((), ())), preferred_element_type=F32, precision=prec)


def _dot_tn(a, b, prec=None):
    return lax.dot_general(a, b, (((0,), (0,)), ((), ())), preferred_element_type=F32, precision=prec)


def _split(a):
    hi = a.astype(BF16)
    return hi, (a - hi.astype(F32)).astype(BF16)


def _mm(a, b, mode, dot=_dot):
    if mode == "hi":
        return dot(a, b, HI)
    if mode == "b1":
        return dot(a.astype(BF16), b.astype(BF16))
    if mode == "xa":
        bh, bl = _split(b)
        ab = a.astype(BF16)
        return dot(ab, bh) + dot(ab, bl)
    if mode == "xb":
        ah, al = _split(a)
        bb = b.astype(BF16)
        return dot(ah, bb) + dot(al, bb)
    ah, al = _split(a)
    bh, bl = _split(b)
    return dot(ah, bh) + (dot(ah, bl) + dot(al, bh))


def _params(*sem):
    return pltpu.CompilerParams(dimension_semantics=sem, vmem_limit_bytes=VMEM_LIMIT)


def _proj_kernel(x_ref, n1_ref, w_ref, pos_ref, invf_ref, sign_ref,
                 yr_ref, q_ref, kc_ref, vc_ref, ks_ref, vs_ref, kw_ref, vw_ref, gl_ref, cos_scr, sin_scr):
    x = x_ref[...]
    ms = jnp.mean(x * x, axis=-1, keepdims=True)
    xn = (x * lax.rsqrt(ms + RMS_EPS) * n1_ref[...]).astype(BF16)
    ang = pos_ref[...].astype(F32) * invf_ref[...]
    cos_scr[...] = jnp.cos(ang)
    sin_scr[...] = jnp.sin(ang) * sign_ref[...]
    lane = lax.broadcasted_iota(jnp.int32, ang.shape, 1)
    first_half = (lane % HEAD_DIM) < (HEAD_DIM // 2)

    def rope(v):
        rot = jnp.where(first_half, pltpu.roll(v, LANES - HEAD_DIM // 2, 1), pltpu.roll(v, HEAD_DIM // 2, 1))
        return v * cos_scr[...] + rot * sin_scr[...]

    def cols(c0, width):
        return _dot(xn, w_ref[:, c0:c0 + width])

    for j in range(RWKV_COLS // 256):
        yr_ref[:, j * 256:(j + 1) * 256] = cols(j * 256, 256)
    scale = HEAD_DIM ** -0.5 * math.log2(math.e)
    for hq in range(0, NSA_Q_HEADS, 2):
        q2 = cols(RWKV_COLS + hq * LANES, 2 * LANES)
        for e in range(2):
            q = rope(q2[:, e * LANES:(e + 1) * LANES]) * scale
            q_ref[:, (hq + e) * LANES:(hq + e + 1) * LANES] = q.astype(BF16)
    c0 = RWKV_COLS + Q_PAD_COLS
    kv = cols(c0, 2 * LANES)
    kc_ref[...] = rope(kv[:, :LANES])
    vc_ref[...] = kv[:, LANES:]
    kv = cols(c0 + 2 * LANES, 2 * LANES)
    ks_ref[...] = rope(kv[:, :LANES]).astype(BF16)
    vs_ref[...] = kv[:, LANES:].astype(BF16)
    kv = cols(c0 + 4 * LANES, 2 * LANES)
    kw_ref[...] = rope(kv[:, :LANES]).astype(BF16)
    vw_ref[...] = kv[:, LANES:].astype(BF16)
    gl_ref[...] = cols(c0 + 6 * LANES, LANES)


def _pack_w_in(w_in):
    w_r = w_in[:, :RWKV_COLS]
    w_q = w_in[:, RWKV_COLS:RWKV_COLS + NSA_WIDTH].reshape(D_MODEL, NSA_KV_HEADS, NSA_HPG, HEAD_DIM)
    zero = jnp.zeros_like(w_q)
    w_q = jnp.stack([jnp.where(jnp.arange(NSA_KV_HEADS)[None, :, None, None] == gg, w_q, zero)
                     for gg in range(NSA_KV_HEADS)], axis=3)
    w_q = w_q.reshape(D_MODEL, Q_PAD_COLS)
    c0 = RWKV_COLS + NSA_WIDTH
    w_kv = w_in[:, c0:c0 + 6 * NSA_KV_WIDTH]
    w_g = w_in[:, c0 + 6 * NSA_KV_WIDTH:]
    w_g = jnp.pad(w_g, ((0, 0), (0, LANES - w_g.shape[1])))
    return jnp.concatenate([w_r, w_q, w_kv, w_g], axis=1).astype(BF16)


def _in_proj(x2, norm1_w, w_packed, positions):
    n = x2.shape[0]
    tm = 512
    half = HEAD_DIM // 2
    inv_freq = ROPE_THETA ** (-jnp.arange(half, dtype=F32) / half)
    invf = jnp.tile(inv_freq, LANES // half).reshape(1, LANES)
    lane = jnp.arange(LANES)
    sign = jnp.where((lane % HEAD_DIM) < half, -1.0, 1.0).astype(F32).reshape(1, LANES)
    row = lambda width: pl.BlockSpec((tm, width), lambda i: (i, 0))
    const = lambda shape: pl.BlockSpec(shape, lambda i: (0, 0))
    out_shape = [
        jax.ShapeDtypeStruct((n, RWKV_COLS), F32),
        jax.ShapeDtypeStruct((n, Q_PAD_COLS), BF16),
        jax.ShapeDtypeStruct((n, LANES), F32),
        jax.ShapeDtypeStruct((n, LANES), F32),
        jax.ShapeDtypeStruct((n, LANES), BF16),
        jax.ShapeDtypeStruct((n, LANES), BF16),
        jax.ShapeDtypeStruct((n, LANES), BF16),
        jax.ShapeDtypeStruct((n, LANES), BF16),
        jax.ShapeDtypeStruct((n, LANES), F32),
    ]
    out_specs = [row(RWKV_COLS), row(Q_PAD_COLS)] + [row(LANES)] * 7
    return pl.pallas_call(
        _proj_kernel,
        grid=(n // tm,),
        in_specs=[row(D_MODEL), const((1, D_MODEL)), const((D_MODEL, PROJ_COLS)), row(1),
                  const((1, LANES)), const((1, LANES))],
        out_specs=out_specs,
        out_shape=out_shape,
        scratch_shapes=[pltpu.VMEM((tm, LANES), F32), pltpu.VMEM((tm, LANES), F32)],
        compiler_params=_params("parallel"),
        name="in_proj",
    )(x2, norm1_w.reshape(1, D_MODEL), w_packed, positions.reshape(n, 1), invf, sign)


RWKV_TT = 256
P_LORA = "b3"
P_GATE = "b1"
P_HS = "b1"
P_CUM = "xa"


def _rwkv_kernel(y_ref, mu_ref, w0_ref, wl_ref, a0_ref, gup_ref, kk_ref, ka_ref, rk_ref,
                 lw_ref, lb_ref, hs_ref, tri_ref, o_ref, prev_ref, s_ref, o_scr):
    i = pl.program_id(1)

    @pl.when(i == 0)
    def _():
        prev_ref[...] = jnp.zeros_like(prev_ref)
        s_ref[...] = jnp.zeros_like(s_ref)

    tt = y_ref.shape[0]
    y = y_ref[...]
    row = lax.broadcasted_iota(jnp.int32, (tt, 1), 0)
    y_prev = jnp.where(row == 0, prev_ref[...], pltpu.roll(y, 1, 0))
    prev_ref[...] = y[tt - 1:tt, :]
    ys = y + (y_prev - y) * mu_ref[...]

    w_ = RWKV_WIDTH
    r = ys[:, 0:w_]
    k = ys[:, w_:2 * w_]
    v = ys[:, 2 * w_:3 * w_]
    z = ys[:, 3 * w_:3 * w_ + LANES]
    gd = ys[:, 3 * w_ + LANES:3 * w_ + 2 * LANES]
    lane = lax.broadcasted_iota(jnp.int32, z.shape, 1)
    zt = jnp.where(lane < D_DECAY_LORA, jnp.tanh(z), z)
    wa = _mm(zt, wl_ref[...], P_LORA)
    w_raw = w0_ref[...] + wa[:, :w_]
    a = jax.nn.sigmoid(a0_ref[...] + wa[:, w_:])
    logw = -jnp.exp(-jax.nn.softplus(-w_raw) - 0.5)
    g = _mm(jax.nn.sigmoid(gd), gup_ref[...], P_GATE)
    hs = hs_ref[...]

    def head_sum(x):
        hw = hs.shape[0]
        return jnp.concatenate([_mm(x[:, c0:c0 + hw], hs, P_HS) for c0 in range(0, w_, hw)], axis=1)

    kk = k * kk_ref[...]
    kk = kk * lax.rsqrt(jnp.maximum(head_sum(kk * kk), 1e-12))
    k2 = k * (1.0 + (a - 1.0) * ka_ref[...])
    alpha = -kk
    beta = kk * a

    cw = _mm(tri_ref[...], logw, P_CUM)
    e_in = jnp.exp(cw)
    e_ex = jnp.exp(cw - logw)
    e_neg = jnp.exp(-cw)
    a_t = alpha * e_ex
    r_t = r * e_in
    b_t = beta * e_neg
    k_t = k2 * e_neg

    m0 = lax.broadcasted_iota(jnp.int32, (CHUNK, LANES), 1) < HEAD_DIM

    def bd(xc):
        return jnp.concatenate([jnp.where(m0, xc, 0.0), jnp.where(m0, 0.0, xc)], axis=0)

    n2 = 2 * CHUNK
    ri = lax.broadcasted_iota(jnp.int32, (n2, n2), 0) % CHUNK
    ci = lax.broadcasted_iota(jnp.int32, (n2, n2), 1) % CHUNK
    strict = ri > ci
    incl = ri >= ci
    eye = jnp.where(lax.broadcasted_iota(jnp.int32, (n2, n2), 0) == lax.broadcasted_iota(jnp.int32, (n2, n2), 1),
                    1.0, 0.0)

    units = [(c, p) for c in range(tt // CHUNK) for p in range(w_ // LANES)]
    w_c = {}
    st = {}
    for c in range(tt // CHUNK):
        rs = slice(c * CHUNK, (c + 1) * CHUNK)
        cw_last = cw[c * CHUNK + CHUNK - 1:c * CHUNK + CHUNK, :]
        e_tot = jnp.exp(cw_last - cw[rs, :])
        w_c[c] = jnp.exp(cw_last)
        for p in range(w_ // LANES):
            ls = slice(p * LANES, (p + 1) * LANES)
            a_bd = bd(a_t[rs, ls]).astype(BF16)
            r_bd = bd(r_t[rs, ls])
            b_bd = bd(b_t[rs, ls]).astype(BF16)
            k_bd = bd(k_t[rs, ls]).astype(BF16)
            gm = _dot_nt(jnp.concatenate([a_bd, r_bd.astype(BF16)], axis=0), jnp.concatenate([b_bd, k_bd], axis=0))
            st[c, p] = dict(
                a_bd=a_bd, r_bd=r_bd,
                v_bd=bd(v[rs, ls]).astype(BF16),
                bh_bd=bd(beta[rs, ls] * e_tot[:, ls]).astype(BF16),
                kh_bd=bd(k2[rs, ls] * e_tot[:, ls]).astype(BF16),
                l_ab=jnp.where(strict, gm[0:n2, 0:n2], 0.0),
                l_ak=jnp.where(strict, gm[0:n2, n2:], 0.0).astype(BF16),
                m_rb=jnp.where(incl, gm[n2:, 0:n2], 0.0).astype(BF16),
                m_rk=jnp.where(incl, gm[n2:, n2:], 0.0).astype(BF16))

    pw, tm_ = {}, {}
    for u_ in units:
        l_ab = st[u_]["l_ab"]
        lb = l_ab.astype(BF16)
        pw[u_] = _dot(lb, lb)
        tm_[u_] = eye + l_ab
    for level in range(5):
        for u_ in units:
            pb = pw[u_].astype(BF16)
            if level < 4:
                y = _dot(pb, jnp.concatenate([pb, tm_[u_].astype(BF16)], axis=1))
                pw[u_] = y[:, 0:n2]
                tm_[u_] = tm_[u_] + y[:, n2:]
            else:
                tm_[u_] = tm_[u_] + _dot(pb, tm_[u_].astype(BF16))

    zero_bd = jnp.zeros((n2, n2), BF16)
    lv = {u_: _dot(st[u_]["l_ak"], st[u_]["v_bd"]).astype(BF16) for u_ in units}
    au = {u_: _dot(tm_[u_].astype(BF16), jnp.concatenate([st[u_]["a_bd"], lv[u_]], axis=1)).astype(BF16)
          for u_ in units}
    for u_ in units:
        d_ = st[u_]
        mo = _dot(jnp.concatenate([d_["m_rb"], d_["m_rk"]], axis=1),
                  jnp.concatenate([au[u_], jnp.concatenate([zero_bd, d_["v_bd"]], axis=1)], axis=0))
        d_["r_hat"] = (d_["r_bd"] + mo[:, 0:n2]).astype(BF16)
        d_["o0"] = mo[:, n2:]
    for u_ in units:
        st[u_]["g"] = _dot_tn(au[u_][:, 0:n2], st[u_]["bh_bd"]).astype(BF16)
    for u_ in units:
        d_ = st[u_]
        d_["n"] = _dot_tn(jnp.concatenate([au[u_][:, n2:], d_["v_bd"]], axis=0),
                          jnp.concatenate([d_["bh_bd"], d_["kh_bd"]], axis=0))

    for c, p in units:
        d_ = st[c, p]
        rs = slice(c * CHUNK, (c + 1) * CHUNK)
        ls = slice(p * LANES, (p + 1) * LANES)
        s_old = s_ref[p]
        sb = s_old.astype(BF16)
        o_bd = _dot_nt(d_["r_hat"], sb) + d_["o0"]
        o_scr[rs, ls] = o_bd[0:CHUNK] + o_bd[CHUNK:]
        s_ref[p] = s_old * w_c[c][:, ls] + _dot(sb, d_["g"]) + d_["n"]

    o = o_scr[...]
    inv_n = 1.0 / HEAD_DIM
    mean = head_sum(o) * inv_n
    d = o - mean
    var = head_sum(d * d) * inv_n
    on = d * lax.rsqrt(var + LNX_EPS) * lw_ref[...] + lb_ref[...]
    bonus = head_sum(r * k2 * rk_ref[...]) * v
    o_ref[...] = ((on + bonus) * g).astype(o_ref.dtype)


def _rwkv(yr, b, t, mu, w0, w_lora_up, a0, a_lora_up, g_lora_up, k_k, k_a, r_k, lnx_w, lnx_b):
    tt = RWKV_TT
    nt = t // tt
    w_ = RWKV_WIDTH
    wl = jnp.zeros((LANES, 2 * w_), F32)
    wl = wl.at[:D_DECAY_LORA, :w_].set(w_lora_up).at[D_DECAY_LORA:, w_:].set(a_lora_up)
    hw = 2 * LANES
    head = jnp.arange(hw) // HEAD_DIM
    hs = (head[:, None] == head[None, :]).astype(F32)
    ti = jnp.arange(tt)
    tri = ((ti[:, None] // CHUNK == ti[None, :] // CHUNK) & (ti[:, None] >= ti[None, :])).astype(F32)
    vec = lambda a_, width: a_.reshape(1, width)
    const = lambda shape: pl.BlockSpec(shape, lambda bi, i: (0, 0))
    return pl.pallas_call(
        _rwkv_kernel,
        grid=(b, nt),
        in_specs=[pl.BlockSpec((tt, RWKV_COLS), lambda bi, i: (bi * nt + i, 0)),
                  const((1, RWKV_COLS)), const((1, w_)), const((LANES, 2 * w_)), const((1, w_)),
                  const((D_GATE_LORA, w_)), const((1, w_)), const((1, w_)), const((1, w_)),
                  const((1, w_)), const((1, w_)), const((hw, hw)), const((tt, tt))],
        out_specs=pl.BlockSpec((tt, w_), lambda bi, i: (bi * nt + i, 0)),
        out_shape=jax.ShapeDtypeStruct((b * t, w_), BF16),
        scratch_shapes=[pltpu.VMEM((1, RWKV_COLS), F32),
                        pltpu.VMEM((w_ // LANES, LANES, LANES), F32),
                        pltpu.VMEM((tt, w_), F32)],
        compiler_params=_params("parallel", "arbitrary"),
        name="rwkv7",
    )(yr, vec(mu, RWKV_COLS), vec(w0, w_), wl, vec(a0, w_), g_lora_up, vec(k_k, w_), vec(k_a, w_),
      vec(r_k, w_), vec(lnx_w, w_), vec(lnx_b, w_), hs, tri)


def _compress_kernel(zk_ref, zv_ref, pek_ref, pev_ref, wk1_ref, wv1_ref, wk2_ref, wv2_ref, kc_ref, vc_ref):
    def one(z_ref, pe_ref, w1_ref, w2_ref, out_ref):
        z = z_ref[0]
        nrow = z.shape[0]
        za = z + pe_ref[0:1, :]
        zb = z + pe_ref[1:2, :]
        acc = jnp.zeros((nrow, LANES), F32)
        for gi in range(NSA_KV_HEADS):
            ha = _mm(za, w1_ref[gi, 0], "b3")
            hb = _mm(zb, w1_ref[gi, 1], "b3")
            hid = ha + pltpu.roll(hb, nrow - 1, 0)
            acc = acc + _mm(jax.nn.gelu(hid), w2_ref[gi], "b3")
        out_ref[0] = acc

    one(zk_ref, pek_ref, wk1_ref, wk2_ref, kc_ref)
    one(zv_ref, pev_ref, wv1_ref, wv2_ref, vc_ref)


def _compress_weights(pos, w1, w2):
    half = CMP_BLOCK // 2
    w1r = w1.reshape(2, half, HEAD_DIM, CMP_HIDDEN)
    w1g = jnp.zeros((NSA_KV_HEADS, 2, half, NSA_KV_HEADS, HEAD_DIM, CMP_HIDDEN), F32)
    for gi in range(NSA_KV_HEADS):
        w1g = w1g.at[gi, :, :, gi].set(w1r)
    w1g = w1g.reshape(NSA_KV_HEADS, 2, half * NSA_KV_WIDTH, CMP_HIDDEN)
    w2g = jnp.zeros((NSA_KV_HEADS, CMP_HIDDEN, NSA_KV_HEADS, HEAD_DIM), F32)
    for gi in range(NSA_KV_HEADS):
        w2g = w2g.at[gi, :, gi].set(w2)
    w2g = w2g.reshape(NSA_KV_HEADS, CMP_HIDDEN, NSA_KV_WIDTH)
    pe = jnp.broadcast_to(pos.reshape(2, half, 1, HEAD_DIM), (2, half, NSA_KV_HEADS, HEAD_DIM))
    pe = pe.reshape(2, half * NSA_KV_WIDTH)
    return pe, w1g, w2g


def _compress(kc, vc, b, t, cmp_pos_k, cmp_pos_v, k_w1, k_w2, v_w1, v_w2):
    nrow = t // CMP_STRIDE
    zw = CMP_STRIDE * NSA_KV_WIDTH
    pek, wk1, wk2 = _compress_weights(cmp_pos_k, k_w1, k_w2)
    pev, wv1, wv2 = _compress_weights(cmp_pos_v, v_w1, v_w2)
    zspec = pl.BlockSpec((1, nrow, zw), lambda bi: (bi, 0, 0))
    ospec = pl.BlockSpec((1, nrow, LANES), lambda bi: (bi, 0, 0))
    c2 = lambda shape: pl.BlockSpec(shape, lambda bi: (0,) * len(shape))
    return pl.pallas_call(
        _compress_kernel,
        grid=(b,),
        in_specs=[zspec, zspec, c2((2, zw)), c2((2, zw)), c2(wk1.shape), c2(wv1.shape), c2(wk2.shape), c2(wv2.shape)],
        out_specs=[ospec, ospec],
        out_shape=[jax.ShapeDtypeStruct((b, nrow, LANES), F32)] * 2,
        compiler_params=_params("parallel"),
        name="nsa_compress",
    )(kc.reshape(b, nrow, zw), vc.reshape(b, nrow, zw), pek, pev, wk1, wv1, wk2, wv2)


SEL_TILE = 256
SEL_UNROLL = 8
WIN_TILE = 128
NSA_QBLK = 2


def _n_win_tiles():
    tq = NSA_QBLK * Q_BLOCK
    span = WINDOW + tq + WIN_TILE - math.gcd(WIN_TILE, tq)
    return -(-span // WIN_TILE)


def _nsa_kernel(q_ref, kcmp_ref, vcmp_ref, ks_ref, vs_ref, kw_ref, vw_ref, gl_ref, ovt_ref, oh_ref,
                o_ref, m_scr, l_scr, acc_scr, sc_scr, imp_scr, cnt_scr, sel_scr, g_scr):
    s2 = pl.program_id(1)
    qb = Q_BLOCK
    nq = NSA_QBLK
    n_head = NSA_Q_HEADS
    rows = nq * n_head * qb
    n_cmp = kcmp_ref.shape[1]
    n_sb = ovt_ref.shape[0]
    s_of = [s2 * nq + qi for qi in range(nq)]
    s_last = s_of[-1]

    def tok(qi, shape):
        return s_of[qi] * qb + lax.broadcasted_iota(jnp.int32, shape, 0)

    def per_head(x64s):
        return jnp.concatenate([x for x in x64s for _ in range(n_head)], axis=0)

    def per_group_head(x):
        parts = []
        for qi in range(nq):
            for gi in range(NSA_KV_HEADS):
                r0 = (qi * NSA_KV_HEADS + gi) * qb
                parts += [x[r0:r0 + qb]] * NSA_HPG
        return jnp.concatenate(parts, axis=0)

    lane64 = lax.broadcasted_iota(jnp.int32, (qb, LANES), 1)
    sig = jax.nn.sigmoid(gl_ref[...])
    for col in range(n_head * N_BRANCH):
        g_scr[col] = jnp.broadcast_to(sig[:, col:col + 1], (nq * qb, LANES))

    qq = jnp.concatenate([q_ref[qi * qb:(qi + 1) * qb, hq * LANES:(hq + 1) * LANES]
                          for qi in range(nq) for hq in range(n_head)], axis=0)

    n_win = _n_win_tiles()
    first = (s2 * nq * qb - WINDOW) // WIN_TILE
    lane_w = lax.broadcasted_iota(jnp.int32, (qb, WIN_TILE), 1)
    win_sc, win_k0 = [], []
    mxw = jnp.full((rows, LANES), NEG_INF, F32)
    for j in range(n_win):
        kt = jnp.maximum(first + j, 0)
        k0 = pl.multiple_of(kt * WIN_TILE, WIN_TILE)
        kp = k0 + lane_w + jnp.where(first + j >= 0, 0, 1 << 24)
        bias = per_head([jnp.where((kp <= tok(qi, kp.shape)) & (kp > tok(qi, kp.shape) - WINDOW), 0.0, NEG_INF)
                         for qi in range(nq)])
        sc_ = _dot_nt(qq, kw_ref[0, pl.ds(k0, WIN_TILE), :]) + bias
        for c0 in range(0, WIN_TILE, LANES):
            mxw = jnp.maximum(mxw, sc_[:, c0:c0 + LANES])
        win_sc.append(sc_)
        win_k0.append(k0)
    m_w = jnp.max(mxw, axis=1, keepdims=True)
    l_w = jnp.zeros((rows, LANES), F32)
    acc_w = jnp.zeros((rows, LANES), F32)
    for j in range(n_win):
        p = jnp.exp2(win_sc[j] - m_w)
        for c0 in range(0, WIN_TILE, LANES):
            l_w = l_w + p[:, c0:c0 + LANES]
        acc_w = acc_w + _dot(p.astype(BF16), vw_ref[0, pl.ds(win_k0[j], WIN_TILE), :])
    o_win = acc_w / jnp.sum(l_w, axis=1, keepdims=True)

    cmp_i = lax.broadcasted_iota(jnp.int32, (qb, n_cmp), 1)
    cbias = per_head([jnp.where((cmp_i * CMP_STRIDE + CMP_BLOCK - 1 <= tok(qi, cmp_i.shape)) & (cmp_i < n_cmp - 1),
                                0.0, NEG_INF) for qi in range(nq)])
    sc = _mm(qq, kcmp_ref[0], "xa", _dot_nt) + cbias
    mx = jnp.max(sc, axis=1, keepdims=True)
    e = jnp.exp2(sc - mx)
    den = jnp.sum(e, axis=1, keepdims=True)
    p_c = e * jnp.where(mx > 0.5 * NEG_INF, 1.0 / den, 0.0)
    o_cmp = _dot(p_c.astype(BF16), vcmp_ref[0].astype(BF16))

    pc_sums = []
    for qg in range(nq * NSA_KV_HEADS):
        r0 = qg * NSA_HPG * qb
        acc = p_c[r0:r0 + qb]
        for h in range(1, NSA_HPG):
            acc = acc + p_c[r0 + h * qb:r0 + (h + 1) * qb]
        pc_sums.append(acc)
    pcs = jnp.concatenate(pc_sums, axis=0)
    imp = _dot_nt(ovt_ref[...], pcs, HI)
    nl = nq * NSA_KV_HEADS * qb
    blk = lax.broadcasted_iota(jnp.int32, (n_sb, nl), 0)
    s_lane = s2 * nq + lax.broadcasted_iota(jnp.int32, (n_sb, nl), 1) // (NSA_KV_HEADS * qb)
    forced = (blk == 0) | (blk == s_lane) | (blk == s_lane - 1)
    imp = jnp.where(blk <= s_lane, imp + jnp.where(forced, FORCE_BONUS, 0.0), NEG_INF)
    n_sel = min(SEL_TOPK, n_sb)
    sel_scr[...] = jnp.where(blk <= s_lane, 1.0, 0.0)

    @pl.when(s_last >= n_sel)
    def _():
        imp_scr[...] = imp
        cnt_scr[...] = jnp.zeros(cnt_scr.shape, F32)
        sub = lax.broadcasted_iota(jnp.int32, (8, nl), 0)
        for ig in range(n_sb // 8):
            @pl.when(ig * 8 <= s_last)
            def _(ig=ig):
                impv = imp_scr[...]
                cnt = [cnt_scr[rg * 8:(rg + 1) * 8, :] for rg in range(n_sb // 8)]
                for i2 in range(ig * 8, ig * 8 + 8):
                    vi = impv[i2:i2 + 1, :]
                    for rg in range(n_sb // 8):
                        vj = impv[rg * 8:(rg + 1) * 8, :]
                        if rg < ig:
                            one = jnp.where(vi > vj, 1.0, 0.0)
                        elif rg > ig:
                            one = jnp.where(vi >= vj, 1.0, 0.0)
                        else:
                            one = jnp.where(sub > i2 - ig * 8, jnp.where(vi >= vj, 1.0, 0.0),
                                            jnp.where(vi > vj, 1.0, 0.0))
                        cnt[rg] = cnt[rg] + one
                for rg in range(n_sb // 8):
                    cnt_scr[rg * 8:(rg + 1) * 8, :] = cnt[rg]
        sel_scr[...] = jnp.where((cnt_scr[...] < n_sel) & (blk <= s_lane), 1.0, 0.0)

    sel_t = sel_scr[...]
    if n_sb < LANES:
        sel_t = jnp.concatenate([sel_t, jnp.zeros((LANES - n_sb, nl), F32)], axis=0)
    unsel = ((sel_t.T - 1.0) * -NEG_INF).astype(BF16)
    q_ext = jnp.concatenate([qq, per_group_head(unsel)], axis=1)

    lane_s = lax.broadcasted_iota(jnp.int32, (qb, SEL_TILE), 1)
    n_tiles = ((s_last + 1) * qb + SEL_TILE - 1) // SEL_TILE
    n_full = n_tiles // SEL_UNROLL
    rem = n_tiles % SEL_UNROLL
    rem_blocks = [w for w in (SEL_UNROLL >> i for i in range(1, SEL_UNROLL.bit_length())) if w > 0]

    def rem_start(w):
        return n_full * SEL_UNROLL + (rem & ~(2 * w - 1))

    def tile_scores(j, mx_):
        k0 = pl.multiple_of(j * SEL_TILE, SEL_TILE)
        causal = per_head([jnp.where(k0 + lane_s <= tok(qi, lane_s.shape), 0.0, NEG_INF) for qi in range(nq)])
        k_ext = jnp.concatenate([ks_ref[0, pl.ds(k0, SEL_TILE), :], oh_ref[j]], axis=1)
        sc_ = _dot_nt(q_ext, k_ext) + causal
        sc_scr[j] = sc_
        for c0 in range(0, SEL_TILE, LANES):
            mx_ = jnp.maximum(mx_, sc_[:, c0:c0 + LANES])
        return mx_

    def pass1(it, mx_):
        for u in range(SEL_UNROLL):
            mx_ = tile_scores(it * SEL_UNROLL + u, mx_)
        return mx_

    m_scr[...] = lax.fori_loop(0, n_full, pass1, jnp.full((rows, LANES), NEG_INF, F32))
    for w in rem_blocks:
        @pl.when((rem & w) != 0)
        def _(w=w):
            mx_ = m_scr[...]
            for u in range(w):
                mx_ = tile_scores(rem_start(w) + u, mx_)
            m_scr[...] = mx_
    m_scr[...] = jnp.broadcast_to(jnp.max(m_scr[...], axis=1, keepdims=True), m_scr.shape)
    l_scr[...] = jnp.zeros(l_scr.shape, F32)
    acc_scr[...] = jnp.zeros(acc_scr.shape, F32)

    def tile_pv(j):
        k0 = pl.multiple_of(j * SEL_TILE, SEL_TILE)
        m_b = jnp.concatenate([m_scr[...]] * (SEL_TILE // LANES), axis=1)
        p = jnp.exp2(sc_scr[j] - m_b)
        l_new = l_scr[...]
        for c0 in range(0, SEL_TILE, LANES):
            l_new = l_new + p[:, c0:c0 + LANES]
        l_scr[...] = l_new
        acc_scr[...] += _dot(p.astype(BF16), vs_ref[0, pl.ds(k0, SEL_TILE), :])

    def pass2(it, c):
        for u in range(SEL_UNROLL):
            tile_pv(it * SEL_UNROLL + u)
        return c

    lax.fori_loop(0, n_full, pass2, 0)
    for w in rem_blocks:
        @pl.when((rem & w) != 0)
        def _(w=w):
            for u in range(w):
                tile_pv(rem_start(w) + u)
    o_sel = acc_scr[...] / jnp.sum(l_scr[...], axis=1, keepdims=True)

    low_half = lane64 < HEAD_DIM
    for qi in range(nq):
        ts = slice(qi * qb, (qi + 1) * qb)
        for h in range(NSA_HPG):
            halves = []
            for gi in range(NSA_KV_HEADS):
                hq = gi * NSA_HPG + h
                r0 = (qi * n_head + hq) * qb
                hr = slice(r0, r0 + qb)
                col = hq * N_BRANCH
                halves.append(g_scr[col, ts] * o_cmp[hr] + g_scr[col + 1, ts] * o_sel[hr] + g_scr[col + 2, ts] * o_win[hr])
            o_ref[ts, h * LANES:(h + 1) * LANES] = jnp.where(low_half, halves[0], halves[1]).astype(o_ref.dtype)


def _nsa_attention(q, k_cmp, v_cmp, ks, vs, kw, vw, gl, b, t):
    tq = NSA_QBLK * Q_BLOCK
    ns = t // tq
    n_sb = t // SEL_BLOCK
    n_cmp = t // CMP_STRIDE
    n_kt = t // SEL_TILE
    cmp_start = jnp.arange(n_cmp) * CMP_STRIDE
    sb = jnp.arange(n_sb)
    ovt = ((cmp_start[None, :] < (sb[:, None] + 1) * SEL_BLOCK)
           & (cmp_start[None, :] + CMP_BLOCK > sb[:, None] * SEL_BLOCK)
           & (jnp.arange(n_cmp)[None, :] < n_cmp - 1)).astype(F32)
    key_blk = (jnp.arange(n_kt)[:, None] * SEL_TILE + jnp.arange(SEL_TILE)[None, :]) // SEL_BLOCK
    onehot = (key_blk[:, :, None] == jnp.arange(LANES)[None, None, :]).astype(BF16)
    rows = NSA_QBLK * NSA_Q_HEADS * Q_BLOCK
    nl = NSA_QBLK * NSA_KV_HEADS * Q_BLOCK
    qspec = pl.BlockSpec((tq, Q_PAD_COLS), lambda bi, si: (bi * ns + si, 0))
    seq = lambda: pl.BlockSpec((1, t, LANES), lambda bi, si: (bi, 0, 0))
    cmp_spec = lambda: pl.BlockSpec((1, n_cmp, LANES), lambda bi, si: (bi, 0, 0))
    k3 = lambda a_: a_.reshape(b, t, LANES)
    return pl.pallas_call(
        _nsa_kernel,
        grid=(b, ns),
        in_specs=[qspec, cmp_spec(), cmp_spec(), seq(), seq(), seq(), seq(),
                  pl.BlockSpec((tq, LANES), lambda bi, si: (bi * ns + si, 0)),
                  pl.BlockSpec((n_sb, n_cmp), lambda bi, si: (0, 0)),
                  pl.BlockSpec((n_kt, SEL_TILE, LANES), lambda bi, si: (0, 0, 0))],
        out_specs=pl.BlockSpec((tq, NSA_WIDTH), lambda bi, si: (bi * ns + si, 0)),
        out_shape=jax.ShapeDtypeStruct((b * t, NSA_WIDTH), BF16),
        scratch_shapes=[pltpu.VMEM((rows, LANES), F32),
                        pltpu.VMEM((rows, LANES), F32),
                        pltpu.VMEM((rows, LANES), F32),
                        pltpu.VMEM((n_kt, rows, SEL_TILE), F32),
                        pltpu.VMEM((n_sb, nl), F32),
                        pltpu.VMEM((n_sb, nl), F32),
                        pltpu.VMEM((n_sb, nl), F32),
                        pltpu.VMEM((NSA_Q_HEADS * N_BRANCH, tq, LANES), F32)],
        compiler_params=_params("parallel", "arbitrary"),
        name="nsa_attention",
    )(q, k_cmp, v_cmp, k3(ks), k3(vs), k3(kw), k3(vw), gl, ovt, onehot)


FF_TILE = 256


def _ffn_kernel(x_ref, orw_ref, ons_ref, wor_ref, won_ref, n2_ref, w1_ref, w3_ref, w2_ref, nf_ref,
                out_ref, h_scr, u_scr, acc_scr):
    j = pl.program_id(1)

    @pl.when(j == 0)
    def _():
        h = x_ref[...] + _dot(orw_ref[...], wor_ref[...]) + _dot(ons_ref[...], won_ref[...])
        h_scr[...] = h
        ms = jnp.mean(h * h, axis=-1, keepdims=True)
        u_scr[...] = (h * lax.rsqrt(ms + RMS_EPS) * n2_ref[...]).astype(BF16)
        acc_scr[...] = jnp.zeros_like(acc_scr)

    u = u_scr[...]
    gate = _dot(u, w1_ref[...])
    up = _dot(u, w3_ref[...])
    act = (jax.nn.silu(gate) * up).astype(BF16)
    acc_scr[...] += _dot(act, w2_ref[...])

    @pl.when(j == pl.num_programs(1) - 1)
    def _():
        h = h_scr[...] + acc_scr[...]
        ms = jnp.mean(h * h, axis=-1, keepdims=True)
        out_ref[...] = h * lax.rsqrt(ms + RMS_EPS) * nf_ref[...]


def _out_ffn(x2, o_rwkv, o_nsa, w_out, norm2_w, ffn_w1, ffn_w3, ffn_w2, final_norm_w):
    n = x2.shape[0]
    tm = 1024
    w_or = w_out[:RWKV_WIDTH].astype(BF16)
    w_n = w_out[RWKV_WIDTH:].reshape(NSA_KV_HEADS, NSA_HPG, HEAD_DIM, D_MODEL)
    w_on = w_n.transpose(1, 0, 2, 3).reshape(NSA_WIDTH, D_MODEL).astype(BF16)
    row = lambda width: pl.BlockSpec((tm, width), lambda i, j: (i, 0))
    const = lambda shape: pl.BlockSpec(shape, lambda i, j: (0, 0))
    return pl.pallas_call(
        _ffn_kernel,
        grid=(n // tm, D_FF // FF_TILE),
        in_specs=[row(D_MODEL), row(RWKV_WIDTH), row(NSA_WIDTH),
                  const((RWKV_WIDTH, D_MODEL)), const((NSA_WIDTH, D_MODEL)), const((1, D_MODEL)),
                  pl.BlockSpec((D_MODEL, FF_TILE), lambda i, j: (0, j)),
                  pl.BlockSpec((D_MODEL, FF_TILE), lambda i, j: (0, j)),
                  pl.BlockSpec((FF_TILE, D_MODEL), lambda i, j: (j, 0)),
                  const((1, D_MODEL))],
        out_specs=row(D_MODEL),
        out_shape=jax.ShapeDtypeStruct((n, D_MODEL), F32),
        scratch_shapes=[pltpu.VMEM((tm, D_MODEL), F32), pltpu.VMEM((tm, D_MODEL), BF16),
                        pltpu.VMEM((tm, D_MODEL), F32)],
        compiler_params=_params("parallel", "arbitrary"),
        name="out_ffn",
    )(x2, o_rwkv, o_nsa, w_or, w_on, norm2_w.reshape(1, D_MODEL), ffn_w1.astype(BF16), ffn_w3.astype(BF16),
      ffn_w2.astype(BF16), final_norm_w.reshape(1, D_MODEL))


def kernel(x, positions, norm1_w, w_in, mu_rwkv, w0, w_lora_up, a0, a_lora_up, g_lora_up, k_k, k_a, r_k, lnx_w, lnx_b, cmp_pos_k, cmp_pos_v, cmp_k_w1, cmp_k_w2, cmp_v_w1, cmp_v_w2, w_out, norm2_w, ffn_w1, ffn_w3, ffn_w2, final_norm_w):
    b, t, d = x.shape
    assert d == D_MODEL and norm1_w.shape[0] == 1, "single-layer block with d_model 1024"
    assert t % RWKV_TT == 0 and t % SEL_TILE == 0 and t % (NSA_QBLK * Q_BLOCK) == 0
    x2 = x.reshape(b * t, d)
    yr, q, kc, vc, ks, vs, kw, vw, gl = _in_proj(x2, norm1_w[0], _pack_w_in(w_in[0]), positions)
    o_rwkv = _rwkv(yr, b, t, mu_rwkv[0], w0[0], w_lora_up[0], a0[0], a_lora_up[0], g_lora_up[0],
                   k_k[0], k_a[0], r_k[0], lnx_w[0], lnx_b[0])
    k_cmp, v_cmp = _compress(kc, vc, b, t, cmp_pos_k[0], cmp_pos_v[0], cmp_k_w1[0], cmp_k_w2[0],
                             cmp_v_w1[0], cmp_v_w2[0])
    o_nsa = _nsa_attention(q, k_cmp, v_cmp, ks, vs, kw, vw, gl, b, t)
    out = _out_ffn(x2, o_rwkv, o_nsa, w_out[0], norm2_w[0], ffn_w1[0], ffn_w3[0], ffn_w2[0], final_norm_w)
    return out.reshape(b, t, d)
```

```python
import math

import jax
import jax.numpy as jnp
from jax import lax
from jax.experimental import pallas as pl
from jax.experimental.pallas import tpu as pltpu

F32 = jnp.float32
BF16 = jnp.bfloat16
HI = lax.Precision.HIGHEST

D_MODEL = 1024
HEAD_DIM = 64
RWKV_WIDTH = 512
RWKV_HEADS = 8
D_DECAY_LORA = 64
D_AAA_LORA = 64
D_GATE_LORA = 128
RWKV_COLS = 3 * RWKV_WIDTH + D_DECAY_LORA + D_AAA_LORA + D_GATE_LORA
NSA_WIDTH = 512
NSA_Q_HEADS = 8
NSA_KV_HEADS = 2
NSA_HPG = 4
NSA_KV_WIDTH = 128
N_BRANCH = 3
CMP_BLOCK = 32
CMP_STRIDE = 16
CMP_HIDDEN = 256
SEL_BLOCK = 64
SEL_TOPK = 16
WINDOW = 512
Q_BLOCK = 64
ROPE_THETA = 10000.0
D_FF = 2816
RMS_EPS = 1e-6
LNX_EPS = 64e-5
FORCE_BONUS = 1e4
NEG_INF = -1e30

LANES = 128
CHUNK = 64
Q_PAD_COLS = NSA_Q_HEADS * LANES
PROJ_COLS = RWKV_COLS + Q_PAD_COLS + 6 * NSA_KV_WIDTH + LANES
VMEM_LIMIT = 56 * 1024 * 1024


def _dot(a, b, prec=None):
    return jnp.dot(a, b, preferred_element_type=F32, precision=prec)


def _dot_nt(a, b, prec=None):
    return lax.dot_general(a, b, (((1,), (1,)), ((), ())), preferred_element_type=F32, precision=prec)


def _dot_tn(a, b, prec=None):
    return lax.dot_general(a, b, (((0,), (0,)), ((), ())), preferred_element_type=F32, precision=prec)


def _split(a):
    hi = a.astype(BF16)
    return hi, (a - hi.astype(F32)).astype(BF16)


def _mm(a, b, mode, dot=_dot):
    if mode == "hi":
        return dot(a, b, HI)
    if mode == "b1":
        return dot(a.astype(BF16), b.astype(BF16))
    if mode == "xa":
        bh, bl = _split(b)
        ab = a.astype(BF16)
        return dot(ab, bh) + dot(ab, bl)
    if mode == "xb":
        ah, al = _split(a)
        bb = b.astype(BF16)
        return dot(ah, bb) + dot(al, bb)
    ah, al = _split(a)
    bh, bl = _split(b)
    return dot(ah, bh) + (dot(ah, bl) + dot(al, bh))


def _params(*sem):
    return pltpu.CompilerParams(dimension_semantics=sem, vmem_limit_bytes=VMEM_LIMIT)


def _proj_kernel(x_ref, n1_ref, w_ref, pos_ref, invf_ref, sign_ref,
                 yr_ref, q_ref, kc_ref, vc_ref, ks_ref, vs_ref, kw_ref, vw_ref, gl_ref, cos_scr, sin_scr):
    x = x_ref[...]
    ms = jnp.mean(x * x, axis=-1, keepdims=True)
    xn = (x * lax.rsqrt(ms + RMS_EPS) * n1_ref[...]).astype(BF16)
    ang = pos_ref[...].astype(F32) * invf_ref[...]
    cos_scr[...] = jnp.cos(ang)
    sin_scr[...] = jnp.sin(ang) * sign_ref[...]
    lane = lax.broadcasted_iota(jnp.int32, ang.shape, 1)
    first_half = (lane % HEAD_DIM) < (HEAD_DIM // 2)

    def rope(v):
        rot = jnp.where(first_half, pltpu.roll(v, LANES - HEAD_DIM // 2, 1), pltpu.roll(v, HEAD_DIM // 2, 1))
        return v * cos_scr[...] + rot * sin_scr[...]

    def cols(c0, width):
        return _dot(xn, w_ref[:, c0:c0 + width])

    for j in range(RWKV_COLS // 256):
        yr_ref[:, j * 256:(j + 1) * 256] = cols(j * 256, 256)
    scale = HEAD_DIM ** -0.5 * math.log2(math.e)
    for hq in range(0, NSA_Q_HEADS, 2):
        q2 = cols(RWKV_COLS + hq * LANES, 2 * LANES)
        for e in range(2):
            q = rope(q2[:, e * LANES:(e + 1) * LANES]) * scale
            q_ref[:, (hq + e) * LANES:(hq + e + 1) * LANES] = q.astype(BF16)
    c0 = RWKV_COLS + Q_PAD_COLS
    kv = cols(c0, 2 * LANES)
    kc_ref[...] = rope(kv[:, :LANES])
    vc_ref[...] = kv[:, LANES:]
    kv = cols(c0 + 2 * LANES, 2 * LANES)
    ks_ref[...] = rope(kv[:, :LANES]).astype(BF16)
    vs_ref[...] = kv[:, LANES:].astype(BF16)
    kv = cols(c0 + 4 * LANES, 2 * LANES)
    kw_ref[...] = rope(kv[:, :LANES]).astype(BF16)
    vw_ref[...] = kv[:, LANES:].astype(BF16)
    gl_ref[...] = cols(c0 + 6 * LANES, LANES)


def _pack_w_in(w_in):
    w_r = w_in[:, :RWKV_COLS]
    w_q = w_in[:, RWKV_COLS:RWKV_COLS + NSA_WIDTH].reshape(D_MODEL, NSA_KV_HEADS, NSA_HPG, HEAD_DIM)
    zero = jnp.zeros_like(w_q)
    w_q = jnp.stack([jnp.where(jnp.arange(NSA_KV_HEADS)[None, :, None, None] == gg, w_q, zero)
                     for gg in range(NSA_KV_HEADS)], axis=3)
    w_q = w_q.reshape(D_MODEL, Q_PAD_COLS)
    c0 = RWKV_COLS + NSA_WIDTH
    w_kv = w_in[:, c0:c0 + 6 * NSA_KV_WIDTH]
    w_g = w_in[:, c0 + 6 * NSA_KV_WIDTH:]
    w_g = jnp.pad(w_g, ((0, 0), (0, LANES - w_g.shape[1])))
    return jnp.concatenate([w_r, w_q, w_kv, w_g], axis=1).astype(BF16)


def _in_proj(x2, norm1_w, w_packed, positions):
    n = x2.shape[0]
    tm = 512
    half = HEAD_DIM // 2
    inv_freq = ROPE_THETA ** (-jnp.arange(half, dtype=F32) / half)
    invf = jnp.tile(inv_freq, LANES // half).reshape(1, LANES)
    lane = jnp.arange(LANES)
    sign = jnp.where((lane % HEAD_DIM) < half, -1.0, 1.0).astype(F32).reshape(1, LANES)
    row = lambda width: pl.BlockSpec((tm, width), lambda i: (i, 0))
    const = lambda shape: pl.BlockSpec(shape, lambda i: (0, 0))
    out_shape = [
        jax.ShapeDtypeStruct((n, RWKV_COLS), F32),
        jax.ShapeDtypeStruct((n, Q_PAD_COLS), BF16),
        jax.ShapeDtypeStruct((n, LANES), F32),
        jax.ShapeDtypeStruct((n, LANES), F32),
        jax.ShapeDtypeStruct((n, LANES), BF16),
        jax.ShapeDtypeStruct((n, LANES), BF16),
        jax.ShapeDtypeStruct((n, LANES), BF16),
        jax.ShapeDtypeStruct((n, LANES), BF16),
        jax.ShapeDtypeStruct((n, LANES), F32),
    ]
    out_specs = [row(RWKV_COLS), row(Q_PAD_COLS)] + [row(LANES)] * 7
    return pl.pallas_call(
        _proj_kernel,
        grid=(n // tm,),
        in_specs=[row(D_MODEL), const((1, D_MODEL)), const((D_MODEL, PROJ_COLS)), row(1),
                  const((1, LANES)), const((1, LANES))],
        out_specs=out_specs,
        out_shape=out_shape,
        scratch_shapes=[pltpu.VMEM((tm, LANES), F32), pltpu.VMEM((tm, LANES), F32)],
        compiler_params=_params("parallel"),
        name="in_proj",
    )(x2, norm1_w.reshape(1, D_MODEL), w_packed, positions.reshape(n, 1), invf, sign)


RWKV_TT = 256
P_LORA = "b3"
P_GATE = "b1"
P_HS = "b1"
P_CUM = "xa"


def _rwkv_kernel(y_ref, mu_ref, w0_ref, wl_ref, a0_ref, gup_ref, kk_ref, ka_ref, rk_ref,
                 lw_ref, lb_ref, hs_ref, tri_ref, o_ref, prev_ref, s_ref, o_scr):
    i = pl.program_id(1)

    @pl.when(i == 0)
    def _():
        prev_ref[...] = jnp.zeros_like(prev_ref)
        s_ref[...] = jnp.zeros_like(s_ref)

    tt = y_ref.shape[0]
    y = y_ref[...]
    row = lax.broadcasted_iota(jnp.int32, (tt, 1), 0)
    y_prev = jnp.where(row == 0, prev_ref[...], pltpu.roll(y, 1, 0))
    prev_ref[...] = y[tt - 1:tt, :]
    ys = y + (y_prev - y) * mu_ref[...]

    w_ = RWKV_WIDTH
    r = ys[:, 0:w_]
    k = ys[:, w_:2 * w_]
    v = ys[:, 2 * w_:3 * w_]
    z = ys[:, 3 * w_:3 * w_ + LANES]
    gd = ys[:, 3 * w_ + LANES:3 * w_ + 2 * LANES]
    lane = lax.broadcasted_iota(jnp.int32, z.shape, 1)
    zt = jnp.where(lane < D_DECAY_LORA, jnp.tanh(z), z)
    wa = _mm(zt, wl_ref[...], P_LORA)
    w_raw = w0_ref[...] + wa[:, :w_]
    a = jax.nn.sigmoid(a0_ref[...] + wa[:, w_:])
    logw = -jnp.exp(-jax.nn.softplus(-w_raw) - 0.5)
    g = _mm(jax.nn.sigmoid(gd), gup_ref[...], P_GATE)
    hs = hs_ref[...]

    def head_sum(x):
        hw = hs.shape[0]
        return jnp.concatenate([_mm(x[:, c0:c0 + hw], hs, P_HS) for c0 in range(0, w_, hw)], axis=1)

    kk = k * kk_ref[...]
    kk = kk * lax.rsqrt(jnp.maximum(head_sum(kk * kk), 1e-12))
    k2 = k * (1.0 + (a - 1.0) * ka_ref[...])
    alpha = -kk
    beta = kk * a

    cw = _mm(tri_ref[...], logw, P_CUM)
    e_in = jnp.exp(cw)
    e_ex = jnp.exp(cw - logw)
    e_neg = jnp.exp(-cw)
    a_t = alpha * e_ex
    r_t = r * e_in
    b_t = beta * e_neg
    k_t = k2 * e_neg

    m0 = lax.broadcasted_iota(jnp.int32, (CHUNK, LANES), 1) < HEAD_DIM

    def bd(xc):
        return jnp.concatenate([jnp.where(m0, xc, 0.0), jnp.where(m0, 0.0, xc)], axis=0)

    n2 = 2 * CHUNK
    ri = lax.broadcasted_iota(jnp.int32, (n2, n2), 0) % CHUNK
    ci = lax.broadcasted_iota(jnp.int32, (n2, n2), 1) % CHUNK
    strict = ri > ci
    incl = ri >= ci
    eye = jnp.where(lax.broadcasted_iota(jnp.int32, (n2, n2), 0) == lax.broadcasted_iota(jnp.int32, (n2, n2), 1),
                    1.0, 0.0)

    units = [(c, p) for c in range(tt // CHUNK) for p in range(w_ // LANES)]
    w_c = {}
    st = {}
    for c in range(tt // CHUNK):
        rs = slice(c * CHUNK, (c + 1) * CHUNK)
        cw_last = cw[c * CHUNK + CHUNK - 1:c * CHUNK + CHUNK, :]
        e_tot = jnp.exp(cw_last - cw[rs, :])
        w_c[c] = jnp.exp(cw_last)
        for p in range(w_ // LANES):
            ls = slice(p * LANES, (p + 1) * LANES)
            a_bd = bd(a_t[rs, ls]).astype(BF16)
            r_bd = bd(r_t[rs, ls])
            b_bd = bd(b_t[rs, ls]).astype(BF16)
            k_bd = bd(k_t[rs, ls]).astype(BF16)
            gm = _dot_nt(jnp.concatenate([a_bd, r_bd.astype(BF16)], axis=0), jnp.concatenate([b_bd, k_bd], axis=0))
            st[c, p] = dict(
                a_bd=a_bd, r_bd=r_bd,
                v_bd=bd(v[rs, ls]).astype(BF16),
                bh_bd=bd(beta[rs, ls] * e_tot[:, ls]).astype(BF16),
                kh_bd=bd(k2[rs, ls] * e_tot[:, ls]).astype(BF16),
                l_ab=jnp.where(strict, gm[0:n2, 0:n2], 0.0),
                l_ak=jnp.where(strict, gm[0:n2, n2:], 0.0).astype(BF16),
                m_rb=jnp.where(incl, gm[n2:, 0:n2], 0.0).astype(BF16),
                m_rk=jnp.where(incl, gm[n2:, n2:], 0.0).astype(BF16))

    pw, tm_ = {}, {}
    for u_ in units:
        l_ab = st[u_]["l_ab"]
        lb = l_ab.astype(BF16)
        pw[u_] = _dot(lb, lb)
        tm_[u_] = eye + l_ab
    for level in range(5):
        for u_ in units:
            pb = pw[u_].astype(BF16)
            if level < 4:
                y = _dot(pb, jnp.concatenate([pb, tm_[u_].astype(BF16)], axis=1))
                pw[u_] = y[:, 0:n2]
                tm_[u_] = tm_[u_] + y[:, n2:]
            else:
                tm_[u_] = tm_[u_] + _dot(pb, tm_[u_].astype(BF16))

    zero_bd = jnp.zeros((n2, n2), BF16)
    lv = {u_: _dot(st[u_]["l_ak"], st[u_]["v_bd"]).astype(BF16) for u_ in units}
    au = {u_: _dot(tm_[u_].astype(BF16), jnp.concatenate([st[u_]["a_bd"], lv[u_]], axis=1)).astype(BF16)
          for u_ in units}
    for u_ in units:
        d_ = st[u_]
        mo = _dot(jnp.concatenate([d_["m_rb"], d_["m_rk"]], axis=1),
                  jnp.concatenate([au[u_], jnp.concatenate([zero_bd, d_["v_bd"]], axis=1)], axis=0))
        d_["r_hat"] = (d_["r_bd"] + mo[:, 0:n2]).astype(BF16)
        d_["o0"] = mo[:, n2:]
    for u_ in units:
        st[u_]["g"] = _dot_tn(au[u_][:, 0:n2], st[u_]["bh_bd"]).astype(BF16)
    for u_ in units:
        d_ = st[u_]
        d_["n"] = _dot_tn(jnp.concatenate([au[u_][:, n2:], d_["v_bd"]], axis=0),
                          jnp.concatenate([d_["bh_bd"], d_["kh_bd"]], axis=0))

    for c, p in units:
        d_ = st[c, p]
        rs = slice(c * CHUNK, (c + 1) * CHUNK)
        ls = slice(p * LANES, (p + 1) * LANES)
        s_old = s_ref[p]
        sb = s_old.astype(BF16)
        o_bd = _dot_nt(d_["r_hat"], sb) + d_["o0"]
        o_scr[rs, ls] = o_bd[0:CHUNK] + o_bd[CHUNK:]
        s_ref[p] = s_old * w_c[c][:, ls] + _dot(sb, d_["g"]) + d_["n"]

    o = o_scr[...]
    inv_n = 1.0 / HEAD_DIM
    mean = head_sum(o) * inv_n
    d = o - mean
    var = head_sum(d * d) * inv_n
    on = d * lax.rsqrt(var + LNX_EPS) * lw_ref[...] + lb_ref[...]
    bonus = head_sum(r * k2 * rk_ref[...]) * v
    o_ref[...] = ((on + bonus) * g).astype(o_ref.dtype)


def _rwkv(yr, b, t, mu, w0, w_lora_up, a0, a_lora_up, g_lora_up, k_k, k_a, r_k, lnx_w, lnx_b):
    tt = RWKV_TT
    nt = t // tt
    w_ = RWKV_WIDTH
    wl = jnp.zeros((LANES, 2 * w_), F32)
    wl = wl.at[:D_DECAY_LORA, :w_].set(w_lora_up).at[D_DECAY_LORA:, w_:].set(a_lora_up)
    hw = 2 * LANES
    head = jnp.arange(hw) // HEAD_DIM
    hs = (head[:, None] == head[None, :]).astype(F32)
    ti = jnp.arange(tt)
    tri = ((ti[:, None] // CHUNK == ti[None, :] // CHUNK) & (ti[:, None] >= ti[None, :])).astype(F32)
    vec = lambda a_, width: a_.reshape(1, width)
    const = lambda shape: pl.BlockSpec(shape, lambda bi, i: (0, 0))
    return pl.pallas_call(
        _rwkv_kernel,
        grid=(b, nt),
        in_specs=[pl.BlockSpec((tt, RWKV_COLS), lambda bi, i: (bi * nt + i, 0)),
                  const((1, RWKV_COLS)), const((1, w_)), const((LANES, 2 * w_)), const((1, w_)),
                  const((D_GATE_LORA, w_)), const((1, w_)), const((1, w_)), const((1, w_)),
                  const((1, w_)), const((1, w_)), const((hw, hw)), const((tt, tt))],
        out_specs=pl.BlockSpec((tt, w_), lambda bi, i: (bi * nt + i, 0)),
        out_shape=jax.ShapeDtypeStruct((b * t, w_), BF16),
        scratch_shapes=[pltpu.VMEM((1, RWKV_COLS), F32),
                        pltpu.VMEM((w_ // LANES, LANES, LANES), F32),
                        pltpu.VMEM((tt, w_), F32)],
        compiler_params=_params("parallel", "arbitrary"),
        name="rwkv7",
    )(yr, vec(mu, RWKV_COLS), vec(w0, w_), wl, vec(a0, w_), g_lora_up, vec(k_k, w_), vec(k_a, w_),
      vec(r_k, w_), vec(lnx_w, w_), vec(lnx_b, w_), hs, tri)


def _compress_kernel(zk_ref, zv_ref, pek_ref, pev_ref, wk1_ref, wv1_ref, wk2_ref, wv2_ref, kc_ref, vc_ref):
    def one(z_ref, pe_ref, w1_ref, w2_ref, out_ref):
        z = z_ref[0]
        nrow = z.shape[0]
        za = z + pe_ref[0:1, :]
        zb = z + pe_ref[1:2, :]
        acc = jnp.zeros((nrow, LANES), F32)
        for gi in range(NSA_KV_HEADS):
            ha = _mm(za, w1_ref[gi, 0], "b3")
            hb = _mm(zb, w1_ref[gi, 1], "b3")
            hid = ha + pltpu.roll(hb, nrow - 1, 0)
            acc = acc + _mm(jax.nn.gelu(hid), w2_ref[gi], "b3")
        out_ref[0] = acc

    one(zk_ref, pek_ref, wk1_ref, wk2_ref, kc_ref)
    one(zv_ref, pev_ref, wv1_ref, wv2_ref, vc_ref)


def _compress_weights(pos, w1, w2):
    half = CMP_BLOCK // 2
    w1r = w1.reshape(2, half, HEAD_DIM, CMP_HIDDEN)
    w1g = jnp.zeros((NSA_KV_HEADS, 2, half, NSA_KV_HEADS, HEAD_DIM, CMP_HIDDEN), F32)
    for gi in range(NSA_KV_HEADS):
        w1g = w1g.at[gi, :, :, gi].set(w1r)
    w1g = w1g.reshape(NSA_KV_HEADS, 2, half * NSA_KV_WIDTH, CMP_HIDDEN)
    w2g = jnp.zeros((NSA_KV_HEADS, CMP_HIDDEN, NSA_KV_HEADS, HEAD_DIM), F32)
    for gi in range(NSA_KV_HEADS):
        w2g = w2g.at[gi, :, gi].set(w2)
    w2g = w2g.reshape(NSA_KV_HEADS, CMP_HIDDEN, NSA_KV_WIDTH)
    pe = jnp.broadcast_to(pos.reshape(2, half, 1, HEAD_DIM), (2, half, NSA_KV_HEADS, HEAD_DIM))
    pe = pe.reshape(2, half * NSA_KV_WIDTH)
    return pe, w1g, w2g


def _compress(kc, vc, b, t, cmp_pos_k, cmp_pos_v, k_w1, k_w2, v_w1, v_w2):
    nrow = t // CMP_STRIDE
    zw = CMP_STRIDE * NSA_KV_WIDTH
    pek, wk1, wk2 = _compress_weights(cmp_pos_k, k_w1, k_w2)
    pev, wv1, wv2 = _compress_weights(cmp_pos_v, v_w1, v_w2)
    zspec = pl.BlockSpec((1, nrow, zw), lambda bi: (bi, 0, 0))
    ospec = pl.BlockSpec((1, nrow, LANES), lambda bi: (bi, 0, 0))
    c2 = lambda shape: pl.BlockSpec(shape, lambda bi: (0,) * len(shape))
    return pl.pallas_call(
        _compress_kernel,
        grid=(b,),
        in_specs=[zspec, zspec, c2((2, zw)), c2((2, zw)), c2(wk1.shape), c2(wv1.shape), c2(wk2.shape), c2(wv2.shape)],
        out_specs=[ospec, ospec],
        out_shape=[jax.ShapeDtypeStruct((b, nrow, LANES), F32)] * 2,
        compiler_params=_params("parallel"),
        name="nsa_compress",
    )(kc.reshape(b, nrow, zw), vc.reshape(b, nrow, zw), pek, pev, wk1, wv1, wk2, wv2)


SEL_TILE = 256
SEL_UNROLL = 8
WIN_TILE = 128
NSA_QBLK = 2


def _n_win_tiles():
    tq = NSA_QBLK * Q_BLOCK
    span = WINDOW + tq + WIN_TILE - math.gcd(WIN_TILE, tq)
    return -(-span // WIN_TILE)


def _nsa_kernel(q_ref, kcmp_ref, vcmp_ref, ks_ref, vs_ref, kw_ref, vw_ref, gl_ref, ovt_ref, oh_ref,
                o_ref, m_scr, l_scr, acc_scr, sc_scr, imp_scr, cnt_scr, sel_scr, g_scr):
    s2 = pl.program_id(1)
    qb = Q_BLOCK
    nq = NSA_QBLK
    n_head = NSA_Q_HEADS
    rows = nq * n_head * qb
    n_cmp = kcmp_ref.shape[1]
    n_sb = ovt_ref.shape[0]
    s_of = [s2 * nq + qi for qi in range(nq)]
    s_last = s_of[-1]

    def tok(qi, shape):
        return s_of[qi] * qb + lax.broadcasted_iota(jnp.int32, shape, 0)

    def per_head(x64s):
        return jnp.concatenate([x for x in x64s for _ in range(n_head)], axis=0)

    def per_group_head(x):
        parts = []
        for qi in range(nq):
            for gi in range(NSA_KV_HEADS):
                r0 = (qi * NSA_KV_HEADS + gi) * qb
                parts += [x[r0:r0 + qb]] * NSA_HPG
        return jnp.concatenate(parts, axis=0)

    lane64 = lax.broadcasted_iota(jnp.int32, (qb, LANES), 1)
    sig = jax.nn.sigmoid(gl_ref[...])
    for col in range(n_head * N_BRANCH):
        g_scr[col] = jnp.broadcast_to(sig[:, col:col + 1], (nq * qb, LANES))

    qq = jnp.concatenate([q_ref[qi * qb:(qi + 1) * qb, hq * LANES:(hq + 1) * LANES]
                          for qi in range(nq) for hq in range(n_head)], axis=0)

    n_win = _n_win_tiles()
    first = (s2 * nq * qb - WINDOW) // WIN_TILE
    lane_w = lax.broadcasted_iota(jnp.int32, (qb, WIN_TILE), 1)
    win_sc, win_k0 = [], []
    mxw = jnp.full((rows, LANES), NEG_INF, F32)
    for j in range(n_win):
        kt = jnp.maximum(first + j, 0)
        k0 = pl.multiple_of(kt * WIN_TILE, WIN_TILE)
        kp = k0 + lane_w + jnp.where(first + j >= 0, 0, 1 << 24)
        bias = per_head([jnp.where((kp <= tok(qi, kp.shape)) & (kp > tok(qi, kp.shape) - WINDOW), 0.0, NEG_INF)
                         for qi in range(nq)])
        sc_ = _dot_nt(qq, kw_ref[0, pl.ds(k0, WIN_TILE), :]) + bias
        for c0 in range(0, WIN_TILE, LANES):
            mxw = jnp.maximum(mxw, sc_[:, c0:c0 + LANES])
        win_sc.append(sc_)
        win_k0.append(k0)
    m_w = jnp.max(mxw, axis=1, keepdims=True)
    l_w = jnp.zeros((rows, LANES), F32)
    acc_w = jnp.zeros((rows, LANES), F32)
    for j in range(n_win):
        p = jnp.exp2(win_sc[j] - m_w)
        for c0 in range(0, WIN_TILE, LANES):
            l_w = l_w + p[:, c0:c0 + LANES]
        acc_w = acc_w + _dot(p.astype(BF16), vw_ref[0, pl.ds(win_k0[j], WIN_TILE), :])
    o_win = acc_w / jnp.sum(l_w, axis=1, keepdims=True)

    cmp_i = lax.broadcasted_iota(jnp.int32, (qb, n_cmp), 1)
    cbias = per_head([jnp.where((cmp_i * CMP_STRIDE + CMP_BLOCK - 1 <= tok(qi, cmp_i.shape)) & (cmp_i < n_cmp - 1),
                                0.0, NEG_INF) for qi in range(nq)])
    sc = _mm(qq, kcmp_ref[0], "xa", _dot_nt) + cbias
    mx = jnp.max(sc, axis=1, keepdims=True)
    e = jnp.exp2(sc - mx)
    den = jnp.sum(e, axis=1, keepdims=True)
    p_c = e * jnp.where(mx > 0.5 * NEG_INF, 1.0 / den, 0.0)
    o_cmp = _dot(p_c.astype(BF16), vcmp_ref[0].astype(BF16))

    pc_sums = []
    for qg in range(nq * NSA_KV_HEADS):
        r0 = qg * NSA_HPG * qb
        acc = p_c[r0:r0 + qb]
        for h in range(1, NSA_HPG):
            acc = acc + p_c[r0 + h * qb:r0 + (h + 1) * qb]
        pc_sums.append(acc)
    pcs = jnp.concatenate(pc_sums, axis=0)
    imp = _dot_nt(ovt_ref[...], pcs, HI)
    nl = nq * NSA_KV_HEADS * qb
    blk = lax.broadcasted_iota(jnp.int32, (n_sb, nl), 0)
    s_lane = s2 * nq + lax.broadcasted_iota(jnp.int32, (n_sb, nl), 1) // (NSA_KV_HEADS * qb)
    forced = (blk == 0) | (blk == s_lane) | (blk == s_lane - 1)
    imp = jnp.where(blk <= s_lane, imp + jnp.where(forced, FORCE_BONUS, 0.0), NEG_INF)
    n_sel = min(SEL_TOPK, n_sb)
    sel_scr[...] = jnp.where(blk <= s_lane, 1.0, 0.0)

    @pl.when(s_last >= n_sel)
    def _():
        imp_scr[...] = imp
        cnt_scr[...] = jnp.zeros(cnt_scr.shape, F32)
        sub = lax.broadcasted_iota(jnp.int32, (8, nl), 0)
        for ig in range(n_sb // 8):
            @pl.when(ig * 8 <= s_last)
            def _(ig=ig):
                impv = imp_scr[...]
                cnt = [cnt_scr[rg * 8:(rg + 1) * 8, :] for rg in range(n_sb // 8)]
                for i2 in range(ig * 8, ig * 8 + 8):
                    vi = impv[i2:i2 + 1, :]
                    for rg in range(n_sb // 8):
                        vj = impv[rg * 8:(rg + 1) * 8, :]
                        if rg < ig:
                            one = jnp.where(vi > vj, 1.0, 0.0)
                        elif rg > ig:
                            one = jnp.where(vi >= vj, 1.0, 0.0)
                        else:
                            one = jnp.where(sub > i2 - ig * 8, jnp.where(vi >= vj, 1.0, 0.0),
                                            jnp.where(vi > vj, 1.0, 0.0))
                        cnt[rg] = cnt[rg] + one
                for rg in range(n_sb // 8):
                    cnt_scr[rg * 8:(rg + 1) * 8, :] = cnt[rg]
        sel_scr[...] = jnp.where((cnt_scr[...] < n_sel) & (blk <= s_lane), 1.0, 0.0)

    sel_t = sel_scr[...]
    if n_sb < LANES:
        sel_t = jnp.concatenate([sel_t, jnp.zeros((LANES - n_sb, nl), F32)], axis=0)
    unsel = ((sel_t.T - 1.0) * -NEG_INF).astype(BF16)
    q_ext = jnp.concatenate([qq, per_group_head(unsel)], axis=1)

    lane_s = lax.broadcasted_iota(jnp.int32, (qb, SEL_TILE), 1)
    n_tiles = ((s_last + 1) * qb + SEL_TILE - 1) // SEL_TILE
    n_full = n_tiles // SEL_UNROLL
    rem = n_tiles % SEL_UNROLL
    rem_blocks = [w for w in (SEL_UNROLL >> i for i in range(1, SEL_UNROLL.bit_length())) if w > 0]

    def rem_start(w):
        return n_full * SEL_UNROLL + (rem & ~(2 * w - 1))

    def tile_scores(j, mx_):
        k0 = pl.multiple_of(j * SEL_TILE, SEL_TILE)
        causal = per_head([jnp.where(k0 + lane_s <= tok(qi, lane_s.shape), 0.0, NEG_INF) for qi in range(nq)])
        k_ext = jnp.concatenate([ks_ref[0, pl.ds(k0, SEL_TILE), :], oh_ref[j]], axis=1)
        sc_ = _dot_nt(q_ext, k_ext) + causal
        sc_scr[j] = sc_
        for c0 in range(0, SEL_TILE, LANES):
            mx_ = jnp.maximum(mx_, sc_[:, c0:c0 + LANES])
        return mx_

    def pass1(it, mx_):
        for u in range(SEL_UNROLL):
            mx_ = tile_scores(it * SEL_UNROLL + u, mx_)
        return mx_

    m_scr[...] = lax.fori_loop(0, n_full, pass1, jnp.full((rows, LANES), NEG_INF, F32))
    for w in rem_blocks:
        @pl.when((rem & w) != 0)
        def _(w=w):
            mx_ = m_scr[...]
            for u in range(w):
                mx_ = tile_scores(rem_start(w) + u, mx_)
            m_scr[...] = mx_
    m_scr[...] = jnp.broadcast_to(jnp.max(m_scr[...], axis=1, keepdims=True), m_scr.shape)
    l_scr[...] = jnp.zeros(l_scr.shape, F32)
    acc_scr[...] = jnp.zeros(acc_scr.shape, F32)

    def tile_pv(j):
        k0 = pl.multiple_of(j * SEL_TILE, SEL_TILE)
        m_b = jnp.concatenate([m_scr[...]] * (SEL_TILE // LANES), axis=1)
        p = jnp.exp2(sc_scr[j] - m_b)
        l_new = l_scr[...]
        for c0 in range(0, SEL_TILE, LANES):
            l_new = l_new + p[:, c0:c0 + LANES]
        l_scr[...] = l_new
        acc_scr[...] += _dot(p.astype(BF16), vs_ref[0, pl.ds(k0, SEL_TILE), :])

    def pass2(it, c):
        for u in range(SEL_UNROLL):
            tile_pv(it * SEL_UNROLL + u)
        return c

    lax.fori_loop(0, n_full, pass2, 0)
    for w in rem_blocks:
        @pl.when((rem & w) != 0)
        def _(w=w):
            for u in range(w):
                tile_pv(rem_start(w) + u)
    o_sel = acc_scr[...] / jnp.sum(l_scr[...], axis=1, keepdims=True)

    low_half = lane64 < HEAD_DIM
    for qi in range(nq):
        ts = slice(qi * qb, (qi + 1) * qb)
        for h in range(NSA_HPG):
            halves = []
            for gi in range(NSA_KV_HEADS):
                hq = gi * NSA_HPG + h
                r0 = (qi * n_head + hq) * qb
                hr = slice(r0, r0 + qb)
                col = hq * N_BRANCH
                halves.append(g_scr[col, ts] * o_cmp[hr] + g_scr[col + 1, ts] * o_sel[hr] + g_scr[col + 2, ts] * o_win[hr])
            o_ref[ts, h * LANES:(h + 1) * LANES] = jnp.where(low_half, halves[0], halves[1]).astype(o_ref.dtype)


def _nsa_attention(q, k_cmp, v_cmp, ks, vs, kw, vw, gl, b, t):
    tq = NSA_QBLK * Q_BLOCK
    ns = t // tq
    n_sb = t // SEL_BLOCK
    n_cmp = t // CMP_STRIDE
    n_kt = t // SEL_TILE
    cmp_start = jnp.arange(n_cmp) * CMP_STRIDE
    sb = jnp.arange(n_sb)
    ovt = ((cmp_start[None, :] < (sb[:, None] + 1) * SEL_BLOCK)
           & (cmp_start[None, :] + CMP_BLOCK > sb[:, None] * SEL_BLOCK)
           & (jnp.arange(n_cmp)[None, :] < n_cmp - 1)).astype(F32)
    key_blk = (jnp.arange(n_kt)[:, None] * SEL_TILE + jnp.arange(SEL_TILE)[None, :]) // SEL_BLOCK
    onehot = (key_blk[:, :, None] == jnp.arange(LANES)[None, None, :]).astype(BF16)
    rows = NSA_QBLK * NSA_Q_HEADS * Q_BLOCK
    nl = NSA_QBLK * NSA_KV_HEADS * Q_BLOCK
    qspec = pl.BlockSpec((tq, Q_PAD_COLS), lambda bi, si: (bi * ns + si, 0))
    seq = lambda: pl.BlockSpec((1, t, LANES), lambda bi, si: (bi, 0, 0))
    cmp_spec = lambda: pl.BlockSpec((1, n_cmp, LANES), lambda bi, si: (bi, 0, 0))
    k3 = lambda a_: a_.reshape(b, t, LANES)
    return pl.pallas_call(
        _nsa_kernel,
        grid=(b, ns),
        in_specs=[qspec, cmp_spec(), cmp_spec(), seq(), seq(), seq(), seq(),
                  pl.BlockSpec((tq, LANES), lambda bi, si: (bi * ns + si, 0)),
                  pl.BlockSpec((n_sb, n_cmp), lambda bi, si: (0, 0)),
                  pl.BlockSpec((n_kt, SEL_TILE, LANES), lambda bi, si: (0, 0, 0))],
        out_specs=pl.BlockSpec((tq, NSA_WIDTH), lambda bi, si: (bi * ns + si, 0)),
        out_shape=jax.ShapeDtypeStruct((b * t, NSA_WIDTH), BF16),
        scratch_shapes=[pltpu.VMEM((rows, LANES), F32),
                        pltpu.VMEM((rows, LANES), F32),
                        pltpu.VMEM((rows, LANES), F32),
                        pltpu.VMEM((n_kt, rows, SEL_TILE), F32),
                        pltpu.VMEM((n_sb, nl), F32),
                        pltpu.VMEM((n_sb, nl), F32),
                        pltpu.VMEM((n_sb, nl), F32),
                        pltpu.VMEM((NSA_Q_HEADS * N_BRANCH, tq, LANES), F32)],
        compiler_params=_params("parallel", "arbitrary"),
        name="nsa_attention",
    )(q, k_cmp, v_cmp, k3(ks), k3(vs), k3(kw), k3(vw), gl, ovt, onehot)


FF_TILE = 256


def _ffn_kernel(x_ref, orw_ref, ons_ref, wor_ref, won_ref, n2_ref, w13_ref, w2_ref, nf_ref,
                out_ref, h_scr, u_scr, acc_scr):
    j = pl.program_id(1)

    @pl.when(j == 0)
    def _():
        h = x_ref[...] + _dot(orw_ref[...], wor_ref[...]) + _dot(ons_ref[...], won_ref[...])
        h_scr[...] = h
        ms = jnp.mean(h * h, axis=-1, keepdims=True)
        u_scr[...] = (h * lax.rsqrt(ms + RMS_EPS) * n2_ref[...]).astype(BF16)
        acc_scr[...] = jnp.zeros_like(acc_scr)

    u = u_scr[...]
    gu = _dot(u, w13_ref[0])
    act = (jax.nn.silu(gu[:, :FF_TILE]) * gu[:, FF_TILE:]).astype(BF16)
    acc_scr[...] += _dot(act, w2_ref[...])

    @pl.when(j == pl.num_programs(1) - 1)
    def _():
        h = h_scr[...] + acc_scr[...]
        ms = jnp.mean(h * h, axis=-1, keepdims=True)
        out_ref[...] = h * lax.rsqrt(ms + RMS_EPS) * nf_ref[...]


def _out_ffn(x2, o_rwkv, o_nsa, w_out, norm2_w, ffn_w1, ffn_w3, ffn_w2, final_norm_w):
    n = x2.shape[0]
    tm = 1024
    w_or = w_out[:RWKV_WIDTH].astype(BF16)
    w_n = w_out[RWKV_WIDTH:].reshape(NSA_KV_HEADS, NSA_HPG, HEAD_DIM, D_MODEL)
    w_on = w_n.transpose(1, 0, 2, 3).reshape(NSA_WIDTH, D_MODEL).astype(BF16)
    n_ff = D_FF // FF_TILE
    w13 = jnp.concatenate([ffn_w1.reshape(D_MODEL, n_ff, FF_TILE), ffn_w3.reshape(D_MODEL, n_ff, FF_TILE)],
                          axis=2).transpose(1, 0, 2).astype(BF16)
    row = lambda width: pl.BlockSpec((tm, width), lambda i, j: (i, 0))
    const = lambda shape: pl.BlockSpec(shape, lambda i, j: (0, 0))
    return pl.pallas_call(
        _ffn_kernel,
        grid=(n // tm, D_FF // FF_TILE),
        in_specs=[row(D_MODEL), row(RWKV_WIDTH), row(NSA_WIDTH),
                  const((RWKV_WIDTH, D_MODEL)), const((NSA_WIDTH, D_MODEL)), const((1, D_MODEL)),
                  pl.BlockSpec((1, D_MODEL, 2 * FF_TILE), lambda i, j: (j, 0, 0)),
                  pl.BlockSpec((FF_TILE, D_MODEL), lambda i, j: (j, 0)),
                  const((1, D_MODEL))],
        out_specs=row(D_MODEL),
        out_shape=jax.ShapeDtypeStruct((n, D_MODEL), F32),
        scratch_shapes=[pltpu.VMEM((tm, D_MODEL), F32), pltpu.VMEM((tm, D_MODEL), BF16),
                        pltpu.VMEM((tm, D_MODEL), F32)],
        compiler_params=_params("parallel", "arbitrary"),
        name="out_ffn",
    )(x2, o_rwkv, o_nsa, w_or, w_on, norm2_w.reshape(1, D_MODEL), w13, ffn_w2.astype(BF16),
      final_norm_w.reshape(1, D_MODEL))


def kernel(x, positions, norm1_w, w_in, mu_rwkv, w0, w_lora_up, a0, a_lora_up, g_lora_up, k_k, k_a, r_k, lnx_w, lnx_b, cmp_pos_k, cmp_pos_v, cmp_k_w1, cmp_k_w2, cmp_v_w1, cmp_v_w2, w_out, norm2_w, ffn_w1, ffn_w3, ffn_w2, final_norm_w):
    b, t, d = x.shape
    assert d == D_MODEL and norm1_w.shape[0] == 1, "single-layer block with d_model 1024"
    assert t % RWKV_TT == 0 and t % SEL_TILE == 0 and t % (NSA_QBLK * Q_BLOCK) == 0
    x2 = x.reshape(b * t, d)
    yr, q, kc, vc, ks, vs, kw, vw, gl = _in_proj(x2, norm1_w[0], _pack_w_in(w_in[0]), positions)
    o_rwkv = _rwkv(yr, b, t, mu_rwkv[0], w0[0], w_lora_up[0], a0[0], a_lora_up[0], g_lora_up[0],
                   k_k[0], k_a[0], r_k[0], lnx_w[0], lnx_b[0])
    k_cmp, v_cmp = _compress(kc, vc, b, t, cmp_pos_k[0], cmp_pos_v[0], cmp_k_w1[0], cmp_k_w2[0],
                             cmp_v_w1[0], cmp_v_w2[0])
    o_nsa = _nsa_attention(q, k_cmp, v_cmp, ks, vs, kw, vw, gl, b, t)
    out = _out_ffn(x2, o_rwkv, o_nsa, w_out[0], norm2_w[0], ffn_w1[0], ffn_w3[0], ffn_w2[0], final_norm_w)
    return out.reshape(b, t, d)
```

```python
import math

import jax
import jax.numpy as jnp
from jax import lax
from jax.experimental import pallas as pl
from jax.experimental.pallas import tpu as pltpu

F32 = jnp.float32
BF16 = jnp.bfloat16
HI = lax.Precision.HIGHEST

D_MODEL = 1024
HEAD_DIM = 64
RWKV_WIDTH = 512
RWKV_HEADS = 8
D_DECAY_LORA = 64
D_AAA_LORA = 64
D_GATE_LORA = 128
RWKV_COLS = 3 * RWKV_WIDTH + D_DECAY_LORA + D_AAA_LORA + D_GATE_LORA
NSA_WIDTH = 512
NSA_Q_HEADS = 8
NSA_KV_HEADS = 2
NSA_HPG = 4
NSA_KV_WIDTH = 128
N_BRANCH = 3
CMP_BLOCK = 32
CMP_STRIDE = 16
CMP_HIDDEN = 256
SEL_BLOCK = 64
SEL_TOPK = 16
WINDOW = 512
Q_BLOCK = 64
ROPE_THETA = 10000.0
D_FF = 2816
RMS_EPS = 1e-6
LNX_EPS = 64e-5
FORCE_BONUS = 1e4
NEG_INF = -1e30

LANES = 128
CHUNK = 64
Q_PAD_COLS = NSA_Q_HEADS * LANES
PROJ_COLS = RWKV_COLS + Q_PAD_COLS + 6 * NSA_KV_WIDTH + LANES
VMEM_LIMIT = 56 * 1024 * 1024


def _dot(a, b, prec=None):
    return jnp.dot(a, b, preferred_element_type=F32, precision=prec)


def _dot_nt(a, b, prec=None):
    return lax.dot_general(a, b, (((1,), (1,)), ((), ())), preferred_element_type=F32, precision=prec)


def _dot_tn(a, b, prec=None):
    return lax.dot_general(a, b, (((0,), (0,)), ((), ())), preferred_element_type=F32, precision=prec)


def _split(a):
    hi = a.astype(BF16)
    return hi, (a - hi.astype(F32)).astype(BF16)


def _mm(a, b, mode, dot=_dot):
    if mode == "hi":
        return dot(a, b, HI)
    if mode == "b1":
        return dot(a.astype(BF16), b.astype(BF16))
    if mode == "xa":
        bh, bl = _split(b)
        ab = a.astype(BF16)
        return dot(ab, bh) + dot(ab, bl)
    if mode == "xb":
        ah, al = _split(a)
        bb = b.astype(BF16)
        return dot(ah, bb) + dot(al, bb)
    ah, al = _split(a)
    bh, bl = _split(b)
    return dot(ah, bh) + (dot(ah, bl) + dot(al, bh))


def _params(*sem):
    return pltpu.CompilerParams(dimension_semantics=sem, vmem_limit_bytes=VMEM_LIMIT)


def _proj_kernel(x_ref, n1_ref, w_ref, pos_ref, invf_ref, sign_ref,
                 yr_ref, q_ref, kc_ref, vc_ref, ks_ref, vs_ref, kw_ref, vw_ref, gl_ref, cos_scr, sin_scr, kv_scr):
    x = x_ref[...]
    ms = jnp.mean(x * x, axis=-1, keepdims=True)
    xn = (x * lax.rsqrt(ms + RMS_EPS) * n1_ref[...]).astype(BF16)
    ang = pos_ref[...].astype(F32) * invf_ref[...]
    cos_scr[...] = jnp.cos(ang)
    sin_scr[...] = jnp.sin(ang) * sign_ref[...]
    lane = lax.broadcasted_iota(jnp.int32, ang.shape, 1)
    first_half = (lane % HEAD_DIM) < (HEAD_DIM // 2)

    def rope(v):
        rot = jnp.where(first_half, pltpu.roll(v, LANES - HEAD_DIM // 2, 1), pltpu.roll(v, HEAD_DIM // 2, 1))
        return v * cos_scr[...] + rot * sin_scr[...]

    def cols(c0, width):
        return _dot(xn, w_ref[:, c0:c0 + width])

    for j in range(RWKV_COLS // 256):
        yr_ref[:, j * 256:(j + 1) * 256] = cols(j * 256, 256)
    scale = HEAD_DIM ** -0.5 * math.log2(math.e)
    for hq in range(0, NSA_Q_HEADS, 2):
        q2 = cols(RWKV_COLS + hq * LANES, 2 * LANES)
        for e in range(2):
            q = rope(q2[:, e * LANES:(e + 1) * LANES]) * scale
            q_ref[:, (hq + e) * LANES:(hq + e + 1) * LANES] = q.astype(BF16)
    c0 = RWKV_COLS + Q_PAD_COLS
    kv = cols(c0, 2 * LANES)
    kv_scr[0] = rope(kv[:, :LANES])
    kv_scr[1] = kv[:, LANES:]
    zrows = kc_ref.shape[0]
    for l in range(CMP_STRIDE):
        kc_ref[:, l * LANES:(l + 1) * LANES] = kv_scr[0, pl.ds(l, zrows, stride=CMP_STRIDE), :]
        vc_ref[:, l * LANES:(l + 1) * LANES] = kv_scr[1, pl.ds(l, zrows, stride=CMP_STRIDE), :]
    kv = cols(c0 + 2 * LANES, 2 * LANES)
    ks_ref[...] = rope(kv[:, :LANES]).astype(BF16)
    vs_ref[...] = kv[:, LANES:].astype(BF16)
    kv = cols(c0 + 4 * LANES, 2 * LANES)
    kw_ref[...] = rope(kv[:, :LANES]).astype(BF16)
    vw_ref[...] = kv[:, LANES:].astype(BF16)
    gl_ref[...] = cols(c0 + 6 * LANES, LANES)


def _pack_w_in(w_in):
    w_r = w_in[:, :RWKV_COLS]
    w_q = w_in[:, RWKV_COLS:RWKV_COLS + NSA_WIDTH].reshape(D_MODEL, NSA_KV_HEADS, NSA_HPG, HEAD_DIM)
    zero = jnp.zeros_like(w_q)
    w_q = jnp.stack([jnp.where(jnp.arange(NSA_KV_HEADS)[None, :, None, None] == gg, w_q, zero)
                     for gg in range(NSA_KV_HEADS)], axis=3)
    w_q = w_q.reshape(D_MODEL, Q_PAD_COLS)
    c0 = RWKV_COLS + NSA_WIDTH
    w_kv = w_in[:, c0:c0 + 6 * NSA_KV_WIDTH]
    w_g = w_in[:, c0 + 6 * NSA_KV_WIDTH:]
    w_g = jnp.pad(w_g, ((0, 0), (0, LANES - w_g.shape[1])))
    return jnp.concatenate([w_r, w_q, w_kv, w_g], axis=1).astype(BF16)


def _in_proj(x2, norm1_w, w_packed, positions):
    n = x2.shape[0]
    tm = 512
    half = HEAD_DIM // 2
    inv_freq = ROPE_THETA ** (-jnp.arange(half, dtype=F32) / half)
    invf = jnp.tile(inv_freq, LANES // half).reshape(1, LANES)
    lane = jnp.arange(LANES)
    sign = jnp.where((lane % HEAD_DIM) < half, -1.0, 1.0).astype(F32).reshape(1, LANES)
    row = lambda width: pl.BlockSpec((tm, width), lambda i: (i, 0))
    const = lambda shape: pl.BlockSpec(shape, lambda i: (0, 0))
    zw = CMP_STRIDE * LANES
    zrow = pl.BlockSpec((tm // CMP_STRIDE, zw), lambda i: (i, 0))
    out_shape = [
        jax.ShapeDtypeStruct((n, RWKV_COLS), F32),
        jax.ShapeDtypeStruct((n, Q_PAD_COLS), BF16),
        jax.ShapeDtypeStruct((n // CMP_STRIDE, zw), F32),
        jax.ShapeDtypeStruct((n // CMP_STRIDE, zw), F32),
        jax.ShapeDtypeStruct((n, LANES), BF16),
        jax.ShapeDtypeStruct((n, LANES), BF16),
        jax.ShapeDtypeStruct((n, LANES), BF16),
        jax.ShapeDtypeStruct((n, LANES), BF16),
        jax.ShapeDtypeStruct((n, LANES), F32),
    ]
    out_specs = [row(RWKV_COLS), row(Q_PAD_COLS), zrow, zrow] + [row(LANES)] * 5
    return pl.pallas_call(
        _proj_kernel,
        grid=(n // tm,),
        in_specs=[row(D_MODEL), const((1, D_MODEL)), const((D_MODEL, PROJ_COLS)), row(1),
                  const((1, LANES)), const((1, LANES))],
        out_specs=out_specs,
        out_shape=out_shape,
        scratch_shapes=[pltpu.VMEM((tm, LANES), F32), pltpu.VMEM((tm, LANES), F32),
                        pltpu.VMEM((2, tm, LANES), F32)],
        compiler_params=_params("parallel"),
        name="in_proj",
    )(x2, norm1_w.reshape(1, D_MODEL), w_packed, positions.reshape(n, 1), invf, sign)


RWKV_TT = 256
P_LORA = "b3"
P_GATE = "b1"
P_HS = "b1"
P_CUM = "xa"


def _rwkv_kernel(y_ref, mu_ref, w0_ref, wl_ref, a0_ref, gup_ref, kk_ref, ka_ref, rk_ref,
                 lw_ref, lb_ref, hs_ref, tri_ref, o_ref, prev_ref, s_ref, o_scr):
    i = pl.program_id(1)

    @pl.when(i == 0)
    def _():
        prev_ref[...] = jnp.zeros_like(prev_ref)
        s_ref[...] = jnp.zeros_like(s_ref)

    tt = y_ref.shape[0]
    y = y_ref[...]
    row = lax.broadcasted_iota(jnp.int32, (tt, 1), 0)
    y_prev = jnp.where(row == 0, prev_ref[...], pltpu.roll(y, 1, 0))
    prev_ref[...] = y[tt - 1:tt, :]
    ys = y + (y_prev - y) * mu_ref[...]

    w_ = RWKV_WIDTH
    r = ys[:, 0:w_]
    k = ys[:, w_:2 * w_]
    v = ys[:, 2 * w_:3 * w_]
    z = ys[:, 3 * w_:3 * w_ + LANES]
    gd = ys[:, 3 * w_ + LANES:3 * w_ + 2 * LANES]
    lane = lax.broadcasted_iota(jnp.int32, z.shape, 1)
    zt = jnp.where(lane < D_DECAY_LORA, jnp.tanh(z), z)
    wa = _mm(zt, wl_ref[...], P_LORA)
    w_raw = w0_ref[...] + wa[:, :w_]
    a = jax.nn.sigmoid(a0_ref[...] + wa[:, w_:])
    logw = -jnp.exp(-jax.nn.softplus(-w_raw) - 0.5)
    g = _mm(jax.nn.sigmoid(gd), gup_ref[...], P_GATE)
    hs = hs_ref[...]

    def head_sum(x):
        hw = hs.shape[0]
        return jnp.concatenate([_mm(x[:, c0:c0 + hw], hs, P_HS) for c0 in range(0, w_, hw)], axis=1)

    kk = k * kk_ref[...]
    kk = kk * lax.rsqrt(jnp.maximum(head_sum(kk * kk), 1e-12))
    k2 = k * (1.0 + (a - 1.0) * ka_ref[...])
    alpha = -kk
    beta = kk * a

    cw = _mm(tri_ref[...], logw, P_CUM)
    e_in = jnp.exp(cw)
    e_ex = jnp.exp(cw - logw)
    e_neg = jnp.exp(-cw)
    a_t = alpha * e_ex
    r_t = r * e_in
    b_t = beta * e_neg
    k_t = k2 * e_neg

    m0 = lax.broadcasted_iota(jnp.int32, (CHUNK, LANES), 1) < HEAD_DIM

    def bd(xc):
        return jnp.concatenate([jnp.where(m0, xc, 0.0), jnp.where(m0, 0.0, xc)], axis=0)

    n2 = 2 * CHUNK
    ri = lax.broadcasted_iota(jnp.int32, (n2, n2), 0) % CHUNK
    ci = lax.broadcasted_iota(jnp.int32, (n2, n2), 1) % CHUNK
    strict = ri > ci
    incl = ri >= ci
    eye = jnp.where(lax.broadcasted_iota(jnp.int32, (n2, n2), 0) == lax.broadcasted_iota(jnp.int32, (n2, n2), 1),
                    1.0, 0.0)

    units = [(c, p) for c in range(tt // CHUNK) for p in range(w_ // LANES)]
    w_c = {}
    st = {}
    for c in range(tt // CHUNK):
        rs = slice(c * CHUNK, (c + 1) * CHUNK)
        cw_last = cw[c * CHUNK + CHUNK - 1:c * CHUNK + CHUNK, :]
        e_tot = jnp.exp(cw_last - cw[rs, :])
        w_c[c] = jnp.exp(cw_last)
        for p in range(w_ // LANES):
            ls = slice(p * LANES, (p + 1) * LANES)
            a_bd = bd(a_t[rs, ls]).astype(BF16)
            r_bd = bd(r_t[rs, ls])
            b_bd = bd(b_t[rs, ls]).astype(BF16)
            k_bd = bd(k_t[rs, ls]).astype(BF16)
            gm = _dot_nt(jnp.concatenate([a_bd, r_bd.astype(BF16)], axis=0), jnp.concatenate([b_bd, k_bd], axis=0))
            st[c, p] = dict(
                a_bd=a_bd, r_bd=r_bd,
                v_bd=bd(v[rs, ls]).astype(BF16),
                bh_bd=bd(beta[rs, ls] * e_tot[:, ls]).astype(BF16),
                kh_bd=bd(k2[rs, ls] * e_tot[:, ls]).astype(BF16),
                l_ab=jnp.where(strict, gm[0:n2, 0:n2], 0.0),
                l_ak=jnp.where(strict, gm[0:n2, n2:], 0.0).astype(BF16),
                m_rb=jnp.where(incl, gm[n2:, 0:n2], 0.0).astype(BF16),
                m_rk=jnp.where(incl, gm[n2:, n2:], 0.0).astype(BF16))

    pw, tm_ = {}, {}
    for u_ in units:
        l_ab = st[u_]["l_ab"]
        lb = l_ab.astype(BF16)
        pw[u_] = _dot(lb, lb)
        tm_[u_] = eye + l_ab
    for level in range(5):
        for u_ in units:
            pb = pw[u_].astype(BF16)
            if level < 4:
                y = _dot(pb, jnp.concatenate([pb, tm_[u_].astype(BF16)], axis=1))
                pw[u_] = y[:, 0:n2]
                tm_[u_] = tm_[u_] + y[:, n2:]
            else:
                tm_[u_] = tm_[u_] + _dot(pb, tm_[u_].astype(BF16))

    zero_bd = jnp.zeros((n2, n2), BF16)
    lv = {u_: _dot(st[u_]["l_ak"], st[u_]["v_bd"]).astype(BF16) for u_ in units}
    au = {u_: _dot(tm_[u_].astype(BF16), jnp.concatenate([st[u_]["a_bd"], lv[u_]], axis=1)).astype(BF16)
          for u_ in units}
    for u_ in units:
        d_ = st[u_]
        mo = _dot(jnp.concatenate([d_["m_rb"], d_["m_rk"]], axis=1),
                  jnp.concatenate([au[u_], jnp.concatenate([zero_bd, d_["v_bd"]], axis=1)], axis=0))
        d_["r_hat"] = (d_["r_bd"] + mo[:, 0:n2]).astype(BF16)
        d_["o0"] = mo[:, n2:]
    for u_ in units:
        st[u_]["g"] = _dot_tn(au[u_][:, 0:n2], st[u_]["bh_bd"]).astype(BF16)
    for u_ in units:
        d_ = st[u_]
        d_["n"] = _dot_tn(jnp.concatenate([au[u_][:, n2:], d_["v_bd"]], axis=0),
                          jnp.concatenate([d_["bh_bd"], d_["kh_bd"]], axis=0))

    for c, p in units:
        d_ = st[c, p]
        rs = slice(c * CHUNK, (c + 1) * CHUNK)
        ls = slice(p * LANES, (p + 1) * LANES)
        s_old = s_ref[p]
        sb = s_old.astype(BF16)
        o_bd = _dot_nt(d_["r_hat"], sb) + d_["o0"]
        o_scr[rs, ls] = o_bd[0:CHUNK] + o_bd[CHUNK:]
        s_ref[p] = s_old * w_c[c][:, ls] + _dot(sb, d_["g"]) + d_["n"]

    o = o_scr[...]
    inv_n = 1.0 / HEAD_DIM
    mean = head_sum(o) * inv_n
    d = o - mean
    var = head_sum(d * d) * inv_n
    on = d * lax.rsqrt(var + LNX_EPS) * lw_ref[...] + lb_ref[...]
    bonus = head_sum(r * k2 * rk_ref[...]) * v
    o_ref[...] = ((on + bonus) * g).astype(o_ref.dtype)


def _rwkv(yr, b, t, mu, w0, w_lora_up, a0, a_lora_up, g_lora_up, k_k, k_a, r_k, lnx_w, lnx_b):
    tt = RWKV_TT
    nt = t // tt
    w_ = RWKV_WIDTH
    wl = jnp.zeros((LANES, 2 * w_), F32)
    wl = wl.at[:D_DECAY_LORA, :w_].set(w_lora_up).at[D_DECAY_LORA:, w_:].set(a_lora_up)
    hw = 2 * LANES
    head = jnp.arange(hw) // HEAD_DIM
    hs = (head[:, None] == head[None, :]).astype(F32)
    ti = jnp.arange(tt)
    tri = ((ti[:, None] // CHUNK == ti[None, :] // CHUNK) & (ti[:, None] >= ti[None, :])).astype(F32)
    vec = lambda a_, width: a_.reshape(1, width)
    const = lambda shape: pl.BlockSpec(shape, lambda bi, i: (0, 0))
    return pl.pallas_call(
        _rwkv_kernel,
        grid=(b, nt),
        in_specs=[pl.BlockSpec((tt, RWKV_COLS), lambda bi, i: (bi * nt + i, 0)),
                  const((1, RWKV_COLS)), const((1, w_)), const((LANES, 2 * w_)), const((1, w_)),
                  const((D_GATE_LORA, w_)), const((1, w_)), const((1, w_)), const((1, w_)),
                  const((1, w_)), const((1, w_)), const((hw, hw)), const((tt, tt))],
        out_specs=pl.BlockSpec((tt, w_), lambda bi, i: (bi * nt + i, 0)),
        out_shape=jax.ShapeDtypeStruct((b * t, w_), BF16),
        scratch_shapes=[pltpu.VMEM((1, RWKV_COLS), F32),
                        pltpu.VMEM((w_ // LANES, LANES, LANES), F32),
                        pltpu.VMEM((tt, w_), F32)],
        compiler_params=_params("parallel", "arbitrary"),
        name="rwkv7",
    )(yr, vec(mu, RWKV_COLS), vec(w0, w_), wl, vec(a0, w_), g_lora_up, vec(k_k, w_), vec(k_a, w_),
      vec(r_k, w_), vec(lnx_w, w_), vec(lnx_b, w_), hs, tri)


def _compress_kernel(zk_ref, zv_ref, pek_ref, pev_ref, wk1_ref, wv1_ref, wk2_ref, wv2_ref, kc_ref, vc_ref):
    def one(z_ref, pe_ref, w1_ref, w2_ref, out_ref):
        z = z_ref[0]
        nrow = z.shape[0]
        za = z + pe_ref[0:1, :]
        zb = z + pe_ref[1:2, :]
        acc = jnp.zeros((nrow, LANES), F32)
        for gi in range(NSA_KV_HEADS):
            ha = _mm(za, w1_ref[gi, 0], "b3")
            hb = _mm(zb, w1_ref[gi, 1], "b3")
            hid = ha + pltpu.roll(hb, nrow - 1, 0)
            acc = acc + _mm(jax.nn.gelu(hid), w2_ref[gi], "b3")
        out_ref[0] = acc

    one(zk_ref, pek_ref, wk1_ref, wk2_ref, kc_ref)
    one(zv_ref, pev_ref, wv1_ref, wv2_ref, vc_ref)


def _compress_weights(pos, w1, w2):
    half = CMP_BLOCK // 2
    w1r = w1.reshape(2, half, HEAD_DIM, CMP_HIDDEN)
    w1g = jnp.zeros((NSA_KV_HEADS, 2, half, NSA_KV_HEADS, HEAD_DIM, CMP_HIDDEN), F32)
    for gi in range(NSA_KV_HEADS):
        w1g = w1g.at[gi, :, :, gi].set(w1r)
    w1g = w1g.reshape(NSA_KV_HEADS, 2, half * NSA_KV_WIDTH, CMP_HIDDEN)
    w2g = jnp.zeros((NSA_KV_HEADS, CMP_HIDDEN, NSA_KV_HEADS, HEAD_DIM), F32)
    for gi in range(NSA_KV_HEADS):
        w2g = w2g.at[gi, :, gi].set(w2)
    w2g = w2g.reshape(NSA_KV_HEADS, CMP_HIDDEN, NSA_KV_WIDTH)
    pe = jnp.broadcast_to(pos.reshape(2, half, 1, HEAD_DIM), (2, half, NSA_KV_HEADS, HEAD_DIM))
    pe = pe.reshape(2, half * NSA_KV_WIDTH)
    return pe, w1g, w2g


def _compress(kc, vc, b, t, cmp_pos_k, cmp_pos_v, k_w1, k_w2, v_w1, v_w2):
    nrow = t // CMP_STRIDE
    zw = CMP_STRIDE * NSA_KV_WIDTH
    pek, wk1, wk2 = _compress_weights(cmp_pos_k, k_w1, k_w2)
    pev, wv1, wv2 = _compress_weights(cmp_pos_v, v_w1, v_w2)
    zspec = pl.BlockSpec((1, nrow, zw), lambda bi: (bi, 0, 0))
    ospec = pl.BlockSpec((1, nrow, LANES), lambda bi: (bi, 0, 0))
    c2 = lambda shape: pl.BlockSpec(shape, lambda bi: (0,) * len(shape))
    return pl.pallas_call(
        _compress_kernel,
        grid=(b,),
        in_specs=[zspec, zspec, c2((2, zw)), c2((2, zw)), c2(wk1.shape), c2(wv1.shape), c2(wk2.shape), c2(wv2.shape)],
        out_specs=[ospec, ospec],
        out_shape=[jax.ShapeDtypeStruct((b, nrow, LANES), F32)] * 2,
        compiler_params=_params("parallel"),
        name="nsa_compress",
    )(kc.reshape(b, nrow, zw), vc.reshape(b, nrow, zw), pek, pev, wk1, wv1, wk2, wv2)


SEL_TILE = 256
SEL_UNROLL = 8
WIN_TILE = 128
NSA_QBLK = 2


def _n_win_tiles():
    tq = NSA_QBLK * Q_BLOCK
    span = WINDOW + tq + WIN_TILE - math.gcd(WIN_TILE, tq)
    return -(-span // WIN_TILE)


def _nsa_kernel(q_ref, kcmp_ref, vcmp_ref, ks_ref, vs_ref, kw_ref, vw_ref, gl_ref, ovt_ref, oh_ref,
                o_ref, m_scr, l_scr, acc_scr, sc_scr, imp_scr, cnt_scr, sel_scr, g_scr):
    s2 = pl.program_id(1)
    qb = Q_BLOCK
    nq = NSA_QBLK
    n_head = NSA_Q_HEADS
    rows = nq * n_head * qb
    n_cmp = kcmp_ref.shape[1]
    n_sb = ovt_ref.shape[0]
    s_of = [s2 * nq + qi for qi in range(nq)]
    s_last = s_of[-1]

    def tok(qi, shape):
        return s_of[qi] * qb + lax.broadcasted_iota(jnp.int32, shape, 0)

    def per_head(x64s):
        return jnp.concatenate([x for x in x64s for _ in range(n_head)], axis=0)

    def per_group_head(x):
        parts = []
        for qi in range(nq):
            for gi in range(NSA_KV_HEADS):
                r0 = (qi * NSA_KV_HEADS + gi) * qb
                parts += [x[r0:r0 + qb]] * NSA_HPG
        return jnp.concatenate(parts, axis=0)

    lane64 = lax.broadcasted_iota(jnp.int32, (qb, LANES), 1)
    sig = jax.nn.sigmoid(gl_ref[...])
    for col in range(n_head * N_BRANCH):
        g_scr[col] = jnp.broadcast_to(sig[:, col:col + 1], (nq * qb, LANES))

    qq = jnp.concatenate([q_ref[qi * qb:(qi + 1) * qb, hq * LANES:(hq + 1) * LANES]
                          for qi in range(nq) for hq in range(n_head)], axis=0)

    n_win = _n_win_tiles()
    first = (s2 * nq * qb - WINDOW) // WIN_TILE
    lane_w = lax.broadcasted_iota(jnp.int32, (qb, WIN_TILE), 1)
    win_sc, win_k0 = [], []
    mxw = jnp.full((rows, LANES), NEG_INF, F32)
    for j in range(n_win):
        kt = jnp.maximum(first + j, 0)
        k0 = pl.multiple_of(kt * WIN_TILE, WIN_TILE)
        kp = k0 + lane_w + jnp.where(first + j >= 0, 0, 1 << 24)
        bias = per_head([jnp.where((kp <= tok(qi, kp.shape)) & (kp > tok(qi, kp.shape) - WINDOW), 0.0, NEG_INF)
                         for qi in range(nq)])
        sc_ = _dot_nt(qq, kw_ref[0, pl.ds(k0, WIN_TILE), :]) + bias
        for c0 in range(0, WIN_TILE, LANES):
            mxw = jnp.maximum(mxw, sc_[:, c0:c0 + LANES])
        win_sc.append(sc_)
        win_k0.append(k0)
    m_w = jnp.max(mxw, axis=1, keepdims=True)
    l_w = jnp.zeros((rows, LANES), F32)
    acc_w = jnp.zeros((rows, LANES), F32)
    for j in range(n_win):
        p = jnp.exp2(win_sc[j] - m_w)
        for c0 in range(0, WIN_TILE, LANES):
            l_w = l_w + p[:, c0:c0 + LANES]
        acc_w = acc_w + _dot(p.astype(BF16), vw_ref[0, pl.ds(win_k0[j], WIN_TILE), :])
    o_win = acc_w / jnp.sum(l_w, axis=1, keepdims=True)

    cmp_i = lax.broadcasted_iota(jnp.int32, (qb, n_cmp), 1)
    cbias = per_head([jnp.where((cmp_i * CMP_STRIDE + CMP_BLOCK - 1 <= tok(qi, cmp_i.shape)) & (cmp_i < n_cmp - 1),
                                0.0, NEG_INF) for qi in range(nq)])
    sc = _mm(qq, kcmp_ref[0], "xa", _dot_nt) + cbias
    mx = jnp.max(sc, axis=1, keepdims=True)
    e = jnp.exp2(sc - mx)
    den = jnp.sum(e, axis=1, keepdims=True)
    p_c = e * jnp.where(mx > 0.5 * NEG_INF, 1.0 / den, 0.0)
    o_cmp = _dot(p_c.astype(BF16), vcmp_ref[0].astype(BF16))

    pc_sums = []
    for qg in range(nq * NSA_KV_HEADS):
        r0 = qg * NSA_HPG * qb
        acc = p_c[r0:r0 + qb]
        for h in range(1, NSA_HPG):
            acc = acc + p_c[r0 + h * qb:r0 + (h + 1) * qb]
        pc_sums.append(acc)
    pcs = jnp.concatenate(pc_sums, axis=0)
    imp = _dot_nt(ovt_ref[...], pcs, HI)
    nl = nq * NSA_KV_HEADS * qb
    blk = lax.broadcasted_iota(jnp.int32, (n_sb, nl), 0)
    s_lane = s2 * nq + lax.broadcasted_iota(jnp.int32, (n_sb, nl), 1) // (NSA_KV_HEADS * qb)
    forced = (blk == 0) | (blk == s_lane) | (blk == s_lane - 1)
    imp = jnp.where(blk <= s_lane, imp + jnp.where(forced, FORCE_BONUS, 0.0), NEG_INF)
    n_sel = min(SEL_TOPK, n_sb)
    sel_scr[...] = jnp.where(blk <= s_lane, 1.0, 0.0)

    @pl.when(s_last >= n_sel)
    def _():
        imp_scr[...] = imp
        cnt_scr[...] = jnp.zeros(cnt_scr.shape, F32)
        sub = lax.broadcasted_iota(jnp.int32, (8, nl), 0)
        for ig in range(n_sb // 8):
            @pl.when(ig * 8 <= s_last)
            def _(ig=ig):
                impv = imp_scr[...]
                cnt = [cnt_scr[rg * 8:(rg + 1) * 8, :] for rg in range(n_sb // 8)]
                for i2 in range(ig * 8, ig * 8 + 8):
                    vi = impv[i2:i2 + 1, :]
                    for rg in range(n_sb // 8):
                        vj = impv[rg * 8:(rg + 1) * 8, :]
                        if rg < ig:
                            one = jnp.where(vi > vj, 1.0, 0.0)
                        elif rg > ig:
                            one = jnp.where(vi >= vj, 1.0, 0.0)
                        else:
                            one = jnp.where(sub > i2 - ig * 8, jnp.where(vi >= vj, 1.0, 0.0),
                                            jnp.where(vi > vj, 1.0, 0.0))
                        cnt[rg] = cnt[rg] + one
                for rg in range(n_sb // 8):
                    cnt_scr[rg * 8:(rg + 1) * 8, :] = cnt[rg]
        sel_scr[...] = jnp.where((cnt_scr[...] < n_sel) & (blk <= s_lane), 1.0, 0.0)

    sel_t = sel_scr[...]
    if n_sb < LANES:
        sel_t = jnp.concatenate([sel_t, jnp.zeros((LANES - n_sb, nl), F32)], axis=0)
    unsel = ((sel_t.T - 1.0) * -NEG_INF).astype(BF16)
    q_ext = jnp.concatenate([qq, per_group_head(unsel)], axis=1)

    lane_s = lax.broadcasted_iota(jnp.int32, (qb, SEL_TILE), 1)
    n_tiles = ((s_last + 1) * qb + SEL_TILE - 1) // SEL_TILE
    n_full = n_tiles // SEL_UNROLL
    rem = n_tiles % SEL_UNROLL
    rem_blocks = [w for w in (SEL_UNROLL >> i for i in range(1, SEL_UNROLL.bit_length())) if w > 0]

    def rem_start(w):
        return n_full * SEL_UNROLL + (rem & ~(2 * w - 1))

    def tile_scores(j, mx_):
        k0 = pl.multiple_of(j * SEL_TILE, SEL_TILE)
        causal = per_head([jnp.where(k0 + lane_s <= tok(qi, lane_s.shape), 0.0, NEG_INF) for qi in range(nq)])
        k_ext = jnp.concatenate([ks_ref[0, pl.ds(k0, SEL_TILE), :], oh_ref[j]], axis=1)
        sc_ = _dot_nt(q_ext, k_ext) + causal
        sc_scr[j] = sc_
        for c0 in range(0, SEL_TILE, LANES):
            mx_ = jnp.maximum(mx_, sc_[:, c0:c0 + LANES])
        return mx_

    def pass1(it, mx_):
        for u in range(SEL_UNROLL):
            mx_ = tile_scores(it * SEL_UNROLL + u, mx_)
        return mx_

    m_scr[...] = lax.fori_loop(0, n_full, pass1, jnp.full((rows, LANES), NEG_INF, F32))
    for w in rem_blocks:
        @pl.when((rem & w) != 0)
        def _(w=w):
            mx_ = m_scr[...]
            for u in range(w):
                mx_ = tile_scores(rem_start(w) + u, mx_)
            m_scr[...] = mx_
    m_scr[...] = jnp.broadcast_to(jnp.max(m_scr[...], axis=1, keepdims=True), m_scr.shape)
    l_scr[...] = jnp.zeros(l_scr.shape, F32)
    acc_scr[...] = jnp.zeros(acc_scr.shape, F32)

    def tile_pv(j):
        k0 = pl.multiple_of(j * SEL_TILE, SEL_TILE)
        m_b = jnp.concatenate([m_scr[...]] * (SEL_TILE // LANES), axis=1)
        p = jnp.exp2(sc_scr[j] - m_b)
        l_new = l_scr[...]
        for c0 in range(0, SEL_TILE, LANES):
            l_new = l_new + p[:, c0:c0 + LANES]
        l_scr[...] = l_new
        acc_scr[...] += _dot(p.astype(BF16), vs_ref[0, pl.ds(k0, SEL_TILE), :])

    def pass2(it, c):
        for u in range(SEL_UNROLL):
            tile_pv(it * SEL_UNROLL + u)
        return c

    lax.fori_loop(0, n_full, pass2, 0)
    for w in rem_blocks:
        @pl.when((rem & w) != 0)
        def _(w=w):
            for u in range(w):
                tile_pv(rem_start(w) + u)
    o_sel = acc_scr[...] / jnp.sum(l_scr[...], axis=1, keepdims=True)

    low_half = lane64 < HEAD_DIM
    for qi in range(nq):
        ts = slice(qi * qb, (qi + 1) * qb)
        for h in range(NSA_HPG):
            halves = []
            for gi in range(NSA_KV_HEADS):
                hq = gi * NSA_HPG + h
                r0 = (qi * n_head + hq) * qb
                hr = slice(r0, r0 + qb)
                col = hq * N_BRANCH
                halves.append(g_scr[col, ts] * o_cmp[hr] + g_scr[col + 1, ts] * o_sel[hr] + g_scr[col + 2, ts] * o_win[hr])
            o_ref[ts, h * LANES:(h + 1) * LANES] = jnp.where(low_half, halves[0], halves[1]).astype(o_ref.dtype)


def _nsa_attention(q, k_cmp, v_cmp, ks, vs, kw, vw, gl, b, t):
    tq = NSA_QBLK * Q_BLOCK
    ns = t // tq
    n_sb = t // SEL_BLOCK
    n_cmp = t // CMP_STRIDE
    n_kt = t // SEL_TILE
    cmp_start = jnp.arange(n_cmp) * CMP_STRIDE
    sb = jnp.arange(n_sb)
    ovt = ((cmp_start[None, :] < (sb[:, None] + 1) * SEL_BLOCK)
           & (cmp_start[None, :] + CMP_BLOCK > sb[:, None] * SEL_BLOCK)
           & (jnp.arange(n_cmp)[None, :] < n_cmp - 1)).astype(F32)
    key_blk = (jnp.arange(n_kt)[:, None] * SEL_TILE + jnp.arange(SEL_TILE)[None, :]) // SEL_BLOCK
    onehot = (key_blk[:, :, None] == jnp.arange(LANES)[None, None, :]).astype(BF16)
    rows = NSA_QBLK * NSA_Q_HEADS * Q_BLOCK
    nl = NSA_QBLK * NSA_KV_HEADS * Q_BLOCK
    qspec = pl.BlockSpec((tq, Q_PAD_COLS), lambda bi, si: (bi * ns + si, 0))
    seq = lambda: pl.BlockSpec((1, t, LANES), lambda bi, si: (bi, 0, 0))
    cmp_spec = lambda: pl.BlockSpec((1, n_cmp, LANES), lambda bi, si: (bi, 0, 0))
    k3 = lambda a_: a_.reshape(b, t, LANES)
    return pl.pallas_call(
        _nsa_kernel,
        grid=(b, ns),
        in_specs=[qspec, cmp_spec(), cmp_spec(), seq(), seq(), seq(), seq(),
                  pl.BlockSpec((tq, LANES), lambda bi, si: (bi * ns + si, 0)),
                  pl.BlockSpec((n_sb, n_cmp), lambda bi, si: (0, 0)),
                  pl.BlockSpec((n_kt, SEL_TILE, LANES), lambda bi, si: (0, 0, 0))],
        out_specs=pl.BlockSpec((tq, NSA_WIDTH), lambda bi, si: (bi * ns + si, 0)),
        out_shape=jax.ShapeDtypeStruct((b * t, NSA_WIDTH), BF16),
        scratch_shapes=[pltpu.VMEM((rows, LANES), F32),
                        pltpu.VMEM((rows, LANES), F32),
                        pltpu.VMEM((rows, LANES), F32),
                        pltpu.VMEM((n_kt, rows, SEL_TILE), F32),
                        pltpu.VMEM((n_sb, nl), F32),
                        pltpu.VMEM((n_sb, nl), F32),
                        pltpu.VMEM((n_sb, nl), F32),
                        pltpu.VMEM((NSA_Q_HEADS * N_BRANCH, tq, LANES), F32)],
        compiler_params=_params("parallel", "arbitrary"),
        name="nsa_attention",
    )(q, k_cmp, v_cmp, k3(ks), k3(vs), k3(kw), k3(vw), gl, ovt, onehot)


FF_TILE = 256


def _ffn_kernel(x_ref, orw_ref, ons_ref, wor_ref, won_ref, n2_ref, w1_ref, w3_ref, w2_ref, nf_ref,
                out_ref, h_scr, u_scr, acc_scr):
    j = pl.program_id(1)

    @pl.when(j == 0)
    def _():
        h = x_ref[...] + _dot(orw_ref[...], wor_ref[...]) + _dot(ons_ref[...], won_ref[...])
        h_scr[...] = h
        ms = jnp.mean(h * h, axis=-1, keepdims=True)
        u_scr[...] = (h * lax.rsqrt(ms + RMS_EPS) * n2_ref[...]).astype(BF16)
        acc_scr[...] = jnp.zeros_like(acc_scr)

    u = u_scr[...]
    gate = _dot(u, w1_ref[...])
    up = _dot(u, w3_ref[...])
    act = (jax.nn.silu(gate) * up).astype(BF16)
    acc_scr[...] += _dot(act, w2_ref[...])

    @pl.when(j == pl.num_programs(1) - 1)
    def _():
        h = h_scr[...] + acc_scr[...]
        ms = jnp.mean(h * h, axis=-1, keepdims=True)
        out_ref[...] = h * lax.rsqrt(ms + RMS_EPS) * nf_ref[...]


def _out_ffn(x2, o_rwkv, o_nsa, w_out, norm2_w, ffn_w1, ffn_w3, ffn_w2, final_norm_w):
    n = x2.shape[0]
    tm = 1024
    w_or = w_out[:RWKV_WIDTH].astype(BF16)
    w_n = w_out[RWKV_WIDTH:].reshape(NSA_KV_HEADS, NSA_HPG, HEAD_DIM, D_MODEL)
    w_on = w_n.transpose(1, 0, 2, 3).reshape(NSA_WIDTH, D_MODEL).astype(BF16)
    row = lambda width: pl.BlockSpec((tm, width), lambda i, j: (i, 0))
    const = lambda shape: pl.BlockSpec(shape, lambda i, j: (0, 0))
    return pl.pallas_call(
        _ffn_kernel,
        grid=(n // tm, D_FF // FF_TILE),
        in_specs=[row(D_MODEL), row(RWKV_WIDTH), row(NSA_WIDTH),
                  const((RWKV_WIDTH, D_MODEL)), const((NSA_WIDTH, D_MODEL)), const((1, D_MODEL)),
                  pl.BlockSpec((D_MODEL, FF_TILE), lambda i, j: (0, j)),
                  pl.BlockSpec((D_MODEL, FF_TILE), lambda i, j: (0, j)),
                  pl.BlockSpec((FF_TILE, D_MODEL), lambda i, j: (j, 0)),
                  const((1, D_MODEL))],
        out_specs=row(D_MODEL),
        out_shape=jax.ShapeDtypeStruct((n, D_MODEL), F32),
        scratch_shapes=[pltpu.VMEM((tm, D_MODEL), F32), pltpu.VMEM((tm, D_MODEL), BF16),
                        pltpu.VMEM((tm, D_MODEL), F32)],
        compiler_params=_params("parallel", "arbitrary"),
        name="out_ffn",
    )(x2, o_rwkv, o_nsa, w_or, w_on, norm2_w.reshape(1, D_MODEL), ffn_w1.astype(BF16), ffn_w3.astype(BF16),
      ffn_w2.astype(BF16), final_norm_w.reshape(1, D_MODEL))


def kernel(x, positions, norm1_w, w_in, mu_rwkv, w0, w_lora_up, a0, a_lora_up, g_lora_up, k_k, k_a, r_k, lnx_w, lnx_b, cmp_pos_k, cmp_pos_v, cmp_k_w1, cmp_k_w2, cmp_v_w1, cmp_v_w2, w_out, norm2_w, ffn_w1, ffn_w3, ffn_w2, final_norm_w):
    b, t, d = x.shape
    assert d == D_MODEL and norm1_w.shape[0] == 1, "single-layer block with d_model 1024"
    assert t % RWKV_TT == 0 and t % SEL_TILE == 0 and t % (NSA_QBLK * Q_BLOCK) == 0
    x2 = x.reshape(b * t, d)
    yr, q, kc, vc, ks, vs, kw, vw, gl = _in_proj(x2, norm1_w[0], _pack_w_in(w_in[0]), positions)
    o_rwkv = _rwkv(yr, b, t, mu_rwkv[0], w0[0], w_lora_up[0], a0[0], a_lora_up[0], g_lora_up[0],
                   k_k[0], k_a[0], r_k[0], lnx_w[0], lnx_b[0])
    k_cmp, v_cmp = _compress(kc, vc, b, t, cmp_pos_k[0], cmp_pos_v[0], cmp_k_w1[0], cmp_k_w2[0],
                             cmp_v_w1[0], cmp_v_w2[0])
    o_nsa = _nsa_attention(q, k_cmp, v_cmp, ks, vs, kw, vw, gl, b, t)
    out = _out_ffn(x2, o_rwkv, o_nsa, w_out[0], norm2_w[0], ffn_w1[0], ffn_w3[0], ffn_w2[0], final_norm_w)
    return out.reshape(b, t, d)
```

```python
import math

import jax
import jax.numpy as jnp
from jax import lax
from jax.experimental import pallas as pl
from jax.experimental.pallas import tpu as pltpu

F32 = jnp.float32
BF16 = jnp.bfloat16
HI = lax.Precision.HIGHEST

D_MODEL = 1024
HEAD_DIM = 64
RWKV_WIDTH = 512
RWKV_HEADS = 8
D_DECAY_LORA = 64
D_AAA_LORA = 64
D_GATE_LORA = 128
RWKV_COLS = 3 * RWKV_WIDTH + D_DECAY_LORA + D_AAA_LORA + D_GATE_LORA
NSA_WIDTH = 512
NSA_Q_HEADS = 8
NSA_KV_HEADS = 2
NSA_HPG = 4
NSA_KV_WIDTH = 128
N_BRANCH = 3
CMP_BLOCK = 32
CMP_STRIDE = 16
CMP_HIDDEN = 256
SEL_BLOCK = 64
SEL_TOPK = 16
WINDOW = 512
Q_BLOCK = 64
ROPE_THETA = 10000.0
D_FF = 2816
RMS_EPS = 1e-6
LNX_EPS = 64e-5
FORCE_BONUS = 1e4
NEG_INF = -1e30

LANES = 128
CHUNK = 64
Q_PAD_COLS = NSA_Q_HEADS * LANES
PROJ_COLS = RWKV_COLS + Q_PAD_COLS + 6 * NSA_KV_WIDTH + LANES
VMEM_LIMIT = 56 * 1024 * 1024


def _dot(a, b, prec=None):
    return jnp.dot(a, b, preferred_element_type=F32, precision=prec)


def _dot_nt(a, b, prec=None):
    return lax.dot_general(a, b, (((1,), (1,)), ((), ())), preferred_element_type=F32, precision=prec)


def _dot_tn(a, b, prec=None):
    return lax.dot_general(a, b, (((0,), (0,)), ((), ())), preferred_element_type=F32, precision=prec)


def _split(a):
    hi = a.astype(BF16)
    return hi, (a - hi.astype(F32)).astype(BF16)


def _mm(a, b, mode, dot=_dot):
    if mode == "hi":
        return dot(a, b, HI)
    if mode == "b1":
        return dot(a.astype(BF16), b.astype(BF16))
    if mode == "xa":
        bh, bl = _split(b)
        ab = a.astype(BF16)
        return dot(ab, bh) + dot(ab, bl)
    if mode == "xb":
        ah, al = _split(a)
        bb = b.astype(BF16)
        return dot(ah, bb) + dot(al, bb)
    ah, al = _split(a)
    bh, bl = _split(b)
    return dot(ah, bh) + (dot(ah, bl) + dot(al, bh))


def _params(*sem):
    return pltpu.CompilerParams(dimension_semantics=sem, vmem_limit_bytes=VMEM_LIMIT)


def _proj_kernel(x_ref, n1_ref, w_ref, pos_ref, invf_ref, sign_ref,
                 yr_ref, q_ref, kc_ref, vc_ref, ks_ref, vs_ref, kw_ref, vw_ref, gl_ref, cos_scr, sin_scr, kv_scr):
    x = x_ref[...]
    ms = jnp.mean(x * x, axis=-1, keepdims=True)
    xn = (x * lax.rsqrt(ms + RMS_EPS) * n1_ref[...]).astype(BF16)
    ang = pos_ref[...].astype(F32) * invf_ref[...]
    cos_scr[...] = jnp.cos(ang)
    sin_scr[...] = jnp.sin(ang) * sign_ref[...]
    lane = lax.broadcasted_iota(jnp.int32, ang.shape, 1)
    first_half = (lane % HEAD_DIM) < (HEAD_DIM // 2)

    def rope(v):
        rot = jnp.where(first_half, pltpu.roll(v, LANES - HEAD_DIM // 2, 1), pltpu.roll(v, HEAD_DIM // 2, 1))
        return v * cos_scr[...] + rot * sin_scr[...]

    def cols(c0, width):
        return _dot(xn, w_ref[:, c0:c0 + width])

    for j in range(RWKV_COLS // 256):
        yr_ref[:, j * 256:(j + 1) * 256] = cols(j * 256, 256)
    scale = HEAD_DIM ** -0.5 * math.log2(math.e)
    for hq in range(0, NSA_Q_HEADS, 2):
        q2 = cols(RWKV_COLS + hq * LANES, 2 * LANES)
        for e in range(2):
            q = rope(q2[:, e * LANES:(e + 1) * LANES]) * scale
            q_ref[:, (hq + e) * LANES:(hq + e + 1) * LANES] = q.astype(BF16)
    c0 = RWKV_COLS + Q_PAD_COLS
    kv = cols(c0, 2 * LANES)
    kv_scr[0] = rope(kv[:, :LANES])
    kv_scr[1] = kv[:, LANES:]
    zrows = kc_ref.shape[0]
    for l in range(CMP_STRIDE):
        kc_ref[:, l * LANES:(l + 1) * LANES] = kv_scr[0, pl.ds(l, zrows, stride=CMP_STRIDE), :]
        vc_ref[:, l * LANES:(l + 1) * LANES] = kv_scr[1, pl.ds(l, zrows, stride=CMP_STRIDE), :]
    kv = cols(c0 + 2 * LANES, 2 * LANES)
    ks_ref[...] = rope(kv[:, :LANES]).astype(BF16)
    vs_ref[...] = kv[:, LANES:].astype(BF16)
    kv = cols(c0 + 4 * LANES, 2 * LANES)
    kw_ref[...] = rope(kv[:, :LANES]).astype(BF16)
    vw_ref[...] = kv[:, LANES:].astype(BF16)
    gl_ref[...] = cols(c0 + 6 * LANES, LANES)


def _pack_w_in(w_in):
    w_r = w_in[:, :RWKV_COLS]
    w_q = w_in[:, RWKV_COLS:RWKV_COLS + NSA_WIDTH].reshape(D_MODEL, NSA_KV_HEADS, NSA_HPG, HEAD_DIM)
    zero = jnp.zeros_like(w_q)
    w_q = jnp.stack([jnp.where(jnp.arange(NSA_KV_HEADS)[None, :, None, None] == gg, w_q, zero)
                     for gg in range(NSA_KV_HEADS)], axis=3)
    w_q = w_q.reshape(D_MODEL, Q_PAD_COLS)
    c0 = RWKV_COLS + NSA_WIDTH
    w_kv = w_in[:, c0:c0 + 6 * NSA_KV_WIDTH]
    w_g = w_in[:, c0 + 6 * NSA_KV_WIDTH:]
    w_g = jnp.pad(w_g, ((0, 0), (0, LANES - w_g.shape[1])))
    return jnp.concatenate([w_r, w_q, w_kv, w_g], axis=1).astype(BF16)


def _in_proj(x2, norm1_w, w_packed, positions):
    n = x2.shape[0]
    tm = 1024
    half = HEAD_DIM // 2
    inv_freq = ROPE_THETA ** (-jnp.arange(half, dtype=F32) / half)
    invf = jnp.tile(inv_freq, LANES // half).reshape(1, LANES)
    lane = jnp.arange(LANES)
    sign = jnp.where((lane % HEAD_DIM) < half, -1.0, 1.0).astype(F32).reshape(1, LANES)
    row = lambda width: pl.BlockSpec((tm, width), lambda i: (i, 0))
    const = lambda shape: pl.BlockSpec(shape, lambda i: (0, 0))
    zw = CMP_STRIDE * LANES
    zrow = pl.BlockSpec((tm // CMP_STRIDE, zw), lambda i: (i, 0))
    out_shape = [
        jax.ShapeDtypeStruct((n, RWKV_COLS), F32),
        jax.ShapeDtypeStruct((n, Q_PAD_COLS), BF16),
        jax.ShapeDtypeStruct((n // CMP_STRIDE, zw), F32),
        jax.ShapeDtypeStruct((n // CMP_STRIDE, zw), F32),
        jax.ShapeDtypeStruct((n, LANES), BF16),
        jax.ShapeDtypeStruct((n, LANES), BF16),
        jax.ShapeDtypeStruct((n, LANES), BF16),
        jax.ShapeDtypeStruct((n, LANES), BF16),
        jax.ShapeDtypeStruct((n, LANES), F32),
    ]
    out_specs = [row(RWKV_COLS), row(Q_PAD_COLS), zrow, zrow] + [row(LANES)] * 5
    return pl.pallas_call(
        _proj_kernel,
        grid=(n // tm,),
        in_specs=[row(D_MODEL), const((1, D_MODEL)), const((D_MODEL, PROJ_COLS)), row(1),
                  const((1, LANES)), const((1, LANES))],
        out_specs=out_specs,
        out_shape=out_shape,
        scratch_shapes=[pltpu.VMEM((tm, LANES), F32), pltpu.VMEM((tm, LANES), F32),
                        pltpu.VMEM((2, tm, LANES), F32)],
        compiler_params=_params("parallel"),
        name="in_proj",
    )(x2, norm1_w.reshape(1, D_MODEL), w_packed, positions.reshape(n, 1), invf, sign)


RWKV_TT = 512
P_LORA = "b3"
P_GATE = "b1"
P_HS = "b1"
P_CUM = "xa"


def _rwkv_kernel(y_ref, mu_ref, w0_ref, wl_ref, a0_ref, gup_ref, kk_ref, ka_ref, rk_ref,
                 lw_ref, lb_ref, hs_ref, tri_ref, o_ref, prev_ref, s_ref, o_scr):
    i = pl.program_id(1)

    @pl.when(i == 0)
    def _():
        prev_ref[...] = jnp.zeros_like(prev_ref)
        s_ref[...] = jnp.zeros_like(s_ref)

    tt = y_ref.shape[0]
    y = y_ref[...]
    row = lax.broadcasted_iota(jnp.int32, (tt, 1), 0)
    y_prev = jnp.where(row == 0, prev_ref[...], pltpu.roll(y, 1, 0))
    prev_ref[...] = y[tt - 1:tt, :]
    ys = y + (y_prev - y) * mu_ref[...]

    w_ = RWKV_WIDTH
    r = ys[:, 0:w_]
    k = ys[:, w_:2 * w_]
    v = ys[:, 2 * w_:3 * w_]
    z = ys[:, 3 * w_:3 * w_ + LANES]
    gd = ys[:, 3 * w_ + LANES:3 * w_ + 2 * LANES]
    lane = lax.broadcasted_iota(jnp.int32, z.shape, 1)
    zt = jnp.where(lane < D_DECAY_LORA, jnp.tanh(z), z)
    wa = _mm(zt, wl_ref[...], P_LORA)
    w_raw = w0_ref[...] + wa[:, :w_]
    a = jax.nn.sigmoid(a0_ref[...] + wa[:, w_:])
    logw = -jnp.exp(-jax.nn.softplus(-w_raw) - 0.5)
    g = _mm(jax.nn.sigmoid(gd), gup_ref[...], P_GATE)
    hs = hs_ref[...]

    def head_sum(x):
        hw = hs.shape[0]
        return jnp.concatenate([_mm(x[:, c0:c0 + hw], hs, P_HS) for c0 in range(0, w_, hw)], axis=1)

    kk = k * kk_ref[...]
    kk = kk * lax.rsqrt(jnp.maximum(head_sum(kk * kk), 1e-12))
    k2 = k * (1.0 + (a - 1.0) * ka_ref[...])
    alpha = -kk
    beta = kk * a

    cw = _mm(tri_ref[...], logw, P_CUM)
    e_in = jnp.exp(cw)
    e_ex = jnp.exp(cw - logw)
    e_neg = jnp.exp(-cw)
    a_t = alpha * e_ex
    r_t = r * e_in
    b_t = beta * e_neg
    k_t = k2 * e_neg

    m0 = lax.broadcasted_iota(jnp.int32, (CHUNK, LANES), 1) < HEAD_DIM

    def bd(xc):
        return jnp.concatenate([jnp.where(m0, xc, 0.0), jnp.where(m0, 0.0, xc)], axis=0)

    n2 = 2 * CHUNK
    ri = lax.broadcasted_iota(jnp.int32, (n2, n2), 0) % CHUNK
    ci = lax.broadcasted_iota(jnp.int32, (n2, n2), 1) % CHUNK
    strict = ri > ci
    incl = ri >= ci
    eye = jnp.where(lax.broadcasted_iota(jnp.int32, (n2, n2), 0) == lax.broadcasted_iota(jnp.int32, (n2, n2), 1),
                    1.0, 0.0)

    units = [(c, p) for c in range(tt // CHUNK) for p in range(w_ // LANES)]
    w_c = {}
    st = {}
    for c in range(tt // CHUNK):
        rs = slice(c * CHUNK, (c + 1) * CHUNK)
        cw_last = cw[c * CHUNK + CHUNK - 1:c * CHUNK + CHUNK, :]
        e_tot = jnp.exp(cw_last - cw[rs, :])
        w_c[c] = jnp.exp(cw_last)
        for p in range(w_ // LANES):
            ls = slice(p * LANES, (p + 1) * LANES)
            a_bd = bd(a_t[rs, ls]).astype(BF16)
            r_bd = bd(r_t[rs, ls])
            b_bd = bd(b_t[rs, ls]).astype(BF16)
            k_bd = bd(k_t[rs, ls]).astype(BF16)
            gm = _dot_nt(jnp.concatenate([a_bd, r_bd.astype(BF16)], axis=0), jnp.concatenate([b_bd, k_bd], axis=0))
            st[c, p] = dict(
                a_bd=a_bd, r_bd=r_bd,
                v_bd=bd(v[rs, ls]).astype(BF16),
                bh_bd=bd(beta[rs, ls] * e_tot[:, ls]).astype(BF16),
                kh_bd=bd(k2[rs, ls] * e_tot[:, ls]).astype(BF16),
                l_ab=jnp.where(strict, gm[0:n2, 0:n2], 0.0),
                l_ak=jnp.where(strict, gm[0:n2, n2:], 0.0).astype(BF16),
                m_rb=jnp.where(incl, gm[n2:, 0:n2], 0.0).astype(BF16),
                m_rk=jnp.where(incl, gm[n2:, n2:], 0.0).astype(BF16))

    pw, tm_ = {}, {}
    for u_ in units:
        l_ab = st[u_]["l_ab"]
        lb = l_ab.astype(BF16)
        pw[u_] = _dot(lb, lb)
        tm_[u_] = eye + l_ab
    for level in range(5):
        for u_ in units:
            pb = pw[u_].astype(BF16)
            if level < 4:
                y = _dot(pb, jnp.concatenate([pb, tm_[u_].astype(BF16)], axis=1))
                pw[u_] = y[:, 0:n2]
                tm_[u_] = tm_[u_] + y[:, n2:]
            else:
                tm_[u_] = tm_[u_] + _dot(pb, tm_[u_].astype(BF16))

    zero_bd = jnp.zeros((n2, n2), BF16)
    lv = {u_: _dot(st[u_]["l_ak"], st[u_]["v_bd"]).astype(BF16) for u_ in units}
    au = {u_: _dot(tm_[u_].astype(BF16), jnp.concatenate([st[u_]["a_bd"], lv[u_]], axis=1)).astype(BF16)
          for u_ in units}
    for u_ in units:
        d_ = st[u_]
        mo = _dot(jnp.concatenate([d_["m_rb"], d_["m_rk"]], axis=1),
                  jnp.concatenate([au[u_], jnp.concatenate([zero_bd, d_["v_bd"]], axis=1)], axis=0))
        d_["r_hat"] = (d_["r_bd"] + mo[:, 0:n2]).astype(BF16)
        d_["o0"] = mo[:, n2:]
    for u_ in units:
        st[u_]["g"] = _dot_tn(au[u_][:, 0:n2], st[u_]["bh_bd"]).astype(BF16)
    for u_ in units:
        d_ = st[u_]
        d_["n"] = _dot_tn(jnp.concatenate([au[u_][:, n2:], d_["v_bd"]], axis=0),
                          jnp.concatenate([d_["bh_bd"], d_["kh_bd"]], axis=0))

    for c, p in units:
        d_ = st[c, p]
        rs = slice(c * CHUNK, (c + 1) * CHUNK)
        ls = slice(p * LANES, (p + 1) * LANES)
        s_old = s_ref[p]
        sb = s_old.astype(BF16)
        o_bd = _dot_nt(d_["r_hat"], sb) + d_["o0"]
        o_scr[rs, ls] = o_bd[0:CHUNK] + o_bd[CHUNK:]
        s_ref[p] = s_old * w_c[c][:, ls] + _dot(sb, d_["g"]) + d_["n"]

    o = o_scr[...]
    inv_n = 1.0 / HEAD_DIM
    mean = head_sum(o) * inv_n
    d = o - mean
    var = head_sum(d * d) * inv_n
    on = d * lax.rsqrt(var + LNX_EPS) * lw_ref[...] + lb_ref[...]
    bonus = head_sum(r * k2 * rk_ref[...]) * v
    o_ref[...] = ((on + bonus) * g).astype(o_ref.dtype)


def _rwkv(yr, b, t, mu, w0, w_lora_up, a0, a_lora_up, g_lora_up, k_k, k_a, r_k, lnx_w, lnx_b):
    tt = RWKV_TT
    nt = t // tt
    w_ = RWKV_WIDTH
    wl = jnp.zeros((LANES, 2 * w_), F32)
    wl = wl.at[:D_DECAY_LORA, :w_].set(w_lora_up).at[D_DECAY_LORA:, w_:].set(a_lora_up)
    hw = 2 * LANES
    head = jnp.arange(hw) // HEAD_DIM
    hs = (head[:, None] == head[None, :]).astype(F32)
    ti = jnp.arange(tt)
    tri = ((ti[:, None] // CHUNK == ti[None, :] // CHUNK) & (ti[:, None] >= ti[None, :])).astype(F32)
    vec = lambda a_, width: a_.reshape(1, width)
    const = lambda shape: pl.BlockSpec(shape, lambda bi, i: (0, 0))
    return pl.pallas_call(
        _rwkv_kernel,
        grid=(b, nt),
        in_specs=[pl.BlockSpec((tt, RWKV_COLS), lambda bi, i: (bi * nt + i, 0)),
                  const((1, RWKV_COLS)), const((1, w_)), const((LANES, 2 * w_)), const((1, w_)),
                  const((D_GATE_LORA, w_)), const((1, w_)), const((1, w_)), const((1, w_)),
                  const((1, w_)), const((1, w_)), const((hw, hw)), const((tt, tt))],
        out_specs=pl.BlockSpec((tt, w_), lambda bi, i: (bi * nt + i, 0)),
        out_shape=jax.ShapeDtypeStruct((b * t, w_), BF16),
        scratch_shapes=[pltpu.VMEM((1, RWKV_COLS), F32),
                        pltpu.VMEM((w_ // LANES, LANES, LANES), F32),
                        pltpu.VMEM((tt, w_), F32)],
        compiler_params=_params("parallel", "arbitrary"),
        name="rwkv7",
    )(yr, vec(mu, RWKV_COLS), vec(w0, w_), wl, vec(a0, w_), g_lora_up, vec(k_k, w_), vec(k_a, w_),
      vec(r_k, w_), vec(lnx_w, w_), vec(lnx_b, w_), hs, tri)


def _compress_kernel(zk_ref, zv_ref, pek_ref, pev_ref, wk1_ref, wv1_ref, wk2_ref, wv2_ref, kc_ref, vc_ref):
    def one(z_ref, pe_ref, w1_ref, w2_ref, out_ref):
        z = z_ref[0]
        nrow = z.shape[0]
        za = z + pe_ref[0:1, :]
        zb = z + pe_ref[1:2, :]
        acc = jnp.zeros((nrow, LANES), F32)
        for gi in range(NSA_KV_HEADS):
            ha = _mm(za, w1_ref[gi, 0], "b3")
            hb = _mm(zb, w1_ref[gi, 1], "b3")
            hid = ha + pltpu.roll(hb, nrow - 1, 0)
            acc = acc + _mm(jax.nn.gelu(hid), w2_ref[gi], "b3")
        out_ref[0] = acc

    one(zk_ref, pek_ref, wk1_ref, wk2_ref, kc_ref)
    one(zv_ref, pev_ref, wv1_ref, wv2_ref, vc_ref)


def _compress_weights(pos, w1, w2):
    half = CMP_BLOCK // 2
    w1r = w1.reshape(2, half, HEAD_DIM, CMP_HIDDEN)
    w1g = jnp.zeros((NSA_KV_HEADS, 2, half, NSA_KV_HEADS, HEAD_DIM, CMP_HIDDEN), F32)
    for gi in range(NSA_KV_HEADS):
        w1g = w1g.at[gi, :, :, gi].set(w1r)
    w1g = w1g.reshape(NSA_KV_HEADS, 2, half * NSA_KV_WIDTH, CMP_HIDDEN)
    w2g = jnp.zeros((NSA_KV_HEADS, CMP_HIDDEN, NSA_KV_HEADS, HEAD_DIM), F32)
    for gi in range(NSA_KV_HEADS):
        w2g = w2g.at[gi, :, gi].set(w2)
    w2g = w2g.reshape(NSA_KV_HEADS, CMP_HIDDEN, NSA_KV_WIDTH)
    pe = jnp.broadcast_to(pos.reshape(2, half, 1, HEAD_DIM), (2, half, NSA_KV_HEADS, HEAD_DIM))
    pe = pe.reshape(2, half * NSA_KV_WIDTH)
    return pe, w1g, w2g


def _compress(kc, vc, b, t, cmp_pos_k, cmp_pos_v, k_w1, k_w2, v_w1, v_w2):
    nrow = t // CMP_STRIDE
    zw = CMP_STRIDE * NSA_KV_WIDTH
    pek, wk1, wk2 = _compress_weights(cmp_pos_k, k_w1, k_w2)
    pev, wv1, wv2 = _compress_weights(cmp_pos_v, v_w1, v_w2)
    zspec = pl.BlockSpec((1, nrow, zw), lambda bi: (bi, 0, 0))
    ospec = pl.BlockSpec((1, nrow, LANES), lambda bi: (bi, 0, 0))
    c2 = lambda shape: pl.BlockSpec(shape, lambda bi: (0,) * len(shape))
    return pl.pallas_call(
        _compress_kernel,
        grid=(b,),
        in_specs=[zspec, zspec, c2((2, zw)), c2((2, zw)), c2(wk1.shape), c2(wv1.shape), c2(wk2.shape), c2(wv2.shape)],
        out_specs=[ospec, ospec],
        out_shape=[jax.ShapeDtypeStruct((b, nrow, LANES), F32)] * 2,
        compiler_params=_params("parallel"),
        name="nsa_compress",
    )(kc.reshape(b, nrow, zw), vc.reshape(b, nrow, zw), pek, pev, wk1, wv1, wk2, wv2)


SEL_TILE = 256
SEL_UNROLL = 8
WIN_TILE = 128
NSA_QBLK = 2


def _n_win_tiles():
    tq = NSA_QBLK * Q_BLOCK
    span = WINDOW + tq + WIN_TILE - math.gcd(WIN_TILE, tq)
    return -(-span // WIN_TILE)


def _nsa_kernel(q_ref, kcmp_ref, vcmp_ref, ks_ref, vs_ref, kw_ref, vw_ref, gl_ref, ovt_ref, oh_ref,
                o_ref, m_scr, l_scr, acc_scr, sc_scr, imp_scr, cnt_scr, sel_scr, g_scr):
    s2 = pl.program_id(1)
    qb = Q_BLOCK
    nq = NSA_QBLK
    n_head = NSA_Q_HEADS
    rows = nq * n_head * qb
    n_cmp = kcmp_ref.shape[1]
    n_sb = ovt_ref.shape[0]
    s_of = [s2 * nq + qi for qi in range(nq)]
    s_last = s_of[-1]

    def tok(qi, shape):
        return s_of[qi] * qb + lax.broadcasted_iota(jnp.int32, shape, 0)

    def per_head(x64s):
        return jnp.concatenate([x for x in x64s for _ in range(n_head)], axis=0)

    def per_group_head(x):
        parts = []
        for qi in range(nq):
            for gi in range(NSA_KV_HEADS):
                r0 = (qi * NSA_KV_HEADS + gi) * qb
                parts += [x[r0:r0 + qb]] * NSA_HPG
        return jnp.concatenate(parts, axis=0)

    lane64 = lax.broadcasted_iota(jnp.int32, (qb, LANES), 1)
    sig = jax.nn.sigmoid(gl_ref[...])
    for col in range(n_head * N_BRANCH):
        g_scr[col] = jnp.broadcast_to(sig[:, col:col + 1], (nq * qb, LANES))

    qq = jnp.concatenate([q_ref[qi * qb:(qi + 1) * qb, hq * LANES:(hq + 1) * LANES]
                          for qi in range(nq) for hq in range(n_head)], axis=0)

    n_win = _n_win_tiles()
    first = (s2 * nq * qb - WINDOW) // WIN_TILE
    lane_w = lax.broadcasted_iota(jnp.int32, (qb, WIN_TILE), 1)
    win_sc, win_k0 = [], []
    mxw = jnp.full((rows, LANES), NEG_INF, F32)
    for j in range(n_win):
        kt = jnp.maximum(first + j, 0)
        k0 = pl.multiple_of(kt * WIN_TILE, WIN_TILE)
        kp = k0 + lane_w + jnp.where(first + j >= 0, 0, 1 << 24)
        bias = per_head([jnp.where((kp <= tok(qi, kp.shape)) & (kp > tok(qi, kp.shape) - WINDOW), 0.0, NEG_INF)
                         for qi in range(nq)])
        sc_ = _dot_nt(qq, kw_ref[0, pl.ds(k0, WIN_TILE), :]) + bias
        for c0 in range(0, WIN_TILE, LANES):
            mxw = jnp.maximum(mxw, sc_[:, c0:c0 + LANES])
        win_sc.append(sc_)
        win_k0.append(k0)
    m_w = jnp.max(mxw, axis=1, keepdims=True)
    l_w = jnp.zeros((rows, LANES), F32)
    acc_w = jnp.zeros((rows, LANES), F32)
    for j in range(n_win):
        p = jnp.exp2(win_sc[j] - m_w)
        for c0 in range(0, WIN_TILE, LANES):
            l_w = l_w + p[:, c0:c0 + LANES]
        acc_w = acc_w + _dot(p.astype(BF16), vw_ref[0, pl.ds(win_k0[j], WIN_TILE), :])
    o_win = acc_w / jnp.sum(l_w, axis=1, keepdims=True)

    cmp_i = lax.broadcasted_iota(jnp.int32, (qb, n_cmp), 1)
    cbias = per_head([jnp.where((cmp_i * CMP_STRIDE + CMP_BLOCK - 1 <= tok(qi, cmp_i.shape)) & (cmp_i < n_cmp - 1),
                                0.0, NEG_INF) for qi in range(nq)])
    sc = _mm(qq, kcmp_ref[0], "xa", _dot_nt) + cbias
    mx = jnp.max(sc, axis=1, keepdims=True)
    e = jnp.exp2(sc - mx)
    den = jnp.sum(e, axis=1, keepdims=True)
    p_c = e * jnp.where(mx > 0.5 * NEG_INF, 1.0 / den, 0.0)
    o_cmp = _dot(p_c.astype(BF16), vcmp_ref[0].astype(BF16))

    pc_sums = []
    for qg in range(nq * NSA_KV_HEADS):
        r0 = qg * NSA_HPG * qb
        acc = p_c[r0:r0 + qb]
        for h in range(1, NSA_HPG):
            acc = acc + p_c[r0 + h * qb:r0 + (h + 1) * qb]
        pc_sums.append(acc)
    pcs = jnp.concatenate(pc_sums, axis=0)
    imp = _dot_nt(ovt_ref[...], pcs, HI)
    nl = nq * NSA_KV_HEADS * qb
    blk = lax.broadcasted_iota(jnp.int32, (n_sb, nl), 0)
    s_lane = s2 * nq + lax.broadcasted_iota(jnp.int32, (n_sb, nl), 1) // (NSA_KV_HEADS * qb)
    forced = (blk == 0) | (blk == s_lane) | (blk == s_lane - 1)
    imp = jnp.where(blk <= s_lane, imp + jnp.where(forced, FORCE_BONUS, 0.0), NEG_INF)
    n_sel = min(SEL_TOPK, n_sb)
    sel_scr[...] = jnp.where(blk <= s_lane, 1.0, 0.0)

    @pl.when(s_last >= n_sel)
    def _():
        imp_scr[...] = imp
        cnt_scr[...] = jnp.zeros(cnt_scr.shape, F32)
        sub = lax.broadcasted_iota(jnp.int32, (8, nl), 0)
        for ig in range(n_sb // 8):
            @pl.when(ig * 8 <= s_last)
            def _(ig=ig):
                impv = imp_scr[...]
                cnt = [cnt_scr[rg * 8:(rg + 1) * 8, :] for rg in range(n_sb // 8)]
                for i2 in range(ig * 8, ig * 8 + 8):
                    vi = impv[i2:i2 + 1, :]
                    for rg in range(n_sb // 8):
                        vj = impv[rg * 8:(rg + 1) * 8, :]
                        if rg < ig:
                            one = jnp.where(vi > vj, 1.0, 0.0)
                        elif rg > ig:
                            one = jnp.where(vi >= vj, 1.0, 0.0)
                        else:
                            one = jnp.where(sub > i2 - ig * 8, jnp.where(vi >= vj, 1.0, 0.0),
                                            jnp.where(vi > vj, 1.0, 0.0))
                        cnt[rg] = cnt[rg] + one
                for rg in range(n_sb // 8):
                    cnt_scr[rg * 8:(rg + 1) * 8, :] = cnt[rg]
        sel_scr[...] = jnp.where((cnt_scr[...] < n_sel) & (blk <= s_lane), 1.0, 0.0)

    sel_t = sel_scr[...]
    if n_sb < LANES:
        sel_t = jnp.concatenate([sel_t, jnp.zeros((LANES - n_sb, nl), F32)], axis=0)
    unsel = ((sel_t.T - 1.0) * -NEG_INF).astype(BF16)
    q_ext = jnp.concatenate([qq, per_group_head(unsel)], axis=1)

    lane_s = lax.broadcasted_iota(jnp.int32, (qb, SEL_TILE), 1)
    n_tiles = ((s_last + 1) * qb + SEL_TILE - 1) // SEL_TILE
    n_full = n_tiles // SEL_UNROLL
    rem = n_tiles % SEL_UNROLL
    rem_blocks = [w for w in (SEL_UNROLL >> i for i in range(1, SEL_UNROLL.bit_length())) if w > 0]

    def rem_start(w):
        return n_full * SEL_UNROLL + (rem & ~(2 * w - 1))

    def tile_scores(j, mx_):
        k0 = pl.multiple_of(j * SEL_TILE, SEL_TILE)
        causal = per_head([jnp.where(k0 + lane_s <= tok(qi, lane_s.shape), 0.0, NEG_INF) for qi in range(nq)])
        k_ext = jnp.concatenate([ks_ref[0, pl.ds(k0, SEL_TILE), :], oh_ref[j]], axis=1)
        sc_ = _dot_nt(q_ext, k_ext) + causal
        sc_scr[j] = sc_
        for c0 in range(0, SEL_TILE, LANES):
            mx_ = jnp.maximum(mx_, sc_[:, c0:c0 + LANES])
        return mx_

    def pass1(it, mx_):
        for u in range(SEL_UNROLL):
            mx_ = tile_scores(it * SEL_UNROLL + u, mx_)
        return mx_

    m_scr[...] = lax.fori_loop(0, n_full, pass1, jnp.full((rows, LANES), NEG_INF, F32))
    for w in rem_blocks:
        @pl.when((rem & w) != 0)
        def _(w=w):
            mx_ = m_scr[...]
            for u in range(w):
                mx_ = tile_scores(rem_start(w) + u, mx_)
            m_scr[...] = mx_
    m_scr[...] = jnp.broadcast_to(jnp.max(m_scr[...], axis=1, keepdims=True), m_scr.shape)
    l_scr[...] = jnp.zeros(l_scr.shape, F32)
    acc_scr[...] = jnp.zeros(acc_scr.shape, F32)

    def tile_pv(j):
        k0 = pl.multiple_of(j * SEL_TILE, SEL_TILE)
        m_b = jnp.concatenate([m_scr[...]] * (SEL_TILE // LANES), axis=1)
        p = jnp.exp2(sc_scr[j] - m_b)
        l_new = l_scr[...]
        for c0 in range(0, SEL_TILE, LANES):
            l_new = l_new + p[:, c0:c0 + LANES]
        l_scr[...] = l_new
        acc_scr[...] += _dot(p.astype(BF16), vs_ref[0, pl.ds(k0, SEL_TILE), :])

    def pass2(it, c):
        for u in range(SEL_UNROLL):
            tile_pv(it * SEL_UNROLL + u)
        return c

    lax.fori_loop(0, n_full, pass2, 0)
    for w in rem_blocks:
        @pl.when((rem & w) != 0)
        def _(w=w):
            for u in range(w):
                tile_pv(rem_start(w) + u)
    o_sel = acc_scr[...] / jnp.sum(l_scr[...], axis=1, keepdims=True)

    low_half = lane64 < HEAD_DIM
    for qi in range(nq):
        ts = slice(qi * qb, (qi + 1) * qb)
        for h in range(NSA_HPG):
            halves = []
            for gi in range(NSA_KV_HEADS):
                hq = gi * NSA_HPG + h
                r0 = (qi * n_head + hq) * qb
                hr = slice(r0, r0 + qb)
                col = hq * N_BRANCH
                halves.append(g_scr[col, ts] * o_cmp[hr] + g_scr[col + 1, ts] * o_sel[hr] + g_scr[col + 2, ts] * o_win[hr])
            o_ref[ts, h * LANES:(h + 1) * LANES] = jnp.where(low_half, halves[0], halves[1]).astype(o_ref.dtype)


def _nsa_attention(q, k_cmp, v_cmp, ks, vs, kw, vw, gl, b, t):
    tq = NSA_QBLK * Q_BLOCK
    ns = t // tq
    n_sb = t // SEL_BLOCK
    n_cmp = t // CMP_STRIDE
    n_kt = t // SEL_TILE
    cmp_start = jnp.arange(n_cmp) * CMP_STRIDE
    sb = jnp.arange(n_sb)
    ovt = ((cmp_start[None, :] < (sb[:, None] + 1) * SEL_BLOCK)
           & (cmp_start[None, :] + CMP_BLOCK > sb[:, None] * SEL_BLOCK)
           & (jnp.arange(n_cmp)[None, :] < n_cmp - 1)).astype(F32)
    key_blk = (jnp.arange(n_kt)[:, None] * SEL_TILE + jnp.arange(SEL_TILE)[None, :]) // SEL_BLOCK
    onehot = (key_blk[:, :, None] == jnp.arange(LANES)[None, None, :]).astype(BF16)
    rows = NSA_QBLK * NSA_Q_HEADS * Q_BLOCK
    nl = NSA_QBLK * NSA_KV_HEADS * Q_BLOCK
    qspec = pl.BlockSpec((tq, Q_PAD_COLS), lambda bi, si: (bi * ns + si, 0))
    seq = lambda: pl.BlockSpec((1, t, LANES), lambda bi, si: (bi, 0, 0))
    cmp_spec = lambda: pl.BlockSpec((1, n_cmp, LANES), lambda bi, si: (bi, 0, 0))
    k3 = lambda a_: a_.reshape(b, t, LANES)
    return pl.pallas_call(
        _nsa_kernel,
        grid=(b, ns),
        in_specs=[qspec, cmp_spec(), cmp_spec(), seq(), seq(), seq(), seq(),
                  pl.BlockSpec((tq, LANES), lambda bi, si: (bi * ns + si, 0)),
                  pl.BlockSpec((n_sb, n_cmp), lambda bi, si: (0, 0)),
                  pl.BlockSpec((n_kt, SEL_TILE, LANES), lambda bi, si: (0, 0, 0))],
        out_specs=pl.BlockSpec((tq, NSA_WIDTH), lambda bi, si: (bi * ns + si, 0)),
        out_shape=jax.ShapeDtypeStruct((b * t, NSA_WIDTH), BF16),
        scratch_shapes=[pltpu.VMEM((rows, LANES), F32),
                        pltpu.VMEM((rows, LANES), F32),
                        pltpu.VMEM((rows, LANES), F32),
                        pltpu.VMEM((n_kt, rows, SEL_TILE), F32),
                        pltpu.VMEM((n_sb, nl), F32),
                        pltpu.VMEM((n_sb, nl), F32),
                        pltpu.VMEM((n_sb, nl), F32),
                        pltpu.VMEM((NSA_Q_HEADS * N_BRANCH, tq, LANES), F32)],
        compiler_params=_params("parallel", "arbitrary"),
        name="nsa_attention",
    )(q, k_cmp, v_cmp, k3(ks), k3(vs), k3(kw), k3(vw), gl, ovt, onehot)


FF_TILE = 256


def _ffn_kernel(x_ref, orw_ref, ons_ref, wor_ref, won_ref, n2_ref, w1_ref, w3_ref, w2_ref, nf_ref,
                out_ref, h_scr, u_scr, acc_scr):
    j = pl.program_id(1)

    @pl.when(j == 0)
    def _():
        h = x_ref[...] + _dot(orw_ref[...], wor_ref[...]) + _dot(ons_ref[...], won_ref[...])
        h_scr[...] = h
        ms = jnp.mean(h * h, axis=-1, keepdims=True)
        u_scr[...] = (h * lax.rsqrt(ms + RMS_EPS) * n2_ref[...]).astype(BF16)
        acc_scr[...] = jnp.zeros_like(acc_scr)

    u = u_scr[...]
    gate = _dot(u, w1_ref[...])
    up = _dot(u, w3_ref[...])
    act = (jax.nn.silu(gate) * up).astype(BF16)
    acc_scr[...] += _dot(act, w2_ref[...])

    @pl.when(j == pl.num_programs(1) - 1)
    def _():
        h = h_scr[...] + acc_scr[...]
        ms = jnp.mean(h * h, axis=-1, keepdims=True)
        out_ref[...] = h * lax.rsqrt(ms + RMS_EPS) * nf_ref[...]


def _out_ffn(x2, o_rwkv, o_nsa, w_out, norm2_w, ffn_w1, ffn_w3, ffn_w2, final_norm_w):
    n = x2.shape[0]
    tm = 1024
    w_or = w_out[:RWKV_WIDTH].astype(BF16)
    w_n = w_out[RWKV_WIDTH:].reshape(NSA_KV_HEADS, NSA_HPG, HEAD_DIM, D_MODEL)
    w_on = w_n.transpose(1, 0, 2, 3).reshape(NSA_WIDTH, D_MODEL).astype(BF16)
    row = lambda width: pl.BlockSpec((tm, width), lambda i, j: (i, 0))
    const = lambda shape: pl.BlockSpec(shape, lambda i, j: (0, 0))
    return pl.pallas_call(
        _ffn_kernel,
        grid=(n // tm, D_FF // FF_TILE),
        in_specs=[row(D_MODEL), row(RWKV_WIDTH), row(NSA_WIDTH),
                  const((RWKV_WIDTH, D_MODEL)), const((NSA_WIDTH, D_MODEL)), const((1, D_MODEL)),
                  pl.BlockSpec((D_MODEL, FF_TILE), lambda i, j: (0, j)),
                  pl.BlockSpec((D_MODEL, FF_TILE), lambda i, j: (0, j)),
                  pl.BlockSpec((FF_TILE, D_MODEL), lambda i, j: (j, 0)),
                  const((1, D_MODEL))],
        out_specs=row(D_MODEL),
        out_shape=jax.ShapeDtypeStruct((n, D_MODEL), F32),
        scratch_shapes=[pltpu.VMEM((tm, D_MODEL), F32), pltpu.VMEM((tm, D_MODEL), BF16),
                        pltpu.VMEM((tm, D_MODEL), F32)],
        compiler_params=_params("parallel", "arbitrary"),
        name="out_ffn",
    )(x2, o_rwkv, o_nsa, w_or, w_on, norm2_w.reshape(1, D_MODEL), ffn_w1.astype(BF16), ffn_w3.astype(BF16),
      ffn_w2.astype(BF16), final_norm_w.reshape(1, D_MODEL))


def kernel(x, positions, norm1_w, w_in, mu_rwkv, w0, w_lora_up, a0, a_lora_up, g_lora_up, k_k, k_a, r_k, lnx_w, lnx_b, cmp_pos_k, cmp_pos_v, cmp_k_w1, cmp_k_w2, cmp_v_w1, cmp_v_w2, w_out, norm2_w, ffn_w1, ffn_w3, ffn_w2, final_norm_w):
    b, t, d = x.shape
    assert d == D_MODEL and norm1_w.shape[0] == 1, "single-layer block with d_model 1024"
    assert t % RWKV_TT == 0 and t % SEL_TILE == 0 and t % (NSA_QBLK * Q_BLOCK) == 0
    x2 = x.reshape(b * t, d)
    yr, q, kc, vc, ks, vs, kw, vw, gl = _in_proj(x2, norm1_w[0], _pack_w_in(w_in[0]), positions)
    o_rwkv = _rwkv(yr, b, t, mu_rwkv[0], w0[0], w_lora_up[0], a0[0], a_lora_up[0], g_lora_up[0],
                   k_k[0], k_a[0], r_k[0], lnx_w[0], lnx_b[0])
    k_cmp, v_cmp = _compress(kc, vc, b, t, cmp_pos_k[0], cmp_pos_v[0], cmp_k_w1[0], cmp_k_w2[0],
                             cmp_v_w1[0], cmp_v_w2[0])
    o_nsa = _nsa_attention(q, k_cmp, v_cmp, ks, vs, kw, vw, gl, b, t)
    out = _out_ffn(x2, o_rwkv, o_nsa, w_out[0], norm2_w[0], ffn_w1[0], ffn_w3[0], ffn_w2[0], final_norm_w)
    return out.reshape(b, t, d)
```

```python
import math

import jax
import jax.numpy as jnp
from jax import lax
from jax.experimental import pallas as pl
from jax.experimental.pallas import tpu as pltpu

F32 = jnp.float32
BF16 = jnp.bfloat16
HI = lax.Precision.HIGHEST

D_MODEL = 1024
HEAD_DIM = 64
RWKV_WIDTH = 512
RWKV_HEADS = 8
D_DECAY_LORA = 64
D_AAA_LORA = 64
D_GATE_LORA = 128
RWKV_COLS = 3 * RWKV_WIDTH + D_DECAY_LORA + D_AAA_LORA + D_GATE_LORA
NSA_WIDTH = 512
NSA_Q_HEADS = 8
NSA_KV_HEADS = 2
NSA_HPG = 4
NSA_KV_WIDTH = 128
N_BRANCH = 3
CMP_BLOCK = 32
CMP_STRIDE = 16
CMP_HIDDEN = 256
SEL_BLOCK = 64
SEL_TOPK = 16
WINDOW = 512
Q_BLOCK = 64
ROPE_THETA = 10000.0
D_FF = 2816
RMS_EPS = 1e-6
LNX_EPS = 64e-5
FORCE_BONUS = 1e4
NEG_INF = -1e30

LANES = 128
CHUNK = 64
Q_PAD_COLS = NSA_Q_HEADS * LANES
PROJ_COLS = RWKV_COLS + Q_PAD_COLS + 6 * NSA_KV_WIDTH + LANES
VMEM_LIMIT = 56 * 1024 * 1024


def _dot(a, b, prec=None):
    return jnp.dot(a, b, preferred_element_type=F32, precision=prec)


def _dot_nt(a, b, prec=None):
    return lax.dot_general(a, b, (((1,), (1,)), ((), ())), preferred_element_type=F32, precision=prec)


def _dot_tn(a, b, prec=None):
    return lax.dot_general(a, b, (((0,), (0,)), ((), ())), preferred_element_type=F32, precision=prec)


def _split(a):
    hi = a.astype(BF16)
    return hi, (a - hi.astype(F32)).astype(BF16)


def _mm(a, b, mode, dot=_dot):
    if mode == "hi":
        return dot(a, b, HI)
    if mode == "b1":
        return dot(a.astype(BF16), b.astype(BF16))
    if mode == "xa":
        bh, bl = _split(b)
        ab = a.astype(BF16)
        return dot(ab, bh) + dot(ab, bl)
    if mode == "xb":
        ah, al = _split(a)
        bb = b.astype(BF16)
        return dot(ah, bb) + dot(al, bb)
    ah, al = _split(a)
    bh, bl = _split(b)
    return dot(ah, bh) + (dot(ah, bl) + dot(al, bh))


def _params(*sem):
    return pltpu.CompilerParams(dimension_semantics=sem, vmem_limit_bytes=VMEM_LIMIT)


def _proj_kernel(x_ref, n1_ref, w_ref, pos_ref, invf_ref, sign_ref,
                 yr_ref, q_ref, kc_ref, vc_ref, ks_ref, vs_ref, kw_ref, vw_ref, gl_ref, cos_scr, sin_scr, kv_scr):
    x = x_ref[...]
    ms = jnp.mean(x * x, axis=-1, keepdims=True)
    xn = (x * lax.rsqrt(ms + RMS_EPS) * n1_ref[...]).astype(BF16)
    ang = pos_ref[...].astype(F32) * invf_ref[...]
    cos_scr[...] = jnp.cos(ang)
    sin_scr[...] = jnp.sin(ang) * sign_ref[...]
    lane = lax.broadcasted_iota(jnp.int32, ang.shape, 1)
    first_half = (lane % HEAD_DIM) < (HEAD_DIM // 2)

    def rope(v):
        rot = jnp.where(first_half, pltpu.roll(v, LANES - HEAD_DIM // 2, 1), pltpu.roll(v, HEAD_DIM // 2, 1))
        return v * cos_scr[...] + rot * sin_scr[...]

    def cols(c0, width):
        return _dot(xn, w_ref[:, c0:c0 + width])

    for j in range(RWKV_COLS // 256):
        yr_ref[:, j * 256:(j + 1) * 256] = cols(j * 256, 256)
    scale = HEAD_DIM ** -0.5 * math.log2(math.e)
    for hq in range(0, NSA_Q_HEADS, 2):
        q2 = cols(RWKV_COLS + hq * LANES, 2 * LANES)
        for e in range(2):
            q = rope(q2[:, e * LANES:(e + 1) * LANES]) * scale
            q_ref[:, (hq + e) * LANES:(hq + e + 1) * LANES] = q.astype(BF16)
    c0 = RWKV_COLS + Q_PAD_COLS
    kv = cols(c0, 2 * LANES)
    kv_scr[0] = rope(kv[:, :LANES])
    kv_scr[1] = kv[:, LANES:]
    zrows = kc_ref.shape[0]
    for l in range(CMP_STRIDE):
        kc_ref[:, l * LANES:(l + 1) * LANES] = kv_scr[0, pl.ds(l, zrows, stride=CMP_STRIDE), :]
        vc_ref[:, l * LANES:(l + 1) * LANES] = kv_scr[1, pl.ds(l, zrows, stride=CMP_STRIDE), :]
    kv = cols(c0 + 2 * LANES, 2 * LANES)
    ks_ref[...] = rope(kv[:, :LANES]).astype(BF16)
    vs_ref[...] = kv[:, LANES:].astype(BF16)
    kv = cols(c0 + 4 * LANES, 2 * LANES)
    kw_ref[...] = rope(kv[:, :LANES]).astype(BF16)
    vw_ref[...] = kv[:, LANES:].astype(BF16)
    gl_ref[...] = cols(c0 + 6 * LANES, LANES)


def _pack_w_in(w_in):
    w_r = w_in[:, :RWKV_COLS]
    w_q = w_in[:, RWKV_COLS:RWKV_COLS + NSA_WIDTH].reshape(D_MODEL, NSA_KV_HEADS, NSA_HPG, HEAD_DIM)
    zero = jnp.zeros_like(w_q)
    w_q = jnp.stack([jnp.where(jnp.arange(NSA_KV_HEADS)[None, :, None, None] == gg, w_q, zero)
                     for gg in range(NSA_KV_HEADS)], axis=3)
    w_q = w_q.reshape(D_MODEL, Q_PAD_COLS)
    c0 = RWKV_COLS + NSA_WIDTH
    w_kv = w_in[:, c0:c0 + 6 * NSA_KV_WIDTH]
    w_g = w_in[:, c0 + 6 * NSA_KV_WIDTH:]
    w_g = jnp.pad(w_g, ((0, 0), (0, LANES - w_g.shape[1])))
    return jnp.concatenate([w_r, w_q, w_kv, w_g], axis=1).astype(BF16)


def _in_proj(x2, norm1_w, w_packed, positions):
    n = x2.shape[0]
    tm = 1024
    half = HEAD_DIM // 2
    inv_freq = ROPE_THETA ** (-jnp.arange(half, dtype=F32) / half)
    invf = jnp.tile(inv_freq, LANES // half).reshape(1, LANES)
    lane = jnp.arange(LANES)
    sign = jnp.where((lane % HEAD_DIM) < half, -1.0, 1.0).astype(F32).reshape(1, LANES)
    row = lambda width: pl.BlockSpec((tm, width), lambda i: (i, 0))
    const = lambda shape: pl.BlockSpec(shape, lambda i: (0, 0))
    zw = CMP_STRIDE * LANES
    zrow = pl.BlockSpec((tm // CMP_STRIDE, zw), lambda i: (i, 0))
    out_shape = [
        jax.ShapeDtypeStruct((n, RWKV_COLS), F32),
        jax.ShapeDtypeStruct((n, Q_PAD_COLS), BF16),
        jax.ShapeDtypeStruct((n // CMP_STRIDE, zw), F32),
        jax.ShapeDtypeStruct((n // CMP_STRIDE, zw), F32),
        jax.ShapeDtypeStruct((n, LANES), BF16),
        jax.ShapeDtypeStruct((n, LANES), BF16),
        jax.ShapeDtypeStruct((n, LANES), BF16),
        jax.ShapeDtypeStruct((n, LANES), BF16),
        jax.ShapeDtypeStruct((n, LANES), F32),
    ]
    out_specs = [row(RWKV_COLS), row(Q_PAD_COLS), zrow, zrow] + [row(LANES)] * 5
    return pl.pallas_call(
        _proj_kernel,
        grid=(n // tm,),
        in_specs=[row(D_MODEL), const((1, D_MODEL)), const((D_MODEL, PROJ_COLS)), row(1),
                  const((1, LANES)), const((1, LANES))],
        out_specs=out_specs,
        out_shape=out_shape,
        scratch_shapes=[pltpu.VMEM((tm, LANES), F32), pltpu.VMEM((tm, LANES), F32),
                        pltpu.VMEM((2, tm, LANES), F32)],
        compiler_params=_params("parallel"),
        name="in_proj",
    )(x2, norm1_w.reshape(1, D_MODEL), w_packed, positions.reshape(n, 1), invf, sign)


RWKV_TT = 512
P_LORA = "b3"
P_GATE = "b1"
P_HS = "b1"
P_CUM = "xa"


def _rwkv_kernel(y_ref, mu_ref, w0_ref, wl_ref, a0_ref, gup_ref, kk_ref, ka_ref, rk_ref,
                 lw_ref, lb_ref, hs_ref, tri_ref, o_ref, prev_ref, s_ref, o_scr):
    i = pl.program_id(1)

    @pl.when(i == 0)
    def _():
        prev_ref[...] = jnp.zeros_like(prev_ref)
        s_ref[...] = jnp.zeros_like(s_ref)

    tt = y_ref.shape[0]
    y = y_ref[...]
    row = lax.broadcasted_iota(jnp.int32, (tt, 1), 0)
    y_prev = jnp.where(row == 0, prev_ref[...], pltpu.roll(y, 1, 0))
    prev_ref[...] = y[tt - 1:tt, :]
    ys = y + (y_prev - y) * mu_ref[...]

    w_ = RWKV_WIDTH
    r = ys[:, 0:w_]
    k = ys[:, w_:2 * w_]
    v = ys[:, 2 * w_:3 * w_]
    z = ys[:, 3 * w_:3 * w_ + LANES]
    gd = ys[:, 3 * w_ + LANES:3 * w_ + 2 * LANES]
    lane = lax.broadcasted_iota(jnp.int32, z.shape, 1)
    zt = jnp.where(lane < D_DECAY_LORA, jnp.tanh(z), z)
    wa = _mm(zt, wl_ref[...], P_LORA)
    w_raw = w0_ref[...] + wa[:, :w_]
    a = jax.nn.sigmoid(a0_ref[...] + wa[:, w_:])
    logw = -jnp.exp(-jax.nn.softplus(-w_raw) - 0.5)
    g = _mm(jax.nn.sigmoid(gd), gup_ref[...], P_GATE)
    hs = hs_ref[...]

    def head_sum(x):
        hw = hs.shape[0]
        return jnp.concatenate([_mm(x[:, c0:c0 + hw], hs, P_HS) for c0 in range(0, w_, hw)], axis=1)

    kk = k * kk_ref[...]
    kk = kk * lax.rsqrt(jnp.maximum(head_sum(kk * kk), 1e-12))
    k2 = k * (1.0 + (a - 1.0) * ka_ref[...])
    alpha = -kk
    beta = kk * a

    cw = _mm(tri_ref[...], logw, P_CUM)
    e_in = jnp.exp(cw)
    e_ex = jnp.exp(cw - logw)
    e_neg = jnp.exp(-cw)
    a_t = alpha * e_ex
    r_t = r * e_in
    b_t = beta * e_neg
    k_t = k2 * e_neg

    m0 = lax.broadcasted_iota(jnp.int32, (CHUNK, LANES), 1) < HEAD_DIM

    def bd(xc):
        return jnp.concatenate([jnp.where(m0, xc, 0.0), jnp.where(m0, 0.0, xc)], axis=0)

    n2 = 2 * CHUNK
    ri = lax.broadcasted_iota(jnp.int32, (n2, n2), 0) % CHUNK
    ci = lax.broadcasted_iota(jnp.int32, (n2, n2), 1) % CHUNK
    strict = ri > ci
    incl = ri >= ci
    eye = jnp.where(lax.broadcasted_iota(jnp.int32, (n2, n2), 0) == lax.broadcasted_iota(jnp.int32, (n2, n2), 1),
                    1.0, 0.0)

    units = [(c, p) for c in range(tt // CHUNK) for p in range(w_ // LANES)]
    w_c = {}
    st = {}
    for c in range(tt // CHUNK):
        rs = slice(c * CHUNK, (c + 1) * CHUNK)
        cw_last = cw[c * CHUNK + CHUNK - 1:c * CHUNK + CHUNK, :]
        e_tot = jnp.exp(cw_last - cw[rs, :])
        w_c[c] = jnp.exp(cw_last)
        for p in range(w_ // LANES):
            ls = slice(p * LANES, (p + 1) * LANES)
            a_bd = bd(a_t[rs, ls]).astype(BF16)
            r_bd = bd(r_t[rs, ls])
            b_bd = bd(b_t[rs, ls]).astype(BF16)
            k_bd = bd(k_t[rs, ls]).astype(BF16)
            gm = _dot_nt(jnp.concatenate([a_bd, r_bd.astype(BF16)], axis=0), jnp.concatenate([b_bd, k_bd], axis=0))
            st[c, p] = dict(
                a_bd=a_bd, r_bd=r_bd,
                v_bd=bd(v[rs, ls]).astype(BF16),
                bh_bd=bd(beta[rs, ls] * e_tot[:, ls]).astype(BF16),
                kh_bd=bd(k2[rs, ls] * e_tot[:, ls]).astype(BF16),
                l_ab=jnp.where(strict, gm[0:n2, 0:n2], 0.0),
                l_ak=jnp.where(strict, gm[0:n2, n2:], 0.0).astype(BF16),
                m_rb=jnp.where(incl, gm[n2:, 0:n2], 0.0).astype(BF16),
                m_rk=jnp.where(incl, gm[n2:, n2:], 0.0).astype(BF16))

    pw, tm_ = {}, {}
    for u_ in units:
        l_ab = st[u_]["l_ab"]
        lb = l_ab.astype(BF16)
        pw[u_] = _dot(lb, lb)
        tm_[u_] = eye + l_ab
    for level in range(5):
        for u_ in units:
            pb = pw[u_].astype(BF16)
            if level < 4:
                y = _dot(pb, jnp.concatenate([pb, tm_[u_].astype(BF16)], axis=1))
                pw[u_] = y[:, 0:n2]
                tm_[u_] = tm_[u_] + y[:, n2:]
            else:
                tm_[u_] = tm_[u_] + _dot(pb, tm_[u_].astype(BF16))

    zero_bd = jnp.zeros((n2, n2), BF16)
    lv = {u_: _dot(st[u_]["l_ak"], st[u_]["v_bd"]).astype(BF16) for u_ in units}
    au = {u_: _dot(tm_[u_].astype(BF16), jnp.concatenate([st[u_]["a_bd"], lv[u_]], axis=1)).astype(BF16)
          for u_ in units}
    for u_ in units:
        d_ = st[u_]
        mo = _dot(jnp.concatenate([d_["m_rb"], d_["m_rk"]], axis=1),
                  jnp.concatenate([au[u_], jnp.concatenate([zero_bd, d_["v_bd"]], axis=1)], axis=0))
        d_["r_hat"] = (d_["r_bd"] + mo[:, 0:n2]).astype(BF16)
        d_["o0"] = mo[:, n2:]
    for u_ in units:
        st[u_]["g"] = _dot_tn(au[u_][:, 0:n2], st[u_]["bh_bd"]).astype(BF16)
    for u_ in units:
        d_ = st[u_]
        d_["n"] = _dot_tn(jnp.concatenate([au[u_][:, n2:], d_["v_bd"]], axis=0),
                          jnp.concatenate([d_["bh_bd"], d_["kh_bd"]], axis=0))

    for c, p in units:
        d_ = st[c, p]
        rs = slice(c * CHUNK, (c + 1) * CHUNK)
        ls = slice(p * LANES, (p + 1) * LANES)
        s_old = s_ref[p]
        sb = s_old.astype(BF16)
        o_bd = _dot_nt(d_["r_hat"], sb) + d_["o0"]
        o_scr[rs, ls] = o_bd[0:CHUNK] + o_bd[CHUNK:]
        s_ref[p] = s_old * w_c[c][:, ls] + _dot(sb, d_["g"]) + d_["n"]

    o = o_scr[...]
    inv_n = 1.0 / HEAD_DIM
    mean = head_sum(o) * inv_n
    d = o - mean
    var = head_sum(d * d) * inv_n
    on = d * lax.rsqrt(var + LNX_EPS) * lw_ref[...] + lb_ref[...]
    bonus = head_sum(r * k2 * rk_ref[...]) * v
    o_ref[...] = ((on + bonus) * g).astype(o_ref.dtype)


def _rwkv(yr, b, t, mu, w0, w_lora_up, a0, a_lora_up, g_lora_up, k_k, k_a, r_k, lnx_w, lnx_b):
    tt = RWKV_TT
    nt = t // tt
    w_ = RWKV_WIDTH
    wl = jnp.zeros((LANES, 2 * w_), F32)
    wl = wl.at[:D_DECAY_LORA, :w_].set(w_lora_up).at[D_DECAY_LORA:, w_:].set(a_lora_up)
    hw = 2 * LANES
    head = jnp.arange(hw) // HEAD_DIM
    hs = (head[:, None] == head[None, :]).astype(F32)
    ti = jnp.arange(tt)
    tri = ((ti[:, None] // CHUNK == ti[None, :] // CHUNK) & (ti[:, None] >= ti[None, :])).astype(F32)
    vec = lambda a_, width: a_.reshape(1, width)
    const = lambda shape: pl.BlockSpec(shape, lambda bi, i: (0, 0))
    return pl.pallas_call(
        _rwkv_kernel,
        grid=(b, nt),
        in_specs=[pl.BlockSpec((tt, RWKV_COLS), lambda bi, i: (bi * nt + i, 0)),
                  const((1, RWKV_COLS)), const((1, w_)), const((LANES, 2 * w_)), const((1, w_)),
                  const((D_GATE_LORA, w_)), const((1, w_)), const((1, w_)), const((1, w_)),
                  const((1, w_)), const((1, w_)), const((hw, hw)), const((tt, tt))],
        out_specs=pl.BlockSpec((tt, w_), lambda bi, i: (bi * nt + i, 0)),
        out_shape=jax.ShapeDtypeStruct((b * t, w_), BF16),
        scratch_shapes=[pltpu.VMEM((1, RWKV_COLS), F32),
                        pltpu.VMEM((w_ // LANES, LANES, LANES), F32),
                        pltpu.VMEM((tt, w_), F32)],
        compiler_params=_params("parallel", "arbitrary"),
        name="rwkv7",
    )(yr, vec(mu, RWKV_COLS), vec(w0, w_), wl, vec(a0, w_), g_lora_up, vec(k_k, w_), vec(k_a, w_),
      vec(r_k, w_), vec(lnx_w, w_), vec(lnx_b, w_), hs, tri)


def _compress_kernel(zk_ref, zv_ref, pek_ref, pev_ref, wk1_ref, wv1_ref, wk2_ref, wv2_ref, kc_ref, vc_ref):
    def one(z_ref, pe_ref, w1_ref, w2_ref, out_ref, mode):
        z = z_ref[0]
        nrow = z.shape[0]
        za = z + pe_ref[0:1, :]
        zb = z + pe_ref[1:2, :]
        acc = jnp.zeros((nrow, LANES), F32)
        for gi in range(NSA_KV_HEADS):
            ha = _mm(za, w1_ref[gi, 0], mode)
            hb = _mm(zb, w1_ref[gi, 1], mode)
            hid = ha + pltpu.roll(hb, nrow - 1, 0)
            acc = acc + _mm(jax.nn.gelu(hid), w2_ref[gi], mode)
        out_ref[0] = acc

    one(zk_ref, pek_ref, wk1_ref, wk2_ref, kc_ref, "b3")
    one(zv_ref, pev_ref, wv1_ref, wv2_ref, vc_ref, "b1")


def _compress_weights(pos, w1, w2):
    half = CMP_BLOCK // 2
    w1r = w1.reshape(2, half, HEAD_DIM, CMP_HIDDEN)
    w1g = jnp.zeros((NSA_KV_HEADS, 2, half, NSA_KV_HEADS, HEAD_DIM, CMP_HIDDEN), F32)
    for gi in range(NSA_KV_HEADS):
        w1g = w1g.at[gi, :, :, gi].set(w1r)
    w1g = w1g.reshape(NSA_KV_HEADS, 2, half * NSA_KV_WIDTH, CMP_HIDDEN)
    w2g = jnp.zeros((NSA_KV_HEADS, CMP_HIDDEN, NSA_KV_HEADS, HEAD_DIM), F32)
    for gi in range(NSA_KV_HEADS):
        w2g = w2g.at[gi, :, gi].set(w2)
    w2g = w2g.reshape(NSA_KV_HEADS, CMP_HIDDEN, NSA_KV_WIDTH)
    pe = jnp.broadcast_to(pos.reshape(2, half, 1, HEAD_DIM), (2, half, NSA_KV_HEADS, HEAD_DIM))
    pe = pe.reshape(2, half * NSA_KV_WIDTH)
    return pe, w1g, w2g


def _compress(kc, vc, b, t, cmp_pos_k, cmp_pos_v, k_w1, k_w2, v_w1, v_w2):
    nrow = t // CMP_STRIDE
    zw = CMP_STRIDE * NSA_KV_WIDTH
    pek, wk1, wk2 = _compress_weights(cmp_pos_k, k_w1, k_w2)
    pev, wv1, wv2 = _compress_weights(cmp_pos_v, v_w1, v_w2)
    zspec = pl.BlockSpec((1, nrow, zw), lambda bi: (bi, 0, 0))
    ospec = pl.BlockSpec((1, nrow, LANES), lambda bi: (bi, 0, 0))
    c2 = lambda shape: pl.BlockSpec(shape, lambda bi: (0,) * len(shape))
    return pl.pallas_call(
        _compress_kernel,
        grid=(b,),
        in_specs=[zspec, zspec, c2((2, zw)), c2((2, zw)), c2(wk1.shape), c2(wv1.shape), c2(wk2.shape), c2(wv2.shape)],
        out_specs=[ospec, ospec],
        out_shape=[jax.ShapeDtypeStruct((b, nrow, LANES), F32)] * 2,
        compiler_params=_params("parallel"),
        name="nsa_compress",
    )(kc.reshape(b, nrow, zw), vc.reshape(b, nrow, zw), pek, pev, wk1, wv1, wk2, wv2)


SEL_TILE = 256
SEL_UNROLL = 8
WIN_TILE = 128
NSA_QBLK = 2


def _n_win_tiles():
    tq = NSA_QBLK * Q_BLOCK
    span = WINDOW + tq + WIN_TILE - math.gcd(WIN_TILE, tq)
    return -(-span // WIN_TILE)


def _nsa_kernel(q_ref, kcmp_ref, vcmp_ref, ks_ref, vs_ref, kw_ref, vw_ref, gl_ref, ovt_ref, oh_ref,
                o_ref, m_scr, l_scr, acc_scr, sc_scr, imp_scr, cnt_scr, sel_scr, g_scr):
    s2 = pl.program_id(1)
    qb = Q_BLOCK
    nq = NSA_QBLK
    n_head = NSA_Q_HEADS
    rows = nq * n_head * qb
    n_cmp = kcmp_ref.shape[1]
    n_sb = ovt_ref.shape[0]
    s_of = [s2 * nq + qi for qi in range(nq)]
    s_last = s_of[-1]

    def tok(qi, shape):
        return s_of[qi] * qb + lax.broadcasted_iota(jnp.int32, shape, 0)

    def per_head(x64s):
        return jnp.concatenate([x for x in x64s for _ in range(n_head)], axis=0)

    def per_group_head(x):
        parts = []
        for qi in range(nq):
            for gi in range(NSA_KV_HEADS):
                r0 = (qi * NSA_KV_HEADS + gi) * qb
                parts += [x[r0:r0 + qb]] * NSA_HPG
        return jnp.concatenate(parts, axis=0)

    lane64 = lax.broadcasted_iota(jnp.int32, (qb, LANES), 1)
    sig = jax.nn.sigmoid(gl_ref[...])
    for col in range(n_head * N_BRANCH):
        g_scr[col] = jnp.broadcast_to(sig[:, col:col + 1], (nq * qb, LANES))

    qq = jnp.concatenate([q_ref[qi * qb:(qi + 1) * qb, hq * LANES:(hq + 1) * LANES]
                          for qi in range(nq) for hq in range(n_head)], axis=0)

    n_win = _n_win_tiles()
    first = (s2 * nq * qb - WINDOW) // WIN_TILE
    lane_w = lax.broadcasted_iota(jnp.int32, (qb, WIN_TILE), 1)
    win_sc, win_k0 = [], []
    mxw = jnp.full((rows, LANES), NEG_INF, F32)
    for j in range(n_win):
        kt = jnp.maximum(first + j, 0)
        k0 = pl.multiple_of(kt * WIN_TILE, WIN_TILE)
        kp = k0 + lane_w + jnp.where(first + j >= 0, 0, 1 << 24)
        bias = per_head([jnp.where((kp <= tok(qi, kp.shape)) & (kp > tok(qi, kp.shape) - WINDOW), 0.0, NEG_INF)
                         for qi in range(nq)])
        sc_ = _dot_nt(qq, kw_ref[0, pl.ds(k0, WIN_TILE), :]) + bias
        for c0 in range(0, WIN_TILE, LANES):
            mxw = jnp.maximum(mxw, sc_[:, c0:c0 + LANES])
        win_sc.append(sc_)
        win_k0.append(k0)
    m_w = jnp.max(mxw, axis=1, keepdims=True)
    l_w = jnp.zeros((rows, LANES), F32)
    acc_w = jnp.zeros((rows, LANES), F32)
    for j in range(n_win):
        p = jnp.exp2(win_sc[j] - m_w)
        for c0 in range(0, WIN_TILE, LANES):
            l_w = l_w + p[:, c0:c0 + LANES]
        acc_w = acc_w + _dot(p.astype(BF16), vw_ref[0, pl.ds(win_k0[j], WIN_TILE), :])
    o_win = acc_w / jnp.sum(l_w, axis=1, keepdims=True)

    cmp_i = lax.broadcasted_iota(jnp.int32, (qb, n_cmp), 1)
    cbias = per_head([jnp.where((cmp_i * CMP_STRIDE + CMP_BLOCK - 1 <= tok(qi, cmp_i.shape)) & (cmp_i < n_cmp - 1),
                                0.0, NEG_INF) for qi in range(nq)])
    sc = _mm(qq, kcmp_ref[0], "xa", _dot_nt) + cbias
    mx = jnp.max(sc, axis=1, keepdims=True)
    e = jnp.exp2(sc - mx)
    den = jnp.sum(e, axis=1, keepdims=True)
    p_c = e * jnp.where(mx > 0.5 * NEG_INF, 1.0 / den, 0.0)
    o_cmp = _dot(p_c.astype(BF16), vcmp_ref[0].astype(BF16))

    pc_sums = []
    for qg in range(nq * NSA_KV_HEADS):
        r0 = qg * NSA_HPG * qb
        acc = p_c[r0:r0 + qb]
        for h in range(1, NSA_HPG):
            acc = acc + p_c[r0 + h * qb:r0 + (h + 1) * qb]
        pc_sums.append(acc)
    pcs = jnp.concatenate(pc_sums, axis=0)
    imp = _dot_nt(ovt_ref[...], pcs, HI)
    nl = nq * NSA_KV_HEADS * qb
    blk = lax.broadcasted_iota(jnp.int32, (n_sb, nl), 0)
    s_lane = s2 * nq + lax.broadcasted_iota(jnp.int32, (n_sb, nl), 1) // (NSA_KV_HEADS * qb)
    forced = (blk == 0) | (blk == s_lane) | (blk == s_lane - 1)
    imp = jnp.where(blk <= s_lane, imp + jnp.where(forced, FORCE_BONUS, 0.0), NEG_INF)
    n_sel = min(SEL_TOPK, n_sb)
    sel_scr[...] = jnp.where(blk <= s_lane, 1.0, 0.0)

    @pl.when(s_last >= n_sel)
    def _():
        imp_scr[...] = imp
        cnt_scr[...] = jnp.zeros(cnt_scr.shape, F32)
        sub = lax.broadcasted_iota(jnp.int32, (8, nl), 0)
        for ig in range(n_sb // 8):
            @pl.when(ig * 8 <= s_last)
            def _(ig=ig):
                impv = imp_scr[...]
                cnt = [cnt_scr[rg * 8:(rg + 1) * 8, :] for rg in range(n_sb // 8)]
                for i2 in range(ig * 8, ig * 8 + 8):
                    vi = impv[i2:i2 + 1, :]
                    for rg in range(n_sb // 8):
                        vj = impv[rg * 8:(rg + 1) * 8, :]
                        if rg < ig:
                            one = jnp.where(vi > vj, 1.0, 0.0)
                        elif rg > ig:
                            one = jnp.where(vi >= vj, 1.0, 0.0)
                        else:
                            one = jnp.where(sub > i2 - ig * 8, jnp.where(vi >= vj, 1.0, 0.0),
                                            jnp.where(vi > vj, 1.0, 0.0))
                        cnt[rg] = cnt[rg] + one
                for rg in range(n_sb // 8):
                    cnt_scr[rg * 8:(rg + 1) * 8, :] = cnt[rg]
        sel_scr[...] = jnp.where((cnt_scr[...] < n_sel) & (blk <= s_lane), 1.0, 0.0)

    sel_t = sel_scr[...]
    if n_sb < LANES:
        sel_t = jnp.concatenate([sel_t, jnp.zeros((LANES - n_sb, nl), F32)], axis=0)
    unsel = ((sel_t.T - 1.0) * -NEG_INF).astype(BF16)
    q_ext = jnp.concatenate([qq, per_group_head(unsel)], axis=1)

    lane_s = lax.broadcasted_iota(jnp.int32, (qb, SEL_TILE), 1)
    n_tiles = ((s_last + 1) * qb + SEL_TILE - 1) // SEL_TILE
    n_full = n_tiles // SEL_UNROLL
    rem = n_tiles % SEL_UNROLL
    rem_blocks = [w for w in (SEL_UNROLL >> i for i in range(1, SEL_UNROLL.bit_length())) if w > 0]

    def rem_start(w):
        return n_full * SEL_UNROLL + (rem & ~(2 * w - 1))

    def tile_scores(j, mx_):
        k0 = pl.multiple_of(j * SEL_TILE, SEL_TILE)
        causal = per_head([jnp.where(k0 + lane_s <= tok(qi, lane_s.shape), 0.0, NEG_INF) for qi in range(nq)])
        k_ext = jnp.concatenate([ks_ref[0, pl.ds(k0, SEL_TILE), :], oh_ref[j]], axis=1)
        sc_ = _dot_nt(q_ext, k_ext) + causal
        sc_scr[j] = sc_
        for c0 in range(0, SEL_TILE, LANES):
            mx_ = jnp.maximum(mx_, sc_[:, c0:c0 + LANES])
        return mx_

    def pass1(it, mx_):
        for u in range(SEL_UNROLL):
            mx_ = tile_scores(it * SEL_UNROLL + u, mx_)
        return mx_

    m_scr[...] = lax.fori_loop(0, n_full, pass1, jnp.full((rows, LANES), NEG_INF, F32))
    for w in rem_blocks:
        @pl.when((rem & w) != 0)
        def _(w=w):
            mx_ = m_scr[...]
            for u in range(w):
                mx_ = tile_scores(rem_start(w) + u, mx_)
            m_scr[...] = mx_
    m_scr[...] = jnp.broadcast_to(jnp.max(m_scr[...], axis=1, keepdims=True), m_scr.shape)
    l_scr[...] = jnp.zeros(l_scr.shape, F32)
    acc_scr[...] = jnp.zeros(acc_scr.shape, F32)

    def tile_pv(j):
        k0 = pl.multiple_of(j * SEL_TILE, SEL_TILE)
        m_b = jnp.concatenate([m_scr[...]] * (SEL_TILE // LANES), axis=1)
        p = jnp.exp2(sc_scr[j] - m_b)
        l_new = l_scr[...]
        for c0 in range(0, SEL_TILE, LANES):
            l_new = l_new + p[:, c0:c0 + LANES]
        l_scr[...] = l_new
        acc_scr[...] += _dot(p.astype(BF16), vs_ref[0, pl.ds(k0, SEL_TILE), :])

    def pass2(it, c):
        for u in range(SEL_UNROLL):
            tile_pv(it * SEL_UNROLL + u)
        return c

    lax.fori_loop(0, n_full, pass2, 0)
    for w in rem_blocks:
        @pl.when((rem & w) != 0)
        def _(w=w):
            for u in range(w):
                tile_pv(rem_start(w) + u)
    o_sel = acc_scr[...] / jnp.sum(l_scr[...], axis=1, keepdims=True)

    low_half = lane64 < HEAD_DIM
    for qi in range(nq):
        ts = slice(qi * qb, (qi + 1) * qb)
        for h in range(NSA_HPG):
            halves = []
            for gi in range(NSA_KV_HEADS):
                hq = gi * NSA_HPG + h
                r0 = (qi * n_head + hq) * qb
                hr = slice(r0, r0 + qb)
                col = hq * N_BRANCH
                halves.append(g_scr[col, ts] * o_cmp[hr] + g_scr[col + 1, ts] * o_sel[hr] + g_scr[col + 2, ts] * o_win[hr])
            o_ref[ts, h * LANES:(h + 1) * LANES] = jnp.where(low_half, halves[0], halves[1]).astype(o_ref.dtype)


def _nsa_attention(q, k_cmp, v_cmp, ks, vs, kw, vw, gl, b, t):
    tq = NSA_QBLK * Q_BLOCK
    ns = t // tq
    n_sb = t // SEL_BLOCK
    n_cmp = t // CMP_STRIDE
    n_kt = t // SEL_TILE
    cmp_start = jnp.arange(n_cmp) * CMP_STRIDE
    sb = jnp.arange(n_sb)
    ovt = ((cmp_start[None, :] < (sb[:, None] + 1) * SEL_BLOCK)
           & (cmp_start[None, :] + CMP_BLOCK > sb[:, None] * SEL_BLOCK)
           & (jnp.arange(n_cmp)[None, :] < n_cmp - 1)).astype(F32)
    key_blk = (jnp.arange(n_kt)[:, None] * SEL_TILE + jnp.arange(SEL_TILE)[None, :]) // SEL_BLOCK
    onehot = (key_blk[:, :, None] == jnp.arange(LANES)[None, None, :]).astype(BF16)
    rows = NSA_QBLK * NSA_Q_HEADS * Q_BLOCK
    nl = NSA_QBLK * NSA_KV_HEADS * Q_BLOCK
    qspec = pl.BlockSpec((tq, Q_PAD_COLS), lambda bi, si: (bi * ns + si, 0))
    seq = lambda: pl.BlockSpec((1, t, LANES), lambda bi, si: (bi, 0, 0))
    cmp_spec = lambda: pl.BlockSpec((1, n_cmp, LANES), lambda bi, si: (bi, 0, 0))
    k3 = lambda a_: a_.reshape(b, t, LANES)
    return pl.pallas_call(
        _nsa_kernel,
        grid=(b, ns),
        in_specs=[qspec, cmp_spec(), cmp_spec(), seq(), seq(), seq(), seq(),
                  pl.BlockSpec((tq, LANES), lambda bi, si: (bi * ns + si, 0)),
                  pl.BlockSpec((n_sb, n_cmp), lambda bi, si: (0, 0)),
                  pl.BlockSpec((n_kt, SEL_TILE, LANES), lambda bi, si: (0, 0, 0))],
        out_specs=pl.BlockSpec((tq, NSA_WIDTH), lambda bi, si: (bi * ns + si, 0)),
        out_shape=jax.ShapeDtypeStruct((b * t, NSA_WIDTH), BF16),
        scratch_shapes=[pltpu.VMEM((rows, LANES), F32),
                        pltpu.VMEM((rows, LANES), F32),
                        pltpu.VMEM((rows, LANES), F32),
                        pltpu.VMEM((n_kt, rows, SEL_TILE), F32),
                        pltpu.VMEM((n_sb, nl), F32),
                        pltpu.VMEM((n_sb, nl), F32),
                        pltpu.VMEM((n_sb, nl), F32),
                        pltpu.VMEM((NSA_Q_HEADS * N_BRANCH, tq, LANES), F32)],
        compiler_params=_params("parallel", "arbitrary"),
        name="nsa_attention",
    )(q, k_cmp, v_cmp, k3(ks), k3(vs), k3(kw), k3(vw), gl, ovt, onehot)


FF_TILE = 256


def _ffn_kernel(x_ref, orw_ref, ons_ref, wor_ref, won_ref, n2_ref, w1_ref, w3_ref, w2_ref, nf_ref,
                out_ref, h_scr, u_scr, acc_scr):
    j = pl.program_id(1)

    @pl.when(j == 0)
    def _():
        h = x_ref[...] + _dot(orw_ref[...], wor_ref[...]) + _dot(ons_ref[...], won_ref[...])
        h_scr[...] = h
        ms = jnp.mean(h * h, axis=-1, keepdims=True)
        u_scr[...] = (h * lax.rsqrt(ms + RMS_EPS) * n2_ref[...]).astype(BF16)
        acc_scr[...] = jnp.zeros_like(acc_scr)

    u = u_scr[...]
    gate = _dot(u, w1_ref[...])
    up = _dot(u, w3_ref[...])
    act = (jax.nn.silu(gate) * up).astype(BF16)
    acc_scr[...] += _dot(act, w2_ref[...])

    @pl.when(j == pl.num_programs(1) - 1)
    def _():
        h = h_scr[...] + acc_scr[...]
        ms = jnp.mean(h * h, axis=-1, keepdims=True)
        out_ref[...] = h * lax.rsqrt(ms + RMS_EPS) * nf_ref[...]


def _out_ffn(x2, o_rwkv, o_nsa, w_out, norm2_w, ffn_w1, ffn_w3, ffn_w2, final_norm_w):
    n = x2.shape[0]
    tm = 1024
    w_or = w_out[:RWKV_WIDTH].astype(BF16)
    w_n = w_out[RWKV_WIDTH:].reshape(NSA_KV_HEADS, NSA_HPG, HEAD_DIM, D_MODEL)
    w_on = w_n.transpose(1, 0, 2, 3).reshape(NSA_WIDTH, D_MODEL).astype(BF16)
    row = lambda width: pl.BlockSpec((tm, width), lambda i, j: (i, 0))
    const = lambda shape: pl.BlockSpec(shape, lambda i, j: (0, 0))
    return pl.pallas_call(
        _ffn_kernel,
        grid=(n // tm, D_FF // FF_TILE),
        in_specs=[row(D_MODEL), row(RWKV_WIDTH), row(NSA_WIDTH),
                  const((RWKV_WIDTH, D_MODEL)), const((NSA_WIDTH, D_MODEL)), const((1, D_MODEL)),
                  pl.BlockSpec((D_MODEL, FF_TILE), lambda i, j: (0, j)),
                  pl.BlockSpec((D_MODEL, FF_TILE), lambda i, j: (0, j)),
                  pl.BlockSpec((FF_TILE, D_MODEL), lambda i, j: (j, 0)),
                  const((1, D_MODEL))],
        out_specs=row(D_MODEL),
        out_shape=jax.ShapeDtypeStruct((n, D_MODEL), F32),
        scratch_shapes=[pltpu.VMEM((tm, D_MODEL), F32), pltpu.VMEM((tm, D_MODEL), BF16),
                        pltpu.VMEM((tm, D_MODEL), F32)],
        compiler_params=_params("parallel", "arbitrary"),
        name="out_ffn",
    )(x2, o_rwkv, o_nsa, w_or, w_on, norm2_w.reshape(1, D_MODEL), ffn_w1.astype(BF16), ffn_w3.astype(BF16),
      ffn_w2.astype(BF16), final_norm_w.reshape(1, D_MODEL))


def kernel(x, positions, norm1_w, w_in, mu_rwkv, w0, w_lora_up, a0, a_lora_up, g_lora_up, k_k, k_a, r_k, lnx_w, lnx_b, cmp_pos_k, cmp_pos_v, cmp_k_w1, cmp_k_w2, cmp_v_w1, cmp_v_w2, w_out, norm2_w, ffn_w1, ffn_w3, ffn_w2, final_norm_w):
    b, t, d = x.shape
    assert d == D_MODEL and norm1_w.shape[0] == 1, "single-layer block with d_model 1024"
    assert t % RWKV_TT == 0 and t % SEL_TILE == 0 and t % (NSA_QBLK * Q_BLOCK) == 0
    x2 = x.reshape(b * t, d)
    yr, q, kc, vc, ks, vs, kw, vw, gl = _in_proj(x2, norm1_w[0], _pack_w_in(w_in[0]), positions)
    o_rwkv = _rwkv(yr, b, t, mu_rwkv[0], w0[0], w_lora_up[0], a0[0], a_lora_up[0], g_lora_up[0],
                   k_k[0], k_a[0], r_k[0], lnx_w[0], lnx_b[0])
    k_cmp, v_cmp = _compress(kc, vc, b, t, cmp_pos_k[0], cmp_pos_v[0], cmp_k_w1[0], cmp_k_w2[0],
                             cmp_v_w1[0], cmp_v_w2[0])
    o_nsa = _nsa_attention(q, k_cmp, v_cmp, ks, vs, kw, vw, gl, b, t)
    out = _out_ffn(x2, o_rwkv, o_nsa, w_out[0], norm2_w[0], ffn_w1[0], ffn_w3[0], ffn_w2[0], final_norm_w)
    return out.reshape(b, t, d)
```
